```python
import math, functools
import jax, jax.numpy as jnp
from jax import lax
import numpy as np

D_MODEL = 2048
BATCH = 4
SEQ = 2048
DEPTH = 1
DEC_BATCH = 128
DEC_SEQ = 4
PAST_LEN = 8192
PAGE_SIZE = 128

A_HEADS = 8
A_DQK = 128
A_DV = 256
A_QK_W = A_HEADS * A_DQK
A_V_W = A_HEADS * A_DV
A_CHUNK = 64
GATE_CAP = 15.0
B_HEADS = 16
B_Q_LORA = 512
B_KV_LORA = 512
B_NOPE = 128
B_ROPE = 64
B_QK = B_NOPE + B_ROPE
B_V = 128
ROPE_THETA = 10000.0
Q_BLOCK = 128
ATTN_SCALE = B_QK ** -0.5
D_FF = 5632
CONV_W = 3
EPS = 1e-6
IN_SPLITS = (A_QK_W, A_QK_W, A_V_W, A_V_W, A_HEADS, A_HEADS, B_Q_LORA, B_KV_LORA, B_ROPE, D_MODEL, D_MODEL)
D_IN = sum(IN_SPLITS)

kernel_name = 'hybrid_mlstm_mla_convffn_step'


def rmsnorm(x, g):
    x32 = x.astype(jnp.float32)
    r = lax.rsqrt(jnp.mean(x32 * x32, axis=-1, keepdims=True) + EPS)
    return (x32 * r * g.astype(jnp.float32)).astype(x.dtype)


def soft_cap(x):
    return GATE_CAP * jnp.tanh(x / GATE_CAP)


def rope(x, pos):
    half = x.shape[-1] // 2
    freqs = ROPE_THETA ** (-jnp.arange(half, dtype=jnp.float32) / half)
    ang = pos.astype(jnp.float32)[..., None] * freqs
    cos = jnp.cos(ang).astype(x.dtype)
    sin = jnp.sin(ang).astype(x.dtype)
    x1, x2 = x[..., :half], x[..., half:]
    return jnp.concatenate([x1 * cos - x2 * sin, x2 * cos + x1 * sin], axis=-1)


def qk_norm(x_nope, x_rope, g_nope, g_rope):
    x = jnp.concatenate([x_nope, x_rope], axis=-1)
    return rmsnorm(x, jnp.concatenate([g_nope, g_rope, g_rope]))


def split_cols(z):
    parts = []
    off = 0
    for w in IN_SPLITS:
        parts.append(z[..., off:off + w])
        off += w
    return parts


def project_in(x, pos, g_attn_norm, w_in, b_gates, g_q_a, w_uq, g_qk_nope_q, g_qk_rope_q, g_kv_a):
    b, s, _ = x.shape
    xn = rmsnorm(x, g_attn_norm)
    z = jnp.einsum('bsd,de->bse', xn, w_in)
    qa, ka, va, oa, ia, fa, cq, ckv, kr, gate_a, gate_b = split_cols(z)
    q_a = qa.reshape(b, s, A_HEADS, A_DQK)
    k_a = ka.reshape(b, s, A_HEADS, A_DQK) * (A_DQK ** -0.5)
    v_a = va.reshape(b, s, A_HEADS, A_DV)
    i_pre = ia + b_gates[:A_HEADS]
    f_pre = fa + b_gates[A_HEADS:]
    q_b = jnp.einsum('bsc,che->bshe', rmsnorm(cq, g_q_a), w_uq)
    q_b = qk_norm(q_b[..., :B_NOPE], rope(q_b[..., B_NOPE:], pos[:, None]), g_qk_nope_q, g_qk_rope_q)
    c_kv = rmsnorm(ckv, g_kv_a)
    k_rope = rope(kr, pos)
    return q_a, k_a, v_a, i_pre, f_pre, oa, q_b, c_kv, k_rope, gate_a, gate_b


def mlstm_chunkwise(q, k, v, log_i, log_f, C0, n0, m0, chunk):
    bsz, nh, s, _ = q.shape
    nc = s // chunk

    def chunks(t):
        return jnp.moveaxis(t.reshape(t.shape[:2] + (nc, chunk) + t.shape[3:]), 2, 0)

    causal = jnp.tril(jnp.ones((chunk, chunk), dtype=bool))

    def step(carry, xc):
        C, n, m = carry
        qc, kc, vc, li, lf = xc
        bcum = jnp.cumsum(lf, axis=-1)
        log_w = bcum[..., :, None] - bcum[..., None, :] + li[..., None, :]
        log_w = jnp.where(causal, log_w, -jnp.inf)
        log_state = bcum + m[..., None]
        m_row = jnp.maximum(log_state, jnp.max(log_w, axis=-1))
        w = jnp.exp(log_w - m_row[..., None])
        w_state = jnp.exp(log_state - m_row)
        sc = jnp.einsum('bhld,bhsd->bhls', qc, kc) * w
        num = jnp.einsum('bhls,bhsv->bhlv', sc, vc) + w_state[..., None] * jnp.einsum('bhld,bhdv->bhlv', qc, C)
        den = jnp.sum(sc, axis=-1) + w_state * jnp.einsum('bhld,bhd->bhl', qc, n)
        h = num / jnp.maximum(jnp.abs(den), jnp.exp(-m_row))[..., None]
        b_last = bcum[..., -1]
        log_k = b_last[..., None] - bcum + li
        m_new = jnp.maximum(b_last + m, jnp.max(log_k, axis=-1))
        wk = jnp.exp(log_k - m_new[..., None])
        decay = jnp.exp(b_last + m - m_new)
        kw = kc * wk[..., None]
        C_new = decay[..., None, None] * C + jnp.einsum('bhsd,bhsv->bhdv', kw, vc)
        n_new = decay[..., None] * n + jnp.sum(kw, axis=2)
        return (C_new, n_new, m_new), h

    (C, n, m), h = lax.scan(step, (C0, n0, m0), tuple(chunks(t) for t in (q, k, v, log_i, log_f)))
    h = jnp.moveaxis(h, 0, 2).reshape(bsz, nh, s, v.shape[-1])
    return h, C, n, m


def mlstm_branch(q, k, v, i_pre, f_pre, o_pre, C0, n0, m0, g_out):
    b, s, nh, _ = q.shape
    f32 = jnp.float32
    tr = lambda t: jnp.swapaxes(t, 1, 2).astype(f32)
    log_i = soft_cap(i_pre.astype(f32))
    log_f = jax.nn.log_sigmoid(soft_cap(f_pre.astype(f32)))
    hid, C, n, m = mlstm_chunkwise(tr(q), tr(k), tr(v), tr(log_i), tr(log_f),
                                   C0.astype(f32), n0.astype(f32), m0.astype(f32), math.gcd(s, A_CHUNK))
    hid = jnp.swapaxes(hid, 1, 2)
    hid = rmsnorm(hid, g_out.reshape(nh, A_DV)).reshape(b, s, nh * A_DV).astype(o_pre.dtype)
    return hid * jax.nn.sigmoid(o_pre), C, n, m


def mla_keys(c_kv, k_rope, w_uk, g_kn, g_kr):
    k_nope = jnp.einsum('...sc,chd->...shd', c_kv, w_uk)
    k_r = jnp.broadcast_to(k_rope[..., None, :], k_nope.shape[:-1] + (B_ROPE,))
    return qk_norm(k_nope, k_r, g_kn, g_kr)


def mla_attend(q, k, c_kv, mask, w_uv):
    s = jnp.einsum('bqhe,bkhe->bhqk', q, k).astype(jnp.float32) * ATTN_SCALE
    s = jnp.where(mask, s, -jnp.inf)
    p = jax.nn.softmax(s, axis=-1).astype(c_kv.dtype)
    o_lat = jnp.einsum('bhqk,bkc->bqhc', p, c_kv)
    return jnp.einsum('bqhc,chv->bqhv', o_lat, w_uv)


def mla_prompt(q, c_kv, k_rope, w_uk, w_uv, g_kn, g_kr):
    b, s = q.shape[:2]
    k = mla_keys(c_kv, k_rope, w_uk, g_kn, g_kr)
    qb = math.gcd(s, Q_BLOCK)
    k_pos = jnp.arange(s)

    def block(i):
        start = i * qb
        q_blk = lax.dynamic_slice_in_dim(q, start, qb, axis=1)
        mask = k_pos[None, :] <= (start + jnp.arange(qb))[:, None]
        return mla_attend(q_blk, k, c_kv, mask, w_uv)

    o = lax.map(block, jnp.arange(s // qb))
    return jnp.moveaxis(o, 0, 1).reshape(b, s, B_HEADS * B_V)


def mla_sample(q, c_kv_new, k_rope_new, cache_kv_latent, cache_k_rope, page_table, layer, w_uk, w_uv, g_kn, g_kr):
    bd, s = q.shape[:2]
    past = page_table.shape[1] * cache_kv_latent.shape[2]
    k_pos = jnp.arange(past + s)
    q_pos = past + jnp.arange(s)
    mask = k_pos[None, :] <= q_pos[:, None]

    def one(args):
        pt, q_seq, c_new, r_new = args
        lat = jnp.concatenate([cache_kv_latent[layer, pt].reshape(past, B_KV_LORA).astype(c_new.dtype), c_new], axis=0)
        kr = jnp.concatenate([cache_k_rope[layer, pt].reshape(past, B_ROPE).astype(r_new.dtype), r_new], axis=0)
        k = mla_keys(lat, kr, w_uk, g_kn, g_kr)
        return mla_attend(q_seq[None], k[None], lat[None], mask, w_uv)[0]

    o = lax.map(one, (page_table, q, c_kv_new, k_rope_new))
    return o.reshape(bd, s, B_HEADS * B_V)


def conv_ffn(x, prev, w_up, w_conv, b_conv, w_down):
    s = x.shape[1]
    u = jnp.einsum('bsd,df->bsf', x, w_up)
    ext = jnp.concatenate([prev.astype(u.dtype), u], axis=1)
    c = b_conv + sum(ext[:, j:j + s] * w_conv[j] for j in range(CONV_W))
    gate, val = c[..., :D_FF], c[..., D_FF:]
    out = jnp.einsum('bsf,fd->bsd', jax.nn.silu(gate) * val, w_down)
    return out, ext[:, ext.shape[1] - (CONV_W - 1):]


def layer(x, pos, C0, n0, m0, conv0, attend, g_attn_norm, w_in, b_gates, g_q_a, w_uq, g_qk_nope_q, g_qk_rope_q,
          g_kv_a, g_mlstm_out, w_branch_a, w_branch_b, w_out, g_ffn_norm, w_up, w_conv, b_conv, w_down):
    q_a, k_a, v_a, i_pre, f_pre, o_pre, q_b, c_kv, k_rope, gate_a, gate_b = project_in(
        x, pos, g_attn_norm, w_in, b_gates, g_q_a, w_uq, g_qk_nope_q, g_qk_rope_q, g_kv_a)
    h_a, C, n, m = mlstm_branch(q_a, k_a, v_a, i_pre, f_pre, o_pre, C0, n0, m0, g_mlstm_out)
    h_b = attend(q_b, c_kv, k_rope)
    merged = jax.nn.sigmoid(gate_a) * (h_a @ w_branch_a) + jax.nn.sigmoid(gate_b) * (h_b @ w_branch_b)
    x = x + merged @ w_out
    f, conv_new = conv_ffn(rmsnorm(x, g_ffn_norm), conv0, w_up, w_conv, b_conv, w_down)
    return x + f, (c_kv, k_rope, C, n, m, conv_new)


def setup_inputs(seed: int = 0) -> dict:
    key = jax.random.key(seed)
    ks = jax.random.split(key, 40)
    f32 = jnp.float32
    n_pages = PAST_LEN // PAGE_SIZE
    n_used = DEC_BATCH * n_pages
    n_pool = n_used + (n_used + 3) // 4
    nrm = lambda k, shape, s=1.0: s * jax.random.normal(k, shape, f32)
    gain = lambda k, shape: 1.0 + 0.05 * jax.random.normal(k, shape, f32)
    b_gates = jnp.concatenate([nrm(ks[10], (DEPTH, A_HEADS), 0.01), 3.0 + nrm(ks[11], (DEPTH, A_HEADS), 0.5)], axis=-1)
    page_table = jax.random.permutation(ks[9], n_pool)[:n_used].reshape(DEC_BATCH, n_pages).astype(jnp.int32)
    return {
        'x_prompt': nrm(ks[0], (BATCH, SEQ, D_MODEL)),
        'x_sample': nrm(ks[1], (DEC_BATCH, DEC_SEQ, D_MODEL)),
        'cache_kv_latent': nrm(ks[2], (DEPTH, n_pool, PAGE_SIZE, B_KV_LORA)),
        'cache_k_rope': nrm(ks[3], (DEPTH, n_pool, PAGE_SIZE, B_ROPE)),
        'state_mlstm_C': nrm(ks[4], (DEPTH, DEC_BATCH, A_HEADS, A_DQK, A_DV), 0.2),
        'state_mlstm_n': nrm(ks[5], (DEPTH, DEC_BATCH, A_HEADS, A_DQK), 0.2),
        'state_mlstm_m': jax.random.uniform(ks[6], (DEPTH, DEC_BATCH, A_HEADS), f32, 0.0, 3.0),
        'state_conv': nrm(ks[7], (DEPTH, DEC_BATCH, CONV_W - 1, 2 * D_FF)),
        'page_table': page_table,
        'g_attn_norm': gain(ks[12], (DEPTH, D_MODEL)),
        'w_in': nrm(ks[13], (DEPTH, D_MODEL, D_IN), D_MODEL ** -0.5),
        'b_gates': b_gates,
        'g_q_a': gain(ks[14], (DEPTH, B_Q_LORA)),
        'w_uq': nrm(ks[15], (DEPTH, B_Q_LORA, B_HEADS, B_QK), B_Q_LORA ** -0.5),
        'g_qk_nope_q': gain(ks[16], (DEPTH, B_NOPE)),
        'g_qk_rope_q': gain(ks[17], (DEPTH, B_ROPE // 2)),
        'g_kv_a': gain(ks[18], (DEPTH, B_KV_LORA)),
        'w_uk': nrm(ks[19], (DEPTH, B_KV_LORA, B_HEADS, B_NOPE), B_KV_LORA ** -0.5),
        'w_uv': nrm(ks[20], (DEPTH, B_KV_LORA, B_HEADS, B_V), B_KV_LORA ** -0.5),
        'g_qk_nope_k': gain(ks[21], (DEPTH, B_NOPE)),
        'g_qk_rope_k': gain(ks[22], (DEPTH, B_ROPE // 2)),
        'g_mlstm_out': gain(ks[23], (DEPTH, A_V_W)),
        'w_branch_a': nrm(ks[24], (DEPTH, A_V_W, D_MODEL), A_V_W ** -0.5),
        'w_branch_b': nrm(ks[25], (DEPTH, B_HEADS * B_V, D_MODEL), (B_HEADS * B_V) ** -0.5),
        'w_out': nrm(ks[26], (DEPTH, D_MODEL, D_MODEL), D_MODEL ** -0.5),
        'g_ffn_norm': gain(ks[27], (DEPTH, D_MODEL)),
        'w_up': nrm(ks[28], (DEPTH, D_MODEL, 2 * D_FF), D_MODEL ** -0.5),
        'w_conv': nrm(ks[29], (DEPTH, CONV_W, 2 * D_FF), CONV_W ** -0.5),
        'b_conv': nrm(ks[30], (DEPTH, 2 * D_FF), 0.01),
        'w_down': nrm(ks[31], (DEPTH, D_FF, D_MODEL), D_FF ** -0.5),
    }


def reference(x_prompt, x_sample, cache_kv_latent, cache_k_rope, state_mlstm_C, state_mlstm_n, state_mlstm_m,
              state_conv, page_table, g_attn_norm, w_in, b_gates, g_q_a, w_uq, g_qk_nope_q, g_qk_rope_q, g_kv_a,
              w_uk, w_uv, g_qk_nope_k, g_qk_rope_k, g_mlstm_out, w_branch_a, w_branch_b, w_out, g_ffn_norm,
              w_up, w_conv, b_conv, w_down):
    f32 = jnp.float32
    bp, s_p, _ = x_prompt.shape
    s_s = x_sample.shape[1]
    past = page_table.shape[1] * cache_kv_latent.shape[2]
    pos_p = jnp.arange(s_p, dtype=jnp.int32)
    pos_s = past + jnp.arange(s_s, dtype=jnp.int32)
    hp, hs = x_prompt, x_sample
    new_p, new_s = [], []
    for l in range(DEPTH):
        wl = (g_attn_norm[l], w_in[l], b_gates[l], g_q_a[l], w_uq[l], g_qk_nope_q[l], g_qk_rope_q[l], g_kv_a[l],
              g_mlstm_out[l], w_branch_a[l], w_branch_b[l], w_out[l], g_ffn_norm[l], w_up[l], w_conv[l], b_conv[l],
              w_down[l])
        attend_p = functools.partial(mla_prompt, w_uk=w_uk[l], w_uv=w_uv[l], g_kn=g_qk_nope_k[l], g_kr=g_qk_rope_k[l])
        attend_s = functools.partial(mla_sample, cache_kv_latent=cache_kv_latent, cache_k_rope=cache_k_rope,
                                     page_table=page_table, layer=l, w_uk=w_uk[l], w_uv=w_uv[l],
                                     g_kn=g_qk_nope_k[l], g_kr=g_qk_rope_k[l])
        C0 = jnp.zeros((bp, A_HEADS, A_DQK, A_DV), f32)
        n0 = jnp.zeros((bp, A_HEADS, A_DQK), f32)
        m0 = jnp.zeros((bp, A_HEADS), f32)
        conv0 = jnp.zeros((bp, CONV_W - 1, 2 * D_FF), x_prompt.dtype)
        hp, st_p = layer(hp, pos_p, C0, n0, m0, conv0, attend_p, *wl)
        hs, st_s = layer(hs, pos_s, state_mlstm_C[l], state_mlstm_n[l], state_mlstm_m[l], state_conv[l], attend_s, *wl)
        new_p.append(st_p)
        new_s.append(st_s)
    dts = (cache_kv_latent.dtype, cache_k_rope.dtype, state_mlstm_C.dtype, state_mlstm_n.dtype,
           state_mlstm_m.dtype, state_conv.dtype)
    stk = lambda lst, i: jnp.stack([st[i] for st in lst], axis=0).astype(dts[i])
    kv_latent_p, k_rope_p, mlstm_C_p, mlstm_n_p, mlstm_m_p, conv_p = [stk(new_p, i) for i in range(6)]
    kv_latent_s, k_rope_s, mlstm_C_s, mlstm_n_s, mlstm_m_s, conv_s = [stk(new_s, i) for i in range(6)]
    return (hp, hs, kv_latent_p, k_rope_p, mlstm_C_p, mlstm_n_p, mlstm_m_p, conv_p,
            kv_latent_s, k_rope_s, mlstm_C_s, mlstm_n_s, mlstm_m_s, conv_s)
```

```python
import functools
import math

import jax
import jax.numpy as jnp
from jax import lax
from jax.experimental import pallas as pl
from jax.experimental.pallas import tpu as pltpu

F32 = jnp.float32
BF16 = jnp.bfloat16

A_HEADS = 8
A_DQK = 128
A_DV = 256
GATE_CAP = 15.0
B_HEADS = 16
B_NOPE = 128
B_ROPE = 64
B_QK = B_NOPE + B_ROPE
B_V = 128
Q_HEAD_PAD = 256
ROPE_THETA = 10000.0
ATTN_SCALE = B_QK ** -0.5
CONV_W = 3
EPS = 1e-6
NEG_INF = float("-inf")

LANES = 128
SUBLANES = 8
BF16_ROWS = 16
MXU_COLS = 256
VMEM_LIMIT = 56 * 1024 * 1024


def _cparams(sem):
    return pltpu.CompilerParams(dimension_semantics=sem, vmem_limit_bytes=VMEM_LIMIT)


def _rms(x, g):
    r = lax.rsqrt(jnp.mean(x * x, axis=-1, keepdims=True) + EPS)
    return x * r * g


def _dot(a, b):
    return jnp.dot(a, b, preferred_element_type=F32)


def _dot_nt(a, b):
    return lax.dot_general(a, b, (((1,), (1,)), ((), ())), preferred_element_type=F32)


def _rope128(x, cos, sin_a, sin_b):
    return x * cos + pltpu.roll(x, 32, 1) * sin_a + pltpu.roll(x, 96, 1) * sin_b


def _norm_mm_kernel(x_ref, g_ref, w_ref, o_ref, xn_ref):
    @pl.when(pl.program_id(1) == 0)
    def _():
        xn_ref[...] = _rms(x_ref[...], g_ref[...]).astype(BF16)

    o_ref[...] = _dot(xn_ref[...], w_ref[...]).astype(o_ref.dtype)


def norm_matmul(x, g, w, bm, bn, out_dtype=F32):
    t, d = x.shape
    n = w.shape[1]
    return pl.pallas_call(
        _norm_mm_kernel,
        grid=(t // bm, n // bn),
        in_specs=[pl.BlockSpec((bm, d), lambda i, j: (i, 0)),
                  pl.BlockSpec((1, d), lambda i, j: (0, 0)),
                  pl.BlockSpec((d, bn), lambda i, j: (0, j))],
        out_specs=pl.BlockSpec((bm, bn), lambda i, j: (i, j)),
        out_shape=jax.ShapeDtypeStruct((t, n), out_dtype),
        scratch_shapes=[pltpu.VMEM((bm, d), BF16)],
        compiler_params=_cparams(("parallel", "arbitrary")),
    )(x, g, w)


def _q_kernel(cq_ref, g_ref, w_ref, cos_ref, sa_ref, sb_ref, gain_ref, o_ref):
    cqn = _rms(cq_ref[...], g_ref[...]).astype(BF16)
    q = _dot(cqn, w_ref[...])
    cos, sa, sb = cos_ref[...], sa_ref[...], sb_ref[...]
    for h in range(B_HEADS):
        lo = h * Q_HEAD_PAD
        nope = q[:, lo:lo + LANES]
        rp = _rope128(q[:, lo + LANES:lo + Q_HEAD_PAD], cos, sa, sb)
        ssq = jnp.sum(nope * nope, axis=-1, keepdims=True) + jnp.sum(rp * rp, axis=-1, keepdims=True)
        r = lax.rsqrt(ssq * (1.0 / B_QK) + EPS)
        o_ref[:, lo:lo + LANES] = (nope * r * gain_ref[:, lo:lo + LANES]).astype(o_ref.dtype)
        o_ref[:, lo + LANES:lo + Q_HEAD_PAD] = (rp * r * gain_ref[:, lo + LANES:lo + Q_HEAD_PAD]).astype(o_ref.dtype)


def q_proj(z_lat, g_q_a, w_uq_pad, cos, sa, sb, gain, bm):
    t = z_lat.shape[0]
    c = g_q_a.shape[1]
    n = w_uq_pad.shape[1]
    return pl.pallas_call(
        _q_kernel,
        grid=(t // bm,),
        in_specs=[pl.BlockSpec((bm, c), lambda i: (i, 0)),
                  pl.BlockSpec((1, c), lambda i: (0, 0)),
                  pl.BlockSpec((c, n), lambda i: (0, 0)),
                  pl.BlockSpec((bm, LANES), lambda i: (i, 0)),
                  pl.BlockSpec((bm, LANES), lambda i: (i, 0)),
                  pl.BlockSpec((bm, LANES), lambda i: (i, 0)),
                  pl.BlockSpec((1, n), lambda i: (0, 0))],
        out_specs=pl.BlockSpec((bm, n), lambda i: (i, 0)),
        out_shape=jax.ShapeDtypeStruct((t, n), BF16),
        compiler_params=_cparams(("parallel",)),
    )(z_lat, g_q_a, w_uq_pad, cos, sa, sb, gain)


def _kv_kernel(ckv_ref, sm_ref, g_ref, wuk_ref, wuv_ref, cos_ref, sa_ref, sb_ref,
               c_ref, kr_ref, k_ref, v_ref):
    c = _rms(ckv_ref[...], g_ref[...])
    c_ref[...] = c
    cb = c.astype(BF16)
    kn = _dot(cb, wuk_ref[...])
    v_ref[...] = _dot(cb, wuv_ref[...]).astype(v_ref.dtype)
    kr = _rope128(sm_ref[...], cos_ref[...], sa_ref[...], sb_ref[...])
    kr_ref[...] = kr
    ssr = jnp.sum(kr * kr, axis=-1, keepdims=True)
    for h in range(B_HEADS):
        nope = kn[:, h * B_NOPE:(h + 1) * B_NOPE]
        r = lax.rsqrt((jnp.sum(nope * nope, axis=-1, keepdims=True) + ssr) * (1.0 / B_QK) + EPS)
        lo = h * Q_HEAD_PAD
        k_ref[:, lo:lo + LANES] = (nope * r).astype(k_ref.dtype)
        k_ref[:, lo + LANES:lo + Q_HEAD_PAD] = (kr * r).astype(k_ref.dtype)


def kv_proj(z_lat, g_kv_a, w_uk, w_uv, cos, sa, sb, bm):
    t = z_lat.shape[0]
    c = g_kv_a.shape[1]
    small_blk = (2 * c) // LANES
    row = lambda i: (i, 0)
    const = lambda i: (0, 0)
    return pl.pallas_call(
        _kv_kernel,
        grid=(t // bm,),
        in_specs=[pl.BlockSpec((bm, c), lambda i: (i, 1)),
                  pl.BlockSpec((bm, LANES), lambda i: (i, small_blk)),
                  pl.BlockSpec((1, c), const),
                  pl.BlockSpec(w_uk.shape, const),
                  pl.BlockSpec(w_uv.shape, const),
                  pl.BlockSpec((bm, LANES), row),
                  pl.BlockSpec((bm, LANES), row),
                  pl.BlockSpec((bm, LANES), row)],
        out_specs=[pl.BlockSpec((bm, c), row),
                   pl.BlockSpec((bm, LANES), row),
                   pl.BlockSpec((bm, B_HEADS * Q_HEAD_PAD), row),
                   pl.BlockSpec((bm, B_HEADS * B_V), row)],
        out_shape=[jax.ShapeDtypeStruct((t, c), F32),
                   jax.ShapeDtypeStruct((t, LANES), F32),
                   jax.ShapeDtypeStruct((t, B_HEADS * Q_HEAD_PAD), BF16),
                   jax.ShapeDtypeStruct((t, B_HEADS * B_V), BF16)],
        compiler_params=_cparams(("parallel",)),
    )(z_lat, z_lat, g_kv_a, w_uk, w_uv, cos, sa, sb)


def _log_gates(gi, gf, bi, bf):
    cap = lambda x: GATE_CAP * jnp.tanh(x * (1.0 / GATE_CAP))
    li = cap(gi + bi)
    y = cap(gf + bf)
    lf = jnp.minimum(y, 0.0) - jnp.log(1.0 + jnp.exp(-jnp.abs(y)))
    return li, lf


def _mlstm_p_kernel(b_ref, q_ref, k_ref, v_ref, gir_ref, gfr_ref, gic_ref, gfc_ref,
                    hid_ref, c_out_ref, n_out_ref, m_out_ref, c_scr, n_scr, m_scr):
    h = pl.program_id(1)
    ci = pl.program_id(2)
    L = q_ref.shape[0]

    @pl.when(ci == 0)
    def _():
        c_scr[...] = jnp.zeros_like(c_scr)
        n_scr[...] = jnp.zeros_like(n_scr)
        m_scr[...] = jnp.zeros_like(m_scr)

    bi = b_ref[h]
    bf = b_ref[A_HEADS + h]
    q = q_ref[...]
    k = k_ref[...] * (A_DQK ** -0.5)
    qb = q.astype(BF16)
    kb = k.astype(BF16)
    vb = v_ref[...].astype(BF16)
    li_r, lf_r = _log_gates(gir_ref[0], gfr_ref[0], bi, bf)
    li_c, lf_c = _log_gates(gic_ref[0], gfc_ref[0], bi, bf)

    t_idx = lax.broadcasted_iota(jnp.int32, (L, L), 0)
    s_idx = lax.broadcasted_iota(jnp.int32, (L, L), 1)
    causal = s_idx <= t_idx
    bcum_c = jnp.sum(jnp.where(causal, lf_r, 0.0), axis=1, keepdims=True)
    bcum_r = jnp.sum(jnp.where(t_idx <= s_idx, lf_c, 0.0), axis=0, keepdims=True)
    log_w = jnp.where(causal, bcum_c - bcum_r + li_r, NEG_INF)
    m_prev = m_scr[:, 0:1]
    log_state = bcum_c + m_prev
    m_row = jnp.maximum(log_state, jnp.max(log_w, axis=1, keepdims=True))
    w = jnp.exp(log_w - m_row)
    w_state = jnp.exp(log_state - m_row)
    sc = _dot_nt(qb, kb) * w
    num = _dot(sc.astype(BF16), vb) + w_state * _dot(qb, c_scr[...].astype(BF16))
    den = jnp.sum(sc, axis=1, keepdims=True) + w_state * jnp.sum(q * n_scr[...], axis=1, keepdims=True)
    hid_ref[...] = num / jnp.maximum(jnp.abs(den), jnp.exp(-m_row))

    b_last = bcum_c[L - 1:L, :]
    log_k = b_last - bcum_c + li_c
    m_new = jnp.maximum(b_last + m_prev, jnp.max(log_k, axis=0, keepdims=True))
    wk = jnp.exp(log_k - m_new)
    decay = jnp.exp(b_last + m_prev - m_new)
    kw = k * wk
    kwt = jnp.transpose(kw).astype(BF16)
    c_new = decay * c_scr[...] + _dot(kwt, vb)
    n_new = decay * n_scr[...] + jnp.sum(kw, axis=0, keepdims=True)
    c_scr[...] = c_new
    n_scr[...] = n_new
    m_scr[...] = jnp.broadcast_to(m_new, m_scr.shape)

    @pl.when(ci == pl.num_programs(2) - 1)
    def _():
        c_out_ref[0, 0] = c_new
        n_out_ref[0, 0] = n_new
        m_out_ref[0, 0] = jnp.broadcast_to(m_new, (1, LANES))


def mlstm_prompt(z_main, b_gates, gi_r, gf_r, gi_c, gf_c, bp, s, chunk):
    nc = s // chunk
    kcol = (A_HEADS * A_DQK) // A_DQK
    vcol = (2 * A_HEADS * A_DQK) // A_DV
    tok = lambda b, h, c, *_: (b * nc + c)
    return pl.pallas_call(
        _mlstm_p_kernel,
        grid_spec=pltpu.PrefetchScalarGridSpec(
            num_scalar_prefetch=1,
            grid=(bp, A_HEADS, nc),
            in_specs=[pl.BlockSpec((chunk, A_DQK), lambda b, h, c, *_: (b * nc + c, h)),
                      pl.BlockSpec((chunk, A_DQK), lambda b, h, c, *_: (b * nc + c, kcol + h)),
                      pl.BlockSpec((chunk, A_DV), lambda b, h, c, *_: (b * nc + c, vcol + h)),
                      pl.BlockSpec((1, 1, chunk), lambda b, h, c, *_: (h, 0, b * nc + c)),
                      pl.BlockSpec((1, 1, chunk), lambda b, h, c, *_: (h, 0, b * nc + c)),
                      pl.BlockSpec((1, chunk, 1), lambda b, h, c, *_: (h, b * nc + c, 0)),
                      pl.BlockSpec((1, chunk, 1), lambda b, h, c, *_: (h, b * nc + c, 0))],
            out_specs=[pl.BlockSpec((chunk, A_DV), lambda b, h, c, *_: (b * nc + c, h)),
                       pl.BlockSpec((1, 1, A_DQK, A_DV), lambda b, h, c, *_: (b, h, 0, 0)),
                       pl.BlockSpec((1, 1, 1, A_DQK), lambda b, h, c, *_: (b, h, 0, 0)),
                       pl.BlockSpec((1, 1, 1, LANES), lambda b, h, c, *_: (b, h, 0, 0))],
            scratch_shapes=[pltpu.VMEM((A_DQK, A_DV), F32),
                            pltpu.VMEM((1, A_DQK), F32),
                            pltpu.VMEM((1, LANES), F32)]),
        out_shape=[jax.ShapeDtypeStruct((bp * s, A_HEADS * A_DV), F32),
                   jax.ShapeDtypeStruct((bp, A_HEADS, A_DQK, A_DV), F32),
                   jax.ShapeDtypeStruct((bp, A_HEADS, 1, A_DQK), F32),
                   jax.ShapeDtypeStruct((bp, A_HEADS, 1, LANES), F32)],
        compiler_params=_cparams(("parallel", "parallel", "arbitrary")),
    )(b_gates, z_main, z_main, z_main, gi_r, gf_r, gi_c, gf_c)


def _mlstm_s_kernel(b_ref, q_ref, k_ref, v_ref, gir_ref, gfr_ref, gic_ref, gfc_ref,
                    mc_ref, nrow_ref, c_in_ref,
                    hid_ref, c_out_ref, n_out_ref, m_out_ref,
                    hc_scr, dec_scr, kwt_scr, *, seq):
    h = pl.program_id(1)
    R = q_ref.shape[0]
    nseq = R // seq
    per8 = SUBLANES // seq
    bi = b_ref[h]
    bf = b_ref[A_HEADS + h]
    q = q_ref[...]
    k = k_ref[...] * (A_DQK ** -0.5)
    v = v_ref[...]
    li_r, lf_r = _log_gates(gir_ref[0], gfr_ref[0], bi, bf)
    li_c, lf_c = _log_gates(gic_ref[0], gfc_ref[0], bi, bf)
    m_prev = mc_ref[0]

    t_idx = lax.broadcasted_iota(jnp.int32, (R, R), 0)
    s_idx = lax.broadcasted_iota(jnp.int32, (R, R), 1)
    same = (t_idx // seq) == (s_idx // seq)
    causal = same & (s_idx <= t_idx)
    bcum_c = jnp.sum(jnp.where(causal, lf_r, 0.0), axis=1, keepdims=True)
    bcum_r = jnp.sum(jnp.where(same & (t_idx <= s_idx), lf_c, 0.0), axis=0, keepdims=True)
    blast_c = jnp.sum(jnp.where(same, lf_r, 0.0), axis=1, keepdims=True)
    blast_r = jnp.sum(jnp.where(same, lf_c, 0.0), axis=0, keepdims=True)
    log_w = jnp.where(causal, bcum_c - bcum_r + li_r, NEG_INF)
    log_state = bcum_c + m_prev
    m_row = jnp.maximum(log_state, jnp.max(log_w, axis=1, keepdims=True))
    w = jnp.exp(log_w - m_row)
    w_state = jnp.exp(log_state - m_row)

    log_k_c = blast_c - bcum_c + li_c
    log_k_r = blast_r - bcum_r + li_r
    seg_max = jnp.max(jnp.where(same, log_k_r, NEG_INF), axis=1, keepdims=True)
    m_new = jnp.maximum(blast_c + m_prev, seg_max)
    wk = jnp.exp(log_k_c - m_new)
    decay = jnp.exp(blast_c + m_prev - m_new)
    kw = k * wk
    kwt_scr[...] = jnp.transpose(kw)
    dec_scr[...] = jnp.broadcast_to(decay, dec_scr.shape)

    row8 = lax.broadcasted_iota(jnp.int32, (SUBLANES, A_DV), 0)
    col = lax.broadcasted_iota(jnp.int32, (A_DQK, R), 1)

    def group(gidx, carry):
        r0 = pl.multiple_of(gidx * SUBLANES, SUBLANES)
        q8 = q_ref[pl.ds(r0, SUBLANES), :]
        hc8 = jnp.zeros((SUBLANES, A_DV), F32)
        for u in range(per8):
            sq = gidx * per8 + u
            c0 = c_in_ref[0, sq, 0]
            res = _dot(q8, c0)
            hc8 = jnp.where((row8 // seq) == u, res, hc8)
            kwt_m = jnp.where((col // seq) == sq, kwt_scr[...], 0.0)
            dsc = dec_scr[pl.ds(sq * seq, 1), 0:1]
            c_out_ref[0, sq, 0] = dsc * c0 + _dot(kwt_m, v)
        hc_scr[pl.ds(r0, SUBLANES), :] = hc8
        return carry

    lax.fori_loop(0, R // SUBLANES, group, 0)

    sc = _dot_nt(q, k) * w
    num = _dot(sc, v) + w_state * hc_scr[...]
    den = jnp.sum(sc, axis=1, keepdims=True) + w_state * jnp.sum(q * nrow_ref[0], axis=1, keepdims=True)
    hid_ref[...] = num / jnp.maximum(jnp.abs(den), jnp.exp(-m_row))

    acc = kw
    shift = 1
    while shift < seq:
        acc = acc + jnp.where((lax.broadcasted_iota(jnp.int32, acc.shape, 0) % seq) >= shift,
                              pltpu.roll(acc, shift, 0), 0.0)
        shift *= 2
    n_out_ref[0] = decay * nrow_ref[0] + acc
    m_out_ref[0] = jnp.broadcast_to(m_new, (R, LANES))


def mlstm_sample(z_main_s, b_gates, gi_r, gf_r, gi_c, gf_c, m_col, n_rows, c_state, rb, seq):
    ts = z_main_s.shape[0]
    nseq_blk = rb // seq
    kcol = (A_HEADS * A_DQK) // A_DQK
    vcol = (2 * A_HEADS * A_DQK) // A_DV
    return pl.pallas_call(
        functools.partial(_mlstm_s_kernel, seq=seq),
        grid_spec=pltpu.PrefetchScalarGridSpec(
            num_scalar_prefetch=1,
            grid=(ts // rb, A_HEADS),
            in_specs=[pl.BlockSpec((rb, A_DQK), lambda i, h, *_: (i, h)),
                      pl.BlockSpec((rb, A_DQK), lambda i, h, *_: (i, kcol + h)),
                      pl.BlockSpec((rb, A_DV), lambda i, h, *_: (i, vcol + h)),
                      pl.BlockSpec((1, 1, rb), lambda i, h, *_: (h, 0, i)),
                      pl.BlockSpec((1, 1, rb), lambda i, h, *_: (h, 0, i)),
                      pl.BlockSpec((1, rb, 1), lambda i, h, *_: (h, i, 0)),
                      pl.BlockSpec((1, rb, 1), lambda i, h, *_: (h, i, 0)),
                      pl.BlockSpec((1, rb, 1), lambda i, h, *_: (h, i, 0)),
                      pl.BlockSpec((1, rb, A_DQK), lambda i, h, *_: (h, i, 0)),
                      pl.BlockSpec((1, nseq_blk, 1, A_DQK, A_DV), lambda i, h, *_: (0, i, h, 0, 0))],
            out_specs=[pl.BlockSpec((rb, A_DV), lambda i, h, *_: (i, h)),
                       pl.BlockSpec((1, nseq_blk, 1, A_DQK, A_DV), lambda i, h, *_: (0, i, h, 0, 0)),
                       pl.BlockSpec((1, rb, A_DQK), lambda i, h, *_: (h, i, 0)),
                       pl.BlockSpec((1, rb, LANES), lambda i, h, *_: (h, i, 0))],
            scratch_shapes=[pltpu.VMEM((rb, A_DV), F32),
                            pltpu.VMEM((rb, LANES), F32),
                            pltpu.VMEM((A_DQK, rb), F32)]),
        out_shape=[jax.ShapeDtypeStruct((ts, A_HEADS * A_DV), F32),
                   jax.ShapeDtypeStruct(c_state.shape, F32),
                   jax.ShapeDtypeStruct((A_HEADS, ts, A_DQK), F32),
                   jax.ShapeDtypeStruct((A_HEADS, ts, LANES), F32)],
        compiler_params=_cparams(("parallel", "parallel")),
    )(b_gates, z_main_s, z_main_s, z_main_s, gi_r, gf_r, gi_c, gf_c, m_col, n_rows, c_state)


def _attn_p_kernel(q_ref, k_ref, v_ref, o_ref, *, tq):
    s = q_ref.shape[0]
    for qt in range(s // tq):
        q = q_ref[qt * tq:(qt + 1) * tq, :]
        nk = (qt + 1) * tq
        sc = _dot_nt(q, k_ref[0:nk, :])
        row = lax.broadcasted_iota(jnp.int32, (tq, nk), 0) + qt * tq
        colk = lax.broadcasted_iota(jnp.int32, (tq, nk), 1)
        sc = jnp.where(colk <= row, sc, NEG_INF)
        m = jnp.max(sc, axis=-1, keepdims=True)
        p = jnp.exp(sc - m)
        l = jnp.sum(p, axis=-1, keepdims=True)
        o = _dot(p.astype(BF16), v_ref[0:nk, :])
        o_ref[qt * tq:(qt + 1) * tq, :] = (o / l).astype(o_ref.dtype)


def attn_prompt(q, k, v, bp, s, tq):
    return pl.pallas_call(
        functools.partial(_attn_p_kernel, tq=tq),
        grid=(bp, B_HEADS),
        in_specs=[pl.BlockSpec((s, Q_HEAD_PAD), lambda b, h: (b, h)),
                  pl.BlockSpec((s, Q_HEAD_PAD), lambda b, h: (b, h)),
                  pl.BlockSpec((s, B_V), lambda b, h: (b, h))],
        out_specs=pl.BlockSpec((s, B_V), lambda b, h: (b, h)),
        out_shape=jax.ShapeDtypeStruct((bp * s, B_HEADS * B_V), BF16),
        compiler_params=_cparams(("parallel", "parallel")),
    )(q, k, v)


def _q_absorb_kernel(q_ref, w_ref, o_ref):
    o_ref[...] = _dot(q_ref[...], w_ref[...]).astype(o_ref.dtype)


def q_absorb(q_s, w_uk_t):
    ts = q_s.shape[0]
    c = w_uk_t.shape[1]
    return pl.pallas_call(
        _q_absorb_kernel,
        grid=(B_HEADS,),
        in_specs=[pl.BlockSpec((ts, B_NOPE), lambda h: (0, 2 * h)),
                  pl.BlockSpec((B_NOPE, c), lambda h: (h, 0))],
        out_specs=pl.BlockSpec((ts, c), lambda h: (0, h)),
        out_shape=jax.ShapeDtypeStruct((ts, B_HEADS * c), BF16),
        compiler_params=_cparams(("parallel",)),
    )(q_s, w_uk_t)


def _attn_s_kernel(pt_ref, *refs, g_pages, n_pages, page, kblk, seq):
    lat_refs = refs[:g_pages]
    kr_refs = refs[g_pages:2 * g_pages]
    (wukt_ref, qabs_ref, q_ref, cnew_ref, krnew_ref, o_ref, lat_scr, kr_scr, s_scr) = refs[2 * g_pages:]
    j = pl.program_id(1)
    nq = qabs_ref.shape[1]
    qabs = qabs_ref[0]
    qrope = q_ref[0][:, LANES:Q_HEAD_PAD]
    ones8 = jnp.ones((SUBLANES, LANES), F32)

    def scores(base):
        lb = lat_scr[pl.ds(base, kblk), :]
        kb = kr_scr[pl.ds(base, kblk), :]
        kt = _dot_nt(wukt_ref[...], lb)
        ssq = jnp.sum((kt * kt).reshape(B_HEADS, B_NOPE, kblk), axis=1)
        kbf = kb.astype(F32)
        ssr = _dot_nt(ones8, kbf * kbf)[0:1]
        r = lax.rsqrt((ssq + ssr) * (1.0 / B_QK) + EPS)
        raw = _dot_nt(qabs, lb) + _dot_nt(qrope, kb)
        return raw * jnp.concatenate([r] * seq, axis=0)

    base0 = j * (g_pages * page)
    for g in range(g_pages):
        off = pl.multiple_of(base0 + g * page, page)
        lat_scr[pl.ds(off, page), :] = lat_refs[g][0, 0].astype(BF16)
        kr_scr[pl.ds(off, page), :] = jnp.concatenate(
            [kr_refs[g][0, 0], jnp.zeros((page, LANES - B_ROPE), F32)], axis=1).astype(BF16)
    blk0 = j * (g_pages * page // kblk)
    for sb in range(g_pages * page // kblk):
        off = pl.multiple_of(base0 + sb * kblk, kblk)
        s_scr[blk0 + sb] = scores(off)

    @pl.when(j == pl.num_programs(1) - 1)
    def _():
        past = n_pages * page
        nblk = past // kblk + 1
        pad_rows = kblk - BF16_ROWS
        lat_scr[past:past + kblk, :] = jnp.concatenate(
            [cnew_ref[0].astype(BF16), jnp.zeros((pad_rows, lat_scr.shape[1]), BF16)], axis=0)
        kr_scr[past:past + kblk, :] = jnp.concatenate(
            [krnew_ref[0].astype(BF16), jnp.zeros((pad_rows, LANES), BF16)], axis=0)
        t_of_row = lax.broadcasted_iota(jnp.int32, (nq, kblk), 0) // B_HEADS
        jn = lax.broadcasted_iota(jnp.int32, (nq, kblk), 1)
        s_scr[nblk - 1] = jnp.where(jn <= t_of_row, scores(past), NEG_INF)

        def row_max(b, m):
            return jnp.maximum(m, jnp.max(s_scr[b], axis=-1, keepdims=True))

        m = lax.fori_loop(0, nblk, row_max, jnp.full((nq, 1), NEG_INF, F32))

        def accumulate(b, carry):
            l, acc = carry
            p = jnp.exp(s_scr[b] - m)
            off = pl.multiple_of(b * kblk, kblk)
            return (l + jnp.sum(p, axis=-1, keepdims=True),
                    acc + _dot(p.astype(BF16), lat_scr[pl.ds(off, kblk), :]))

        l, acc = lax.fori_loop(0, nblk, accumulate,
                               (jnp.zeros((nq, 1), F32), jnp.zeros((nq, lat_scr.shape[1]), F32)))
        o_ref[0] = acc / l


def attn_sample(page_table, cache_lat, cache_kr, w_uk_t, q_abs, q_s, c_new16, kr_new16, g_pages, seq):
    db, n_pages = page_table.shape
    page, c = cache_lat.shape[2], cache_lat.shape[3]
    kblk = MXU_COLS
    nq = q_abs.shape[1]
    nk = n_pages * page + kblk
    lat_specs = [pl.BlockSpec((1, 1, page, c), functools.partial(
        lambda b, j, pt, g: (0, pt[b, j * g_pages + g], 0, 0), g=g)) for g in range(g_pages)]
    kr_specs = [pl.BlockSpec((1, 1, page, B_ROPE), functools.partial(
        lambda b, j, pt, g: (0, pt[b, j * g_pages + g], 0, 0), g=g)) for g in range(g_pages)]
    per_b = lambda b, j, pt: (b, 0, 0)
    return pl.pallas_call(
        functools.partial(_attn_s_kernel, g_pages=g_pages, n_pages=n_pages, page=page, kblk=kblk, seq=seq),
        grid_spec=pltpu.PrefetchScalarGridSpec(
            num_scalar_prefetch=1,
            grid=(db, n_pages // g_pages),
            in_specs=lat_specs + kr_specs + [
                pl.BlockSpec(w_uk_t.shape, lambda b, j, pt: (0, 0)),
                pl.BlockSpec((1, nq, c), per_b),
                pl.BlockSpec((1, nq, Q_HEAD_PAD), per_b),
                pl.BlockSpec((1, BF16_ROWS, c), per_b),
                pl.BlockSpec((1, BF16_ROWS, LANES), per_b)],
            out_specs=pl.BlockSpec((1, nq, c), per_b),
            scratch_shapes=[pltpu.VMEM((nk, c), BF16),
                            pltpu.VMEM((nk, LANES), BF16),
                            pltpu.VMEM((nk // kblk, nq, kblk), F32)]),
        out_shape=jax.ShapeDtypeStruct((db, nq, c), F32),
        compiler_params=_cparams(("arbitrary", "arbitrary")),
    )(page_table, *([cache_lat] * g_pages), *([cache_kr] * g_pages), w_uk_t, q_abs, q_s, c_new16, kr_new16)


def _uv_kernel(o_ref, w_ref, h_ref):
    h_ref[...] = _dot(o_ref[...].astype(BF16), w_ref[...]).astype(h_ref.dtype)


def uv_expand(o_lat2, w_uv):
    ts = o_lat2.shape[0]
    c = w_uv.shape[0]
    return pl.pallas_call(
        _uv_kernel,
        grid=(B_HEADS,),
        in_specs=[pl.BlockSpec((ts, c), lambda h: (0, h)),
                  pl.BlockSpec((c, B_V), lambda h: (0, h))],
        out_specs=pl.BlockSpec((ts, B_V), lambda h: (0, h)),
        out_shape=jax.ShapeDtypeStruct((ts, B_HEADS * B_V), BF16),
        compiler_params=_cparams(("parallel",)),
    )(o_lat2, w_uv)


def _merge_kernel(hid_ref, oa_ref, ga_ref, gb_ref, hb_ref, g_ref, wa_ref, wb_ref, o_ref, ha_scr):
    @pl.when(pl.program_id(1) == 0)
    def _():
        for h in range(A_HEADS):
            sl = slice(h * A_DV, (h + 1) * A_DV)
            ha_scr[:, sl] = (_rms(hid_ref[:, sl], g_ref[:, sl]) * jax.nn.sigmoid(oa_ref[:, sl])).astype(BF16)

    a = _dot(ha_scr[...], wa_ref[...])
    b = _dot(hb_ref[...], wb_ref[...])
    o_ref[...] = (jax.nn.sigmoid(ga_ref[...]) * a + jax.nn.sigmoid(gb_ref[...]) * b).astype(o_ref.dtype)


def merge(hid, z_main, h_b, g_out, w_a, w_b, bm, bn):
    t, d = hid.shape
    nj = d // bn
    oa_blk = (2 * A_HEADS * A_DQK + A_HEADS * A_DV) // d
    ga_col = (2 * A_HEADS * A_DQK + 2 * A_HEADS * A_DV) // bn
    return pl.pallas_call(
        _merge_kernel,
        grid=(t // bm, nj),
        in_specs=[pl.BlockSpec((bm, d), lambda i, j: (i, 0)),
                  pl.BlockSpec((bm, d), lambda i, j: (i, oa_blk)),
                  pl.BlockSpec((bm, bn), lambda i, j: (i, ga_col + j)),
                  pl.BlockSpec((bm, bn), lambda i, j: (i, ga_col + nj + j)),
                  pl.BlockSpec((bm, d), lambda i, j: (i, 0)),
                  pl.BlockSpec((1, d), lambda i, j: (0, 0)),
                  pl.BlockSpec((d, bn), lambda i, j: (0, j)),
                  pl.BlockSpec((d, bn), lambda i, j: (0, j))],
        out_specs=pl.BlockSpec((bm, bn), lambda i, j: (i, j)),
        out_shape=jax.ShapeDtypeStruct((t, d), BF16),
        scratch_shapes=[pltpu.VMEM((bm, d), BF16)],
        compiler_params=_cparams(("parallel", "arbitrary")),
    )(hid, z_main, z_main, z_main, h_b, g_out, w_a, w_b)


def _resid_mm_kernel(x_ref, a_ref, w_ref, o_ref):
    o_ref[...] = x_ref[...] + _dot(a_ref[...], w_ref[...])


def resid_matmul(x, a, w, bm, bn):
    t, d = x.shape
    k = a.shape[1]
    return pl.pallas_call(
        _resid_mm_kernel,
        grid=(t // bm, d // bn),
        in_specs=[pl.BlockSpec((bm, bn), lambda i, j: (i, j)),
                  pl.BlockSpec((bm, k), lambda i, j: (i, 0)),
                  pl.BlockSpec((k, bn), lambda i, j: (0, j))],
        out_specs=pl.BlockSpec((bm, bn), lambda i, j: (i, j)),
        out_shape=jax.ShapeDtypeStruct((t, d), F32),
        compiler_params=_cparams(("parallel", "parallel")),
    )(x, a, w)


def _ffn_kernel(*refs, seq_len, halo, has_prev, tail):
    it = iter(refs)
    x_ref = next(it)
    xh_ref = next(it) if halo else None
    g_ref, wg_ref, wv_ref, cg_ref, cv_ref, bg_ref, bv_ref, wd_ref = (next(it) for _ in range(8))
    prev = [next(it) for _ in range(4)] if has_prev else None
    y_ref, ug_ref, uv_ref, xn_scr, acc_scr = (next(it) for _ in range(5))
    i = pl.program_id(0)
    j = pl.program_id(1)
    bm = x_ref.shape[0]
    hp = SUBLANES

    @pl.when(j == 0)
    def _():
        xn_scr[hp:, :] = _rms(x_ref[...], g_ref[...]).astype(BF16)
        if halo:
            xn_scr[0:hp, :] = _rms(xh_ref[...], g_ref[...]).astype(BF16)
        else:
            xn_scr[0:hp, :] = jnp.zeros((hp, xn_scr.shape[1]), BF16)
        acc_scr[...] = jnp.zeros_like(acc_scr)

    xall = xn_scr[...]
    pos = (lax.broadcasted_iota(jnp.int32, (bm, 1), 0) + i * bm) % seq_len

    def conv(u, wc_ref, bc_ref, pe0_ref, pe1_ref):
        e2 = u[hp:, :]
        e1 = jnp.where(pos >= 1, u[hp - 1:hp - 1 + bm, :], 0.0)
        e0 = jnp.where(pos >= 2, u[hp - 2:hp - 2 + bm, :], 0.0)
        if has_prev:
            e1 = e1 + pe1_ref[...]
            e0 = e0 + pe0_ref[...]
        return bc_ref[...] + ((e0 * wc_ref[0:1, :] + e1 * wc_ref[1:2, :]) + e2 * wc_ref[2:3, :])

    ug = _dot(xall, wg_ref[...])
    uv = _dot(xall, wv_ref[...])
    ug_ref[0] = ug[hp + bm - tail:, :]
    uv_ref[0] = uv[hp + bm - tail:, :]
    cg = conv(ug, cg_ref, bg_ref, prev[0] if has_prev else None, prev[2] if has_prev else None)
    cv = conv(uv, cv_ref, bv_ref, prev[1] if has_prev else None, prev[3] if has_prev else None)
    act = (cg * jax.nn.sigmoid(cg) * cv).astype(BF16)
    acc_scr[...] += _dot(act, wd_ref[...])

    @pl.when(j == pl.num_programs(1) - 1)
    def _():
        y_ref[...] = x_ref[...] + acc_scr[...]


def conv_ffn(x1, rows, g, w_up, w_conv, b_conv, w_down, bm, bf, seq_len, halo, prev=None):
    d = x1.shape[1]
    f = w_down.shape[0]
    nj = f // bf
    nb = rows // bm
    tail = bm if prev is not None else SUBLANES
    hb = bm // SUBLANES
    in_specs = [pl.BlockSpec((bm, d), lambda i, j: (i, 0))]
    args = [x1]
    if halo:
        in_specs.append(pl.BlockSpec((SUBLANES, d), lambda i, j: (jnp.maximum(i * hb - 1, 0), 0)))
        args.append(x1)
    in_specs += [pl.BlockSpec((1, d), lambda i, j: (0, 0)),
                 pl.BlockSpec((d, bf), lambda i, j: (0, j)),
                 pl.BlockSpec((d, bf), lambda i, j: (0, nj + j)),
                 pl.BlockSpec((CONV_W, bf), lambda i, j: (0, j)),
                 pl.BlockSpec((CONV_W, bf), lambda i, j: (0, nj + j)),
                 pl.BlockSpec((1, bf), lambda i, j: (0, j)),
                 pl.BlockSpec((1, bf), lambda i, j: (0, nj + j)),
                 pl.BlockSpec((bf, d), lambda i, j: (j, 0))]
    args += [g, w_up, w_up, w_conv, w_conv, b_conv, b_conv, w_down]
    if prev is not None:
        pe0, pe1 = prev
        in_specs += [pl.BlockSpec((bm, bf), lambda i, j: (i, j)),
                     pl.BlockSpec((bm, bf), lambda i, j: (i, nj + j)),
                     pl.BlockSpec((bm, bf), lambda i, j: (i, j)),
                     pl.BlockSpec((bm, bf), lambda i, j: (i, nj + j))]
        args += [pe0, pe0, pe1, pe1]
    return pl.pallas_call(
        functools.partial(_ffn_kernel, seq_len=seq_len, halo=halo, has_prev=prev is not None, tail=tail),
        grid=(nb, nj),
        in_specs=in_specs,
        out_specs=[pl.BlockSpec((bm, d), lambda i, j: (i, 0)),
                   pl.BlockSpec((1, tail, bf), lambda i, j: (i, 0, j)),
                   pl.BlockSpec((1, tail, bf), lambda i, j: (i, 0, j))],
        out_shape=[jax.ShapeDtypeStruct((rows, d), F32),
                   jax.ShapeDtypeStruct((nb, tail, f), F32),
                   jax.ShapeDtypeStruct((nb, tail, f), F32)],
        scratch_shapes=[pltpu.VMEM((bm + SUBLANES, d), BF16),
                        pltpu.VMEM((bm, d), F32)],
        compiler_params=_cparams(("parallel", "arbitrary")),
    )(*args)


def _pick(n, prefs):
    for p in prefs:
        if n % p == 0:
            return p
    return n


def _layer(x_prompt, x_sample, cache_lat, cache_kr, c_state, n_state, m_state, conv_state, page_table,
           g_attn_norm, w_in, b_gates, g_q_a, w_uq, g_qk_nope_q, g_qk_rope_q, g_kv_a, w_uk, w_uv,
           g_qk_nope_k, g_qk_rope_k, g_mlstm_out, w_branch_a, w_branch_b, w_out, g_ffn_norm,
           w_up, w_conv, b_conv, w_down):
    bp, s, d = x_prompt.shape
    db, seq, _ = x_sample.shape
    n_pages = page_table.shape[1]
    page = cache_lat.shape[2]
    past = n_pages * page
    tp, ts = bp * s, db * seq
    t = tp + ts
    c_lat = g_kv_a.shape[0]
    c_q = g_q_a.shape[0]
    f = w_down.shape[0]
    qk_w = A_HEADS * A_DQK
    v_w = A_HEADS * A_DV
    row = lambda v: v.reshape(1, -1)

    o_i = 2 * qk_w + 2 * v_w
    o_cq = o_i + 2 * A_HEADS
    o_kr = o_cq + c_q + c_lat
    o_g = o_kr + B_ROPE
    w_main = jnp.concatenate([w_in[:, :o_i], w_in[:, o_g:]], axis=1).astype(BF16)
    n_small = LANES - B_ROPE - 2 * A_HEADS
    w_lat = jnp.concatenate([w_in[:, o_cq:o_kr], w_in[:, o_kr:o_g], w_in[:, o_i:o_cq],
                             jnp.zeros((d, n_small), F32)], axis=1).astype(BF16)
    w_uq_pad = jnp.pad(w_uq, ((0, 0), (0, 0), (0, Q_HEAD_PAD - B_QK))).reshape(c_q, B_HEADS * Q_HEAD_PAD).astype(BF16)
    w_uk2 = w_uk.reshape(c_lat, B_HEADS * B_NOPE).astype(BF16)
    w_uk_t = jnp.transpose(w_uk.reshape(c_lat, B_HEADS * B_NOPE)).astype(BF16)
    w_uv2 = w_uv.reshape(c_lat, B_HEADS * B_V).astype(BF16)
    gq = jnp.concatenate([g_qk_nope_q * g_qk_nope_k, g_qk_rope_q * g_qk_rope_k, g_qk_rope_q * g_qk_rope_k,
                          jnp.zeros((Q_HEAD_PAD - B_QK,), F32)]) * ATTN_SCALE
    gq = jnp.tile(gq, B_HEADS).reshape(1, -1)

    half = B_ROPE // 2
    pos = jnp.concatenate([jnp.tile(jnp.arange(s, dtype=jnp.int32), bp),
                           jnp.tile(past + jnp.arange(seq, dtype=jnp.int32), db)]).astype(F32)
    freqs = ROPE_THETA ** (-jnp.arange(half, dtype=F32) / half)
    ang = pos[:, None] * freqs
    cos, sin = jnp.cos(ang), jnp.sin(ang)
    zh = jnp.zeros((t, half), F32)
    zr = jnp.zeros((t, LANES - B_ROPE), F32)
    cos128 = jnp.concatenate([cos, cos, zr], axis=1)
    sin_a = jnp.concatenate([zh, sin, zr], axis=1)
    sin_b = jnp.concatenate([-sin, zh, zr], axis=1)

    x = jnp.concatenate([x_prompt.reshape(tp, d), x_sample.reshape(ts, d)], axis=0)
    bm = _pick(t, (512, 256, 128, 64, 32, 16, 8))
    z_main = norm_matmul(x, row(g_attn_norm), w_main, bm, _pick(w_main.shape[1], (1024, 512, 256, 128)))
    z_lat = norm_matmul(x, row(g_attn_norm), w_lat, bm, w_lat.shape[1])

    bq = _pick(t, (256, 128, 64, 32, 16, 8))
    q = q_proj(z_lat, row(g_q_a), w_uq_pad, cos128, sin_a, sin_b, gq, bq)
    c_kv, kr128, k, v = kv_proj(z_lat, row(g_kv_a), w_uk2, w_uv2, cos128, sin_a, sin_b, bq)

    small = z_lat[:, c_q + c_lat:]
    gates = small[:, B_ROPE:B_ROPE + 2 * A_HEADS]
    g_t = jnp.transpose(gates)
    gi_r, gf_r = g_t[:A_HEADS, None, :], g_t[A_HEADS:, None, :]
    gi_c, gf_c = g_t[:A_HEADS, :, None], g_t[A_HEADS:, :, None]
    chunk = _pick(s, (256, 128, 64, 32, 16, 8))
    hid_p, c_p, n_p, m_p = mlstm_prompt(z_main, b_gates, gi_r[:, :, :tp], gf_r[:, :, :tp],
                                        gi_c[:, :tp], gf_c[:, :tp], bp, s, chunk)
    rb = _pick(ts, (128, 64, 32, 16, 8))
    m_col = jnp.repeat(jnp.transpose(m_state), seq, axis=1)[:, :, None]
    n_rows = jnp.repeat(jnp.transpose(n_state, (1, 0, 2)), seq, axis=1)
    hid_s, c_s, n_s_rows, m_s_rows = mlstm_sample(
        z_main[tp:], b_gates, gi_r[:, :, tp:], gf_r[:, :, tp:], gi_c[:, tp:], gf_c[:, tp:],
        m_col, n_rows, c_state[None], rb, seq)
    n_s = jnp.transpose(n_s_rows[:, seq - 1::seq, :], (1, 0, 2))
    m_s = jnp.transpose(m_s_rows[:, seq - 1::seq, 0])

    h_b_p = attn_prompt(q, k, v, bp, s, _pick(s, (512, 256, 128)))
    q_s = q[tp:]
    q_abs = q_absorb(q_s, w_uk_t).reshape(db, seq * B_HEADS, c_lat)
    pad16 = lambda a: jnp.pad(a.reshape(db, seq, -1), ((0, 0), (0, BF16_ROWS - seq), (0, 0)))
    g_pages = _pick(n_pages, (16, 8, 4, 2))
    o_lat = attn_sample(page_table, cache_lat, cache_kr, w_uk_t, q_abs,
                        q_s.reshape(db, seq * B_HEADS, Q_HEAD_PAD),
                        pad16(c_kv[tp:]), pad16(kr128[tp:]), g_pages, seq)
    h_b_s = uv_expand(o_lat.reshape(ts, B_HEADS * c_lat), w_uv2)

    hid = jnp.concatenate([hid_p, hid_s], axis=0)
    h_b = jnp.concatenate([h_b_p, h_b_s], axis=0)
    bn = _pick(d, (1024, 512, 256, 128))
    merged = merge(hid, z_main, h_b, row(g_mlstm_out), w_branch_a.astype(BF16), w_branch_b.astype(BF16), bm, bn)
    x1 = resid_matmul(x, merged, w_out.astype(BF16), bm, bn)

    w_up_b, w_down_b = w_up.astype(BF16), w_down.astype(BF16)
    bff = _pick(f, (512, 256, 128))
    bm_p = _pick(s, (512, 256, 128, 64, 32, 16, 8))
    y_p, tg_p, tv_p = conv_ffn(x1, tp, row(g_ffn_norm), w_up_b, w_conv, row(b_conv), w_down_b,
                               bm_p, bff, s, halo=True)
    prev = conv_state
    pe0 = jnp.concatenate([prev, jnp.zeros((db, seq - (CONV_W - 1), 2 * f), F32)], axis=1).reshape(ts, 2 * f)
    pe1 = jnp.concatenate([prev[:, 1:], jnp.zeros((db, seq - 1, 2 * f), F32)], axis=1).reshape(ts, 2 * f)
    bm_s = _pick(ts, (512, 256, 128, 64, 32, 16, 8))
    y_s, tg_s, tv_s = conv_ffn(x1[tp:], ts, row(g_ffn_norm), w_up_b, w_conv, row(b_conv), w_down_b,
                               bm_s, bff, seq, halo=False, prev=(pe0, pe1))

    nb_seq = s // bm_p
    tail_p = jnp.concatenate([tg_p, tv_p], axis=-1).reshape(bp, nb_seq, SUBLANES, 2 * f)
    conv_p = tail_p[:, nb_seq - 1, SUBLANES - (CONV_W - 1):, :]
    u_s = jnp.concatenate([tg_s, tv_s], axis=-1).reshape(db, seq, 2 * f)
    conv_s = u_s[:, seq - (CONV_W - 1):, :]

    new_p = (c_kv[:tp].reshape(bp, s, c_lat), kr128[:tp, :B_ROPE].reshape(bp, s, B_ROPE),
             c_p, n_p[:, :, 0, :], m_p[:, :, 0, 0], conv_p)
    new_s = (c_kv[tp:].reshape(db, seq, c_lat), kr128[tp:, :B_ROPE].reshape(db, seq, B_ROPE),
             c_s[0], n_s, m_s, conv_s)
    return y_p.reshape(bp, s, d), y_s.reshape(db, seq, d), new_p, new_s


def kernel(x_prompt, x_sample, cache_kv_latent, cache_k_rope, state_mlstm_C, state_mlstm_n, state_mlstm_m, state_conv, page_table, g_attn_norm, w_in, b_gates, g_q_a, w_uq, g_qk_nope_q, g_qk_rope_q, g_kv_a, w_uk, w_uv, g_qk_nope_k, g_qk_rope_k, g_mlstm_out, w_branch_a, w_branch_b, w_out, g_ffn_norm, w_up, w_conv, b_conv, w_down):
    depth = w_in.shape[0]
    assert depth == 1, "single-layer trunk"
    l = 0
    y_p, y_s, new_p, new_s = _layer(
        x_prompt, x_sample, cache_kv_latent[l:l + 1], cache_k_rope[l:l + 1], state_mlstm_C[l], state_mlstm_n[l],
        state_mlstm_m[l], state_conv[l], page_table, g_attn_norm[l], w_in[l], b_gates[l], g_q_a[l], w_uq[l],
        g_qk_nope_q[l], g_qk_rope_q[l], g_kv_a[l], w_uk[l], w_uv[l], g_qk_nope_k[l], g_qk_rope_k[l],
        g_mlstm_out[l], w_branch_a[l], w_branch_b[l], w_out[l], g_ffn_norm[l], w_up[l], w_conv[l], b_conv[l],
        w_down[l])
    dts = (cache_kv_latent.dtype, cache_k_rope.dtype, state_mlstm_C.dtype, state_mlstm_n.dtype,
           state_mlstm_m.dtype, state_conv.dtype)
    st_p = tuple(a[None].astype(dt) for a, dt in zip(new_p, dts))
    st_s = tuple(a[None].astype(dt) for a, dt in zip(new_s, dts))
    return (y_p, y_s) + st_p + st_s
```

```python
import functools
import math

import jax
import jax.numpy as jnp
from jax import lax
from jax.experimental import pallas as pl
from jax.experimental.pallas import tpu as pltpu

F32 = jnp.float32
BF16 = jnp.bfloat16

A_HEADS = 8
A_DQK = 128
A_DV = 256
K_SCALE = A_DQK ** -0.5
GATE_CAP = 15.0
B_HEADS = 16
B_NOPE = 128
B_ROPE = 64
B_QK = B_NOPE + B_ROPE
B_V = 128
Q_HEAD_PAD = 256
ROPE_THETA = 10000.0
ATTN_SCALE = B_QK ** -0.5
CONV_W = 3
EPS = 1e-6
NEG_INF = float("-inf")

LANES = 128
SUBLANES = 8
BF16_ROWS = 16
MXU_COLS = 256
VMEM_LIMIT = 56 * 1024 * 1024


def _cparams(sem):
    return pltpu.CompilerParams(dimension_semantics=sem, vmem_limit_bytes=VMEM_LIMIT)


def _rms(x, g):
    r = lax.rsqrt(jnp.mean(x * x, axis=-1, keepdims=True) + EPS)
    return x * r * g


def _dot(a, b):
    return jnp.dot(a, b, preferred_element_type=F32)


def _dot_nt(a, b):
    return lax.dot_general(a, b, (((1,), (1,)), ((), ())), preferred_element_type=F32)


def _rope128(x, cos, sin_a, sin_b):
    return x * cos + pltpu.roll(x, 32, 1) * sin_a + pltpu.roll(x, 96, 1) * sin_b


def _norm_mm_kernel(x_ref, g_ref, w_ref, o_ref, xn_ref):
    @pl.when(pl.program_id(1) == 0)
    def _():
        xn_ref[...] = _rms(x_ref[...], g_ref[...]).astype(BF16)

    o_ref[...] = _dot(xn_ref[...], w_ref[...]).astype(o_ref.dtype)


def norm_matmul(x, g, w, bm, bn, out_dtype=F32):
    t, d = x.shape
    n = w.shape[1]
    return pl.pallas_call(
        _norm_mm_kernel,
        grid=(t // bm, n // bn),
        in_specs=[pl.BlockSpec((bm, d), lambda i, j: (i, 0)),
                  pl.BlockSpec((1, d), lambda i, j: (0, 0)),
                  pl.BlockSpec((d, bn), lambda i, j: (0, j))],
        out_specs=pl.BlockSpec((bm, bn), lambda i, j: (i, j)),
        out_shape=jax.ShapeDtypeStruct((t, n), out_dtype),
        scratch_shapes=[pltpu.VMEM((bm, d), BF16)],
        compiler_params=_cparams(("parallel", "arbitrary")),
    )(x, g, w)


def _q_kernel(cq_ref, g_ref, w_ref, cos_ref, sa_ref, sb_ref, gain_ref, o_ref):
    cqn = _rms(cq_ref[...], g_ref[...]).astype(BF16)
    q = _dot(cqn, w_ref[...])
    cos, sa, sb = cos_ref[...], sa_ref[...], sb_ref[...]
    for h in range(B_HEADS):
        lo = h * Q_HEAD_PAD
        nope = q[:, lo:lo + LANES]
        rp = _rope128(q[:, lo + LANES:lo + Q_HEAD_PAD], cos, sa, sb)
        ssq = jnp.sum(nope * nope, axis=-1, keepdims=True) + jnp.sum(rp * rp, axis=-1, keepdims=True)
        r = lax.rsqrt(ssq * (1.0 / B_QK) + EPS)
        o_ref[:, lo:lo + LANES] = (nope * r * gain_ref[:, lo:lo + LANES]).astype(o_ref.dtype)
        o_ref[:, lo + LANES:lo + Q_HEAD_PAD] = (rp * r * gain_ref[:, lo + LANES:lo + Q_HEAD_PAD]).astype(o_ref.dtype)


def q_proj(z_lat, g_q_a, w_uq_pad, cos, sa, sb, gain, bm):
    t = z_lat.shape[0]
    c = g_q_a.shape[1]
    n = w_uq_pad.shape[1]
    return pl.pallas_call(
        _q_kernel,
        grid=(t // bm,),
        in_specs=[pl.BlockSpec((bm, c), lambda i: (i, 0)),
                  pl.BlockSpec((1, c), lambda i: (0, 0)),
                  pl.BlockSpec((c, n), lambda i: (0, 0)),
                  pl.BlockSpec((bm, LANES), lambda i: (i, 0)),
                  pl.BlockSpec((bm, LANES), lambda i: (i, 0)),
                  pl.BlockSpec((bm, LANES), lambda i: (i, 0)),
                  pl.BlockSpec((1, n), lambda i: (0, 0))],
        out_specs=pl.BlockSpec((bm, n), lambda i: (i, 0)),
        out_shape=jax.ShapeDtypeStruct((t, n), BF16),
        compiler_params=_cparams(("parallel",)),
    )(z_lat, g_q_a, w_uq_pad, cos, sa, sb, gain)


def _kv_kernel(ckv_ref, sm_ref, g_ref, wuk_ref, wuv_ref, cos_ref, sa_ref, sb_ref,
               c_ref, kr_ref, k_ref, v_ref):
    c = _rms(ckv_ref[...], g_ref[...])
    c_ref[...] = c
    cb = c.astype(BF16)
    kn = _dot(cb, wuk_ref[...])
    v_ref[...] = _dot(cb, wuv_ref[...]).astype(v_ref.dtype)
    kr = _rope128(sm_ref[...], cos_ref[...], sa_ref[...], sb_ref[...])
    kr_ref[...] = kr
    ssr = jnp.sum(kr * kr, axis=-1, keepdims=True)
    for h in range(B_HEADS):
        nope = kn[:, h * B_NOPE:(h + 1) * B_NOPE]
        r = lax.rsqrt((jnp.sum(nope * nope, axis=-1, keepdims=True) + ssr) * (1.0 / B_QK) + EPS)
        lo = h * Q_HEAD_PAD
        k_ref[:, lo:lo + LANES] = (nope * r).astype(k_ref.dtype)
        k_ref[:, lo + LANES:lo + Q_HEAD_PAD] = (kr * r).astype(k_ref.dtype)


def kv_proj(z_lat, g_kv_a, w_uk, w_uv, cos, sa, sb, bm):
    t = z_lat.shape[0]
    c = g_kv_a.shape[1]
    small_blk = (2 * c) // LANES
    row = lambda i: (i, 0)
    const = lambda i: (0, 0)
    return pl.pallas_call(
        _kv_kernel,
        grid=(t // bm,),
        in_specs=[pl.BlockSpec((bm, c), lambda i: (i, 1)),
                  pl.BlockSpec((bm, LANES), lambda i: (i, small_blk)),
                  pl.BlockSpec((1, c), const),
                  pl.BlockSpec(w_uk.shape, const),
                  pl.BlockSpec(w_uv.shape, const),
                  pl.BlockSpec((bm, LANES), row),
                  pl.BlockSpec((bm, LANES), row),
                  pl.BlockSpec((bm, LANES), row)],
        out_specs=[pl.BlockSpec((bm, c), row),
                   pl.BlockSpec((bm, LANES), row),
                   pl.BlockSpec((bm, B_HEADS * Q_HEAD_PAD), row),
                   pl.BlockSpec((bm, B_HEADS * B_V), row)],
        out_shape=[jax.ShapeDtypeStruct((t, c), F32),
                   jax.ShapeDtypeStruct((t, LANES), F32),
                   jax.ShapeDtypeStruct((t, B_HEADS * Q_HEAD_PAD), BF16),
                   jax.ShapeDtypeStruct((t, B_HEADS * B_V), BF16)],
        compiler_params=_cparams(("parallel",)),
    )(z_lat, z_lat, g_kv_a, w_uk, w_uv, cos, sa, sb)


def _log_gates(gi, gf, bi, bf):
    cap = lambda x: GATE_CAP * jnp.tanh(x * (1.0 / GATE_CAP))
    li = cap(gi + bi)
    y = cap(gf + bf)
    lf = jnp.minimum(y, 0.0) - jnp.log(1.0 + jnp.exp(-jnp.abs(y)))
    return li, lf


def _mlstm_p_kernel(b_ref, q_ref, k_ref, v_ref, gir_ref, gfr_ref, gic_ref, gfc_ref,
                    hid_ref, c_out_ref, n_out_ref, m_out_ref, c_scr, n_scr, m_scr):
    h = pl.program_id(1)
    ci = pl.program_id(2)
    L = q_ref.shape[0]

    @pl.when(ci == 0)
    def _():
        c_scr[...] = jnp.zeros_like(c_scr)
        n_scr[...] = jnp.zeros_like(n_scr)
        m_scr[...] = jnp.zeros_like(m_scr)

    bi = b_ref[h]
    bf = b_ref[A_HEADS + h]
    qb = q_ref[...]
    kb = k_ref[...]
    vb = v_ref[...]
    q = qb.astype(F32)
    li_r, lf_r = _log_gates(gir_ref[0], gfr_ref[0], bi, bf)
    li_c, lf_c = _log_gates(gic_ref[0], gfc_ref[0], bi, bf)

    t_idx = lax.broadcasted_iota(jnp.int32, (L, L), 0)
    s_idx = lax.broadcasted_iota(jnp.int32, (L, L), 1)
    causal = s_idx <= t_idx
    bcum_c = jnp.sum(jnp.where(causal, lf_r, 0.0), axis=1, keepdims=True)
    bcum_r = jnp.sum(jnp.where(t_idx <= s_idx, lf_c, 0.0), axis=0, keepdims=True)
    log_w = jnp.where(causal, bcum_c - bcum_r + li_r, NEG_INF)
    m_prev = m_scr[:, 0:1]
    log_state = bcum_c + m_prev
    m_row = jnp.maximum(log_state, jnp.max(log_w, axis=1, keepdims=True))
    w = jnp.exp(log_w - m_row) * K_SCALE
    w_state = jnp.exp(log_state - m_row)
    sc = _dot_nt(qb, kb) * w
    num = _dot(sc.astype(BF16), vb) + w_state * _dot(qb, c_scr[...].astype(BF16))
    den = jnp.sum(sc, axis=1, keepdims=True) + w_state * jnp.sum(q * n_scr[...], axis=1, keepdims=True)
    hid_ref[...] = num / jnp.maximum(jnp.abs(den), jnp.exp(-m_row))

    b_last = bcum_c[L - 1:L, :]
    log_k = b_last - bcum_c + li_c
    m_new = jnp.maximum(b_last + m_prev, jnp.max(log_k, axis=0, keepdims=True))
    wk = jnp.exp(log_k - m_new) * K_SCALE
    decay = jnp.exp(b_last + m_prev - m_new)
    kw = kb.astype(F32) * wk
    kwt = jnp.transpose(kw).astype(BF16)
    c_new = decay * c_scr[...] + _dot(kwt, vb)
    n_new = decay * n_scr[...] + jnp.sum(kw, axis=0, keepdims=True)
    c_scr[...] = c_new
    n_scr[...] = n_new
    m_scr[...] = jnp.broadcast_to(m_new, m_scr.shape)

    @pl.when(ci == pl.num_programs(2) - 1)
    def _():
        c_out_ref[0, 0] = c_new
        n_out_ref[0, 0] = n_new
        m_out_ref[0, 0] = jnp.broadcast_to(m_new, (1, LANES))


def mlstm_prompt(z_main, b_gates, gi_r, gf_r, gi_c, gf_c, bp, s, chunk):
    nc = s // chunk
    kcol = (A_HEADS * A_DQK) // A_DQK
    vcol = (2 * A_HEADS * A_DQK) // A_DV
    tok = lambda b, h, c, *_: (b * nc + c)
    return pl.pallas_call(
        _mlstm_p_kernel,
        grid_spec=pltpu.PrefetchScalarGridSpec(
            num_scalar_prefetch=1,
            grid=(bp, A_HEADS, nc),
            in_specs=[pl.BlockSpec((chunk, A_DQK), lambda b, h, c, *_: (b * nc + c, h)),
                      pl.BlockSpec((chunk, A_DQK), lambda b, h, c, *_: (b * nc + c, kcol + h)),
                      pl.BlockSpec((chunk, A_DV), lambda b, h, c, *_: (b * nc + c, vcol + h)),
                      pl.BlockSpec((1, 1, chunk), lambda b, h, c, *_: (h, 0, b * nc + c)),
                      pl.BlockSpec((1, 1, chunk), lambda b, h, c, *_: (h, 0, b * nc + c)),
                      pl.BlockSpec((1, chunk, 1), lambda b, h, c, *_: (h, b * nc + c, 0)),
                      pl.BlockSpec((1, chunk, 1), lambda b, h, c, *_: (h, b * nc + c, 0))],
            out_specs=[pl.BlockSpec((chunk, A_DV), lambda b, h, c, *_: (b * nc + c, h)),
                       pl.BlockSpec((1, 1, A_DQK, A_DV), lambda b, h, c, *_: (b, h, 0, 0)),
                       pl.BlockSpec((1, 1, 1, A_DQK), lambda b, h, c, *_: (b, h, 0, 0)),
                       pl.BlockSpec((1, 1, 1, LANES), lambda b, h, c, *_: (b, h, 0, 0))],
            scratch_shapes=[pltpu.VMEM((A_DQK, A_DV), F32),
                            pltpu.VMEM((1, A_DQK), F32),
                            pltpu.VMEM((1, LANES), F32)]),
        out_shape=[jax.ShapeDtypeStruct((bp * s, A_HEADS * A_DV), F32),
                   jax.ShapeDtypeStruct((bp, A_HEADS, A_DQK, A_DV), F32),
                   jax.ShapeDtypeStruct((bp, A_HEADS, 1, A_DQK), F32),
                   jax.ShapeDtypeStruct((bp, A_HEADS, 1, LANES), F32)],
        compiler_params=_cparams(("parallel", "parallel", "arbitrary")),
    )(b_gates, z_main, z_main, z_main, gi_r, gf_r, gi_c, gf_c)


def _mlstm_s_kernel(b_ref, q_ref, k_ref, v_ref, gir_ref, gfr_ref, gic_ref, gfc_ref,
                    mc_ref, nrow_ref, c_in_ref,
                    hid_ref, c_out_ref, n_out_ref, m_out_ref,
                    hc_scr, dec_scr, kwt_scr, q_scr, *, seq):
    h = pl.program_id(1)
    R = q_ref.shape[0]
    nseq = R // seq
    per8 = SUBLANES // seq
    bi = b_ref[h]
    bf = b_ref[A_HEADS + h]
    q = q_ref[...].astype(F32)
    k = k_ref[...].astype(F32)
    v = v_ref[...].astype(F32)
    q_scr[...] = q
    li_r, lf_r = _log_gates(gir_ref[0], gfr_ref[0], bi, bf)
    li_c, lf_c = _log_gates(gic_ref[0], gfc_ref[0], bi, bf)
    m_prev = mc_ref[0]

    t_idx = lax.broadcasted_iota(jnp.int32, (R, R), 0)
    s_idx = lax.broadcasted_iota(jnp.int32, (R, R), 1)
    same = (t_idx // seq) == (s_idx // seq)
    causal = same & (s_idx <= t_idx)
    bcum_c = jnp.sum(jnp.where(causal, lf_r, 0.0), axis=1, keepdims=True)
    bcum_r = jnp.sum(jnp.where(same & (t_idx <= s_idx), lf_c, 0.0), axis=0, keepdims=True)
    blast_c = jnp.sum(jnp.where(same, lf_r, 0.0), axis=1, keepdims=True)
    blast_r = jnp.sum(jnp.where(same, lf_c, 0.0), axis=0, keepdims=True)
    log_w = jnp.where(causal, bcum_c - bcum_r + li_r, NEG_INF)
    log_state = bcum_c + m_prev
    m_row = jnp.maximum(log_state, jnp.max(log_w, axis=1, keepdims=True))
    w = jnp.exp(log_w - m_row) * K_SCALE
    w_state = jnp.exp(log_state - m_row)

    log_k_c = blast_c - bcum_c + li_c
    log_k_r = blast_r - bcum_r + li_r
    seg_max = jnp.max(jnp.where(same, log_k_r, NEG_INF), axis=1, keepdims=True)
    m_new = jnp.maximum(blast_c + m_prev, seg_max)
    wk = jnp.exp(log_k_c - m_new) * K_SCALE
    decay = jnp.exp(blast_c + m_prev - m_new)
    kw = k * wk
    kwt_scr[...] = jnp.transpose(kw)
    dec_scr[...] = jnp.broadcast_to(decay, dec_scr.shape)

    row8 = lax.broadcasted_iota(jnp.int32, (SUBLANES, A_DV), 0)
    col = lax.broadcasted_iota(jnp.int32, (A_DQK, R), 1)

    def group(gidx, carry):
        r0 = pl.multiple_of(gidx * SUBLANES, SUBLANES)
        q8 = q_scr[pl.ds(r0, SUBLANES), :]
        hc8 = jnp.zeros((SUBLANES, A_DV), F32)
        for u in range(per8):
            sq = gidx * per8 + u
            c0 = c_in_ref[0, sq, 0]
            res = _dot(q8, c0)
            hc8 = jnp.where((row8 // seq) == u, res, hc8)
            kwt_m = jnp.where((col // seq) == sq, kwt_scr[...], 0.0)
            dsc = dec_scr[pl.ds(sq * seq, 1), 0:1]
            c_out_ref[0, sq, 0] = dsc * c0 + _dot(kwt_m, v)
        hc_scr[pl.ds(r0, SUBLANES), :] = hc8
        return carry

    lax.fori_loop(0, R // SUBLANES, group, 0)

    sc = _dot_nt(q, k) * w
    num = _dot(sc, v) + w_state * hc_scr[...]
    den = jnp.sum(sc, axis=1, keepdims=True) + w_state * jnp.sum(q * nrow_ref[0], axis=1, keepdims=True)
    hid_ref[...] = num / jnp.maximum(jnp.abs(den), jnp.exp(-m_row))

    acc = kw
    shift = 1
    while shift < seq:
        acc = acc + jnp.where((lax.broadcasted_iota(jnp.int32, acc.shape, 0) % seq) >= shift,
                              pltpu.roll(acc, shift, 0), 0.0)
        shift *= 2
    n_out_ref[0] = decay * nrow_ref[0] + acc
    m_out_ref[0] = jnp.broadcast_to(m_new, (R, LANES))


def mlstm_sample(z_main_s, b_gates, gi_r, gf_r, gi_c, gf_c, m_col, n_rows, c_state, rb, seq):
    ts = z_main_s.shape[0]
    nseq_blk = rb // seq
    kcol = (A_HEADS * A_DQK) // A_DQK
    vcol = (2 * A_HEADS * A_DQK) // A_DV
    return pl.pallas_call(
        functools.partial(_mlstm_s_kernel, seq=seq),
        grid_spec=pltpu.PrefetchScalarGridSpec(
            num_scalar_prefetch=1,
            grid=(ts // rb, A_HEADS),
            in_specs=[pl.BlockSpec((rb, A_DQK), lambda i, h, *_: (i, h)),
                      pl.BlockSpec((rb, A_DQK), lambda i, h, *_: (i, kcol + h)),
                      pl.BlockSpec((rb, A_DV), lambda i, h, *_: (i, vcol + h)),
                      pl.BlockSpec((1, 1, rb), lambda i, h, *_: (h, 0, i)),
                      pl.BlockSpec((1, 1, rb), lambda i, h, *_: (h, 0, i)),
                      pl.BlockSpec((1, rb, 1), lambda i, h, *_: (h, i, 0)),
                      pl.BlockSpec((1, rb, 1), lambda i, h, *_: (h, i, 0)),
                      pl.BlockSpec((1, rb, 1), lambda i, h, *_: (h, i, 0)),
                      pl.BlockSpec((1, rb, A_DQK), lambda i, h, *_: (h, i, 0)),
                      pl.BlockSpec((1, nseq_blk, 1, A_DQK, A_DV), lambda i, h, *_: (0, i, h, 0, 0))],
            out_specs=[pl.BlockSpec((rb, A_DV), lambda i, h, *_: (i, h)),
                       pl.BlockSpec((1, nseq_blk, 1, A_DQK, A_DV), lambda i, h, *_: (0, i, h, 0, 0)),
                       pl.BlockSpec((1, rb, A_DQK), lambda i, h, *_: (h, i, 0)),
                       pl.BlockSpec((1, rb, LANES), lambda i, h, *_: (h, i, 0))],
            scratch_shapes=[pltpu.VMEM((rb, A_DV), F32),
                            pltpu.VMEM((rb, LANES), F32),
                            pltpu.VMEM((A_DQK, rb), F32),
                            pltpu.VMEM((rb, A_DQK), F32)]),
        out_shape=[jax.ShapeDtypeStruct((ts, A_HEADS * A_DV), F32),
                   jax.ShapeDtypeStruct(c_state.shape, F32),
                   jax.ShapeDtypeStruct((A_HEADS, ts, A_DQK), F32),
                   jax.ShapeDtypeStruct((A_HEADS, ts, LANES), F32)],
        compiler_params=_cparams(("parallel", "parallel")),
    )(b_gates, z_main_s, z_main_s, z_main_s, gi_r, gf_r, gi_c, gf_c, m_col, n_rows, c_state)


def _attn_p_kernel(q_ref, k_ref, v_ref, o_ref, *, tq):
    s = q_ref.shape[0]
    for qt in range(s // tq):
        q = q_ref[qt * tq:(qt + 1) * tq, :]
        nk = (qt + 1) * tq
        sc = _dot_nt(q, k_ref[0:nk, :])
        row = lax.broadcasted_iota(jnp.int32, (tq, nk), 0) + qt * tq
        colk = lax.broadcasted_iota(jnp.int32, (tq, nk), 1)
        sc = jnp.where(colk <= row, sc, NEG_INF)
        m = jnp.max(sc, axis=-1, keepdims=True)
        p = jnp.exp(sc - m)
        l = jnp.sum(p, axis=-1, keepdims=True)
        o = _dot(p.astype(BF16), v_ref[0:nk, :])
        o_ref[qt * tq:(qt + 1) * tq, :] = (o / l).astype(o_ref.dtype)


def attn_prompt(q, k, v, bp, s, tq):
    return pl.pallas_call(
        functools.partial(_attn_p_kernel, tq=tq),
        grid=(bp, B_HEADS),
        in_specs=[pl.BlockSpec((s, Q_HEAD_PAD), lambda b, h: (b, h)),
                  pl.BlockSpec((s, Q_HEAD_PAD), lambda b, h: (b, h)),
                  pl.BlockSpec((s, B_V), lambda b, h: (b, h))],
        out_specs=pl.BlockSpec((s, B_V), lambda b, h: (b, h)),
        out_shape=jax.ShapeDtypeStruct((bp * s, B_HEADS * B_V), BF16),
        compiler_params=_cparams(("parallel", "parallel")),
    )(q, k, v)


def _q_absorb_kernel(q_ref, w_ref, o_ref):
    o_ref[...] = _dot(q_ref[...], w_ref[...]).astype(o_ref.dtype)


def q_absorb(q_s, w_uk_t):
    ts = q_s.shape[0]
    c = w_uk_t.shape[1]
    return pl.pallas_call(
        _q_absorb_kernel,
        grid=(B_HEADS,),
        in_specs=[pl.BlockSpec((ts, B_NOPE), lambda h: (0, 2 * h)),
                  pl.BlockSpec((B_NOPE, c), lambda h: (h, 0))],
        out_specs=pl.BlockSpec((ts, c), lambda h: (0, h)),
        out_shape=jax.ShapeDtypeStruct((ts, B_HEADS * c), BF16),
        compiler_params=_cparams(("parallel",)),
    )(q_s, w_uk_t)


def _attn_s_kernel(pt_ref, *refs, g_pages, page, kblk, seq):
    lat_refs = refs[:g_pages]
    kr_refs = refs[g_pages:2 * g_pages]
    (wukt_ref, qabs_ref, q_ref, cnew_ref, krnew_ref, o_ref,
     wcat_scr, lat_scr, m_scr, l_scr, acc_scr) = refs[2 * g_pages:]
    b = pl.program_id(0)
    j = pl.program_id(1)
    nq = qabs_ref.shape[1]
    nkey = wukt_ref.shape[0]
    c = wukt_ref.shape[1]
    per_blk = kblk // page

    @pl.when((b == 0) & (j == 0))
    def _():
        wcat_scr[0:nkey, :] = wukt_ref[...]

    @pl.when(j == 0)
    def _():
        wcat_scr[nkey:nkey + nq, :] = qabs_ref[0]
        m_scr[...] = jnp.full(m_scr.shape, NEG_INF, F32)
        l_scr[...] = jnp.zeros_like(l_scr)
        acc_scr[...] = jnp.zeros_like(acc_scr)

    qrope = q_ref[0][:, LANES:LANES + B_ROPE]

    def scores(lb, krt):
        a = _dot_nt(wcat_scr[...], lb)
        kt = a[0:nkey]
        ssq = jnp.sum((kt * kt).reshape(B_HEADS, B_NOPE, kblk), axis=1)
        ssr = jnp.sum(krt * krt, axis=0, keepdims=True)
        r = lax.rsqrt((ssq + ssr) * (1.0 / B_QK) + EPS)
        return (a[nkey:nkey + nq] + _dot(qrope, krt.astype(BF16))) * jnp.concatenate([r] * seq, axis=0)

    def fold(s, lat, carry):
        m, l, acc = carry
        m_new = jnp.maximum(m, jnp.max(s, axis=-1, keepdims=True))
        alpha = jnp.exp(m - m_new)
        p = jnp.exp(s - m_new)
        return m_new, alpha * l + jnp.sum(p, axis=-1, keepdims=True), alpha * acc + _dot(p.astype(BF16), lat)

    s_blocks = []
    for sb in range(g_pages // per_blk):
        pages = range(sb * per_blk, (sb + 1) * per_blk)
        lb = jnp.concatenate([lat_refs[g][0, 0].astype(BF16) for g in pages], axis=0)
        krt = jnp.concatenate([kr_refs[g][0, 0] for g in pages], axis=1)
        lat_scr[sb * kblk:(sb + 1) * kblk, :] = lb
        s_blocks.append(scores(lb, krt))
    carry = fold(jnp.concatenate(s_blocks, axis=1), lat_scr[...], (m_scr[:, 0:1], l_scr[:, 0:1], acc_scr[...]))
    m, l, acc = carry
    m_scr[...] = jnp.broadcast_to(m, m_scr.shape)
    l_scr[...] = jnp.broadcast_to(l, l_scr.shape)
    acc_scr[...] = acc

    @pl.when(j == pl.num_programs(1) - 1)
    def _():
        lb = jnp.concatenate([cnew_ref[0].astype(BF16), jnp.zeros((kblk - BF16_ROWS, c), BF16)], axis=0)
        krt = jnp.concatenate([krnew_ref[0], jnp.zeros((B_ROPE, kblk - LANES), F32)], axis=1)
        t_of_row = lax.broadcasted_iota(jnp.int32, (nq, kblk), 0) // B_HEADS
        jn = lax.broadcasted_iota(jnp.int32, (nq, kblk), 1)
        s = jnp.where(jn <= t_of_row, scores(lb, krt), NEG_INF)
        _, l2, acc2 = fold(s, lb, carry)
        o_ref[0] = acc2 / l2


def attn_sample(page_table, cache_lat, cache_kr_t, w_uk_t, q_abs, q_s, c_new16, kr_new_t, g_pages, seq):
    db, n_pages = page_table.shape
    page, c = cache_lat.shape[2], cache_lat.shape[3]
    kblk = MXU_COLS
    assert kblk % page == 0 and g_pages % (kblk // page) == 0 and n_pages % g_pages == 0
    nq = q_abs.shape[1]
    nkey = w_uk_t.shape[0]
    lat_specs = [pl.BlockSpec((1, 1, page, c), functools.partial(
        lambda b, j, pt, g: (0, pt[b, j * g_pages + g], 0, 0), g=g)) for g in range(g_pages)]
    kr_specs = [pl.BlockSpec((1, 1, B_ROPE, page), functools.partial(
        lambda b, j, pt, g: (0, pt[b, j * g_pages + g], 0, 0), g=g)) for g in range(g_pages)]
    per_b = lambda b, j, pt: (b, 0, 0)
    return pl.pallas_call(
        functools.partial(_attn_s_kernel, g_pages=g_pages, page=page, kblk=kblk, seq=seq),
        grid_spec=pltpu.PrefetchScalarGridSpec(
            num_scalar_prefetch=1,
            grid=(db, n_pages // g_pages),
            in_specs=lat_specs + kr_specs + [
                pl.BlockSpec(w_uk_t.shape, lambda b, j, pt: (0, 0)),
                pl.BlockSpec((1, nq, c), per_b),
                pl.BlockSpec((1, nq, Q_HEAD_PAD), per_b),
                pl.BlockSpec((1, BF16_ROWS, c), per_b),
                pl.BlockSpec((1, B_ROPE, LANES), per_b)],
            out_specs=pl.BlockSpec((1, nq, c), per_b),
            scratch_shapes=[pltpu.VMEM((nkey + nq, c), BF16),
                            pltpu.VMEM((g_pages * page, c), BF16),
                            pltpu.VMEM((nq, LANES), F32),
                            pltpu.VMEM((nq, LANES), F32),
                            pltpu.VMEM((nq, c), F32)]),
        out_shape=jax.ShapeDtypeStruct((db, nq, c), F32),
        compiler_params=_cparams(("arbitrary", "arbitrary")),
    )(page_table, *([cache_lat] * g_pages), *([cache_kr_t] * g_pages), w_uk_t, q_abs, q_s, c_new16, kr_new_t)


def _uv_kernel(o_ref, w_ref, h_ref):
    h_ref[...] = _dot(o_ref[...].astype(BF16), w_ref[...]).astype(h_ref.dtype)


def uv_expand(o_lat2, w_uv):
    ts = o_lat2.shape[0]
    c = w_uv.shape[0]
    return pl.pallas_call(
        _uv_kernel,
        grid=(B_HEADS,),
        in_specs=[pl.BlockSpec((ts, c), lambda h: (0, h)),
                  pl.BlockSpec((c, B_V), lambda h: (0, h))],
        out_specs=pl.BlockSpec((ts, B_V), lambda h: (0, h)),
        out_shape=jax.ShapeDtypeStruct((ts, B_HEADS * B_V), BF16),
        compiler_params=_cparams(("parallel",)),
    )(o_lat2, w_uv)


def _merge_kernel(hid_ref, oa_ref, ga_ref, gb_ref, hb_ref, g_ref, wa_ref, wb_ref, o_ref, ha_scr):
    @pl.when(pl.program_id(1) == 0)
    def _():
        for h in range(A_HEADS):
            sl = slice(h * A_DV, (h + 1) * A_DV)
            ha_scr[:, sl] = (_rms(hid_ref[:, sl], g_ref[:, sl]) * jax.nn.sigmoid(oa_ref[:, sl])).astype(BF16)

    a = _dot(ha_scr[...], wa_ref[...])
    b = _dot(hb_ref[...], wb_ref[...])
    o_ref[...] = (jax.nn.sigmoid(ga_ref[...]) * a + jax.nn.sigmoid(gb_ref[...]) * b).astype(o_ref.dtype)


def merge(hid, z_gate, h_b, g_out, w_a, w_b, bm, bn):
    t, d = hid.shape
    nj = d // bn
    return pl.pallas_call(
        _merge_kernel,
        grid=(t // bm, nj),
        in_specs=[pl.BlockSpec((bm, d), lambda i, j: (i, 0)),
                  pl.BlockSpec((bm, d), lambda i, j: (i, 0)),
                  pl.BlockSpec((bm, bn), lambda i, j: (i, nj + j)),
                  pl.BlockSpec((bm, bn), lambda i, j: (i, 2 * nj + j)),
                  pl.BlockSpec((bm, d), lambda i, j: (i, 0)),
                  pl.BlockSpec((1, d), lambda i, j: (0, 0)),
                  pl.BlockSpec((d, bn), lambda i, j: (0, j)),
                  pl.BlockSpec((d, bn), lambda i, j: (0, j))],
        out_specs=pl.BlockSpec((bm, bn), lambda i, j: (i, j)),
        out_shape=jax.ShapeDtypeStruct((t, d), BF16),
        scratch_shapes=[pltpu.VMEM((bm, d), BF16)],
        compiler_params=_cparams(("parallel", "arbitrary")),
    )(hid, z_gate, z_gate, z_gate, h_b, g_out, w_a, w_b)


def _resid_mm_kernel(x_ref, a_ref, w_ref, o_ref):
    o_ref[...] = x_ref[...] + _dot(a_ref[...], w_ref[...])


def resid_matmul(x, a, w, bm, bn):
    t, d = x.shape
    k = a.shape[1]
    return pl.pallas_call(
        _resid_mm_kernel,
        grid=(t // bm, d // bn),
        in_specs=[pl.BlockSpec((bm, bn), lambda i, j: (i, j)),
                  pl.BlockSpec((bm, k), lambda i, j: (i, 0)),
                  pl.BlockSpec((k, bn), lambda i, j: (0, j))],
        out_specs=pl.BlockSpec((bm, bn), lambda i, j: (i, j)),
        out_shape=jax.ShapeDtypeStruct((t, d), F32),
        compiler_params=_cparams(("parallel", "parallel")),
    )(x, a, w)


def _ffn_kernel(*refs, seq_len, halo, has_prev, tail):
    it = iter(refs)
    x_ref = next(it)
    xh_ref = next(it) if halo else None
    g_ref, wg_ref, wv_ref, cg_ref, cv_ref, bg_ref, bv_ref, wd_ref = (next(it) for _ in range(8))
    prev = [next(it) for _ in range(4)] if has_prev else None
    y_ref, ug_ref, uv_ref, xn_scr, acc_scr = (next(it) for _ in range(5))
    i = pl.program_id(0)
    j = pl.program_id(1)
    bm = x_ref.shape[0]
    hp = SUBLANES

    @pl.when(j == 0)
    def _():
        xn_scr[hp:, :] = _rms(x_ref[...], g_ref[...]).astype(BF16)
        if halo:
            xn_scr[0:hp, :] = _rms(xh_ref[...], g_ref[...]).astype(BF16)
        else:
            xn_scr[0:hp, :] = jnp.zeros((hp, xn_scr.shape[1]), BF16)
        acc_scr[...] = jnp.zeros_like(acc_scr)

    xall = xn_scr[...]
    pos = (lax.broadcasted_iota(jnp.int32, (bm, 1), 0) + i * bm) % seq_len

    def conv(u, wc_ref, bc_ref, pe0_ref, pe1_ref):
        e2 = u[hp:, :]
        e1 = jnp.where(pos >= 1, u[hp - 1:hp - 1 + bm, :], 0.0)
        e0 = jnp.where(pos >= 2, u[hp - 2:hp - 2 + bm, :], 0.0)
        if has_prev:
            e1 = e1 + pe1_ref[...]
            e0 = e0 + pe0_ref[...]
        return bc_ref[...] + ((e0 * wc_ref[0:1, :] + e1 * wc_ref[1:2, :]) + e2 * wc_ref[2:3, :])

    ug = _dot(xall, wg_ref[...])
    uv = _dot(xall, wv_ref[...])
    ug_ref[0] = ug[hp + bm - tail:, :]
    uv_ref[0] = uv[hp + bm - tail:, :]
    cg = conv(ug, cg_ref, bg_ref, prev[0] if has_prev else None, prev[2] if has_prev else None)
    cv = conv(uv, cv_ref, bv_ref, prev[1] if has_prev else None, prev[3] if has_prev else None)
    act = (cg * jax.nn.sigmoid(cg) * cv).astype(BF16)
    acc_scr[...] += _dot(act, wd_ref[...])

    @pl.when(j == pl.num_programs(1) - 1)
    def _():
        y_ref[...] = x_ref[...] + acc_scr[...]


def conv_ffn(x1, rows, g, w_up, w_conv, b_conv, w_down, bm, bf, seq_len, halo, prev=None):
    d = x1.shape[1]
    f = w_down.shape[0]
    nj = f // bf
    nb = rows // bm
    tail = bm if prev is not None else SUBLANES
    hb = bm // SUBLANES
    in_specs = [pl.BlockSpec((bm, d), lambda i, j: (i, 0))]
    args = [x1]
    if halo:
        in_specs.append(pl.BlockSpec((SUBLANES, d), lambda i, j: (jnp.maximum(i * hb - 1, 0), 0)))
        args.append(x1)
    in_specs += [pl.BlockSpec((1, d), lambda i, j: (0, 0)),
                 pl.BlockSpec((d, bf), lambda i, j: (0, j)),
                 pl.BlockSpec((d, bf), lambda i, j: (0, nj + j)),
                 pl.BlockSpec((CONV_W, bf), lambda i, j: (0, j)),
                 pl.BlockSpec((CONV_W, bf), lambda i, j: (0, nj + j)),
                 pl.BlockSpec((1, bf), lambda i, j: (0, j)),
                 pl.BlockSpec((1, bf), lambda i, j: (0, nj + j)),
                 pl.BlockSpec((bf, d), lambda i, j: (j, 0))]
    args += [g, w_up, w_up, w_conv, w_conv, b_conv, b_conv, w_down]
    if prev is not None:
        pe0, pe1 = prev
        in_specs += [pl.BlockSpec((bm, bf), lambda i, j: (i, j)),
                     pl.BlockSpec((bm, bf), lambda i, j: (i, nj + j)),
                     pl.BlockSpec((bm, bf), lambda i, j: (i, j)),
                     pl.BlockSpec((bm, bf), lambda i, j: (i, nj + j))]
        args += [pe0, pe0, pe1, pe1]
    return pl.pallas_call(
        functools.partial(_ffn_kernel, seq_len=seq_len, halo=halo, has_prev=prev is not None, tail=tail),
        grid=(nb, nj),
        in_specs=in_specs,
        out_specs=[pl.BlockSpec((bm, d), lambda i, j: (i, 0)),
                   pl.BlockSpec((1, tail, bf), lambda i, j: (i, 0, j)),
                   pl.BlockSpec((1, tail, bf), lambda i, j: (i, 0, j))],
        out_shape=[jax.ShapeDtypeStruct((rows, d), F32),
                   jax.ShapeDtypeStruct((nb, tail, f), F32),
                   jax.ShapeDtypeStruct((nb, tail, f), F32)],
        scratch_shapes=[pltpu.VMEM((bm + SUBLANES, d), BF16),
                        pltpu.VMEM((bm, d), F32)],
        compiler_params=_cparams(("parallel", "arbitrary")),
    )(*args)


def _pick(n, prefs):
    for p in prefs:
        if n % p == 0:
            return p
    return n


def _layer(x_prompt, x_sample, cache_lat, cache_kr, c_state, n_state, m_state, conv_state, page_table,
           g_attn_norm, w_in, b_gates, g_q_a, w_uq, g_qk_nope_q, g_qk_rope_q, g_kv_a, w_uk, w_uv,
           g_qk_nope_k, g_qk_rope_k, g_mlstm_out, w_branch_a, w_branch_b, w_out, g_ffn_norm,
           w_up, w_conv, b_conv, w_down):
    bp, s, d = x_prompt.shape
    db, seq, _ = x_sample.shape
    n_pages = page_table.shape[1]
    page = cache_lat.shape[2]
    past = n_pages * page
    tp, ts = bp * s, db * seq
    t = tp + ts
    c_lat = g_kv_a.shape[0]
    c_q = g_q_a.shape[0]
    f = w_down.shape[0]
    qk_w = A_HEADS * A_DQK
    v_w = A_HEADS * A_DV
    row = lambda v: v.reshape(1, -1)

    o_i = 2 * qk_w + 2 * v_w
    o_cq = o_i + 2 * A_HEADS
    o_kr = o_cq + c_q + c_lat
    o_g = o_kr + B_ROPE
    o_v = 2 * qk_w + v_w
    w_qkv = w_in[:, :o_v].astype(BF16)
    w_gate = jnp.concatenate([w_in[:, o_v:o_i], w_in[:, o_g:]], axis=1).astype(BF16)
    n_small = LANES - B_ROPE - 2 * A_HEADS
    w_lat = jnp.concatenate([w_in[:, o_cq:o_kr], w_in[:, o_kr:o_g], w_in[:, o_i:o_cq],
                             jnp.zeros((d, n_small), F32)], axis=1).astype(BF16)
    w_uq_pad = jnp.pad(w_uq, ((0, 0), (0, 0), (0, Q_HEAD_PAD - B_QK))).reshape(c_q, B_HEADS * Q_HEAD_PAD).astype(BF16)
    w_uk2 = w_uk.reshape(c_lat, B_HEADS * B_NOPE).astype(BF16)
    w_uk_t = jnp.transpose(w_uk.reshape(c_lat, B_HEADS * B_NOPE)).astype(BF16)
    w_uv2 = w_uv.reshape(c_lat, B_HEADS * B_V).astype(BF16)
    gq = jnp.concatenate([g_qk_nope_q * g_qk_nope_k, g_qk_rope_q * g_qk_rope_k, g_qk_rope_q * g_qk_rope_k,
                          jnp.zeros((Q_HEAD_PAD - B_QK,), F32)]) * ATTN_SCALE
    gq = jnp.tile(gq, B_HEADS).reshape(1, -1)

    w_a_b, w_b_b, w_out_b = w_branch_a.astype(BF16), w_branch_b.astype(BF16), w_out.astype(BF16)
    w_up_b, w_down_b = w_up.astype(BF16), w_down.astype(BF16)
    half = B_ROPE // 2
    freqs = ROPE_THETA ** (-jnp.arange(half, dtype=F32) / half)

    def token_stage(x, pos, bm):
        rows = x.shape[0]
        ang = pos.astype(F32)[:, None] * freqs
        cos, sin = jnp.cos(ang), jnp.sin(ang)
        zh = jnp.zeros((rows, half), F32)
        zr = jnp.zeros((rows, LANES - B_ROPE), F32)
        cos128 = jnp.concatenate([cos, cos, zr], axis=1)
        sin_a = jnp.concatenate([zh, sin, zr], axis=1)
        sin_b = jnp.concatenate([-sin, zh, zr], axis=1)
        bn = _pick(d, (1024, 512, 256, 128))
        z_qkv = norm_matmul(x, row(g_attn_norm), w_qkv, bm, bn, BF16)
        z_gate = norm_matmul(x, row(g_attn_norm), w_gate, bm, bn)
        z_lat = norm_matmul(x, row(g_attn_norm), w_lat, bm, w_lat.shape[1])
        bq = _pick(rows, (256, 128, 64, 32, 16, 8))
        q = q_proj(z_lat, row(g_q_a), w_uq_pad, cos128, sin_a, sin_b, gq, bq)
        c_kv, kr128, k, v = kv_proj(z_lat, row(g_kv_a), w_uk2, w_uv2, cos128, sin_a, sin_b, bq)
        g_t = jnp.transpose(z_lat[:, c_q + c_lat + B_ROPE:c_q + c_lat + B_ROPE + 2 * A_HEADS])
        gates = (g_t[:A_HEADS, None, :], g_t[A_HEADS:, None, :], g_t[:A_HEADS, :, None], g_t[A_HEADS:, :, None])
        return z_qkv, z_gate, q, c_kv, kr128, k, v, gates

    def mix_stage(x, hid, z_gate, h_b, bm):
        bn = _pick(d, (1024, 512, 256, 128))
        merged = merge(hid, z_gate, h_b, row(g_mlstm_out), w_a_b, w_b_b, bm, bn)
        return resid_matmul(x, merged, w_out_b, _pick(x.shape[0], (2 * bm, bm)), bn)

    bff = _pick(f, (512, 256, 128))

    xp = x_prompt.reshape(tp, d)
    bm_p = _pick(tp, (1024, 512, 256, 128, 64, 32, 16, 8))
    zq_p, zg_p, q_p, ckv_p, kr_p, k_p, v_p, gates_p = token_stage(xp, jnp.tile(jnp.arange(s, dtype=jnp.int32), bp), bm_p)
    chunk = _pick(s, (256, 128, 64, 32, 16, 8))
    hid_p, c_p, n_p, m_p = mlstm_prompt(zq_p, b_gates, *gates_p, bp, s, chunk)
    hb_p = attn_prompt(q_p, k_p, v_p, bp, s, _pick(s, (512, 256, 128)))
    x1_p = mix_stage(xp, hid_p, zg_p, hb_p, _pick(tp, (512, 256, 128, 64, 32, 16, 8)))
    bmf_p = _pick(s, (512, 256, 128, 64, 32, 16, 8))
    y_p, tg_p, tv_p = conv_ffn(x1_p, tp, row(g_ffn_norm), w_up_b, w_conv, row(b_conv), w_down_b,
                               bmf_p, bff, s, halo=True)
    nb_seq = s // bmf_p
    tail_p = jnp.concatenate([tg_p, tv_p], axis=-1).reshape(bp, nb_seq, SUBLANES, 2 * f)
    conv_p = tail_p[:, nb_seq - 1, SUBLANES - (CONV_W - 1):, :]

    xs = x_sample.reshape(ts, d)
    bm_s = _pick(ts, (512, 256, 128, 64, 32, 16, 8))
    zq_s, zg_s, q_s, ckv_s, kr_s, _, _, gates_s = token_stage(
        xs, jnp.tile(past + jnp.arange(seq, dtype=jnp.int32), db), bm_s)
    rb = _pick(ts, (128, 64, 32, 16, 8))
    m_col = jnp.repeat(jnp.transpose(m_state), seq, axis=1)[:, :, None]
    n_rows = jnp.repeat(jnp.transpose(n_state, (1, 0, 2)), seq, axis=1)
    hid_s, c_s, n_s_rows, m_s_rows = mlstm_sample(zq_s, b_gates, *gates_s, m_col, n_rows, c_state[None], rb, seq)
    n_s = jnp.transpose(n_s_rows[:, seq - 1::seq, :], (1, 0, 2))
    m_s = jnp.transpose(m_s_rows[:, seq - 1::seq, 0])

    q_abs = q_absorb(q_s, w_uk_t).reshape(db, seq * B_HEADS, c_lat)
    c_new16 = jnp.pad(ckv_s.reshape(db, seq, c_lat), ((0, 0), (0, BF16_ROWS - seq), (0, 0)))
    kr_new_t = jnp.pad(jnp.transpose(kr_s[:, :B_ROPE].reshape(db, seq, B_ROPE), (0, 2, 1)),
                       ((0, 0), (0, 0), (0, LANES - seq)))
    g_pages = _pick(n_pages, (32, 16, 8, 4, 2))
    o_lat = attn_sample(page_table, cache_lat, jnp.swapaxes(cache_kr, 2, 3), w_uk_t, q_abs,
                        q_s.reshape(db, seq * B_HEADS, Q_HEAD_PAD), c_new16, kr_new_t, g_pages, seq)
    hb_s = uv_expand(o_lat.reshape(ts, B_HEADS * c_lat), w_uv2)
    x1_s = mix_stage(xs, hid_s, zg_s, hb_s, bm_s)
    prev = conv_state
    pe0 = jnp.concatenate([prev, jnp.zeros((db, seq - (CONV_W - 1), 2 * f), F32)], axis=1).reshape(ts, 2 * f)
    pe1 = jnp.concatenate([prev[:, 1:], jnp.zeros((db, seq - 1, 2 * f), F32)], axis=1).reshape(ts, 2 * f)
    y_s, tg_s, tv_s = conv_ffn(x1_s, ts, row(g_ffn_norm), w_up_b, w_conv, row(b_conv), w_down_b,
                               bm_s, bff, seq, halo=False, prev=(pe0, pe1))
    u_s = jnp.concatenate([tg_s, tv_s], axis=-1).reshape(db, seq, 2 * f)
    conv_s = u_s[:, seq - (CONV_W - 1):, :]

    new_p = (ckv_p.reshape(bp, s, c_lat), kr_p[:, :B_ROPE].reshape(bp, s, B_ROPE),
             c_p, n_p[:, :, 0, :], m_p[:, :, 0, 0], conv_p)
    new_s = (ckv_s.reshape(db, seq, c_lat), kr_s[:, :B_ROPE].reshape(db, seq, B_ROPE),
             c_s[0], n_s, m_s, conv_s)
    return y_p.reshape(bp, s, d), y_s.reshape(db, seq, d), new_p, new_s


def kernel(x_prompt, x_sample, cache_kv_latent, cache_k_rope, state_mlstm_C, state_mlstm_n, state_mlstm_m, state_conv, page_table, g_attn_norm, w_in, b_gates, g_q_a, w_uq, g_qk_nope_q, g_qk_rope_q, g_kv_a, w_uk, w_uv, g_qk_nope_k, g_qk_rope_k, g_mlstm_out, w_branch_a, w_branch_b, w_out, g_ffn_norm, w_up, w_conv, b_conv, w_down):
    depth = w_in.shape[0]
    assert depth == 1, "single-layer trunk"
    l = 0
    y_p, y_s, new_p, new_s = _layer(
        x_prompt, x_sample, cache_kv_latent, cache_k_rope, state_mlstm_C[l], state_mlstm_n[l],
        state_mlstm_m[l], state_conv[l], page_table, g_attn_norm[l], w_in[l], b_gates[l], g_q_a[l], w_uq[l],
        g_qk_nope_q[l], g_qk_rope_q[l], g_kv_a[l], w_uk[l], w_uv[l], g_qk_nope_k[l], g_qk_rope_k[l],
        g_mlstm_out[l], w_branch_a[l], w_branch_b[l], w_out[l], g_ffn_norm[l], w_up[l], w_conv[l], b_conv[l],
        w_down[l])
    dts = (cache_kv_latent.dtype, cache_k_rope.dtype, state_mlstm_C.dtype, state_mlstm_n.dtype,
           state_mlstm_m.dtype, state_conv.dtype)
    st_p = tuple(a[None].astype(dt) for a, dt in zip(new_p, dts))
    st_s = tuple(a[None].astype(dt) for a, dt in zip(new_s, dts))
    return (y_p, y_s) + st_p + st_s
```

```python
import functools
import math

import jax
import jax.numpy as jnp
from jax import lax
from jax.experimental import pallas as pl
from jax.experimental.pallas import tpu as pltpu

F32 = jnp.float32
BF16 = jnp.bfloat16

A_HEADS = 8
A_DQK = 128
A_DV = 256
K_SCALE = A_DQK ** -0.5
GATE_CAP = 15.0
B_HEADS = 16
B_NOPE = 128
B_ROPE = 64
B_QK = B_NOPE + B_ROPE
B_V = 128
Q_HEAD_PAD = 256
ROPE_THETA = 10000.0
ATTN_SCALE = B_QK ** -0.5
CONV_W = 3
EPS = 1e-6
NEG_INF = float("-inf")

LANES = 128
SUBLANES = 8
BF16_ROWS = 16
MXU_COLS = 256
VMEM_LIMIT = 56 * 1024 * 1024


def _cparams(sem):
    return pltpu.CompilerParams(dimension_semantics=sem, vmem_limit_bytes=VMEM_LIMIT)


def _rms(x, g):
    r = lax.rsqrt(jnp.mean(x * x, axis=-1, keepdims=True) + EPS)
    return x * r * g


def _dot(a, b):
    return jnp.dot(a, b, preferred_element_type=F32)


def _dot_nt(a, b):
    return lax.dot_general(a, b, (((1,), (1,)), ((), ())), preferred_element_type=F32)


def _rope128(x, cos, sin_a, sin_b):
    return x * cos + pltpu.roll(x, 32, 1) * sin_a + pltpu.roll(x, 96, 1) * sin_b


def _norm_mm_kernel(x_ref, g_ref, w_ref, o_ref, xn_ref):
    @pl.when(pl.program_id(1) == 0)
    def _():
        xn_ref[...] = _rms(x_ref[...], g_ref[...]).astype(BF16)

    o_ref[...] = _dot(xn_ref[...], w_ref[...]).astype(o_ref.dtype)


def norm_matmul(x, g, w, bm, bn, out_dtype=F32):
    t, d = x.shape
    n = w.shape[1]
    return pl.pallas_call(
        _norm_mm_kernel,
        grid=(t // bm, n // bn),
        in_specs=[pl.BlockSpec((bm, d), lambda i, j: (i, 0)),
                  pl.BlockSpec((1, d), lambda i, j: (0, 0)),
                  pl.BlockSpec((d, bn), lambda i, j: (0, j))],
        out_specs=pl.BlockSpec((bm, bn), lambda i, j: (i, j)),
        out_shape=jax.ShapeDtypeStruct((t, n), out_dtype),
        scratch_shapes=[pltpu.VMEM((bm, d), BF16)],
        compiler_params=_cparams(("parallel", "arbitrary")),
    )(x, g, w)


def _q_kernel(cq_ref, g_ref, w_ref, cos_ref, sa_ref, sb_ref, gain_ref, o_ref):
    cqn = _rms(cq_ref[...], g_ref[...]).astype(BF16)
    q = _dot(cqn, w_ref[...])
    cos, sa, sb = cos_ref[...], sa_ref[...], sb_ref[...]
    for h in range(B_HEADS):
        lo = h * Q_HEAD_PAD
        nope = q[:, lo:lo + LANES]
        rp = _rope128(q[:, lo + LANES:lo + Q_HEAD_PAD], cos, sa, sb)
        ssq = jnp.sum(nope * nope, axis=-1, keepdims=True) + jnp.sum(rp * rp, axis=-1, keepdims=True)
        r = lax.rsqrt(ssq * (1.0 / B_QK) + EPS)
        o_ref[:, lo:lo + LANES] = (nope * r * gain_ref[:, lo:lo + LANES]).astype(o_ref.dtype)
        o_ref[:, lo + LANES:lo + Q_HEAD_PAD] = (rp * r * gain_ref[:, lo + LANES:lo + Q_HEAD_PAD]).astype(o_ref.dtype)


def q_proj(z_lat, g_q_a, w_uq_pad, cos, sa, sb, gain, bm):
    t = z_lat.shape[0]
    c = g_q_a.shape[1]
    n = w_uq_pad.shape[1]
    return pl.pallas_call(
        _q_kernel,
        grid=(t // bm,),
        in_specs=[pl.BlockSpec((bm, c), lambda i: (i, 0)),
                  pl.BlockSpec((1, c), lambda i: (0, 0)),
                  pl.BlockSpec((c, n), lambda i: (0, 0)),
                  pl.BlockSpec((bm, LANES), lambda i: (i, 0)),
                  pl.BlockSpec((bm, LANES), lambda i: (i, 0)),
                  pl.BlockSpec((bm, LANES), lambda i: (i, 0)),
                  pl.BlockSpec((1, n), lambda i: (0, 0))],
        out_specs=pl.BlockSpec((bm, n), lambda i: (i, 0)),
        out_shape=jax.ShapeDtypeStruct((t, n), BF16),
        compiler_params=_cparams(("parallel",)),
    )(z_lat, g_q_a, w_uq_pad, cos, sa, sb, gain)


def _kv_kernel(ckv_ref, sm_ref, g_ref, wuk_ref, wuv_ref, cos_ref, sa_ref, sb_ref,
               c_ref, kr_ref, k_ref, v_ref):
    c = _rms(ckv_ref[...], g_ref[...])
    c_ref[...] = c
    cb = c.astype(BF16)
    kn = _dot(cb, wuk_ref[...])
    v_ref[...] = _dot(cb, wuv_ref[...]).astype(v_ref.dtype)
    kr = _rope128(sm_ref[...], cos_ref[...], sa_ref[...], sb_ref[...])
    kr_ref[...] = kr
    ssr = jnp.sum(kr * kr, axis=-1, keepdims=True)
    for h in range(B_HEADS):
        nope = kn[:, h * B_NOPE:(h + 1) * B_NOPE]
        r = lax.rsqrt((jnp.sum(nope * nope, axis=-1, keepdims=True) + ssr) * (1.0 / B_QK) + EPS)
        lo = h * Q_HEAD_PAD
        k_ref[:, lo:lo + LANES] = (nope * r).astype(k_ref.dtype)
        k_ref[:, lo + LANES:lo + Q_HEAD_PAD] = (kr * r).astype(k_ref.dtype)


def kv_proj(z_lat, g_kv_a, w_uk, w_uv, cos, sa, sb, bm):
    t = z_lat.shape[0]
    c = g_kv_a.shape[1]
    small_blk = (2 * c) // LANES
    row = lambda i: (i, 0)
    const = lambda i: (0, 0)
    return pl.pallas_call(
        _kv_kernel,
        grid=(t // bm,),
        in_specs=[pl.BlockSpec((bm, c), lambda i: (i, 1)),
                  pl.BlockSpec((bm, LANES), lambda i: (i, small_blk)),
                  pl.BlockSpec((1, c), const),
                  pl.BlockSpec(w_uk.shape, const),
                  pl.BlockSpec(w_uv.shape, const),
                  pl.BlockSpec((bm, LANES), row),
                  pl.BlockSpec((bm, LANES), row),
                  pl.BlockSpec((bm, LANES), row)],
        out_specs=[pl.BlockSpec((bm, c), row),
                   pl.BlockSpec((bm, LANES), row),
                   pl.BlockSpec((bm, B_HEADS * Q_HEAD_PAD), row),
                   pl.BlockSpec((bm, B_HEADS * B_V), row)],
        out_shape=[jax.ShapeDtypeStruct((t, c), F32),
                   jax.ShapeDtypeStruct((t, LANES), F32),
                   jax.ShapeDtypeStruct((t, B_HEADS * Q_HEAD_PAD), BF16),
                   jax.ShapeDtypeStruct((t, B_HEADS * B_V), BF16)],
        compiler_params=_cparams(("parallel",)),
    )(z_lat, z_lat, g_kv_a, w_uk, w_uv, cos, sa, sb)


def _log_gates(gi, gf, bi, bf):
    cap = lambda x: GATE_CAP * jnp.tanh(x * (1.0 / GATE_CAP))
    li = cap(gi + bi)
    y = cap(gf + bf)
    lf = jnp.minimum(y, 0.0) - jnp.log(1.0 + jnp.exp(-jnp.abs(y)))
    return li, lf


def _mlstm_p_kernel(b_ref, q_ref, k_ref, v_ref, gir_ref, gfr_ref, gic_ref, gfc_ref,
                    hid_ref, c_out_ref, n_out_ref, m_out_ref, c_scr, n_scr, m_scr):
    h = pl.program_id(1)
    ci = pl.program_id(2)
    L = q_ref.shape[0]

    @pl.when(ci == 0)
    def _():
        c_scr[...] = jnp.zeros_like(c_scr)
        n_scr[...] = jnp.zeros_like(n_scr)
        m_scr[...] = jnp.zeros_like(m_scr)

    bi = b_ref[h]
    bf = b_ref[A_HEADS + h]
    qb = q_ref[...]
    kb = k_ref[...]
    vb = v_ref[...]
    q = qb.astype(F32)
    li_r, lf_r = _log_gates(gir_ref[0], gfr_ref[0], bi, bf)
    li_c, lf_c = _log_gates(gic_ref[0], gfc_ref[0], bi, bf)

    t_idx = lax.broadcasted_iota(jnp.int32, (L, L), 0)
    s_idx = lax.broadcasted_iota(jnp.int32, (L, L), 1)
    causal = s_idx <= t_idx
    bcum_c = jnp.sum(jnp.where(causal, lf_r, 0.0), axis=1, keepdims=True)
    bcum_r = jnp.sum(jnp.where(t_idx <= s_idx, lf_c, 0.0), axis=0, keepdims=True)
    log_w = jnp.where(causal, bcum_c - bcum_r + li_r, NEG_INF)
    m_prev = m_scr[:, 0:1]
    log_state = bcum_c + m_prev
    m_row = jnp.maximum(log_state, jnp.max(log_w, axis=1, keepdims=True))
    w = jnp.exp(log_w - m_row) * K_SCALE
    w_state = jnp.exp(log_state - m_row)
    sc = _dot_nt(qb, kb) * w
    num = _dot(sc.astype(BF16), vb) + w_state * _dot(qb, c_scr[...].astype(BF16))
    den = jnp.sum(sc, axis=1, keepdims=True) + w_state * jnp.sum(q * n_scr[...], axis=1, keepdims=True)
    hid_ref[...] = num / jnp.maximum(jnp.abs(den), jnp.exp(-m_row))

    b_last = bcum_c[L - 1:L, :]
    log_k = b_last - bcum_c + li_c
    m_new = jnp.maximum(b_last + m_prev, jnp.max(log_k, axis=0, keepdims=True))
    wk = jnp.exp(log_k - m_new) * K_SCALE
    decay = jnp.exp(b_last + m_prev - m_new)
    kw = kb.astype(F32) * wk
    kwt = jnp.transpose(kw).astype(BF16)
    c_new = decay * c_scr[...] + _dot(kwt, vb)
    n_new = decay * n_scr[...] + jnp.sum(kw, axis=0, keepdims=True)
    c_scr[...] = c_new
    n_scr[...] = n_new
    m_scr[...] = jnp.broadcast_to(m_new, m_scr.shape)

    @pl.when(ci == pl.num_programs(2) - 1)
    def _():
        c_out_ref[0, 0] = c_new
        n_out_ref[0, 0] = n_new
        m_out_ref[0, 0] = jnp.broadcast_to(m_new, (1, LANES))


def mlstm_prompt(z_main, b_gates, gi_r, gf_r, gi_c, gf_c, bp, s, chunk):
    nc = s // chunk
    kcol = (A_HEADS * A_DQK) // A_DQK
    vcol = (2 * A_HEADS * A_DQK) // A_DV
    tok = lambda b, h, c, *_: (b * nc + c)
    return pl.pallas_call(
        _mlstm_p_kernel,
        grid_spec=pltpu.PrefetchScalarGridSpec(
            num_scalar_prefetch=1,
            grid=(bp, A_HEADS, nc),
            in_specs=[pl.BlockSpec((chunk, A_DQK), lambda b, h, c, *_: (b * nc + c, h)),
                      pl.BlockSpec((chunk, A_DQK), lambda b, h, c, *_: (b * nc + c, kcol + h)),
                      pl.BlockSpec((chunk, A_DV), lambda b, h, c, *_: (b * nc + c, vcol + h)),
                      pl.BlockSpec((1, 1, chunk), lambda b, h, c, *_: (h, 0, b * nc + c)),
                      pl.BlockSpec((1, 1, chunk), lambda b, h, c, *_: (h, 0, b * nc + c)),
                      pl.BlockSpec((1, chunk, 1), lambda b, h, c, *_: (h, b * nc + c, 0)),
                      pl.BlockSpec((1, chunk, 1), lambda b, h, c, *_: (h, b * nc + c, 0))],
            out_specs=[pl.BlockSpec((chunk, A_DV), lambda b, h, c, *_: (b * nc + c, h)),
                       pl.BlockSpec((1, 1, A_DQK, A_DV), lambda b, h, c, *_: (b, h, 0, 0)),
                       pl.BlockSpec((1, 1, 1, A_DQK), lambda b, h, c, *_: (b, h, 0, 0)),
                       pl.BlockSpec((1, 1, 1, LANES), lambda b, h, c, *_: (b, h, 0, 0))],
            scratch_shapes=[pltpu.VMEM((A_DQK, A_DV), F32),
                            pltpu.VMEM((1, A_DQK), F32),
                            pltpu.VMEM((1, LANES), F32)]),
        out_shape=[jax.ShapeDtypeStruct((bp * s, A_HEADS * A_DV), F32),
                   jax.ShapeDtypeStruct((bp, A_HEADS, A_DQK, A_DV), F32),
                   jax.ShapeDtypeStruct((bp, A_HEADS, 1, A_DQK), F32),
                   jax.ShapeDtypeStruct((bp, A_HEADS, 1, LANES), F32)],
        compiler_params=_cparams(("parallel", "parallel", "arbitrary")),
    )(b_gates, z_main, z_main, z_main, gi_r, gf_r, gi_c, gf_c)


def _mlstm_s_kernel(b_ref, q_ref, k_ref, v_ref, gir_ref, gfr_ref, gic_ref, gfc_ref,
                    mc_ref, nrow_ref, c_in_ref,
                    hid_ref, c_out_ref, n_out_ref, m_out_ref,
                    hc_scr, dec_scr, kwt_scr, q_scr, *, seq):
    h = pl.program_id(1)
    R = q_ref.shape[0]
    nseq = R // seq
    per8 = SUBLANES // seq
    bi = b_ref[h]
    bf = b_ref[A_HEADS + h]
    q = q_ref[...].astype(F32)
    k = k_ref[...].astype(F32)
    v = v_ref[...].astype(F32)
    q_scr[...] = q
    li_r, lf_r = _log_gates(gir_ref[0], gfr_ref[0], bi, bf)
    li_c, lf_c = _log_gates(gic_ref[0], gfc_ref[0], bi, bf)
    m_prev = mc_ref[0]

    t_idx = lax.broadcasted_iota(jnp.int32, (R, R), 0)
    s_idx = lax.broadcasted_iota(jnp.int32, (R, R), 1)
    same = (t_idx // seq) == (s_idx // seq)
    causal = same & (s_idx <= t_idx)
    bcum_c = jnp.sum(jnp.where(causal, lf_r, 0.0), axis=1, keepdims=True)
    bcum_r = jnp.sum(jnp.where(same & (t_idx <= s_idx), lf_c, 0.0), axis=0, keepdims=True)
    blast_c = jnp.sum(jnp.where(same, lf_r, 0.0), axis=1, keepdims=True)
    blast_r = jnp.sum(jnp.where(same, lf_c, 0.0), axis=0, keepdims=True)
    log_w = jnp.where(causal, bcum_c - bcum_r + li_r, NEG_INF)
    log_state = bcum_c + m_prev
    m_row = jnp.maximum(log_state, jnp.max(log_w, axis=1, keepdims=True))
    w = jnp.exp(log_w - m_row) * K_SCALE
    w_state = jnp.exp(log_state - m_row)

    log_k_c = blast_c - bcum_c + li_c
    log_k_r = blast_r - bcum_r + li_r
    seg_max = jnp.max(jnp.where(same, log_k_r, NEG_INF), axis=1, keepdims=True)
    m_new = jnp.maximum(blast_c + m_prev, seg_max)
    wk = jnp.exp(log_k_c - m_new) * K_SCALE
    decay = jnp.exp(blast_c + m_prev - m_new)
    kw = k * wk
    kwt_scr[...] = jnp.transpose(kw)
    dec_scr[...] = jnp.broadcast_to(decay, dec_scr.shape)

    row8 = lax.broadcasted_iota(jnp.int32, (SUBLANES, A_DV), 0)
    col = lax.broadcasted_iota(jnp.int32, (A_DQK, R), 1)

    def group(gidx, carry):
        r0 = pl.multiple_of(gidx * SUBLANES, SUBLANES)
        q8 = q_scr[pl.ds(r0, SUBLANES), :]
        hc8 = jnp.zeros((SUBLANES, A_DV), F32)
        for u in range(per8):
            sq = gidx * per8 + u
            c0 = c_in_ref[0, sq, 0]
            res = _dot(q8, c0)
            hc8 = jnp.where((row8 // seq) == u, res, hc8)
            kwt_m = jnp.where((col // seq) == sq, kwt_scr[...], 0.0)
            dsc = dec_scr[pl.ds(sq * seq, 1), 0:1]
            c_out_ref[0, sq, 0] = dsc * c0 + _dot(kwt_m, v)
        hc_scr[pl.ds(r0, SUBLANES), :] = hc8
        return carry

    lax.fori_loop(0, R // SUBLANES, group, 0)

    sc = _dot_nt(q, k) * w
    num = _dot(sc, v) + w_state * hc_scr[...]
    den = jnp.sum(sc, axis=1, keepdims=True) + w_state * jnp.sum(q * nrow_ref[0], axis=1, keepdims=True)
    hid_ref[...] = num / jnp.maximum(jnp.abs(den), jnp.exp(-m_row))

    acc = kw
    shift = 1
    while shift < seq:
        acc = acc + jnp.where((lax.broadcasted_iota(jnp.int32, acc.shape, 0) % seq) >= shift,
                              pltpu.roll(acc, shift, 0), 0.0)
        shift *= 2
    n_out_ref[0] = decay * nrow_ref[0] + acc
    m_out_ref[0] = jnp.broadcast_to(m_new, (R, LANES))


def mlstm_sample(z_main_s, b_gates, gi_r, gf_r, gi_c, gf_c, m_col, n_rows, c_state, rb, seq):
    ts = z_main_s.shape[0]
    nseq_blk = rb // seq
    kcol = (A_HEADS * A_DQK) // A_DQK
    vcol = (2 * A_HEADS * A_DQK) // A_DV
    return pl.pallas_call(
        functools.partial(_mlstm_s_kernel, seq=seq),
        grid_spec=pltpu.PrefetchScalarGridSpec(
            num_scalar_prefetch=1,
            grid=(ts // rb, A_HEADS),
            in_specs=[pl.BlockSpec((rb, A_DQK), lambda i, h, *_: (i, h)),
                      pl.BlockSpec((rb, A_DQK), lambda i, h, *_: (i, kcol + h)),
                      pl.BlockSpec((rb, A_DV), lambda i, h, *_: (i, vcol + h)),
                      pl.BlockSpec((1, 1, rb), lambda i, h, *_: (h, 0, i)),
                      pl.BlockSpec((1, 1, rb), lambda i, h, *_: (h, 0, i)),
                      pl.BlockSpec((1, rb, 1), lambda i, h, *_: (h, i, 0)),
                      pl.BlockSpec((1, rb, 1), lambda i, h, *_: (h, i, 0)),
                      pl.BlockSpec((1, rb, 1), lambda i, h, *_: (h, i, 0)),
                      pl.BlockSpec((1, rb, A_DQK), lambda i, h, *_: (h, i, 0)),
                      pl.BlockSpec((1, nseq_blk, 1, A_DQK, A_DV), lambda i, h, *_: (0, i, h, 0, 0))],
            out_specs=[pl.BlockSpec((rb, A_DV), lambda i, h, *_: (i, h)),
                       pl.BlockSpec((1, nseq_blk, 1, A_DQK, A_DV), lambda i, h, *_: (0, i, h, 0, 0)),
                       pl.BlockSpec((1, rb, A_DQK), lambda i, h, *_: (h, i, 0)),
                       pl.BlockSpec((1, rb, LANES), lambda i, h, *_: (h, i, 0))],
            scratch_shapes=[pltpu.VMEM((rb, A_DV), F32),
                            pltpu.VMEM((rb, LANES), F32),
                            pltpu.VMEM((A_DQK, rb), F32),
                            pltpu.VMEM((rb, A_DQK), F32)]),
        out_shape=[jax.ShapeDtypeStruct((ts, A_HEADS * A_DV), F32),
                   jax.ShapeDtypeStruct(c_state.shape, F32),
                   jax.ShapeDtypeStruct((A_HEADS, ts, A_DQK), F32),
                   jax.ShapeDtypeStruct((A_HEADS, ts, LANES), F32)],
        compiler_params=_cparams(("parallel", "parallel")),
    )(b_gates, z_main_s, z_main_s, z_main_s, gi_r, gf_r, gi_c, gf_c, m_col, n_rows, c_state)


def _attn_p_kernel(q_ref, k_ref, v_ref, o_ref, *, tq):
    s = q_ref.shape[0]
    for qt in range(s // tq):
        q = q_ref[qt * tq:(qt + 1) * tq, :]
        nk = (qt + 1) * tq
        sc = _dot_nt(q, k_ref[0:nk, :])
        row = lax.broadcasted_iota(jnp.int32, (tq, nk), 0) + qt * tq
        colk = lax.broadcasted_iota(jnp.int32, (tq, nk), 1)
        sc = jnp.where(colk <= row, sc, NEG_INF)
        m = jnp.max(sc, axis=-1, keepdims=True)
        p = jnp.exp(sc - m)
        l = jnp.sum(p, axis=-1, keepdims=True)
        o = _dot(p.astype(BF16), v_ref[0:nk, :])
        o_ref[qt * tq:(qt + 1) * tq, :] = (o / l).astype(o_ref.dtype)


def attn_prompt(q, k, v, bp, s, tq):
    return pl.pallas_call(
        functools.partial(_attn_p_kernel, tq=tq),
        grid=(bp, B_HEADS),
        in_specs=[pl.BlockSpec((s, Q_HEAD_PAD), lambda b, h: (b, h)),
                  pl.BlockSpec((s, Q_HEAD_PAD), lambda b, h: (b, h)),
                  pl.BlockSpec((s, B_V), lambda b, h: (b, h))],
        out_specs=pl.BlockSpec((s, B_V), lambda b, h: (b, h)),
        out_shape=jax.ShapeDtypeStruct((bp * s, B_HEADS * B_V), BF16),
        compiler_params=_cparams(("parallel", "parallel")),
    )(q, k, v)


def _q_absorb_kernel(q_ref, w_ref, o_ref):
    o_ref[...] = _dot(q_ref[...], w_ref[...]).astype(o_ref.dtype)


def q_absorb(q_s, w_uk_t):
    ts = q_s.shape[0]
    c = w_uk_t.shape[1]
    return pl.pallas_call(
        _q_absorb_kernel,
        grid=(B_HEADS,),
        in_specs=[pl.BlockSpec((ts, B_NOPE), lambda h: (0, 2 * h)),
                  pl.BlockSpec((B_NOPE, c), lambda h: (h, 0))],
        out_specs=pl.BlockSpec((ts, c), lambda h: (0, h)),
        out_shape=jax.ShapeDtypeStruct((ts, B_HEADS * c), BF16),
        compiler_params=_cparams(("parallel",)),
    )(q_s, w_uk_t)


def _attn_s_kernel(pt_ref, *refs, g_pages, page, kblk, seq):
    (lat_hbm, kr_hbm, wukt_ref, qabs_ref, q_ref, cnew_ref, krnew_ref, o_ref,
     wcat_scr, lat_scr, m_scr, l_scr, acc_scr, lat_buf, kr_buf, lat_sem, kr_sem) = refs
    b = pl.program_id(0)
    j = pl.program_id(1)
    nj = pl.num_programs(1)
    nq = qabs_ref.shape[1]
    nkey = wukt_ref.shape[0]
    c = wukt_ref.shape[1]
    per_blk = kblk // page

    step = b * nj + j
    slot = step % 2

    def page_copies(bb, jj, sl, g):
        pg = pt_ref[bb, jj * g_pages + g]
        return (pltpu.make_async_copy(lat_hbm.at[0, pg], lat_buf.at[sl, g], lat_sem.at[sl]),
                pltpu.make_async_copy(kr_hbm.at[0, pg], kr_buf.at[sl, g], kr_sem.at[sl]))

    def start_pages(bb, jj, sl):
        for g in range(g_pages):
            for cp in page_copies(bb, jj, sl, g):
                cp.start()

    @pl.when(step == 0)
    def _():
        start_pages(0, 0, 0)
        wcat_scr[0:nkey, :] = wukt_ref[...]

    @pl.when(step + 1 < pl.num_programs(0) * nj)
    def _():
        nxt = step + 1
        start_pages(nxt // nj, nxt % nj, 1 - slot)

    for g in range(g_pages):
        for cp in page_copies(b, j, slot, g):
            cp.wait()
    lat_refs = [lat_buf.at[slot, g] for g in range(g_pages)]
    kr_refs = [kr_buf.at[slot, g] for g in range(g_pages)]

    @pl.when(j == 0)
    def _():
        wcat_scr[nkey:nkey + nq, :] = qabs_ref[0]
        m_scr[...] = jnp.full(m_scr.shape, NEG_INF, F32)
        l_scr[...] = jnp.zeros_like(l_scr)
        acc_scr[...] = jnp.zeros_like(acc_scr)

    qrope = q_ref[0][:, LANES:LANES + B_ROPE]

    def scores(lb, krt):
        a = _dot_nt(wcat_scr[...], lb)
        kt = a[0:nkey]
        ssq = jnp.sum((kt * kt).reshape(B_HEADS, B_NOPE, kblk), axis=1)
        ssr = jnp.sum(krt * krt, axis=0, keepdims=True)
        r = lax.rsqrt((ssq + ssr) * (1.0 / B_QK) + EPS)
        return (a[nkey:nkey + nq] + _dot(qrope, krt.astype(BF16))) * jnp.concatenate([r] * seq, axis=0)

    def fold(s, lat, carry):
        m, l, acc = carry
        m_new = jnp.maximum(m, jnp.max(s, axis=-1, keepdims=True))
        alpha = jnp.exp(m - m_new)
        p = jnp.exp(s - m_new)
        return m_new, alpha * l + jnp.sum(p, axis=-1, keepdims=True), alpha * acc + _dot(p.astype(BF16), lat)

    s_blocks = []
    for sb in range(g_pages // per_blk):
        pages = range(sb * per_blk, (sb + 1) * per_blk)
        lb = jnp.concatenate([lat_refs[g][...].astype(BF16) for g in pages], axis=0)
        krt = jnp.concatenate([kr_refs[g][...] for g in pages], axis=1)
        lat_scr[sb * kblk:(sb + 1) * kblk, :] = lb
        s_blocks.append(scores(lb, krt))
    carry = fold(jnp.concatenate(s_blocks, axis=1), lat_scr[...], (m_scr[:, 0:1], l_scr[:, 0:1], acc_scr[...]))
    m, l, acc = carry
    m_scr[...] = jnp.broadcast_to(m, m_scr.shape)
    l_scr[...] = jnp.broadcast_to(l, l_scr.shape)
    acc_scr[...] = acc

    @pl.when(j == pl.num_programs(1) - 1)
    def _():
        lb = jnp.concatenate([cnew_ref[0].astype(BF16), jnp.zeros((kblk - BF16_ROWS, c), BF16)], axis=0)
        krt = jnp.concatenate([krnew_ref[0], jnp.zeros((B_ROPE, kblk - LANES), F32)], axis=1)
        t_of_row = lax.broadcasted_iota(jnp.int32, (nq, kblk), 0) // B_HEADS
        jn = lax.broadcasted_iota(jnp.int32, (nq, kblk), 1)
        s = jnp.where(jn <= t_of_row, scores(lb, krt), NEG_INF)
        _, l2, acc2 = fold(s, lb, carry)
        o_ref[0] = acc2 / l2


def attn_sample(page_table, cache_lat, cache_kr_t, w_uk_t, q_abs, q_s, c_new16, kr_new_t, g_pages, seq):
    db, n_pages = page_table.shape
    page, c = cache_lat.shape[2], cache_lat.shape[3]
    kblk = MXU_COLS
    assert kblk % page == 0 and g_pages % (kblk // page) == 0 and n_pages % g_pages == 0
    nq = q_abs.shape[1]
    nkey = w_uk_t.shape[0]
    per_b = lambda b, j, pt: (b, 0, 0)
    return pl.pallas_call(
        functools.partial(_attn_s_kernel, g_pages=g_pages, page=page, kblk=kblk, seq=seq),
        grid_spec=pltpu.PrefetchScalarGridSpec(
            num_scalar_prefetch=1,
            grid=(db, n_pages // g_pages),
            in_specs=[
                pl.BlockSpec(memory_space=pl.ANY),
                pl.BlockSpec(memory_space=pl.ANY),
                pl.BlockSpec(w_uk_t.shape, lambda b, j, pt: (0, 0)),
                pl.BlockSpec((1, nq, c), per_b),
                pl.BlockSpec((1, nq, Q_HEAD_PAD), per_b),
                pl.BlockSpec((1, BF16_ROWS, c), per_b),
                pl.BlockSpec((1, B_ROPE, LANES), per_b)],
            out_specs=pl.BlockSpec((1, nq, c), per_b),
            scratch_shapes=[pltpu.VMEM((nkey + nq, c), BF16),
                            pltpu.VMEM((g_pages * page, c), BF16),
                            pltpu.VMEM((nq, LANES), F32),
                            pltpu.VMEM((nq, LANES), F32),
                            pltpu.VMEM((nq, c), F32),
                            pltpu.VMEM((2, g_pages, page, c), F32),
                            pltpu.VMEM((2, g_pages, B_ROPE, page), F32),
                            pltpu.SemaphoreType.DMA((2,)),
                            pltpu.SemaphoreType.DMA((2,))]),
        out_shape=jax.ShapeDtypeStruct((db, nq, c), F32),
        compiler_params=_cparams(("arbitrary", "arbitrary")),
    )(page_table, cache_lat, cache_kr_t, w_uk_t, q_abs, q_s, c_new16, kr_new_t)


def _uv_kernel(o_ref, w_ref, h_ref):
    h_ref[...] = _dot(o_ref[...].astype(BF16), w_ref[...]).astype(h_ref.dtype)


def uv_expand(o_lat2, w_uv):
    ts = o_lat2.shape[0]
    c = w_uv.shape[0]
    return pl.pallas_call(
        _uv_kernel,
        grid=(B_HEADS,),
        in_specs=[pl.BlockSpec((ts, c), lambda h: (0, h)),
                  pl.BlockSpec((c, B_V), lambda h: (0, h))],
        out_specs=pl.BlockSpec((ts, B_V), lambda h: (0, h)),
        out_shape=jax.ShapeDtypeStruct((ts, B_HEADS * B_V), BF16),
        compiler_params=_cparams(("parallel",)),
    )(o_lat2, w_uv)


def _merge_kernel(hid_ref, oa_ref, ga_ref, gb_ref, hb_ref, g_ref, wa_ref, wb_ref, o_ref, ha_scr):
    @pl.when(pl.program_id(1) == 0)
    def _():
        for h in range(A_HEADS):
            sl = slice(h * A_DV, (h + 1) * A_DV)
            ha_scr[:, sl] = (_rms(hid_ref[:, sl], g_ref[:, sl]) * jax.nn.sigmoid(oa_ref[:, sl])).astype(BF16)

    a = _dot(ha_scr[...], wa_ref[...])
    b = _dot(hb_ref[...], wb_ref[...])
    o_ref[...] = (jax.nn.sigmoid(ga_ref[...]) * a + jax.nn.sigmoid(gb_ref[...]) * b).astype(o_ref.dtype)


def merge(hid, z_gate, h_b, g_out, w_a, w_b, bm, bn):
    t, d = hid.shape
    nj = d // bn
    return pl.pallas_call(
        _merge_kernel,
        grid=(t // bm, nj),
        in_specs=[pl.BlockSpec((bm, d), lambda i, j: (i, 0)),
                  pl.BlockSpec((bm, d), lambda i, j: (i, 0)),
                  pl.BlockSpec((bm, bn), lambda i, j: (i, nj + j)),
                  pl.BlockSpec((bm, bn), lambda i, j: (i, 2 * nj + j)),
                  pl.BlockSpec((bm, d), lambda i, j: (i, 0)),
                  pl.BlockSpec((1, d), lambda i, j: (0, 0)),
                  pl.BlockSpec((d, bn), lambda i, j: (0, j)),
                  pl.BlockSpec((d, bn), lambda i, j: (0, j))],
        out_specs=pl.BlockSpec((bm, bn), lambda i, j: (i, j)),
        out_shape=jax.ShapeDtypeStruct((t, d), BF16),
        scratch_shapes=[pltpu.VMEM((bm, d), BF16)],
        compiler_params=_cparams(("parallel", "arbitrary")),
    )(hid, z_gate, z_gate, z_gate, h_b, g_out, w_a, w_b)


def _resid_mm_kernel(x_ref, a_ref, w_ref, o_ref):
    o_ref[...] = x_ref[...] + _dot(a_ref[...], w_ref[...])


def resid_matmul(x, a, w, bm, bn):
    t, d = x.shape
    k = a.shape[1]
    return pl.pallas_call(
        _resid_mm_kernel,
        grid=(t // bm, d // bn),
        in_specs=[pl.BlockSpec((bm, bn), lambda i, j: (i, j)),
                  pl.BlockSpec((bm, k), lambda i, j: (i, 0)),
                  pl.BlockSpec((k, bn), lambda i, j: (0, j))],
        out_specs=pl.BlockSpec((bm, bn), lambda i, j: (i, j)),
        out_shape=jax.ShapeDtypeStruct((t, d), F32),
        compiler_params=_cparams(("parallel", "parallel")),
    )(x, a, w)


def _ffn_p_kernel(x_ref, xh_ref, g_ref, wg_ref, wv_ref, cg_ref, cv_ref, bg_ref, bv_ref, wd_ref,
                  y_ref, ug_ref, uv_ref, xn_scr, acc_scr, *, seq_len):
    i = pl.program_id(0)
    j = pl.program_id(1)
    bm = x_ref.shape[0]
    hp = SUBLANES

    @pl.when(j == 0)
    def _():
        xn_scr[hp:, :] = _rms(x_ref[...], g_ref[...]).astype(BF16)
        xn_scr[0:hp, :] = _rms(xh_ref[...], g_ref[...]).astype(BF16)
        acc_scr[...] = jnp.zeros_like(acc_scr)

    xall = xn_scr[...]
    pos = (lax.broadcasted_iota(jnp.int32, (bm, 1), 0) + i * bm) % seq_len

    def conv(u, wc_ref, bc_ref):
        e2 = u[hp:, :]
        e1 = jnp.where(pos >= 1, u[hp - 1:hp - 1 + bm, :], 0.0)
        e0 = jnp.where(pos >= 2, u[hp - 2:hp - 2 + bm, :], 0.0)
        return bc_ref[...] + ((e0 * wc_ref[0:1, :] + e1 * wc_ref[1:2, :]) + e2 * wc_ref[2:3, :])

    ug = _dot(xall, wg_ref[...])
    uv = _dot(xall, wv_ref[...])
    ug_ref[0] = ug[bm:, :]
    uv_ref[0] = uv[bm:, :]
    cg = conv(ug, cg_ref, bg_ref)
    cv = conv(uv, cv_ref, bv_ref)
    act = (cg * jax.nn.sigmoid(cg) * cv).astype(BF16)
    acc_scr[...] += _dot(act, wd_ref[...])

    @pl.when(j == pl.num_programs(1) - 1)
    def _():
        y_ref[...] = x_ref[...] + acc_scr[...]


def _ffn_weight_specs(d, bf, nj):
    return [pl.BlockSpec((1, d), lambda i, j: (0, 0)),
            pl.BlockSpec((d, bf), lambda i, j: (0, j)),
            pl.BlockSpec((d, bf), lambda i, j: (0, nj + j)),
            pl.BlockSpec((CONV_W, bf), lambda i, j: (0, j)),
            pl.BlockSpec((CONV_W, bf), lambda i, j: (0, nj + j)),
            pl.BlockSpec((1, bf), lambda i, j: (0, j)),
            pl.BlockSpec((1, bf), lambda i, j: (0, nj + j)),
            pl.BlockSpec((bf, d), lambda i, j: (j, 0))]


def conv_ffn_prompt(x1, g, w_up, w_conv, b_conv, w_down, bm, bf, seq_len):
    rows, d = x1.shape
    f = w_down.shape[0]
    nj = f // bf
    nb = rows // bm
    hb = bm // SUBLANES
    return pl.pallas_call(
        functools.partial(_ffn_p_kernel, seq_len=seq_len),
        grid=(nb, nj),
        in_specs=[pl.BlockSpec((bm, d), lambda i, j: (i, 0)),
                  pl.BlockSpec((SUBLANES, d), lambda i, j: (jnp.maximum(i * hb - 1, 0), 0))]
        + _ffn_weight_specs(d, bf, nj),
        out_specs=[pl.BlockSpec((bm, d), lambda i, j: (i, 0)),
                   pl.BlockSpec((1, SUBLANES, bf), lambda i, j: (i, 0, j)),
                   pl.BlockSpec((1, SUBLANES, bf), lambda i, j: (i, 0, j))],
        out_shape=[jax.ShapeDtypeStruct((rows, d), F32),
                   jax.ShapeDtypeStruct((nb, SUBLANES, f), F32),
                   jax.ShapeDtypeStruct((nb, SUBLANES, f), F32)],
        scratch_shapes=[pltpu.VMEM((bm + SUBLANES, d), BF16),
                        pltpu.VMEM((bm, d), F32)],
        compiler_params=_cparams(("parallel", "arbitrary")),
    )(x1, x1, g, w_up, w_up, w_conv, w_conv, b_conv, b_conv, w_down)


def _ffn_s_kernel(x_ref, g_ref, wg_ref, wv_ref, cg_ref, cv_ref, bg_ref, bv_ref, wd_ref, prev_ref,
                  y_ref, new_ref, xn_scr, acc_scr, *, seq):
    j = pl.program_id(1)
    nb = x_ref.shape[0] // seq

    @pl.when(j == 0)
    def _():
        xn_scr[...] = _rms(x_ref[...], g_ref[...]).astype(BF16)
        acc_scr[...] = jnp.zeros_like(acc_scr)

    xn = xn_scr[...]

    def conv(u, half, wc_ref, bc_ref):
        ext = [prev_ref[r, half] for r in range(CONV_W - 1)] + [u[t * nb:(t + 1) * nb, :] for t in range(seq)]
        for r in range(CONV_W - 1):
            new_ref[r, half] = ext[seq + r]
        return jnp.concatenate(
            [bc_ref[...] + ((ext[t] * wc_ref[0:1, :] + ext[t + 1] * wc_ref[1:2, :]) + ext[t + 2] * wc_ref[2:3, :])
             for t in range(seq)], axis=0)

    cg = conv(_dot(xn, wg_ref[...]), 0, cg_ref, bg_ref)
    cv = conv(_dot(xn, wv_ref[...]), 1, cv_ref, bv_ref)
    act = (cg * jax.nn.sigmoid(cg) * cv).astype(BF16)
    acc_scr[...] += _dot(act, wd_ref[...])

    @pl.when(j == pl.num_programs(1) - 1)
    def _():
        y_ref[...] = x_ref[...] + acc_scr[...]


def conv_ffn_sample(x1_t, prev_t, g, w_up, w_conv, b_conv, w_down, bf, seq):
    rows, d = x1_t.shape
    f = w_down.shape[0]
    nj = f // bf
    nb = rows // seq
    hist = pl.BlockSpec((CONV_W - 1, 2, nb, bf), lambda i, j: (0, 0, 0, j))
    return pl.pallas_call(
        functools.partial(_ffn_s_kernel, seq=seq),
        grid=(1, nj),
        in_specs=[pl.BlockSpec((rows, d), lambda i, j: (0, 0))] + _ffn_weight_specs(d, bf, nj) + [hist],
        out_specs=[pl.BlockSpec((rows, d), lambda i, j: (0, 0)), hist],
        out_shape=[jax.ShapeDtypeStruct((rows, d), F32),
                   jax.ShapeDtypeStruct(prev_t.shape, F32)],
        scratch_shapes=[pltpu.VMEM((rows, d), BF16),
                        pltpu.VMEM((rows, d), F32)],
        compiler_params=_cparams(("parallel", "arbitrary")),
    )(x1_t, g, w_up, w_up, w_conv, w_conv, b_conv, b_conv, w_down, prev_t)


def _pick(n, prefs):
    for p in prefs:
        if n % p == 0:
            return p
    return n


def _layer(x_prompt, x_sample, cache_lat, cache_kr, c_state, n_state, m_state, conv_state, page_table,
           g_attn_norm, w_in, b_gates, g_q_a, w_uq, g_qk_nope_q, g_qk_rope_q, g_kv_a, w_uk, w_uv,
           g_qk_nope_k, g_qk_rope_k, g_mlstm_out, w_branch_a, w_branch_b, w_out, g_ffn_norm,
           w_up, w_conv, b_conv, w_down):
    bp, s, d = x_prompt.shape
    db, seq, _ = x_sample.shape
    n_pages = page_table.shape[1]
    page = cache_lat.shape[2]
    past = n_pages * page
    tp, ts = bp * s, db * seq
    t = tp + ts
    c_lat = g_kv_a.shape[0]
    c_q = g_q_a.shape[0]
    f = w_down.shape[0]
    qk_w = A_HEADS * A_DQK
    v_w = A_HEADS * A_DV
    row = lambda v: v.reshape(1, -1)

    o_i = 2 * qk_w + 2 * v_w
    o_cq = o_i + 2 * A_HEADS
    o_kr = o_cq + c_q + c_lat
    o_g = o_kr + B_ROPE
    o_v = 2 * qk_w + v_w
    w_qkv = w_in[:, :o_v].astype(BF16)
    w_gate = jnp.concatenate([w_in[:, o_v:o_i], w_in[:, o_g:]], axis=1).astype(BF16)
    n_small = LANES - B_ROPE - 2 * A_HEADS
    w_lat = jnp.concatenate([w_in[:, o_cq:o_kr], w_in[:, o_kr:o_g], w_in[:, o_i:o_cq],
                             jnp.zeros((d, n_small), F32)], axis=1).astype(BF16)
    w_uq_pad = jnp.pad(w_uq, ((0, 0), (0, 0), (0, Q_HEAD_PAD - B_QK))).reshape(c_q, B_HEADS * Q_HEAD_PAD).astype(BF16)
    w_uk2 = w_uk.reshape(c_lat, B_HEADS * B_NOPE).astype(BF16)
    w_uk_t = jnp.transpose(w_uk.reshape(c_lat, B_HEADS * B_NOPE)).astype(BF16)
    w_uv2 = w_uv.reshape(c_lat, B_HEADS * B_V).astype(BF16)
    gq = jnp.concatenate([g_qk_nope_q * g_qk_nope_k, g_qk_rope_q * g_qk_rope_k, g_qk_rope_q * g_qk_rope_k,
                          jnp.zeros((Q_HEAD_PAD - B_QK,), F32)]) * ATTN_SCALE
    gq = jnp.tile(gq, B_HEADS).reshape(1, -1)

    w_a_b, w_b_b, w_out_b = w_branch_a.astype(BF16), w_branch_b.astype(BF16), w_out.astype(BF16)
    w_up_b, w_down_b = w_up.astype(BF16), w_down.astype(BF16)
    half = B_ROPE // 2
    freqs = ROPE_THETA ** (-jnp.arange(half, dtype=F32) / half)

    def token_stage(x, pos, bm):
        rows = x.shape[0]
        ang = pos.astype(F32)[:, None] * freqs
        cos, sin = jnp.cos(ang), jnp.sin(ang)
        zh = jnp.zeros((rows, half), F32)
        zr = jnp.zeros((rows, LANES - B_ROPE), F32)
        cos128 = jnp.concatenate([cos, cos, zr], axis=1)
        sin_a = jnp.concatenate([zh, sin, zr], axis=1)
        sin_b = jnp.concatenate([-sin, zh, zr], axis=1)
        bn = _pick(d, (1024, 512, 256, 128))
        z_qkv = norm_matmul(x, row(g_attn_norm), w_qkv, bm, bn, BF16)
        z_gate = norm_matmul(x, row(g_attn_norm), w_gate, bm, bn)
        z_lat = norm_matmul(x, row(g_attn_norm), w_lat, bm, w_lat.shape[1])
        bq = _pick(rows, (256, 128, 64, 32, 16, 8))
        q = q_proj(z_lat, row(g_q_a), w_uq_pad, cos128, sin_a, sin_b, gq, bq)
        c_kv, kr128, k, v = kv_proj(z_lat, row(g_kv_a), w_uk2, w_uv2, cos128, sin_a, sin_b, bq)
        g_t = jnp.transpose(z_lat[:, c_q + c_lat + B_ROPE:c_q + c_lat + B_ROPE + 2 * A_HEADS])
        gates = (g_t[:A_HEADS, None, :], g_t[A_HEADS:, None, :], g_t[:A_HEADS, :, None], g_t[A_HEADS:, :, None])
        return z_qkv, z_gate, q, c_kv, kr128, k, v, gates

    def mix_stage(x, hid, z_gate, h_b, bm):
        bn = _pick(d, (1024, 512, 256, 128))
        merged = merge(hid, z_gate, h_b, row(g_mlstm_out), w_a_b, w_b_b, bm, bn)
        return resid_matmul(x, merged, w_out_b, _pick(x.shape[0], (2 * bm, bm)), bn)

    bff = _pick(f, (512, 256, 128))

    xp = x_prompt.reshape(tp, d)
    bm_p = _pick(tp, (1024, 512, 256, 128, 64, 32, 16, 8))
    zq_p, zg_p, q_p, ckv_p, kr_p, k_p, v_p, gates_p = token_stage(xp, jnp.tile(jnp.arange(s, dtype=jnp.int32), bp), bm_p)
    chunk = _pick(s, (256, 128, 64, 32, 16, 8))
    hid_p, c_p, n_p, m_p = mlstm_prompt(zq_p, b_gates, *gates_p, bp, s, chunk)
    hb_p = attn_prompt(q_p, k_p, v_p, bp, s, _pick(s, (512, 256, 128)))
    x1_p = mix_stage(xp, hid_p, zg_p, hb_p, _pick(tp, (512, 256, 128, 64, 32, 16, 8)))
    bmf_p = _pick(s, (512, 256, 128, 64, 32, 16, 8))
    y_p, tg_p, tv_p = conv_ffn_prompt(x1_p, row(g_ffn_norm), w_up_b, w_conv, row(b_conv), w_down_b, bmf_p, bff, s)
    nb_seq = s // bmf_p
    tail_p = jnp.concatenate([tg_p, tv_p], axis=-1).reshape(bp, nb_seq, SUBLANES, 2 * f)
    conv_p = tail_p[:, nb_seq - 1, SUBLANES - (CONV_W - 1):, :]

    xs = x_sample.reshape(ts, d)
    bm_s = _pick(ts, (512, 256, 128, 64, 32, 16, 8))
    zq_s, zg_s, q_s, ckv_s, kr_s, _, _, gates_s = token_stage(
        xs, jnp.tile(past + jnp.arange(seq, dtype=jnp.int32), db), bm_s)
    rb = _pick(ts, (128, 64, 32, 16, 8))
    m_col = jnp.repeat(jnp.transpose(m_state), seq, axis=1)[:, :, None]
    n_rows = jnp.repeat(jnp.transpose(n_state, (1, 0, 2)), seq, axis=1)
    hid_s, c_s, n_s_rows, m_s_rows = mlstm_sample(zq_s, b_gates, *gates_s, m_col, n_rows, c_state[None], rb, seq)
    n_s = jnp.transpose(n_s_rows[:, seq - 1::seq, :], (1, 0, 2))
    m_s = jnp.transpose(m_s_rows[:, seq - 1::seq, 0])

    q_abs = q_absorb(q_s, w_uk_t).reshape(db, seq * B_HEADS, c_lat)
    c_new16 = jnp.pad(ckv_s.reshape(db, seq, c_lat), ((0, 0), (0, BF16_ROWS - seq), (0, 0)))
    kr_new_t = jnp.pad(jnp.transpose(kr_s[:, :B_ROPE].reshape(db, seq, B_ROPE), (0, 2, 1)),
                       ((0, 0), (0, 0), (0, LANES - seq)))
    g_pages = _pick(n_pages, (32, 16, 8, 4, 2))
    o_lat = attn_sample(page_table, cache_lat, jnp.swapaxes(cache_kr, 2, 3), w_uk_t, q_abs,
                        q_s.reshape(db, seq * B_HEADS, Q_HEAD_PAD), c_new16, kr_new_t, g_pages, seq)
    hb_s = uv_expand(o_lat.reshape(ts, B_HEADS * c_lat), w_uv2)
    x1_s = mix_stage(xs, hid_s, zg_s, hb_s, bm_s)
    x1_t = jnp.transpose(x1_s.reshape(db, seq, d), (1, 0, 2)).reshape(ts, d)
    prev_t = jnp.transpose(conv_state.reshape(db, CONV_W - 1, 2, f), (1, 2, 0, 3))
    y_t, new_t = conv_ffn_sample(x1_t, prev_t, row(g_ffn_norm), w_up_b, w_conv, row(b_conv), w_down_b, bff, seq)
    y_s = jnp.transpose(y_t.reshape(seq, db, d), (1, 0, 2))
    conv_s = jnp.transpose(new_t, (2, 0, 1, 3)).reshape(db, CONV_W - 1, 2 * f)

    new_p = (ckv_p.reshape(bp, s, c_lat), kr_p[:, :B_ROPE].reshape(bp, s, B_ROPE),
             c_p, n_p[:, :, 0, :], m_p[:, :, 0, 0], conv_p)
    new_s = (ckv_s.reshape(db, seq, c_lat), kr_s[:, :B_ROPE].reshape(db, seq, B_ROPE),
             c_s[0], n_s, m_s, conv_s)
    return y_p.reshape(bp, s, d), y_s.reshape(db, seq, d), new_p, new_s


def kernel(x_prompt, x_sample, cache_kv_latent, cache_k_rope, state_mlstm_C, state_mlstm_n, state_mlstm_m, state_conv, page_table, g_attn_norm, w_in, b_gates, g_q_a, w_uq, g_qk_nope_q, g_qk_rope_q, g_kv_a, w_uk, w_uv, g_qk_nope_k, g_qk_rope_k, g_mlstm_out, w_branch_a, w_branch_b, w_out, g_ffn_norm, w_up, w_conv, b_conv, w_down):
    depth = w_in.shape[0]
    assert depth == 1, "single-layer trunk"
    l = 0
    y_p, y_s, new_p, new_s = _layer(
        x_prompt, x_sample, cache_kv_latent, cache_k_rope, state_mlstm_C[l], state_mlstm_n[l],
        state_mlstm_m[l], state_conv[l], page_table, g_attn_norm[l], w_in[l], b_gates[l], g_q_a[l], w_uq[l],
        g_qk_nope_q[l], g_qk_rope_q[l], g_kv_a[l], w_uk[l], w_uv[l], g_qk_nope_k[l], g_qk_rope_k[l],
        g_mlstm_out[l], w_branch_a[l], w_branch_b[l], w_out[l], g_ffn_norm[l], w_up[l], w_conv[l], b_conv[l],
        w_down[l])
    dts = (cache_kv_latent.dtype, cache_k_rope.dtype, state_mlstm_C.dtype, state_mlstm_n.dtype,
           state_mlstm_m.dtype, state_conv.dtype)
    st_p = tuple(a[None].astype(dt) for a, dt in zip(new_p, dts))
    st_s = tuple(a[None].astype(dt) for a, dt in zip(new_s, dts))
    return (y_p, y_s) + st_p + st_s
```

```python
import functools
import math

import jax
import jax.numpy as jnp
from jax import lax
from jax.experimental import pallas as pl
from jax.experimental.pallas import tpu as pltpu

F32 = jnp.float32
BF16 = jnp.bfloat16

A_HEADS = 8
A_DQK = 128
A_DV = 256
K_SCALE = A_DQK ** -0.5
GATE_CAP = 15.0
B_HEADS = 16
B_NOPE = 128
B_ROPE = 64
B_QK = B_NOPE + B_ROPE
B_V = 128
Q_HEAD_PAD = 256
ROPE_THETA = 10000.0
ATTN_SCALE = B_QK ** -0.5
CONV_W = 3
EPS = 1e-6
NEG_INF = float("-inf")

LANES = 128
SUBLANES = 8
BF16_ROWS = 16
MXU_COLS = 256
VMEM_LIMIT = 56 * 1024 * 1024


def _cparams(sem):
    return pltpu.CompilerParams(dimension_semantics=sem, vmem_limit_bytes=VMEM_LIMIT)


def _rms(x, g):
    r = lax.rsqrt(jnp.mean(x * x, axis=-1, keepdims=True) + EPS)
    return x * r * g


def _dot(a, b):
    return jnp.dot(a, b, preferred_element_type=F32)


def _dot_nt(a, b):
    return lax.dot_general(a, b, (((1,), (1,)), ((), ())), preferred_element_type=F32)


def _rope128(x, cos, sin_a, sin_b):
    return x * cos + pltpu.roll(x, 32, 1) * sin_a + pltpu.roll(x, 96, 1) * sin_b


def _norm_mm_kernel(x_ref, g_ref, w_ref, o_ref, xn_ref):
    @pl.when(pl.program_id(1) == 0)
    def _():
        xn_ref[...] = _rms(x_ref[...], g_ref[...]).astype(BF16)

    o_ref[...] = _dot(xn_ref[...], w_ref[...]).astype(o_ref.dtype)


def norm_matmul(x, g, w, bm, bn, out_dtype=F32):
    t, d = x.shape
    n = w.shape[1]
    return pl.pallas_call(
        _norm_mm_kernel,
        grid=(t // bm, n // bn),
        in_specs=[pl.BlockSpec((bm, d), lambda i, j: (i, 0)),
                  pl.BlockSpec((1, d), lambda i, j: (0, 0)),
                  pl.BlockSpec((d, bn), lambda i, j: (0, j))],
        out_specs=pl.BlockSpec((bm, bn), lambda i, j: (i, j)),
        out_shape=jax.ShapeDtypeStruct((t, n), out_dtype),
        scratch_shapes=[pltpu.VMEM((bm, d), BF16)],
        compiler_params=_cparams(("parallel", "arbitrary")),
    )(x, g, w)


def _q_kernel(cq_ref, g_ref, w_ref, cos_ref, sa_ref, sb_ref, gain_ref, o_ref):
    cqn = _rms(cq_ref[...], g_ref[...]).astype(BF16)
    q = _dot(cqn, w_ref[...])
    cos, sa, sb = cos_ref[...], sa_ref[...], sb_ref[...]
    for h in range(B_HEADS):
        lo = h * Q_HEAD_PAD
        nope = q[:, lo:lo + LANES]
        rp = _rope128(q[:, lo + LANES:lo + Q_HEAD_PAD], cos, sa, sb)
        ssq = jnp.sum(nope * nope, axis=-1, keepdims=True) + jnp.sum(rp * rp, axis=-1, keepdims=True)
        r = lax.rsqrt(ssq * (1.0 / B_QK) + EPS)
        o_ref[:, lo:lo + LANES] = (nope * r * gain_ref[:, lo:lo + LANES]).astype(o_ref.dtype)
        o_ref[:, lo + LANES:lo + Q_HEAD_PAD] = (rp * r * gain_ref[:, lo + LANES:lo + Q_HEAD_PAD]).astype(o_ref.dtype)


def q_proj(z_lat, g_q_a, w_uq_pad, cos, sa, sb, gain, bm):
    t = z_lat.shape[0]
    c = g_q_a.shape[1]
    n = w_uq_pad.shape[1]
    return pl.pallas_call(
        _q_kernel,
        grid=(t // bm,),
        in_specs=[pl.BlockSpec((bm, c), lambda i: (i, 0)),
                  pl.BlockSpec((1, c), lambda i: (0, 0)),
                  pl.BlockSpec((c, n), lambda i: (0, 0)),
                  pl.BlockSpec((bm, LANES), lambda i: (i, 0)),
                  pl.BlockSpec((bm, LANES), lambda i: (i, 0)),
                  pl.BlockSpec((bm, LANES), lambda i: (i, 0)),
                  pl.BlockSpec((1, n), lambda i: (0, 0))],
        out_specs=pl.BlockSpec((bm, n), lambda i: (i, 0)),
        out_shape=jax.ShapeDtypeStruct((t, n), BF16),
        compiler_params=_cparams(("parallel",)),
    )(z_lat, g_q_a, w_uq_pad, cos, sa, sb, gain)


def _kv_kernel(ckv_ref, sm_ref, g_ref, wuk_ref, wuv_ref, cos_ref, sa_ref, sb_ref,
               c_ref, kr_ref, k_ref, v_ref):
    c = _rms(ckv_ref[...], g_ref[...])
    c_ref[...] = c
    cb = c.astype(BF16)
    kn = _dot(cb, wuk_ref[...])
    v_ref[...] = _dot(cb, wuv_ref[...]).astype(v_ref.dtype)
    kr = _rope128(sm_ref[...], cos_ref[...], sa_ref[...], sb_ref[...])
    kr_ref[...] = kr
    ssr = jnp.sum(kr * kr, axis=-1, keepdims=True)
    for h in range(B_HEADS):
        nope = kn[:, h * B_NOPE:(h + 1) * B_NOPE]
        r = lax.rsqrt((jnp.sum(nope * nope, axis=-1, keepdims=True) + ssr) * (1.0 / B_QK) + EPS)
        lo = h * Q_HEAD_PAD
        k_ref[:, lo:lo + LANES] = (nope * r).astype(k_ref.dtype)
        k_ref[:, lo + LANES:lo + Q_HEAD_PAD] = (kr * r).astype(k_ref.dtype)


def kv_proj(z_lat, g_kv_a, w_uk, w_uv, cos, sa, sb, bm):
    t = z_lat.shape[0]
    c = g_kv_a.shape[1]
    small_blk = (2 * c) // LANES
    row = lambda i: (i, 0)
    const = lambda i: (0, 0)
    return pl.pallas_call(
        _kv_kernel,
        grid=(t // bm,),
        in_specs=[pl.BlockSpec((bm, c), lambda i: (i, 1)),
                  pl.BlockSpec((bm, LANES), lambda i: (i, small_blk)),
                  pl.BlockSpec((1, c), const),
                  pl.BlockSpec(w_uk.shape, const),
                  pl.BlockSpec(w_uv.shape, const),
                  pl.BlockSpec((bm, LANES), row),
                  pl.BlockSpec((bm, LANES), row),
                  pl.BlockSpec((bm, LANES), row)],
        out_specs=[pl.BlockSpec((bm, c), row),
                   pl.BlockSpec((bm, LANES), row),
                   pl.BlockSpec((bm, B_HEADS * Q_HEAD_PAD), row),
                   pl.BlockSpec((bm, B_HEADS * B_V), row)],
        out_shape=[jax.ShapeDtypeStruct((t, c), F32),
                   jax.ShapeDtypeStruct((t, LANES), F32),
                   jax.ShapeDtypeStruct((t, B_HEADS * Q_HEAD_PAD), BF16),
                   jax.ShapeDtypeStruct((t, B_HEADS * B_V), BF16)],
        compiler_params=_cparams(("parallel",)),
    )(z_lat, z_lat, g_kv_a, w_uk, w_uv, cos, sa, sb)


def _log_gates(gi, gf, bi, bf):
    cap = lambda x: GATE_CAP * jnp.tanh(x * (1.0 / GATE_CAP))
    li = cap(gi + bi)
    y = cap(gf + bf)
    lf = jnp.minimum(y, 0.0) - jnp.log(1.0 + jnp.exp(-jnp.abs(y)))
    return li, lf


def _mlstm_p_kernel(b_ref, q_ref, k_ref, v_ref, gir_ref, gfr_ref, gic_ref, gfc_ref,
                    hid_ref, c_out_ref, n_out_ref, m_out_ref, c_scr, n_scr, m_scr):
    ci = pl.program_id(1)
    L = q_ref.shape[0]

    @pl.when(ci == 0)
    def _():
        c_scr[...] = jnp.zeros_like(c_scr)
        n_scr[...] = jnp.zeros_like(n_scr)
        m_scr[...] = jnp.zeros_like(m_scr)

    t_idx = lax.broadcasted_iota(jnp.int32, (L, L), 0)
    s_idx = lax.broadcasted_iota(jnp.int32, (L, L), 1)
    causal = s_idx <= t_idx
    anti = t_idx <= s_idx
    for h in range(A_HEADS):
        bi = b_ref[h]
        bf = b_ref[A_HEADS + h]
        qb = q_ref[:, h * A_DQK:(h + 1) * A_DQK]
        kb = k_ref[:, h * A_DQK:(h + 1) * A_DQK]
        vb = v_ref[:, h * A_DV:(h + 1) * A_DV]
        q = qb.astype(F32)
        li_r, lf_r = _log_gates(gir_ref[h], gfr_ref[h], bi, bf)
        li_c, lf_c = _log_gates(gic_ref[h], gfc_ref[h], bi, bf)

        bcum_c = jnp.sum(jnp.where(causal, lf_r, 0.0), axis=1, keepdims=True)
        bcum_r = jnp.sum(jnp.where(anti, lf_c, 0.0), axis=0, keepdims=True)
        log_w = jnp.where(causal, bcum_c - bcum_r + li_r, NEG_INF)
        m_prev = m_scr[h][:, 0:1]
        c_prev = c_scr[h]
        n_prev = n_scr[h]
        log_state = bcum_c + m_prev
        m_row = jnp.maximum(log_state, jnp.max(log_w, axis=1, keepdims=True))
        w = jnp.exp(log_w - m_row) * K_SCALE
        w_state = jnp.exp(log_state - m_row)
        sc = _dot_nt(qb, kb) * w
        num = _dot(sc.astype(BF16), vb) + w_state * _dot(qb, c_prev.astype(BF16))
        den = jnp.sum(sc, axis=1, keepdims=True) + w_state * jnp.sum(q * n_prev, axis=1, keepdims=True)
        hid_ref[:, h * A_DV:(h + 1) * A_DV] = num / jnp.maximum(jnp.abs(den), jnp.exp(-m_row))

        b_last = bcum_c[L - 1:L, :]
        log_k = b_last - bcum_c + li_c
        m_new = jnp.maximum(b_last + m_prev, jnp.max(log_k, axis=0, keepdims=True))
        wk = jnp.exp(log_k - m_new) * K_SCALE
        decay = jnp.exp(b_last + m_prev - m_new)
        kw = kb.astype(F32) * wk
        kwt = jnp.transpose(kw).astype(BF16)
        c_scr[h] = decay * c_prev + _dot(kwt, vb)
        n_scr[h] = decay * n_prev + jnp.sum(kw, axis=0, keepdims=True)
        m_scr[h] = jnp.broadcast_to(m_new, (1, LANES))

    @pl.when(ci == pl.num_programs(1) - 1)
    def _():
        c_out_ref[0] = c_scr[...]
        n_out_ref[0] = n_scr[...]
        m_out_ref[0] = m_scr[...]


def mlstm_prompt(z_main, b_gates, gi_r, gf_r, gi_c, gf_c, bp, s, chunk):
    nc = s // chunk
    qk_w, v_w = A_HEADS * A_DQK, A_HEADS * A_DV
    assert (2 * qk_w) % v_w == 0
    tok = lambda b, c, *_: b * nc + c
    return pl.pallas_call(
        _mlstm_p_kernel,
        grid_spec=pltpu.PrefetchScalarGridSpec(
            num_scalar_prefetch=1,
            grid=(bp, nc),
            in_specs=[pl.BlockSpec((chunk, qk_w), lambda b, c, *_: (tok(b, c), 0)),
                      pl.BlockSpec((chunk, qk_w), lambda b, c, *_: (tok(b, c), 1)),
                      pl.BlockSpec((chunk, v_w), lambda b, c, *_: (tok(b, c), (2 * qk_w) // v_w)),
                      pl.BlockSpec((A_HEADS, 1, chunk), lambda b, c, *_: (0, 0, tok(b, c))),
                      pl.BlockSpec((A_HEADS, 1, chunk), lambda b, c, *_: (0, 0, tok(b, c))),
                      pl.BlockSpec((A_HEADS, chunk, 1), lambda b, c, *_: (0, tok(b, c), 0)),
                      pl.BlockSpec((A_HEADS, chunk, 1), lambda b, c, *_: (0, tok(b, c), 0))],
            out_specs=[pl.BlockSpec((chunk, v_w), lambda b, c, *_: (tok(b, c), 0)),
                       pl.BlockSpec((1, A_HEADS, A_DQK, A_DV), lambda b, c, *_: (b, 0, 0, 0)),
                       pl.BlockSpec((1, A_HEADS, 1, A_DQK), lambda b, c, *_: (b, 0, 0, 0)),
                       pl.BlockSpec((1, A_HEADS, 1, LANES), lambda b, c, *_: (b, 0, 0, 0))],
            scratch_shapes=[pltpu.VMEM((A_HEADS, A_DQK, A_DV), F32),
                            pltpu.VMEM((A_HEADS, 1, A_DQK), F32),
                            pltpu.VMEM((A_HEADS, 1, LANES), F32)]),
        out_shape=[jax.ShapeDtypeStruct((bp * s, v_w), F32),
                   jax.ShapeDtypeStruct((bp, A_HEADS, A_DQK, A_DV), F32),
                   jax.ShapeDtypeStruct((bp, A_HEADS, 1, A_DQK), F32),
                   jax.ShapeDtypeStruct((bp, A_HEADS, 1, LANES), F32)],
        compiler_params=_cparams(("parallel", "arbitrary")),
    )(b_gates, z_main, z_main, z_main, gi_r, gf_r, gi_c, gf_c)


def _mlstm_s_kernel(b_ref, q_ref, k_ref, v_ref, gir_ref, gfr_ref, gic_ref, gfc_ref,
                    mc_ref, nrow_ref, c_in_ref,
                    hid_ref, c_out_ref, n_out_ref, m_out_ref,
                    hc_scr, dec_scr, kwt_scr, q_scr, *, seq):
    h = pl.program_id(1)
    R = q_ref.shape[0]
    nseq = R // seq
    per8 = SUBLANES // seq
    bi = b_ref[h]
    bf = b_ref[A_HEADS + h]
    q = q_ref[...].astype(F32)
    k = k_ref[...].astype(F32)
    v = v_ref[...].astype(F32)
    q_scr[...] = q
    li_r, lf_r = _log_gates(gir_ref[0], gfr_ref[0], bi, bf)
    li_c, lf_c = _log_gates(gic_ref[0], gfc_ref[0], bi, bf)
    m_prev = mc_ref[0]

    t_idx = lax.broadcasted_iota(jnp.int32, (R, R), 0)
    s_idx = lax.broadcasted_iota(jnp.int32, (R, R), 1)
    same = (t_idx // seq) == (s_idx // seq)
    causal = same & (s_idx <= t_idx)
    bcum_c = jnp.sum(jnp.where(causal, lf_r, 0.0), axis=1, keepdims=True)
    bcum_r = jnp.sum(jnp.where(same & (t_idx <= s_idx), lf_c, 0.0), axis=0, keepdims=True)
    blast_c = jnp.sum(jnp.where(same, lf_r, 0.0), axis=1, keepdims=True)
    blast_r = jnp.sum(jnp.where(same, lf_c, 0.0), axis=0, keepdims=True)
    log_w = jnp.where(causal, bcum_c - bcum_r + li_r, NEG_INF)
    log_state = bcum_c + m_prev
    m_row = jnp.maximum(log_state, jnp.max(log_w, axis=1, keepdims=True))
    w = jnp.exp(log_w - m_row) * K_SCALE
    w_state = jnp.exp(log_state - m_row)

    log_k_c = blast_c - bcum_c + li_c
    log_k_r = blast_r - bcum_r + li_r
    seg_max = jnp.max(jnp.where(same, log_k_r, NEG_INF), axis=1, keepdims=True)
    m_new = jnp.maximum(blast_c + m_prev, seg_max)
    wk = jnp.exp(log_k_c - m_new) * K_SCALE
    decay = jnp.exp(blast_c + m_prev - m_new)
    kw = k * wk
    kwt_scr[...] = jnp.transpose(kw)
    dec_scr[...] = jnp.broadcast_to(decay, dec_scr.shape)

    row8 = lax.broadcasted_iota(jnp.int32, (SUBLANES, A_DV), 0)
    col = lax.broadcasted_iota(jnp.int32, (A_DQK, R), 1)

    def group(gidx, carry):
        r0 = pl.multiple_of(gidx * SUBLANES, SUBLANES)
        q8 = q_scr[pl.ds(r0, SUBLANES), :]
        hc8 = jnp.zeros((SUBLANES, A_DV), F32)
        for u in range(per8):
            sq = gidx * per8 + u
            c0 = c_in_ref[0, sq, 0]
            res = _dot(q8, c0)
            hc8 = jnp.where((row8 // seq) == u, res, hc8)
            kwt_m = jnp.where((col // seq) == sq, kwt_scr[...], 0.0)
            dsc = dec_scr[pl.ds(sq * seq, 1), 0:1]
            c_out_ref[0, sq, 0] = dsc * c0 + _dot(kwt_m, v)
        hc_scr[pl.ds(r0, SUBLANES), :] = hc8
        return carry

    lax.fori_loop(0, R // SUBLANES, group, 0)

    sc = _dot_nt(q, k) * w
    num = _dot(sc, v) + w_state * hc_scr[...]
    den = jnp.sum(sc, axis=1, keepdims=True) + w_state * jnp.sum(q * nrow_ref[0], axis=1, keepdims=True)
    hid_ref[...] = num / jnp.maximum(jnp.abs(den), jnp.exp(-m_row))

    acc = kw
    shift = 1
    while shift < seq:
        acc = acc + jnp.where((lax.broadcasted_iota(jnp.int32, acc.shape, 0) % seq) >= shift,
                              pltpu.roll(acc, shift, 0), 0.0)
        shift *= 2
    n_out_ref[0] = decay * nrow_ref[0] + acc
    m_out_ref[0] = jnp.broadcast_to(m_new, (R, LANES))


def mlstm_sample(z_main_s, b_gates, gi_r, gf_r, gi_c, gf_c, m_col, n_rows, c_state, rb, seq):
    ts = z_main_s.shape[0]
    nseq_blk = rb // seq
    kcol = (A_HEADS * A_DQK) // A_DQK
    vcol = (2 * A_HEADS * A_DQK) // A_DV
    return pl.pallas_call(
        functools.partial(_mlstm_s_kernel, seq=seq),
        grid_spec=pltpu.PrefetchScalarGridSpec(
            num_scalar_prefetch=1,
            grid=(ts // rb, A_HEADS),
            in_specs=[pl.BlockSpec((rb, A_DQK), lambda i, h, *_: (i, h)),
                      pl.BlockSpec((rb, A_DQK), lambda i, h, *_: (i, kcol + h)),
                      pl.BlockSpec((rb, A_DV), lambda i, h, *_: (i, vcol + h)),
                      pl.BlockSpec((1, 1, rb), lambda i, h, *_: (h, 0, i)),
                      pl.BlockSpec((1, 1, rb), lambda i, h, *_: (h, 0, i)),
                      pl.BlockSpec((1, rb, 1), lambda i, h, *_: (h, i, 0)),
                      pl.BlockSpec((1, rb, 1), lambda i, h, *_: (h, i, 0)),
                      pl.BlockSpec((1, rb, 1), lambda i, h, *_: (h, i, 0)),
                      pl.BlockSpec((1, rb, A_DQK), lambda i, h, *_: (h, i, 0)),
                      pl.BlockSpec((1, nseq_blk, 1, A_DQK, A_DV), lambda i, h, *_: (0, i, h, 0, 0))],
            out_specs=[pl.BlockSpec((rb, A_DV), lambda i, h, *_: (i, h)),
                       pl.BlockSpec((1, nseq_blk, 1, A_DQK, A_DV), lambda i, h, *_: (0, i, h, 0, 0)),
                       pl.BlockSpec((1, rb, A_DQK), lambda i, h, *_: (h, i, 0)),
                       pl.BlockSpec((1, rb, LANES), lambda i, h, *_: (h, i, 0))],
            scratch_shapes=[pltpu.VMEM((rb, A_DV), F32),
                            pltpu.VMEM((rb, LANES), F32),
                            pltpu.VMEM((A_DQK, rb), F32),
                            pltpu.VMEM((rb, A_DQK), F32)]),
        out_shape=[jax.ShapeDtypeStruct((ts, A_HEADS * A_DV), F32),
                   jax.ShapeDtypeStruct(c_state.shape, F32),
                   jax.ShapeDtypeStruct((A_HEADS, ts, A_DQK), F32),
                   jax.ShapeDtypeStruct((A_HEADS, ts, LANES), F32)],
        compiler_params=_cparams(("parallel", "parallel")),
    )(b_gates, z_main_s, z_main_s, z_main_s, gi_r, gf_r, gi_c, gf_c, m_col, n_rows, c_state)


def _attn_p_kernel(q_ref, k_ref, v_ref, o_ref, *, tq, heads):
    s = q_ref.shape[0]
    lower = lax.broadcasted_iota(jnp.int32, (tq, tq), 1) <= lax.broadcasted_iota(jnp.int32, (tq, tq), 0)
    for hh in range(heads):
        qk = slice(hh * Q_HEAD_PAD, (hh + 1) * Q_HEAD_PAD)
        vv = slice(hh * B_V, (hh + 1) * B_V)
        for qt in range(s // tq):
            lo, hi = qt * tq, (qt + 1) * tq
            q = q_ref[lo:hi, qk]
            sd = jnp.where(lower, _dot_nt(q, k_ref[lo:hi, qk]), NEG_INF)
            m = jnp.max(sd, axis=-1, keepdims=True)
            if qt:
                sp = _dot_nt(q, k_ref[0:lo, qk])
                m = jnp.maximum(m, jnp.max(sp, axis=-1, keepdims=True))
            pd = jnp.exp(sd - m)
            l = jnp.sum(pd, axis=-1, keepdims=True)
            o = _dot(pd.astype(BF16), v_ref[lo:hi, vv])
            if qt:
                pp = jnp.exp(sp - m)
                l = l + jnp.sum(pp, axis=-1, keepdims=True)
                o = o + _dot(pp.astype(BF16), v_ref[0:lo, vv])
            o_ref[lo:hi, vv] = (o / l).astype(o_ref.dtype)


def attn_prompt(q, k, v, bp, s, tq, heads=2):
    return pl.pallas_call(
        functools.partial(_attn_p_kernel, tq=tq, heads=heads),
        grid=(bp, B_HEADS // heads),
        in_specs=[pl.BlockSpec((s, heads * Q_HEAD_PAD), lambda b, h: (b, h)),
                  pl.BlockSpec((s, heads * Q_HEAD_PAD), lambda b, h: (b, h)),
                  pl.BlockSpec((s, heads * B_V), lambda b, h: (b, h))],
        out_specs=pl.BlockSpec((s, heads * B_V), lambda b, h: (b, h)),
        out_shape=jax.ShapeDtypeStruct((bp * s, B_HEADS * B_V), BF16),
        compiler_params=_cparams(("parallel", "parallel")),
    )(q, k, v)


def _q_absorb_kernel(q_ref, w_ref, o_ref):
    o_ref[...] = _dot(q_ref[...], w_ref[...]).astype(o_ref.dtype)


def q_absorb(q_s, w_uk_t):
    ts = q_s.shape[0]
    c = w_uk_t.shape[1]
    return pl.pallas_call(
        _q_absorb_kernel,
        grid=(B_HEADS,),
        in_specs=[pl.BlockSpec((ts, B_NOPE), lambda h: (0, 2 * h)),
                  pl.BlockSpec((B_NOPE, c), lambda h: (h, 0))],
        out_specs=pl.BlockSpec((ts, c), lambda h: (0, h)),
        out_shape=jax.ShapeDtypeStruct((ts, B_HEADS * c), BF16),
        compiler_params=_cparams(("parallel",)),
    )(q_s, w_uk_t)


def _attn_s_kernel(pt_ref, *refs, g_pages, page, kblk, seq):
    (lat_hbm, kr_hbm, wukt_ref, qabs_ref, q_ref, cnew_ref, krnew_ref, o_ref,
     wcat_scr, lat_scr, m_scr, l_scr, acc_scr, lat_buf, kr_buf, lat_sem, kr_sem) = refs
    b = pl.program_id(0)
    j = pl.program_id(1)
    nj = pl.num_programs(1)
    nq = qabs_ref.shape[1]
    nkey = wukt_ref.shape[0]
    c = wukt_ref.shape[1]
    per_blk = kblk // page

    step = b * nj + j
    slot = step % 2

    def page_copies(bb, jj, sl, g):
        pg = pt_ref[bb, jj * g_pages + g]
        return (pltpu.make_async_copy(lat_hbm.at[0, pg], lat_buf.at[sl, g], lat_sem.at[sl]),
                pltpu.make_async_copy(kr_hbm.at[0, pg], kr_buf.at[sl, g], kr_sem.at[sl]))

    def start_pages(bb, jj, sl):
        for g in range(g_pages):
            for cp in page_copies(bb, jj, sl, g):
                cp.start()

    @pl.when(step == 0)
    def _():
        start_pages(0, 0, 0)
        wcat_scr[0:nkey, :] = wukt_ref[...]

    @pl.when(step + 1 < pl.num_programs(0) * nj)
    def _():
        nxt = step + 1
        start_pages(nxt // nj, nxt % nj, 1 - slot)

    for g in range(g_pages):
        for cp in page_copies(b, j, slot, g):
            cp.wait()
    lat_refs = [lat_buf.at[slot, g] for g in range(g_pages)]
    kr_refs = [kr_buf.at[slot, g] for g in range(g_pages)]

    @pl.when(j == 0)
    def _():
        wcat_scr[nkey:nkey + nq, :] = qabs_ref[0]
        m_scr[...] = jnp.full(m_scr.shape, NEG_INF, F32)
        l_scr[...] = jnp.zeros_like(l_scr)
        acc_scr[...] = jnp.zeros_like(acc_scr)

    qrope = q_ref[0][:, LANES:LANES + B_ROPE]

    def scores(lb, krt):
        a = _dot_nt(wcat_scr[...], lb)
        kt = a[0:nkey]
        ssq = jnp.sum((kt * kt).reshape(B_HEADS, B_NOPE, kblk), axis=1)
        ssr = jnp.sum(krt * krt, axis=0, keepdims=True)
        r = lax.rsqrt((ssq + ssr) * (1.0 / B_QK) + EPS)
        return (a[nkey:nkey + nq] + _dot(qrope, krt.astype(BF16))) * jnp.concatenate([r] * seq, axis=0)

    def fold(s, lat, carry):
        m, l, acc = carry
        m_new = jnp.maximum(m, jnp.max(s, axis=-1, keepdims=True))
        alpha = jnp.exp(m - m_new)
        p = jnp.exp(s - m_new)
        return m_new, alpha * l + jnp.sum(p, axis=-1, keepdims=True), alpha * acc + _dot(p.astype(BF16), lat)

    s_blocks = []
    for sb in range(g_pages // per_blk):
        pages = range(sb * per_blk, (sb + 1) * per_blk)
        lb = jnp.concatenate([lat_refs[g][...].astype(BF16) for g in pages], axis=0)
        krt = jnp.concatenate([kr_refs[g][...] for g in pages], axis=1)
        lat_scr[sb * kblk:(sb + 1) * kblk, :] = lb
        s_blocks.append(scores(lb, krt))
    carry = fold(jnp.concatenate(s_blocks, axis=1), lat_scr[...], (m_scr[:, 0:1], l_scr[:, 0:1], acc_scr[...]))
    m, l, acc = carry
    m_scr[...] = jnp.broadcast_to(m, m_scr.shape)
    l_scr[...] = jnp.broadcast_to(l, l_scr.shape)
    acc_scr[...] = acc

    @pl.when(j == pl.num_programs(1) - 1)
    def _():
        lb = jnp.concatenate([cnew_ref[0].astype(BF16), jnp.zeros((kblk - BF16_ROWS, c), BF16)], axis=0)
        krt = jnp.concatenate([krnew_ref[0], jnp.zeros((B_ROPE, kblk - LANES), F32)], axis=1)
        t_of_row = lax.broadcasted_iota(jnp.int32, (nq, kblk), 0) // B_HEADS
        jn = lax.broadcasted_iota(jnp.int32, (nq, kblk), 1)
        s = jnp.where(jn <= t_of_row, scores(lb, krt), NEG_INF)
        _, l2, acc2 = fold(s, lb, carry)
        o_ref[0] = acc2 / l2


def attn_sample(page_table, cache_lat, cache_kr_t, w_uk_t, q_abs, q_s, c_new16, kr_new_t, g_pages, seq):
    db, n_pages = page_table.shape
    page, c = cache_lat.shape[2], cache_lat.shape[3]
    kblk = MXU_COLS
    assert kblk % page == 0 and g_pages % (kblk // page) == 0 and n_pages % g_pages == 0
    nq = q_abs.shape[1]
    nkey = w_uk_t.shape[0]
    per_b = lambda b, j, pt: (b, 0, 0)
    return pl.pallas_call(
        functools.partial(_attn_s_kernel, g_pages=g_pages, page=page, kblk=kblk, seq=seq),
        grid_spec=pltpu.PrefetchScalarGridSpec(
            num_scalar_prefetch=1,
            grid=(db, n_pages // g_pages),
            in_specs=[
                pl.BlockSpec(memory_space=pl.ANY),
                pl.BlockSpec(memory_space=pl.ANY),
                pl.BlockSpec(w_uk_t.shape, lambda b, j, pt: (0, 0)),
                pl.BlockSpec((1, nq, c), per_b),
                pl.BlockSpec((1, nq, Q_HEAD_PAD), per_b),
                pl.BlockSpec((1, BF16_ROWS, c), per_b),
                pl.BlockSpec((1, B_ROPE, LANES), per_b)],
            out_specs=pl.BlockSpec((1, nq, c), per_b),
            scratch_shapes=[pltpu.VMEM((nkey + nq, c), BF16),
                            pltpu.VMEM((g_pages * page, c), BF16),
                            pltpu.VMEM((nq, LANES), F32),
                            pltpu.VMEM((nq, LANES), F32),
                            pltpu.VMEM((nq, c), F32),
                            pltpu.VMEM((2, g_pages, page, c), F32),
                            pltpu.VMEM((2, g_pages, B_ROPE, page), F32),
                            pltpu.SemaphoreType.DMA((2,)),
                            pltpu.SemaphoreType.DMA((2,))]),
        out_shape=jax.ShapeDtypeStruct((db, nq, c), F32),
        compiler_params=_cparams(("arbitrary", "arbitrary")),
    )(page_table, cache_lat, cache_kr_t, w_uk_t, q_abs, q_s, c_new16, kr_new_t)


def _uv_kernel(o_ref, w_ref, h_ref):
    h_ref[...] = _dot(o_ref[...].astype(BF16), w_ref[...]).astype(h_ref.dtype)


def uv_expand(o_lat2, w_uv):
    ts = o_lat2.shape[0]
    c = w_uv.shape[0]
    return pl.pallas_call(
        _uv_kernel,
        grid=(B_HEADS,),
        in_specs=[pl.BlockSpec((ts, c), lambda h: (0, h)),
                  pl.BlockSpec((c, B_V), lambda h: (0, h))],
        out_specs=pl.BlockSpec((ts, B_V), lambda h: (0, h)),
        out_shape=jax.ShapeDtypeStruct((ts, B_HEADS * B_V), BF16),
        compiler_params=_cparams(("parallel",)),
    )(o_lat2, w_uv)


def _merge_kernel(hid_ref, oa_ref, ga_ref, gb_ref, hb_ref, g_ref, wa_ref, wb_ref, o_ref, ha_scr):
    @pl.when(pl.program_id(1) == 0)
    def _():
        for h in range(A_HEADS):
            sl = slice(h * A_DV, (h + 1) * A_DV)
            gate = jax.nn.sigmoid(oa_ref[:, sl].astype(F32))
            ha_scr[:, sl] = (_rms(hid_ref[:, sl], g_ref[:, sl]) * gate).astype(BF16)

    a = _dot(ha_scr[...], wa_ref[...])
    b = _dot(hb_ref[...], wb_ref[...])
    o_ref[...] = (jax.nn.sigmoid(ga_ref[...].astype(F32)) * a
                  + jax.nn.sigmoid(gb_ref[...].astype(F32)) * b).astype(o_ref.dtype)


def merge(hid, z_main, h_b, g_out, w_a, w_b, bm, bn):
    t, d = hid.shape
    nj = d // bn
    return pl.pallas_call(
        _merge_kernel,
        grid=(t // bm, nj),
        in_specs=[pl.BlockSpec((bm, d), lambda i, j: (i, 0)),
                  pl.BlockSpec((bm, d), lambda i, j: (i, 2)),
                  pl.BlockSpec((bm, bn), lambda i, j: (i, 3 * nj + j)),
                  pl.BlockSpec((bm, bn), lambda i, j: (i, 4 * nj + j)),
                  pl.BlockSpec((bm, d), lambda i, j: (i, 0)),
                  pl.BlockSpec((1, d), lambda i, j: (0, 0)),
                  pl.BlockSpec((d, bn), lambda i, j: (0, j)),
                  pl.BlockSpec((d, bn), lambda i, j: (0, j))],
        out_specs=pl.BlockSpec((bm, bn), lambda i, j: (i, j)),
        out_shape=jax.ShapeDtypeStruct((t, d), BF16),
        scratch_shapes=[pltpu.VMEM((bm, d), BF16)],
        compiler_params=_cparams(("parallel", "arbitrary")),
    )(hid, z_main, z_main, z_main, h_b, g_out, w_a, w_b)


def _resid_mm_kernel(x_ref, a_ref, w_ref, o_ref):
    o_ref[...] = x_ref[...] + _dot(a_ref[...], w_ref[...])


def resid_matmul(x, a, w, bm, bn):
    t, d = x.shape
    k = a.shape[1]
    return pl.pallas_call(
        _resid_mm_kernel,
        grid=(t // bm, d // bn),
        in_specs=[pl.BlockSpec((bm, bn), lambda i, j: (i, j)),
                  pl.BlockSpec((bm, k), lambda i, j: (i, 0)),
                  pl.BlockSpec((k, bn), lambda i, j: (0, j))],
        out_specs=pl.BlockSpec((bm, bn), lambda i, j: (i, j)),
        out_shape=jax.ShapeDtypeStruct((t, d), F32),
        compiler_params=_cparams(("parallel", "parallel")),
    )(x, a, w)


def _ffn_p_kernel(x_ref, xh_ref, g_ref, wg_ref, wv_ref, cg_ref, cv_ref, bg_ref, bv_ref, wd_ref,
                  y_ref, ug_ref, uv_ref, xn_scr, acc_scr, *, seq_len):
    i = pl.program_id(0)
    j = pl.program_id(1)
    bm = x_ref.shape[0]
    hp = SUBLANES

    @pl.when(j == 0)
    def _():
        xn_scr[hp:, :] = _rms(x_ref[...], g_ref[...]).astype(BF16)
        xn_scr[0:hp, :] = _rms(xh_ref[...], g_ref[...]).astype(BF16)
        acc_scr[...] = jnp.zeros_like(acc_scr)

    xall = xn_scr[...]
    pos = (lax.broadcasted_iota(jnp.int32, (bm, 1), 0) + i * bm) % seq_len

    def conv(u, wc_ref, bc_ref):
        e2 = u[hp:, :]
        e1 = jnp.where(pos >= 1, u[hp - 1:hp - 1 + bm, :], 0.0)
        e0 = jnp.where(pos >= 2, u[hp - 2:hp - 2 + bm, :], 0.0)
        return bc_ref[...] + ((e0 * wc_ref[0:1, :] + e1 * wc_ref[1:2, :]) + e2 * wc_ref[2:3, :])

    ug = _dot(xall, wg_ref[...])
    uv = _dot(xall, wv_ref[...])
    ug_ref[0] = ug[bm:, :]
    uv_ref[0] = uv[bm:, :]
    cg = conv(ug, cg_ref, bg_ref)
    cv = conv(uv, cv_ref, bv_ref)
    act = (cg * jax.nn.sigmoid(cg) * cv).astype(BF16)
    acc_scr[...] += _dot(act, wd_ref[...])

    @pl.when(j == pl.num_programs(1) - 1)
    def _():
        y_ref[...] = x_ref[...] + acc_scr[...]


def _ffn_weight_specs(d, bf, nj):
    return [pl.BlockSpec((1, d), lambda i, j: (0, 0)),
            pl.BlockSpec((d, bf), lambda i, j: (0, j)),
            pl.BlockSpec((d, bf), lambda i, j: (0, nj + j)),
            pl.BlockSpec((CONV_W, bf), lambda i, j: (0, j)),
            pl.BlockSpec((CONV_W, bf), lambda i, j: (0, nj + j)),
            pl.BlockSpec((1, bf), lambda i, j: (0, j)),
            pl.BlockSpec((1, bf), lambda i, j: (0, nj + j)),
            pl.BlockSpec((bf, d), lambda i, j: (j, 0))]


def conv_ffn_prompt(x1, g, w_up, w_conv, b_conv, w_down, bm, bf, seq_len):
    rows, d = x1.shape
    f = w_down.shape[0]
    nj = f // bf
    nb = rows // bm
    hb = bm // SUBLANES
    return pl.pallas_call(
        functools.partial(_ffn_p_kernel, seq_len=seq_len),
        grid=(nb, nj),
        in_specs=[pl.BlockSpec((bm, d), lambda i, j: (i, 0)),
                  pl.BlockSpec((SUBLANES, d), lambda i, j: (jnp.maximum(i * hb - 1, 0), 0))]
        + _ffn_weight_specs(d, bf, nj),
        out_specs=[pl.BlockSpec((bm, d), lambda i, j: (i, 0)),
                   pl.BlockSpec((1, SUBLANES, bf), lambda i, j: (i, 0, j)),
                   pl.BlockSpec((1, SUBLANES, bf), lambda i, j: (i, 0, j))],
        out_shape=[jax.ShapeDtypeStruct((rows, d), F32),
                   jax.ShapeDtypeStruct((nb, SUBLANES, f), F32),
                   jax.ShapeDtypeStruct((nb, SUBLANES, f), F32)],
        scratch_shapes=[pltpu.VMEM((bm + SUBLANES, d), BF16),
                        pltpu.VMEM((bm, d), F32)],
        compiler_params=_cparams(("parallel", "arbitrary")),
    )(x1, x1, g, w_up, w_up, w_conv, w_conv, b_conv, b_conv, w_down)


def _ffn_s_kernel(x_ref, g_ref, wg_ref, wv_ref, cg_ref, cv_ref, bg_ref, bv_ref, wd_ref, prev_ref,
                  y_ref, new_ref, xn_scr, acc_scr, *, seq):
    j = pl.program_id(1)
    nb = x_ref.shape[0] // seq

    @pl.when(j == 0)
    def _():
        xn_scr[...] = _rms(x_ref[...], g_ref[...]).astype(BF16)
        acc_scr[...] = jnp.zeros_like(acc_scr)

    xn = xn_scr[...]

    def conv(u, half, wc_ref, bc_ref):
        ext = [prev_ref[r, half] for r in range(CONV_W - 1)] + [u[t * nb:(t + 1) * nb, :] for t in range(seq)]
        for r in range(CONV_W - 1):
            new_ref[r, half] = ext[seq + r]
        return jnp.concatenate(
            [bc_ref[...] + ((ext[t] * wc_ref[0:1, :] + ext[t + 1] * wc_ref[1:2, :]) + ext[t + 2] * wc_ref[2:3, :])
             for t in range(seq)], axis=0)

    cg = conv(_dot(xn, wg_ref[...]), 0, cg_ref, bg_ref)
    cv = conv(_dot(xn, wv_ref[...]), 1, cv_ref, bv_ref)
    act = (cg * jax.nn.sigmoid(cg) * cv).astype(BF16)
    acc_scr[...] += _dot(act, wd_ref[...])

    @pl.when(j == pl.num_programs(1) - 1)
    def _():
        y_ref[...] = x_ref[...] + acc_scr[...]


def conv_ffn_sample(x1_t, prev_t, g, w_up, w_conv, b_conv, w_down, bf, seq):
    rows, d = x1_t.shape
    f = w_down.shape[0]
    nj = f // bf
    nb = rows // seq
    hist = pl.BlockSpec((CONV_W - 1, 2, nb, bf), lambda i, j: (0, 0, 0, j))
    return pl.pallas_call(
        functools.partial(_ffn_s_kernel, seq=seq),
        grid=(1, nj),
        in_specs=[pl.BlockSpec((rows, d), lambda i, j: (0, 0))] + _ffn_weight_specs(d, bf, nj) + [hist],
        out_specs=[pl.BlockSpec((rows, d), lambda i, j: (0, 0)), hist],
        out_shape=[jax.ShapeDtypeStruct((rows, d), F32),
                   jax.ShapeDtypeStruct(prev_t.shape, F32)],
        scratch_shapes=[pltpu.VMEM((rows, d), BF16),
                        pltpu.VMEM((rows, d), F32)],
        compiler_params=_cparams(("parallel", "arbitrary")),
    )(x1_t, g, w_up, w_up, w_conv, w_conv, b_conv, b_conv, w_down, prev_t)


def _pick(n, prefs):
    for p in prefs:
        if n % p == 0:
            return p
    return n


def _layer(x_prompt, x_sample, cache_lat, cache_kr, c_state, n_state, m_state, conv_state, page_table,
           g_attn_norm, w_in, b_gates, g_q_a, w_uq, g_qk_nope_q, g_qk_rope_q, g_kv_a, w_uk, w_uv,
           g_qk_nope_k, g_qk_rope_k, g_mlstm_out, w_branch_a, w_branch_b, w_out, g_ffn_norm,
           w_up, w_conv, b_conv, w_down):
    bp, s, d = x_prompt.shape
    db, seq, _ = x_sample.shape
    n_pages = page_table.shape[1]
    page = cache_lat.shape[2]
    past = n_pages * page
    tp, ts = bp * s, db * seq
    t = tp + ts
    c_lat = g_kv_a.shape[0]
    c_q = g_q_a.shape[0]
    f = w_down.shape[0]
    qk_w = A_HEADS * A_DQK
    v_w = A_HEADS * A_DV
    row = lambda v: v.reshape(1, -1)

    o_i = 2 * qk_w + 2 * v_w
    o_cq = o_i + 2 * A_HEADS
    o_kr = o_cq + c_q + c_lat
    o_g = o_kr + B_ROPE
    w_main = jnp.concatenate([w_in[:, :o_i], w_in[:, o_g:]], axis=1).astype(BF16)
    n_small = LANES - B_ROPE - 2 * A_HEADS
    w_lat = jnp.concatenate([w_in[:, o_cq:o_kr], w_in[:, o_kr:o_g], w_in[:, o_i:o_cq],
                             jnp.zeros((d, n_small), F32)], axis=1).astype(BF16)
    w_uq_pad = jnp.pad(w_uq, ((0, 0), (0, 0), (0, Q_HEAD_PAD - B_QK))).reshape(c_q, B_HEADS * Q_HEAD_PAD).astype(BF16)
    w_uk2 = w_uk.reshape(c_lat, B_HEADS * B_NOPE).astype(BF16)
    w_uk_t = jnp.transpose(w_uk.reshape(c_lat, B_HEADS * B_NOPE)).astype(BF16)
    w_uv2 = w_uv.reshape(c_lat, B_HEADS * B_V).astype(BF16)
    gq = jnp.concatenate([g_qk_nope_q * g_qk_nope_k, g_qk_rope_q * g_qk_rope_k, g_qk_rope_q * g_qk_rope_k,
                          jnp.zeros((Q_HEAD_PAD - B_QK,), F32)]) * ATTN_SCALE
    gq = jnp.tile(gq, B_HEADS).reshape(1, -1)

    w_a_b, w_b_b, w_out_b = w_branch_a.astype(BF16), w_branch_b.astype(BF16), w_out.astype(BF16)
    w_up_b, w_down_b = w_up.astype(BF16), w_down.astype(BF16)
    half = B_ROPE // 2
    freqs = ROPE_THETA ** (-jnp.arange(half, dtype=F32) / half)

    def token_stage(x, pos, bm):
        rows = x.shape[0]
        ang = pos.astype(F32)[:, None] * freqs
        cos, sin = jnp.cos(ang), jnp.sin(ang)
        zh = jnp.zeros((rows, half), F32)
        zr = jnp.zeros((rows, LANES - B_ROPE), F32)
        cos128 = jnp.concatenate([cos, cos, zr], axis=1)
        sin_a = jnp.concatenate([zh, sin, zr], axis=1)
        sin_b = jnp.concatenate([-sin, zh, zr], axis=1)
        bn = _pick(d, (1024, 512, 256, 128))
        z_main = norm_matmul(x, row(g_attn_norm), w_main, bm, bn, BF16)
        z_lat = norm_matmul(x, row(g_attn_norm), w_lat, bm, w_lat.shape[1])
        bq = _pick(rows, (256, 128, 64, 32, 16, 8))
        q = q_proj(z_lat, row(g_q_a), w_uq_pad, cos128, sin_a, sin_b, gq, bq)
        c_kv, kr128, k, v = kv_proj(z_lat, row(g_kv_a), w_uk2, w_uv2, cos128, sin_a, sin_b, bq)
        g_t = jnp.transpose(z_lat[:, c_q + c_lat + B_ROPE:c_q + c_lat + B_ROPE + 2 * A_HEADS])
        gates = (g_t[:A_HEADS, None, :], g_t[A_HEADS:, None, :], g_t[:A_HEADS, :, None], g_t[A_HEADS:, :, None])
        return z_main, q, c_kv, kr128, k, v, gates

    def mix_stage(x, hid, z_main, h_b, bm):
        bn = _pick(d, (1024, 512, 256, 128))
        merged = merge(hid, z_main, h_b, row(g_mlstm_out), w_a_b, w_b_b, bm, bn)
        return resid_matmul(x, merged, w_out_b, _pick(x.shape[0], (2 * bm, bm)), bn)

    bff = _pick(f, (512, 256, 128))

    xp = x_prompt.reshape(tp, d)
    bm_p = _pick(tp, (1024, 512, 256, 128, 64, 32, 16, 8))
    zm_p, q_p, ckv_p, kr_p, k_p, v_p, gates_p = token_stage(xp, jnp.tile(jnp.arange(s, dtype=jnp.int32), bp), bm_p)
    chunk = _pick(s, (256, 128, 64, 32, 16, 8))
    hid_p, c_p, n_p, m_p = mlstm_prompt(zm_p, b_gates, *gates_p, bp, s, chunk)
    hb_p = attn_prompt(q_p, k_p, v_p, bp, s, _pick(s, (512, 256, 128)))
    x1_p = mix_stage(xp, hid_p, zm_p, hb_p, _pick(tp, (512, 256, 128, 64, 32, 16, 8)))
    bmf_p = _pick(s, (512, 256, 128, 64, 32, 16, 8))
    y_p, tg_p, tv_p = conv_ffn_prompt(x1_p, row(g_ffn_norm), w_up_b, w_conv, row(b_conv), w_down_b, bmf_p, bff, s)
    nb_seq = s // bmf_p
    tail_p = jnp.concatenate([tg_p, tv_p], axis=-1).reshape(bp, nb_seq, SUBLANES, 2 * f)
    conv_p = tail_p[:, nb_seq - 1, SUBLANES - (CONV_W - 1):, :]

    xs = x_sample.reshape(ts, d)
    bm_s = _pick(ts, (512, 256, 128, 64, 32, 16, 8))
    zm_s, q_s, ckv_s, kr_s, _, _, gates_s = token_stage(
        xs, jnp.tile(past + jnp.arange(seq, dtype=jnp.int32), db), bm_s)
    rb = _pick(ts, (128, 64, 32, 16, 8))
    m_col = jnp.repeat(jnp.transpose(m_state), seq, axis=1)[:, :, None]
    n_rows = jnp.repeat(jnp.transpose(n_state, (1, 0, 2)), seq, axis=1)
    hid_s, c_s, n_s_rows, m_s_rows = mlstm_sample(zm_s, b_gates, *gates_s, m_col, n_rows, c_state[None], rb, seq)
    n_s = jnp.transpose(n_s_rows[:, seq - 1::seq, :], (1, 0, 2))
    m_s = jnp.transpose(m_s_rows[:, seq - 1::seq, 0])

    q_abs = q_absorb(q_s, w_uk_t).reshape(db, seq * B_HEADS, c_lat)
    c_new16 = jnp.pad(ckv_s.reshape(db, seq, c_lat), ((0, 0), (0, BF16_ROWS - seq), (0, 0)))
    kr_new_t = jnp.pad(jnp.transpose(kr_s[:, :B_ROPE].reshape(db, seq, B_ROPE), (0, 2, 1)),
                       ((0, 0), (0, 0), (0, LANES - seq)))
    g_pages = _pick(n_pages, (32, 16, 8, 4, 2))
    o_lat = attn_sample(page_table, cache_lat, jnp.swapaxes(cache_kr, 2, 3), w_uk_t, q_abs,
                        q_s.reshape(db, seq * B_HEADS, Q_HEAD_PAD), c_new16, kr_new_t, g_pages, seq)
    hb_s = uv_expand(o_lat.reshape(ts, B_HEADS * c_lat), w_uv2)
    x1_s = mix_stage(xs, hid_s, zm_s, hb_s, bm_s)
    x1_t = jnp.transpose(x1_s.reshape(db, seq, d), (1, 0, 2)).reshape(ts, d)
    prev_t = jnp.transpose(conv_state.reshape(db, CONV_W - 1, 2, f), (1, 2, 0, 3))
    y_t, new_t = conv_ffn_sample(x1_t, prev_t, row(g_ffn_norm), w_up_b, w_conv, row(b_conv), w_down_b, bff, seq)
    y_s = jnp.transpose(y_t.reshape(seq, db, d), (1, 0, 2))
    conv_s = jnp.transpose(new_t, (2, 0, 1, 3)).reshape(db, CONV_W - 1, 2 * f)

    new_p = (ckv_p.reshape(bp, s, c_lat), kr_p[:, :B_ROPE].reshape(bp, s, B_ROPE),
             c_p, n_p[:, :, 0, :], m_p[:, :, 0, 0], conv_p)
    new_s = (ckv_s.reshape(db, seq, c_lat), kr_s[:, :B_ROPE].reshape(db, seq, B_ROPE),
             c_s[0], n_s, m_s, conv_s)
    return y_p.reshape(bp, s, d), y_s.reshape(db, seq, d), new_p, new_s


def kernel(x_prompt, x_sample, cache_kv_latent, cache_k_rope, state_mlstm_C, state_mlstm_n, state_mlstm_m, state_conv, page_table, g_attn_norm, w_in, b_gates, g_q_a, w_uq, g_qk_nope_q, g_qk_rope_q, g_kv_a, w_uk, w_uv, g_qk_nope_k, g_qk_rope_k, g_mlstm_out, w_branch_a, w_branch_b, w_out, g_ffn_norm, w_up, w_conv, b_conv, w_down):
    depth = w_in.shape[0]
    assert depth == 1, "single-layer trunk"
    l = 0
    y_p, y_s, new_p, new_s = _layer(
        x_prompt, x_sample, cache_kv_latent, cache_k_rope, state_mlstm_C[l], state_mlstm_n[l],
        state_mlstm_m[l], state_conv[l], page_table, g_attn_norm[l], w_in[l], b_gates[l], g_q_a[l], w_uq[l],
        g_qk_nope_q[l], g_qk_rope_q[l], g_kv_a[l], w_uk[l], w_uv[l], g_qk_nope_k[l], g_qk_rope_k[l],
        g_mlstm_out[l], w_branch_a[l], w_branch_b[l], w_out[l], g_ffn_norm[l], w_up[l], w_conv[l], b_conv[l],
        w_down[l])
    dts = (cache_kv_latent.dtype, cache_k_rope.dtype, state_mlstm_C.dtype, state_mlstm_n.dtype,
           state_mlstm_m.dtype, state_conv.dtype)
    st_p = tuple(a[None].astype(dt) for a, dt in zip(new_p, dts))
    st_s = tuple(a[None].astype(dt) for a, dt in zip(new_s, dts))
    return (y_p, y_s) + st_p + st_s
```

```python
import functools
import math

import jax
import jax.numpy as jnp
from jax import lax
from jax.experimental import pallas as pl
from jax.experimental.pallas import tpu as pltpu

F32 = jnp.float32
BF16 = jnp.bfloat16

A_HEADS = 8
A_DQK = 128
A_DV = 256
K_SCALE = A_DQK ** -0.5
GATE_CAP = 15.0
B_HEADS = 16
B_NOPE = 128
B_ROPE = 64
B_QK = B_NOPE + B_ROPE
B_V = 128
Q_HEAD_PAD = 256
ROPE_THETA = 10000.0
ATTN_SCALE = B_QK ** -0.5
CONV_W = 3
EPS = 1e-6
NEG_INF = float("-inf")

LANES = 128
SUBLANES = 8
MXU_COLS = 256
VMEM_LIMIT = 56 * 1024 * 1024


def _cparams(sem):
    return pltpu.CompilerParams(dimension_semantics=sem, vmem_limit_bytes=VMEM_LIMIT)


def _rms(x, g):
    r = lax.rsqrt(jnp.mean(x * x, axis=-1, keepdims=True) + EPS)
    return x * r * g


def _dot(a, b):
    return jnp.dot(a, b, preferred_element_type=F32)


def _dot_nt(a, b):
    return lax.dot_general(a, b, (((1,), (1,)), ((), ())), preferred_element_type=F32)


def _rope128(x, cos, sin_a, sin_b):
    return x * cos + pltpu.roll(x, 32, 1) * sin_a + pltpu.roll(x, 96, 1) * sin_b


def _norm_mm_kernel(x_ref, g_ref, w_ref, o_ref, xn_ref):
    @pl.when(pl.program_id(1) == 0)
    def _():
        xn_ref[...] = _rms(x_ref[...], g_ref[...]).astype(BF16)

    o_ref[...] = _dot(xn_ref[...], w_ref[...]).astype(o_ref.dtype)


def norm_matmul(x, g, w, bm, bn, out_dtype=F32):
    t, d = x.shape
    n = w.shape[1]
    return pl.pallas_call(
        _norm_mm_kernel,
        grid=(t // bm, n // bn),
        in_specs=[pl.BlockSpec((bm, d), lambda i, j: (i, 0)),
                  pl.BlockSpec((1, d), lambda i, j: (0, 0)),
                  pl.BlockSpec((d, bn), lambda i, j: (0, j))],
        out_specs=pl.BlockSpec((bm, bn), lambda i, j: (i, j)),
        out_shape=jax.ShapeDtypeStruct((t, n), out_dtype),
        scratch_shapes=[pltpu.VMEM((bm, d), BF16)],
        compiler_params=_cparams(("parallel", "arbitrary")),
    )(x, g, w)


def _q_kernel(cq_ref, g_ref, w_ref, cos_ref, sa_ref, sb_ref, gain_ref, o_ref):
    cqn = _rms(cq_ref[...], g_ref[...]).astype(BF16)
    q = _dot(cqn, w_ref[...])
    cos, sa, sb = cos_ref[...], sa_ref[...], sb_ref[...]
    for h in range(B_HEADS):
        lo = h * Q_HEAD_PAD
        nope = q[:, lo:lo + LANES]
        rp = _rope128(q[:, lo + LANES:lo + Q_HEAD_PAD], cos, sa, sb)
        ssq = jnp.sum(nope * nope, axis=-1, keepdims=True) + jnp.sum(rp * rp, axis=-1, keepdims=True)
        r = lax.rsqrt(ssq * (1.0 / B_QK) + EPS)
        o_ref[:, lo:lo + LANES] = (nope * r * gain_ref[:, lo:lo + LANES]).astype(o_ref.dtype)
        o_ref[:, lo + LANES:lo + Q_HEAD_PAD] = (rp * r * gain_ref[:, lo + LANES:lo + Q_HEAD_PAD]).astype(o_ref.dtype)


def q_proj(z_lat, g_q_a, w_uq_pad, cos, sa, sb, gain, bm):
    t = z_lat.shape[0]
    c = g_q_a.shape[1]
    n = w_uq_pad.shape[1]
    return pl.pallas_call(
        _q_kernel,
        grid=(t // bm,),
        in_specs=[pl.BlockSpec((bm, c), lambda i: (i, 0)),
                  pl.BlockSpec((1, c), lambda i: (0, 0)),
                  pl.BlockSpec((c, n), lambda i: (0, 0)),
                  pl.BlockSpec((bm, LANES), lambda i: (i, 0)),
                  pl.BlockSpec((bm, LANES), lambda i: (i, 0)),
                  pl.BlockSpec((bm, LANES), lambda i: (i, 0)),
                  pl.BlockSpec((1, n), lambda i: (0, 0))],
        out_specs=pl.BlockSpec((bm, n), lambda i: (i, 0)),
        out_shape=jax.ShapeDtypeStruct((t, n), BF16),
        compiler_params=_cparams(("parallel",)),
    )(z_lat, g_q_a, w_uq_pad, cos, sa, sb, gain)


def _kv_kernel(ckv_ref, sm_ref, g_ref, wuk_ref, wuv_ref, cos_ref, sa_ref, sb_ref,
               c_ref, kr_ref, k_ref, v_ref):
    c = _rms(ckv_ref[...], g_ref[...])
    c_ref[...] = c
    cb = c.astype(BF16)
    kn = _dot(cb, wuk_ref[...])
    v_ref[...] = _dot(cb, wuv_ref[...]).astype(v_ref.dtype)
    kr = _rope128(sm_ref[...], cos_ref[...], sa_ref[...], sb_ref[...])
    kr_ref[...] = kr
    ssr = jnp.sum(kr * kr, axis=-1, keepdims=True)
    for h in range(B_HEADS):
        nope = kn[:, h * B_NOPE:(h + 1) * B_NOPE]
        r = lax.rsqrt((jnp.sum(nope * nope, axis=-1, keepdims=True) + ssr) * (1.0 / B_QK) + EPS)
        lo = h * Q_HEAD_PAD
        k_ref[:, lo:lo + LANES] = (nope * r).astype(k_ref.dtype)
        k_ref[:, lo + LANES:lo + Q_HEAD_PAD] = (kr * r).astype(k_ref.dtype)


def kv_proj(z_lat, g_kv_a, w_uk, w_uv, cos, sa, sb, bm):
    t = z_lat.shape[0]
    c = g_kv_a.shape[1]
    small_blk = (2 * c) // LANES
    row = lambda i: (i, 0)
    const = lambda i: (0, 0)
    return pl.pallas_call(
        _kv_kernel,
        grid=(t // bm,),
        in_specs=[pl.BlockSpec((bm, c), lambda i: (i, 1)),
                  pl.BlockSpec((bm, LANES), lambda i: (i, small_blk)),
                  pl.BlockSpec((1, c), const),
                  pl.BlockSpec(w_uk.shape, const),
                  pl.BlockSpec(w_uv.shape, const),
                  pl.BlockSpec((bm, LANES), row),
                  pl.BlockSpec((bm, LANES), row),
                  pl.BlockSpec((bm, LANES), row)],
        out_specs=[pl.BlockSpec((bm, c), row),
                   pl.BlockSpec((bm, LANES), row),
                   pl.BlockSpec((bm, B_HEADS * Q_HEAD_PAD), row),
                   pl.BlockSpec((bm, B_HEADS * B_V), row)],
        out_shape=[jax.ShapeDtypeStruct((t, c), F32),
                   jax.ShapeDtypeStruct((t, LANES), F32),
                   jax.ShapeDtypeStruct((t, B_HEADS * Q_HEAD_PAD), BF16),
                   jax.ShapeDtypeStruct((t, B_HEADS * B_V), BF16)],
        compiler_params=_cparams(("parallel",)),
    )(z_lat, z_lat, g_kv_a, w_uk, w_uv, cos, sa, sb)


def _log_gates(gi, gf, bi, bf):
    cap = lambda x: GATE_CAP * jnp.tanh(x * (1.0 / GATE_CAP))
    li = cap(gi + bi)
    y = cap(gf + bf)
    lf = jnp.minimum(y, 0.0) - jnp.log(1.0 + jnp.exp(-jnp.abs(y)))
    return li, lf


def _mlstm_p_kernel(b_ref, q_ref, k_ref, v_ref, gir_ref, gfr_ref, gic_ref, gfc_ref,
                    hid_ref, c_out_ref, n_out_ref, m_out_ref, c_scr, n_scr, m_scr):
    ci = pl.program_id(1)
    L = q_ref.shape[0]

    @pl.when(ci == 0)
    def _():
        c_scr[...] = jnp.zeros_like(c_scr)
        n_scr[...] = jnp.zeros_like(n_scr)
        m_scr[...] = jnp.zeros_like(m_scr)

    t_idx = lax.broadcasted_iota(jnp.int32, (L, L), 0)
    s_idx = lax.broadcasted_iota(jnp.int32, (L, L), 1)
    causal = s_idx <= t_idx
    anti = t_idx <= s_idx
    for h in range(A_HEADS):
        bi = b_ref[h]
        bf = b_ref[A_HEADS + h]
        qb = q_ref[:, h * A_DQK:(h + 1) * A_DQK]
        kb = k_ref[:, h * A_DQK:(h + 1) * A_DQK]
        vb = v_ref[:, h * A_DV:(h + 1) * A_DV]
        q = qb.astype(F32)
        li_r, lf_r = _log_gates(gir_ref[h], gfr_ref[h], bi, bf)
        li_c, lf_c = _log_gates(gic_ref[h], gfc_ref[h], bi, bf)

        bcum_c = jnp.sum(jnp.where(causal, lf_r, 0.0), axis=1, keepdims=True)
        bcum_r = jnp.sum(jnp.where(anti, lf_c, 0.0), axis=0, keepdims=True)
        log_w = jnp.where(causal, bcum_c - bcum_r + li_r, NEG_INF)
        m_prev = m_scr[h][:, 0:1]
        c_prev = c_scr[h]
        n_prev = n_scr[h]
        log_state = bcum_c + m_prev
        m_row = jnp.maximum(log_state, jnp.max(log_w, axis=1, keepdims=True))
        w = jnp.exp(log_w - m_row) * K_SCALE
        w_state = jnp.exp(log_state - m_row)
        sc = _dot_nt(qb, kb) * w
        num = _dot(sc.astype(BF16), vb) + w_state * _dot(qb, c_prev.astype(BF16))
        den = jnp.sum(sc, axis=1, keepdims=True) + w_state * jnp.sum(q * n_prev, axis=1, keepdims=True)
        hid_ref[:, h * A_DV:(h + 1) * A_DV] = (
            num / jnp.maximum(jnp.abs(den), jnp.exp(-m_row))).astype(hid_ref.dtype)

        b_last = bcum_c[L - 1:L, :]
        log_k = b_last - bcum_c + li_c
        m_new = jnp.maximum(b_last + m_prev, jnp.max(log_k, axis=0, keepdims=True))
        wk = jnp.exp(log_k - m_new) * K_SCALE
        decay = jnp.exp(b_last + m_prev - m_new)
        kw = kb.astype(F32) * wk
        kwt = jnp.transpose(kw).astype(BF16)
        c_scr[h] = decay * c_prev + _dot(kwt, vb)
        n_scr[h] = decay * n_prev + jnp.sum(kw, axis=0, keepdims=True)
        m_scr[h] = jnp.broadcast_to(m_new, (1, LANES))

    @pl.when(ci == pl.num_programs(1) - 1)
    def _():
        c_out_ref[0] = c_scr[...]
        n_out_ref[0] = n_scr[...]
        m_out_ref[0] = m_scr[...]


def mlstm_prompt(z_main, b_gates, gi_r, gf_r, gi_c, gf_c, bp, s, chunk):
    nc = s // chunk
    qk_w, v_w = A_HEADS * A_DQK, A_HEADS * A_DV
    assert (2 * qk_w) % v_w == 0
    tok = lambda b, c, *_: b * nc + c
    return pl.pallas_call(
        _mlstm_p_kernel,
        grid_spec=pltpu.PrefetchScalarGridSpec(
            num_scalar_prefetch=1,
            grid=(bp, nc),
            in_specs=[pl.BlockSpec((chunk, qk_w), lambda b, c, *_: (tok(b, c), 0)),
                      pl.BlockSpec((chunk, qk_w), lambda b, c, *_: (tok(b, c), 1)),
                      pl.BlockSpec((chunk, v_w), lambda b, c, *_: (tok(b, c), (2 * qk_w) // v_w)),
                      pl.BlockSpec((A_HEADS, 1, chunk), lambda b, c, *_: (0, 0, tok(b, c))),
                      pl.BlockSpec((A_HEADS, 1, chunk), lambda b, c, *_: (0, 0, tok(b, c))),
                      pl.BlockSpec((A_HEADS, chunk, 1), lambda b, c, *_: (0, tok(b, c), 0)),
                      pl.BlockSpec((A_HEADS, chunk, 1), lambda b, c, *_: (0, tok(b, c), 0))],
            out_specs=[pl.BlockSpec((chunk, v_w), lambda b, c, *_: (tok(b, c), 0)),
                       pl.BlockSpec((1, A_HEADS, A_DQK, A_DV), lambda b, c, *_: (b, 0, 0, 0)),
                       pl.BlockSpec((1, A_HEADS, 1, A_DQK), lambda b, c, *_: (b, 0, 0, 0)),
                       pl.BlockSpec((1, A_HEADS, 1, LANES), lambda b, c, *_: (b, 0, 0, 0))],
            scratch_shapes=[pltpu.VMEM((A_HEADS, A_DQK, A_DV), F32),
                            pltpu.VMEM((A_HEADS, 1, A_DQK), F32),
                            pltpu.VMEM((A_HEADS, 1, LANES), F32)]),
        out_shape=[jax.ShapeDtypeStruct((bp * s, v_w), BF16),
                   jax.ShapeDtypeStruct((bp, A_HEADS, A_DQK, A_DV), F32),
                   jax.ShapeDtypeStruct((bp, A_HEADS, 1, A_DQK), F32),
                   jax.ShapeDtypeStruct((bp, A_HEADS, 1, LANES), F32)],
        compiler_params=_cparams(("parallel", "arbitrary")),
    )(b_gates, z_main, z_main, z_main, gi_r, gf_r, gi_c, gf_c)


def _mlstm_s_kernel(b_ref, q_ref, k_ref, v_ref, gir_ref, gfr_ref, gic_ref, gfc_ref,
                    mc_ref, nrow_ref, c_in_ref,
                    hid_ref, c_out_ref, n_out_ref, m_out_ref,
                    hc_scr, dec_scr, kwt_scr, q_scr, *, seq):
    h = pl.program_id(1)
    R = q_ref.shape[0]
    nseq = R // seq
    per8 = SUBLANES // seq
    bi = b_ref[h]
    bf = b_ref[A_HEADS + h]
    q = q_ref[...].astype(F32)
    k = k_ref[...].astype(F32)
    v = v_ref[...].astype(F32)
    q_scr[...] = q
    li_r, lf_r = _log_gates(gir_ref[0], gfr_ref[0], bi, bf)
    li_c, lf_c = _log_gates(gic_ref[0], gfc_ref[0], bi, bf)
    m_prev = mc_ref[0]

    t_idx = lax.broadcasted_iota(jnp.int32, (R, R), 0)
    s_idx = lax.broadcasted_iota(jnp.int32, (R, R), 1)
    same = (t_idx // seq) == (s_idx // seq)
    causal = same & (s_idx <= t_idx)
    bcum_c = jnp.sum(jnp.where(causal, lf_r, 0.0), axis=1, keepdims=True)
    bcum_r = jnp.sum(jnp.where(same & (t_idx <= s_idx), lf_c, 0.0), axis=0, keepdims=True)
    blast_c = jnp.sum(jnp.where(same, lf_r, 0.0), axis=1, keepdims=True)
    blast_r = jnp.sum(jnp.where(same, lf_c, 0.0), axis=0, keepdims=True)
    log_w = jnp.where(causal, bcum_c - bcum_r + li_r, NEG_INF)
    log_state = bcum_c + m_prev
    m_row = jnp.maximum(log_state, jnp.max(log_w, axis=1, keepdims=True))
    w = jnp.exp(log_w - m_row) * K_SCALE
    w_state = jnp.exp(log_state - m_row)

    log_k_c = blast_c - bcum_c + li_c
    log_k_r = blast_r - bcum_r + li_r
    seg_max = jnp.max(jnp.where(same, log_k_r, NEG_INF), axis=1, keepdims=True)
    m_new = jnp.maximum(blast_c + m_prev, seg_max)
    wk = jnp.exp(log_k_c - m_new) * K_SCALE
    decay = jnp.exp(blast_c + m_prev - m_new)
    kw = k * wk
    kwt_scr[...] = jnp.transpose(kw)
    dec_scr[...] = jnp.broadcast_to(decay, dec_scr.shape)

    row8 = lax.broadcasted_iota(jnp.int32, (SUBLANES, A_DV), 0)
    col = lax.broadcasted_iota(jnp.int32, (A_DQK, R), 1)

    def group(gidx, carry):
        r0 = pl.multiple_of(gidx * SUBLANES, SUBLANES)
        q8 = q_scr[pl.ds(r0, SUBLANES), :]
        hc8 = jnp.zeros((SUBLANES, A_DV), F32)
        for u in range(per8):
            sq = gidx * per8 + u
            c0 = c_in_ref[0, sq, 0]
            res = _dot(q8, c0)
            hc8 = jnp.where((row8 // seq) == u, res, hc8)
            kwt_m = jnp.where((col // seq) == sq, kwt_scr[...], 0.0)
            dsc = dec_scr[pl.ds(sq * seq, 1), 0:1]
            c_out_ref[0, sq, 0] = dsc * c0 + _dot(kwt_m, v)
        hc_scr[pl.ds(r0, SUBLANES), :] = hc8
        return carry

    lax.fori_loop(0, R // SUBLANES, group, 0)

    sc = _dot_nt(q, k) * w
    num = _dot(sc, v) + w_state * hc_scr[...]
    den = jnp.sum(sc, axis=1, keepdims=True) + w_state * jnp.sum(q * nrow_ref[0], axis=1, keepdims=True)
    hid_ref[...] = (num / jnp.maximum(jnp.abs(den), jnp.exp(-m_row))).astype(hid_ref.dtype)

    acc = kw
    shift = 1
    while shift < seq:
        acc = acc + jnp.where((lax.broadcasted_iota(jnp.int32, acc.shape, 0) % seq) >= shift,
                              pltpu.roll(acc, shift, 0), 0.0)
        shift *= 2
    n_out_ref[0] = decay * nrow_ref[0] + acc
    m_out_ref[0] = jnp.broadcast_to(m_new, (R, LANES))


def mlstm_sample(z_main_s, b_gates, gi_r, gf_r, gi_c, gf_c, m_col, n_rows, c_state, rb, seq):
    ts = z_main_s.shape[0]
    nseq_blk = rb // seq
    kcol = (A_HEADS * A_DQK) // A_DQK
    vcol = (2 * A_HEADS * A_DQK) // A_DV
    return pl.pallas_call(
        functools.partial(_mlstm_s_kernel, seq=seq),
        grid_spec=pltpu.PrefetchScalarGridSpec(
            num_scalar_prefetch=1,
            grid=(ts // rb, A_HEADS),
            in_specs=[pl.BlockSpec((rb, A_DQK), lambda i, h, *_: (i, h)),
                      pl.BlockSpec((rb, A_DQK), lambda i, h, *_: (i, kcol + h)),
                      pl.BlockSpec((rb, A_DV), lambda i, h, *_: (i, vcol + h)),
                      pl.BlockSpec((1, 1, rb), lambda i, h, *_: (h, 0, i)),
                      pl.BlockSpec((1, 1, rb), lambda i, h, *_: (h, 0, i)),
                      pl.BlockSpec((1, rb, 1), lambda i, h, *_: (h, i, 0)),
                      pl.BlockSpec((1, rb, 1), lambda i, h, *_: (h, i, 0)),
                      pl.BlockSpec((1, rb, 1), lambda i, h, *_: (h, i, 0)),
                      pl.BlockSpec((1, rb, A_DQK), lambda i, h, *_: (h, i, 0)),
                      pl.BlockSpec((1, nseq_blk, 1, A_DQK, A_DV), lambda i, h, *_: (0, i, h, 0, 0))],
            out_specs=[pl.BlockSpec((rb, A_DV), lambda i, h, *_: (i, h)),
                       pl.BlockSpec((1, nseq_blk, 1, A_DQK, A_DV), lambda i, h, *_: (0, i, h, 0, 0)),
                       pl.BlockSpec((1, rb, A_DQK), lambda i, h, *_: (h, i, 0)),
                       pl.BlockSpec((1, rb, LANES), lambda i, h, *_: (h, i, 0))],
            scratch_shapes=[pltpu.VMEM((rb, A_DV), F32),
                            pltpu.VMEM((rb, LANES), F32),
                            pltpu.VMEM((A_DQK, rb), F32),
                            pltpu.VMEM((rb, A_DQK), F32)]),
        out_shape=[jax.ShapeDtypeStruct((ts, A_HEADS * A_DV), BF16),
                   jax.ShapeDtypeStruct(c_state.shape, F32),
                   jax.ShapeDtypeStruct((A_HEADS, ts, A_DQK), F32),
                   jax.ShapeDtypeStruct((A_HEADS, ts, LANES), F32)],
        compiler_params=_cparams(("parallel", "parallel")),
    )(b_gates, z_main_s, z_main_s, z_main_s, gi_r, gf_r, gi_c, gf_c, m_col, n_rows, c_state)


def _attn_p_kernel(q_ref, k_ref, v_ref, o_ref, *, tq, heads):
    s = q_ref.shape[0]
    lower = lax.broadcasted_iota(jnp.int32, (tq, tq), 1) <= lax.broadcasted_iota(jnp.int32, (tq, tq), 0)
    for hh in range(heads):
        qk = slice(hh * Q_HEAD_PAD, (hh + 1) * Q_HEAD_PAD)
        vv = slice(hh * B_V, (hh + 1) * B_V)
        for qt in range(s // tq):
            lo, hi = qt * tq, (qt + 1) * tq
            q = q_ref[lo:hi, qk]
            sd = jnp.where(lower, _dot_nt(q, k_ref[lo:hi, qk]), NEG_INF)
            m = jnp.max(sd, axis=-1, keepdims=True)
            if qt:
                sp = _dot_nt(q, k_ref[0:lo, qk])
                m = jnp.maximum(m, jnp.max(sp, axis=-1, keepdims=True))
            pd = jnp.exp(sd - m)
            l = jnp.sum(pd, axis=-1, keepdims=True)
            o = _dot(pd.astype(BF16), v_ref[lo:hi, vv])
            if qt:
                pp = jnp.exp(sp - m)
                l = l + jnp.sum(pp, axis=-1, keepdims=True)
                o = o + _dot(pp.astype(BF16), v_ref[0:lo, vv])
            o_ref[lo:hi, vv] = (o / l).astype(o_ref.dtype)


def attn_prompt(q, k, v, bp, s, tq, heads=2):
    return pl.pallas_call(
        functools.partial(_attn_p_kernel, tq=tq, heads=heads),
        grid=(bp, B_HEADS // heads),
        in_specs=[pl.BlockSpec((s, heads * Q_HEAD_PAD), lambda b, h: (b, h)),
                  pl.BlockSpec((s, heads * Q_HEAD_PAD), lambda b, h: (b, h)),
                  pl.BlockSpec((s, heads * B_V), lambda b, h: (b, h))],
        out_specs=pl.BlockSpec((s, heads * B_V), lambda b, h: (b, h)),
        out_shape=jax.ShapeDtypeStruct((bp * s, B_HEADS * B_V), BF16),
        compiler_params=_cparams(("parallel", "parallel")),
    )(q, k, v)


def _q_absorb_kernel(q_ref, w_ref, o_ref):
    o_ref[...] = _dot(q_ref[...], w_ref[...]).astype(o_ref.dtype)


def q_absorb(q_s, w_uk_t):
    ts = q_s.shape[0]
    c = w_uk_t.shape[1]
    return pl.pallas_call(
        _q_absorb_kernel,
        grid=(B_HEADS,),
        in_specs=[pl.BlockSpec((ts, B_NOPE), lambda h: (0, 2 * h)),
                  pl.BlockSpec((B_NOPE, c), lambda h: (h, 0))],
        out_specs=pl.BlockSpec((ts, c), lambda h: (0, h)),
        out_shape=jax.ShapeDtypeStruct((ts, B_HEADS * c), BF16),
        compiler_params=_cparams(("parallel",)),
    )(q_s, w_uk_t)


def _attn_s_kernel(pt_ref, *refs, g_pages, page, kblk, seq):
    (lat_hbm, kr_hbm, wukt_ref, qabs_ref, q_ref, cnew_ref, knew_ref, o_ref,
     wcat_scr, lat_scr, m_scr, l_scr, acc_scr, lat_buf, kr_buf, lat_sem, kr_sem) = refs
    b = pl.program_id(0)
    j = pl.program_id(1)
    nj = pl.num_programs(1)
    nq = qabs_ref.shape[1]
    nkey = wukt_ref.shape[0]
    c = wukt_ref.shape[1]
    per_blk = kblk // page

    step = b * nj + j
    slot = step % 2

    def page_copies(bb, jj, sl, g):
        pg = pt_ref[bb, jj * g_pages + g]
        return (pltpu.make_async_copy(lat_hbm.at[0, pg], lat_buf.at[sl, g], lat_sem.at[sl]),
                pltpu.make_async_copy(kr_hbm.at[0, pg], kr_buf.at[sl, g], kr_sem.at[sl]))

    def start_pages(bb, jj, sl):
        for g in range(g_pages):
            for cp in page_copies(bb, jj, sl, g):
                cp.start()

    @pl.when(step == 0)
    def _():
        start_pages(0, 0, 0)
        wcat_scr[0:nkey, :] = wukt_ref[...]

    @pl.when(step + 1 < pl.num_programs(0) * nj)
    def _():
        nxt = step + 1
        start_pages(nxt // nj, nxt % nj, 1 - slot)

    for g in range(g_pages):
        for cp in page_copies(b, j, slot, g):
            cp.wait()
    lat_refs = [lat_buf.at[slot, g] for g in range(g_pages)]
    kr_refs = [kr_buf.at[slot, g] for g in range(g_pages)]

    @pl.when(j == 0)
    def _():
        wcat_scr[nkey:nkey + nq, :] = qabs_ref[0]
        m_scr[...] = jnp.full(m_scr.shape, NEG_INF, F32)
        l_scr[...] = jnp.zeros_like(l_scr)
        acc_scr[...] = jnp.zeros_like(acc_scr)

    qrope = q_ref[0][:, LANES:LANES + B_ROPE]

    def scores(lb, krt):
        a = _dot_nt(wcat_scr[...], lb)
        kt = a[0:nkey]
        ssq = jnp.sum((kt * kt).reshape(B_HEADS, B_NOPE, kblk), axis=1)
        ssr = jnp.sum(krt * krt, axis=0, keepdims=True)
        r = lax.rsqrt((ssq + ssr) * (1.0 / B_QK) + EPS)
        return (a[nkey:nkey + nq] + _dot(qrope, krt.astype(BF16))) * jnp.concatenate([r] * seq, axis=0)

    def fold(s, lat, carry):
        m, l, acc = carry
        m_new = jnp.maximum(m, jnp.max(s, axis=-1, keepdims=True))
        alpha = jnp.exp(m - m_new)
        p = jnp.exp(s - m_new)
        return m_new, alpha * l + jnp.sum(p, axis=-1, keepdims=True), alpha * acc + _dot(p.astype(BF16), lat)

    s_blocks = []
    for sb in range(g_pages // per_blk):
        pages = range(sb * per_blk, (sb + 1) * per_blk)
        lb = jnp.concatenate([lat_refs[g][...].astype(BF16) for g in pages], axis=0)
        krt = jnp.concatenate([kr_refs[g][...] for g in pages], axis=1)
        lat_scr[sb * kblk:(sb + 1) * kblk, :] = lb
        s_blocks.append(scores(lb, krt))
    carry = fold(jnp.concatenate(s_blocks, axis=1), lat_scr[...], (m_scr[:, 0:1], l_scr[:, 0:1], acc_scr[...]))
    m, l, acc = carry
    m_scr[...] = jnp.broadcast_to(m, m_scr.shape)
    l_scr[...] = jnp.broadcast_to(l, l_scr.shape)
    acc_scr[...] = acc

    @pl.when(j == pl.num_programs(1) - 1)
    def _():
        qf = q_ref[0].astype(F32)
        t_of_row = lax.broadcasted_iota(jnp.int32, (nq, 1), 0) // B_HEADS
        cols = []
        for jn in range(seq):
            kj = knew_ref[0, jn * B_HEADS:(jn + 1) * B_HEADS, :].astype(F32)
            sj = jnp.sum(qf * jnp.concatenate([kj] * seq, axis=0), axis=-1, keepdims=True)
            cols.append(jnp.where(t_of_row >= jn, sj, NEG_INF))
        m2 = m
        for sj in cols:
            m2 = jnp.maximum(m2, sj)
        alpha = jnp.exp(m - m2)
        l2 = alpha * l
        acc2 = alpha * acc
        for jn, sj in enumerate(cols):
            pj = jnp.exp(sj - m2)
            l2 = l2 + pj
            acc2 = acc2 + pj * cnew_ref[0, jn:jn + 1, :]
        o_ref[0] = acc2 / l2


def attn_sample(page_table, cache_lat, cache_kr_t, w_uk_t, q_abs, q_s, c_new, k_new, g_pages, seq):
    db, n_pages = page_table.shape
    page, c = cache_lat.shape[2], cache_lat.shape[3]
    kblk = MXU_COLS
    assert kblk % page == 0 and g_pages % (kblk // page) == 0 and n_pages % g_pages == 0
    nq = q_abs.shape[1]
    nkey = w_uk_t.shape[0]
    per_b = lambda b, j, pt: (b, 0, 0)
    return pl.pallas_call(
        functools.partial(_attn_s_kernel, g_pages=g_pages, page=page, kblk=kblk, seq=seq),
        grid_spec=pltpu.PrefetchScalarGridSpec(
            num_scalar_prefetch=1,
            grid=(db, n_pages // g_pages),
            in_specs=[
                pl.BlockSpec(memory_space=pl.ANY),
                pl.BlockSpec(memory_space=pl.ANY),
                pl.BlockSpec(w_uk_t.shape, lambda b, j, pt: (0, 0)),
                pl.BlockSpec((1, nq, c), per_b),
                pl.BlockSpec((1, nq, Q_HEAD_PAD), per_b),
                pl.BlockSpec((1, seq, c), per_b),
                pl.BlockSpec((1, nq, Q_HEAD_PAD), per_b)],
            out_specs=pl.BlockSpec((1, nq, c), per_b),
            scratch_shapes=[pltpu.VMEM((nkey + nq, c), BF16),
                            pltpu.VMEM((g_pages * page, c), BF16),
                            pltpu.VMEM((nq, LANES), F32),
                            pltpu.VMEM((nq, LANES), F32),
                            pltpu.VMEM((nq, c), F32),
                            pltpu.VMEM((2, g_pages, page, c), F32),
                            pltpu.VMEM((2, g_pages, B_ROPE, page), F32),
                            pltpu.SemaphoreType.DMA((2,)),
                            pltpu.SemaphoreType.DMA((2,))]),
        out_shape=jax.ShapeDtypeStruct((db, nq, c), F32),
        compiler_params=_cparams(("arbitrary", "arbitrary")),
    )(page_table, cache_lat, cache_kr_t, w_uk_t, q_abs, q_s, c_new, k_new)


def _uv_kernel(o_ref, w_ref, h_ref):
    h_ref[...] = _dot(o_ref[...].astype(BF16), w_ref[...]).astype(h_ref.dtype)


def uv_expand(o_lat2, w_uv):
    ts = o_lat2.shape[0]
    c = w_uv.shape[0]
    return pl.pallas_call(
        _uv_kernel,
        grid=(B_HEADS,),
        in_specs=[pl.BlockSpec((ts, c), lambda h: (0, h)),
                  pl.BlockSpec((c, B_V), lambda h: (0, h))],
        out_specs=pl.BlockSpec((ts, B_V), lambda h: (0, h)),
        out_shape=jax.ShapeDtypeStruct((ts, B_HEADS * B_V), BF16),
        compiler_params=_cparams(("parallel",)),
    )(o_lat2, w_uv)


def _merge_kernel(hid_ref, oa_ref, ga_ref, gb_ref, hb_ref, g_ref, wa_ref, wb_ref, o_ref, ha_scr):
    @pl.when(pl.program_id(1) == 0)
    def _():
        for h in range(A_HEADS):
            sl = slice(h * A_DV, (h + 1) * A_DV)
            gate = jax.nn.sigmoid(oa_ref[:, sl].astype(F32))
            ha_scr[:, sl] = (_rms(hid_ref[:, sl].astype(F32), g_ref[:, sl]) * gate).astype(BF16)

    a = _dot(ha_scr[...], wa_ref[...])
    b = _dot(hb_ref[...], wb_ref[...])
    o_ref[...] = (jax.nn.sigmoid(ga_ref[...].astype(F32)) * a
                  + jax.nn.sigmoid(gb_ref[...].astype(F32)) * b).astype(o_ref.dtype)


def merge(hid, z_main, h_b, g_out, w_a, w_b, bm, bn):
    t, d = hid.shape
    nj = d // bn
    return pl.pallas_call(
        _merge_kernel,
        grid=(t // bm, nj),
        in_specs=[pl.BlockSpec((bm, d), lambda i, j: (i, 0)),
                  pl.BlockSpec((bm, d), lambda i, j: (i, 2)),
                  pl.BlockSpec((bm, bn), lambda i, j: (i, 3 * nj + j)),
                  pl.BlockSpec((bm, bn), lambda i, j: (i, 4 * nj + j)),
                  pl.BlockSpec((bm, d), lambda i, j: (i, 0)),
                  pl.BlockSpec((1, d), lambda i, j: (0, 0)),
                  pl.BlockSpec((d, bn), lambda i, j: (0, j)),
                  pl.BlockSpec((d, bn), lambda i, j: (0, j))],
        out_specs=pl.BlockSpec((bm, bn), lambda i, j: (i, j)),
        out_shape=jax.ShapeDtypeStruct((t, d), BF16),
        scratch_shapes=[pltpu.VMEM((bm, d), BF16)],
        compiler_params=_cparams(("parallel", "arbitrary")),
    )(hid, z_main, z_main, z_main, h_b, g_out, w_a, w_b)


def _resid_mm_kernel(x_ref, a_ref, w_ref, o_ref):
    o_ref[...] = x_ref[...] + _dot(a_ref[...], w_ref[...])


def resid_matmul(x, a, w, bm, bn):
    t, d = x.shape
    k = a.shape[1]
    return pl.pallas_call(
        _resid_mm_kernel,
        grid=(t // bm, d // bn),
        in_specs=[pl.BlockSpec((bm, bn), lambda i, j: (i, j)),
                  pl.BlockSpec((bm, k), lambda i, j: (i, 0)),
                  pl.BlockSpec((k, bn), lambda i, j: (0, j))],
        out_specs=pl.BlockSpec((bm, bn), lambda i, j: (i, j)),
        out_shape=jax.ShapeDtypeStruct((t, d), F32),
        compiler_params=_cparams(("parallel", "parallel")),
    )(x, a, w)


def _ffn_p_kernel(x_ref, xh_ref, g_ref, wg_ref, wv_ref, cg_ref, cv_ref, bg_ref, bv_ref, wd_ref,
                  y_ref, ug_ref, uv_ref, xn_scr, acc_scr, *, seq_len):
    i = pl.program_id(0)
    j = pl.program_id(1)
    bm = x_ref.shape[0]
    hp = SUBLANES

    @pl.when(j == 0)
    def _():
        xn_scr[hp:, :] = _rms(x_ref[...], g_ref[...]).astype(BF16)
        xn_scr[0:hp, :] = _rms(xh_ref[...], g_ref[...]).astype(BF16)
        acc_scr[...] = jnp.zeros_like(acc_scr)

    xall = xn_scr[...]
    pos = (lax.broadcasted_iota(jnp.int32, (bm, 1), 0) + i * bm) % seq_len

    def conv(u, wc_ref, bc_ref):
        e2 = u[hp:, :]
        e1 = jnp.where(pos >= 1, u[hp - 1:hp - 1 + bm, :], 0.0)
        e0 = jnp.where(pos >= 2, u[hp - 2:hp - 2 + bm, :], 0.0)
        return bc_ref[...] + ((e0 * wc_ref[0:1, :] + e1 * wc_ref[1:2, :]) + e2 * wc_ref[2:3, :])

    ug = _dot(xall, wg_ref[...])
    uv = _dot(xall, wv_ref[...])
    ug_ref[0] = ug[bm:, :]
    uv_ref[0] = uv[bm:, :]
    cg = conv(ug, cg_ref, bg_ref)
    cv = conv(uv, cv_ref, bv_ref)
    act = (cg * jax.nn.sigmoid(cg) * cv).astype(BF16)
    acc_scr[...] += _dot(act, wd_ref[...])

    @pl.when(j == pl.num_programs(1) - 1)
    def _():
        y_ref[...] = x_ref[...] + acc_scr[...]


def _ffn_weight_specs(d, bf, nj):
    return [pl.BlockSpec((1, d), lambda i, j: (0, 0)),
            pl.BlockSpec((d, bf), lambda i, j: (0, j)),
            pl.BlockSpec((d, bf), lambda i, j: (0, nj + j)),
            pl.BlockSpec((CONV_W, bf), lambda i, j: (0, j)),
            pl.BlockSpec((CONV_W, bf), lambda i, j: (0, nj + j)),
            pl.BlockSpec((1, bf), lambda i, j: (0, j)),
            pl.BlockSpec((1, bf), lambda i, j: (0, nj + j)),
            pl.BlockSpec((bf, d), lambda i, j: (j, 0))]


def conv_ffn_prompt(x1, g, w_up, w_conv, b_conv, w_down, bm, bf, seq_len):
    rows, d = x1.shape
    f = w_down.shape[0]
    nj = f // bf
    nb = rows // bm
    hb = bm // SUBLANES
    return pl.pallas_call(
        functools.partial(_ffn_p_kernel, seq_len=seq_len),
        grid=(nb, nj),
        in_specs=[pl.BlockSpec((bm, d), lambda i, j: (i, 0)),
                  pl.BlockSpec((SUBLANES, d), lambda i, j: (jnp.maximum(i * hb - 1, 0), 0))]
        + _ffn_weight_specs(d, bf, nj),
        out_specs=[pl.BlockSpec((bm, d), lambda i, j: (i, 0)),
                   pl.BlockSpec((1, SUBLANES, bf), lambda i, j: (i, 0, j)),
                   pl.BlockSpec((1, SUBLANES, bf), lambda i, j: (i, 0, j))],
        out_shape=[jax.ShapeDtypeStruct((rows, d), F32),
                   jax.ShapeDtypeStruct((nb, SUBLANES, f), F32),
                   jax.ShapeDtypeStruct((nb, SUBLANES, f), F32)],
        scratch_shapes=[pltpu.VMEM((bm + SUBLANES, d), BF16),
                        pltpu.VMEM((bm, d), F32)],
        compiler_params=_cparams(("parallel", "arbitrary")),
    )(x1, x1, g, w_up, w_up, w_conv, w_conv, b_conv, b_conv, w_down)


def _ffn_s_kernel(x_ref, g_ref, wg_ref, wv_ref, cg_ref, cv_ref, bg_ref, bv_ref, wd_ref, prev_ref,
                  y_ref, new_ref, xn_scr, acc_scr, *, seq):
    j = pl.program_id(1)
    nb = x_ref.shape[0] // seq

    @pl.when(j == 0)
    def _():
        xn_scr[...] = _rms(x_ref[...], g_ref[...]).astype(BF16)
        acc_scr[...] = jnp.zeros_like(acc_scr)

    xn = xn_scr[...]

    def conv(u, half, wc_ref, bc_ref):
        ext = [prev_ref[r, half] for r in range(CONV_W - 1)] + [u[t * nb:(t + 1) * nb, :] for t in range(seq)]
        for r in range(CONV_W - 1):
            new_ref[r, half] = ext[seq + r]
        return jnp.concatenate(
            [bc_ref[...] + ((ext[t] * wc_ref[0:1, :] + ext[t + 1] * wc_ref[1:2, :]) + ext[t + 2] * wc_ref[2:3, :])
             for t in range(seq)], axis=0)

    cg = conv(_dot(xn, wg_ref[...]), 0, cg_ref, bg_ref)
    cv = conv(_dot(xn, wv_ref[...]), 1, cv_ref, bv_ref)
    act = (cg * jax.nn.sigmoid(cg) * cv).astype(BF16)
    acc_scr[...] += _dot(act, wd_ref[...])

    @pl.when(j == pl.num_programs(1) - 1)
    def _():
        y_ref[...] = x_ref[...] + acc_scr[...]


def conv_ffn_sample(x1_t, prev_t, g, w_up, w_conv, b_conv, w_down, bf, seq):
    rows, d = x1_t.shape
    f = w_down.shape[0]
    nj = f // bf
    nb = rows // seq
    hist = pl.BlockSpec((CONV_W - 1, 2, nb, bf), lambda i, j: (0, 0, 0, j))
    return pl.pallas_call(
        functools.partial(_ffn_s_kernel, seq=seq),
        grid=(1, nj),
        in_specs=[pl.BlockSpec((rows, d), lambda i, j: (0, 0))] + _ffn_weight_specs(d, bf, nj) + [hist],
        out_specs=[pl.BlockSpec((rows, d), lambda i, j: (0, 0)), hist],
        out_shape=[jax.ShapeDtypeStruct((rows, d), F32),
                   jax.ShapeDtypeStruct(prev_t.shape, F32)],
        scratch_shapes=[pltpu.VMEM((rows, d), BF16),
                        pltpu.VMEM((rows, d), F32)],
        compiler_params=_cparams(("parallel", "arbitrary")),
    )(x1_t, g, w_up, w_up, w_conv, w_conv, b_conv, b_conv, w_down, prev_t)


def _pick(n, prefs):
    for p in prefs:
        if n % p == 0:
            return p
    return n


def _layer(x_prompt, x_sample, cache_lat, cache_kr, c_state, n_state, m_state, conv_state, page_table,
           g_attn_norm, w_in, b_gates, g_q_a, w_uq, g_qk_nope_q, g_qk_rope_q, g_kv_a, w_uk, w_uv,
           g_qk_nope_k, g_qk_rope_k, g_mlstm_out, w_branch_a, w_branch_b, w_out, g_ffn_norm,
           w_up, w_conv, b_conv, w_down):
    bp, s, d = x_prompt.shape
    db, seq, _ = x_sample.shape
    n_pages = page_table.shape[1]
    page = cache_lat.shape[2]
    past = n_pages * page
    tp, ts = bp * s, db * seq
    t = tp + ts
    c_lat = g_kv_a.shape[0]
    c_q = g_q_a.shape[0]
    f = w_down.shape[0]
    qk_w = A_HEADS * A_DQK
    v_w = A_HEADS * A_DV
    row = lambda v: v.reshape(1, -1)

    o_i = 2 * qk_w + 2 * v_w
    o_cq = o_i + 2 * A_HEADS
    o_kr = o_cq + c_q + c_lat
    o_g = o_kr + B_ROPE
    w_main = jnp.concatenate([w_in[:, :o_i], w_in[:, o_g:]], axis=1).astype(BF16)
    n_small = LANES - B_ROPE - 2 * A_HEADS
    w_lat = jnp.concatenate([w_in[:, o_cq:o_kr], w_in[:, o_kr:o_g], w_in[:, o_i:o_cq],
                             jnp.zeros((d, n_small), F32)], axis=1).astype(BF16)
    w_uq_pad = jnp.pad(w_uq, ((0, 0), (0, 0), (0, Q_HEAD_PAD - B_QK))).reshape(c_q, B_HEADS * Q_HEAD_PAD).astype(BF16)
    w_uk2 = w_uk.reshape(c_lat, B_HEADS * B_NOPE).astype(BF16)
    w_uk_t = jnp.transpose(w_uk.reshape(c_lat, B_HEADS * B_NOPE)).astype(BF16)
    w_uv2 = w_uv.reshape(c_lat, B_HEADS * B_V).astype(BF16)
    gq = jnp.concatenate([g_qk_nope_q * g_qk_nope_k, g_qk_rope_q * g_qk_rope_k, g_qk_rope_q * g_qk_rope_k,
                          jnp.zeros((Q_HEAD_PAD - B_QK,), F32)]) * ATTN_SCALE
    gq = jnp.tile(gq, B_HEADS).reshape(1, -1)

    w_a_b, w_b_b, w_out_b = w_branch_a.astype(BF16), w_branch_b.astype(BF16), w_out.astype(BF16)
    w_up_b, w_down_b = w_up.astype(BF16), w_down.astype(BF16)
    half = B_ROPE // 2
    freqs = ROPE_THETA ** (-jnp.arange(half, dtype=F32) / half)

    def token_stage(x, pos, bm):
        rows = x.shape[0]
        ang = pos.astype(F32)[:, None] * freqs
        cos, sin = jnp.cos(ang), jnp.sin(ang)
        zh = jnp.zeros((rows, half), F32)
        zr = jnp.zeros((rows, LANES - B_ROPE), F32)
        cos128 = jnp.concatenate([cos, cos, zr], axis=1)
        sin_a = jnp.concatenate([zh, sin, zr], axis=1)
        sin_b = jnp.concatenate([-sin, zh, zr], axis=1)
        bn = _pick(d, (1024, 512, 256, 128))
        z_main = norm_matmul(x, row(g_attn_norm), w_main, bm, bn, BF16)
        z_lat = norm_matmul(x, row(g_attn_norm), w_lat, bm, w_lat.shape[1])
        bq = _pick(rows, (256, 128, 64, 32, 16, 8))
        q = q_proj(z_lat, row(g_q_a), w_uq_pad, cos128, sin_a, sin_b, gq, bq)
        c_kv, kr128, k, v = kv_proj(z_lat, row(g_kv_a), w_uk2, w_uv2, cos128, sin_a, sin_b, bq)
        g_t = jnp.transpose(z_lat[:, c_q + c_lat + B_ROPE:c_q + c_lat + B_ROPE + 2 * A_HEADS])
        gates = (g_t[:A_HEADS, None, :], g_t[A_HEADS:, None, :], g_t[:A_HEADS, :, None], g_t[A_HEADS:, :, None])
        return z_main, q, c_kv, kr128, k, v, gates

    def mix_stage(x, hid, z_main, h_b):
        bm = _pick(x.shape[0], (1024, 512, 256, 128, 64, 32, 16, 8))
        merged = merge(hid, z_main, h_b, row(g_mlstm_out), w_a_b, w_b_b, bm, _pick(d, (512, 256, 128)))
        return resid_matmul(x, merged, w_out_b, bm, _pick(d, (1024, 512, 256, 128)))

    bff = _pick(f, (512, 256, 128))

    xp = x_prompt.reshape(tp, d)
    bm_p = _pick(tp, (1024, 512, 256, 128, 64, 32, 16, 8))
    zm_p, q_p, ckv_p, kr_p, k_p, v_p, gates_p = token_stage(xp, jnp.tile(jnp.arange(s, dtype=jnp.int32), bp), bm_p)
    chunk = _pick(s, (256, 128, 64, 32, 16, 8))
    hid_p, c_p, n_p, m_p = mlstm_prompt(zm_p, b_gates, *gates_p, bp, s, chunk)
    hb_p = attn_prompt(q_p, k_p, v_p, bp, s, _pick(s, (512, 256, 128)))
    x1_p = mix_stage(xp, hid_p, zm_p, hb_p)
    bmf_p = _pick(s, (512, 256, 128, 64, 32, 16, 8))
    y_p, tg_p, tv_p = conv_ffn_prompt(x1_p, row(g_ffn_norm), w_up_b, w_conv, row(b_conv), w_down_b, bmf_p, bff, s)
    nb_seq = s // bmf_p
    tail_p = jnp.concatenate([tg_p, tv_p], axis=-1).reshape(bp, nb_seq, SUBLANES, 2 * f)
    conv_p = tail_p[:, nb_seq - 1, SUBLANES - (CONV_W - 1):, :]

    xs = x_sample.reshape(ts, d)
    bm_s = _pick(ts, (512, 256, 128, 64, 32, 16, 8))
    zm_s, q_s, ckv_s, kr_s, k_s, _, gates_s = token_stage(
        xs, jnp.tile(past + jnp.arange(seq, dtype=jnp.int32), db), bm_s)
    rb = _pick(ts, (128, 64, 32, 16, 8))
    m_col = jnp.repeat(jnp.transpose(m_state), seq, axis=1)[:, :, None]
    n_rows = jnp.repeat(jnp.transpose(n_state, (1, 0, 2)), seq, axis=1)
    hid_s, c_s, n_s_rows, m_s_rows = mlstm_sample(zm_s, b_gates, *gates_s, m_col, n_rows, c_state[None], rb, seq)
    n_s = jnp.transpose(n_s_rows[:, seq - 1::seq, :], (1, 0, 2))
    m_s = jnp.transpose(m_s_rows[:, seq - 1::seq, 0])

    q_abs = q_absorb(q_s, w_uk_t).reshape(db, seq * B_HEADS, c_lat)
    g_pages = _pick(n_pages, (32, 16, 8, 4, 2))
    o_lat = attn_sample(page_table, cache_lat, jnp.swapaxes(cache_kr, 2, 3), w_uk_t, q_abs,
                        q_s.reshape(db, seq * B_HEADS, Q_HEAD_PAD), ckv_s.reshape(db, seq, c_lat),
                        k_s.reshape(db, seq * B_HEADS, Q_HEAD_PAD), g_pages, seq)
    hb_s = uv_expand(o_lat.reshape(ts, B_HEADS * c_lat), w_uv2)
    x1_s = mix_stage(xs, hid_s, zm_s, hb_s)
    x1_t = jnp.transpose(x1_s.reshape(db, seq, d), (1, 0, 2)).reshape(ts, d)
    prev_t = jnp.transpose(conv_state.reshape(db, CONV_W - 1, 2, f), (1, 2, 0, 3))
    y_t, new_t = conv_ffn_sample(x1_t, prev_t, row(g_ffn_norm), w_up_b, w_conv, row(b_conv), w_down_b, bff, seq)
    y_s = jnp.transpose(y_t.reshape(seq, db, d), (1, 0, 2))
    conv_s = jnp.transpose(new_t, (2, 0, 1, 3)).reshape(db, CONV_W - 1, 2 * f)

    new_p = (ckv_p.reshape(bp, s, c_lat), kr_p[:, :B_ROPE].reshape(bp, s, B_ROPE),
             c_p, n_p[:, :, 0, :], m_p[:, :, 0, 0], conv_p)
    new_s = (ckv_s.reshape(db, seq, c_lat), kr_s[:, :B_ROPE].reshape(db, seq, B_ROPE),
             c_s[0], n_s, m_s, conv_s)
    return y_p.reshape(bp, s, d), y_s.reshape(db, seq, d), new_p, new_s


def kernel(x_prompt, x_sample, cache_kv_latent, cache_k_rope, state_mlstm_C, state_mlstm_n, state_mlstm_m, state_conv, page_table, g_attn_norm, w_in, b_gates, g_q_a, w_uq, g_qk_nope_q, g_qk_rope_q, g_kv_a, w_uk, w_uv, g_qk_nope_k, g_qk_rope_k, g_mlstm_out, w_branch_a, w_branch_b, w_out, g_ffn_norm, w_up, w_conv, b_conv, w_down):
    depth = w_in.shape[0]
    assert depth == 1, "single-layer trunk"
    l = 0
    y_p, y_s, new_p, new_s = _layer(
        x_prompt, x_sample, cache_kv_latent, cache_k_rope, state_mlstm_C[l], state_mlstm_n[l],
        state_mlstm_m[l], state_conv[l], page_table, g_attn_norm[l], w_in[l], b_gates[l], g_q_a[l], w_uq[l],
        g_qk_nope_q[l], g_qk_rope_q[l], g_kv_a[l], w_uk[l], w_uv[l], g_qk_nope_k[l], g_qk_rope_k[l],
        g_mlstm_out[l], w_branch_a[l], w_branch_b[l], w_out[l], g_ffn_norm[l], w_up[l], w_conv[l], b_conv[l],
        w_down[l])
    dts = (cache_kv_latent.dtype, cache_k_rope.dtype, state_mlstm_C.dtype, state_mlstm_n.dtype,
           state_mlstm_m.dtype, state_conv.dtype)
    st_p = tuple(a[None].astype(dt) for a, dt in zip(new_p, dts))
    st_s = tuple(a[None].astype(dt) for a, dt in zip(new_s, dts))
    return (y_p, y_s) + st_p + st_s
```

```python
import functools
import math

import jax
import jax.numpy as jnp
from jax import lax
from jax.experimental import pallas as pl
from jax.experimental.pallas import tpu as pltpu

F32 = jnp.float32
BF16 = jnp.bfloat16

A_HEADS = 8
A_DQK = 128
A_DV = 256
K_SCALE = A_DQK ** -0.5
GATE_CAP = 15.0
B_HEADS = 16
B_NOPE = 128
B_ROPE = 64
B_QK = B_NOPE + B_ROPE
B_V = 128
Q_HEAD_PAD = 256
ROPE_THETA = 10000.0
ATTN_SCALE = B_QK ** -0.5
CONV_W = 3
EPS = 1e-6
NEG_INF = float("-inf")

LANES = 128
SUBLANES = 8
MXU_COLS = 256
VMEM_LIMIT = 56 * 1024 * 1024


def _cparams(sem):
    return pltpu.CompilerParams(dimension_semantics=sem, vmem_limit_bytes=VMEM_LIMIT)


def _rms(x, g):
    r = lax.rsqrt(jnp.mean(x * x, axis=-1, keepdims=True) + EPS)
    return x * r * g


def _dot(a, b):
    return jnp.dot(a, b, preferred_element_type=F32)


def _dot_nt(a, b):
    return lax.dot_general(a, b, (((1,), (1,)), ((), ())), preferred_element_type=F32)


def _rope128(x, cos, sin_a, sin_b):
    return x * cos + pltpu.roll(x, 32, 1) * sin_a + pltpu.roll(x, 96, 1) * sin_b


def _norm_mm_kernel(x_ref, g_ref, w_ref, o_ref, xn_ref):
    @pl.when(pl.program_id(1) == 0)
    def _():
        xn_ref[...] = _rms(x_ref[...], g_ref[...]).astype(BF16)

    o_ref[...] = _dot(xn_ref[...], w_ref[...]).astype(o_ref.dtype)


def norm_matmul(x, g, w, bm, bn, out_dtype=F32):
    t, d = x.shape
    n = w.shape[1]
    return pl.pallas_call(
        _norm_mm_kernel,
        grid=(t // bm, n // bn),
        in_specs=[pl.BlockSpec((bm, d), lambda i, j: (i, 0)),
                  pl.BlockSpec((1, d), lambda i, j: (0, 0)),
                  pl.BlockSpec((d, bn), lambda i, j: (0, j))],
        out_specs=pl.BlockSpec((bm, bn), lambda i, j: (i, j)),
        out_shape=jax.ShapeDtypeStruct((t, n), out_dtype),
        scratch_shapes=[pltpu.VMEM((bm, d), BF16)],
        compiler_params=_cparams(("parallel", "arbitrary")),
    )(x, g, w)


def _norm_mm2_kernel(x_ref, g_ref, wa_ref, wb_ref, o_ref, xn_ref, *, na_tiles):
    j = pl.program_id(1)

    @pl.when(j == 0)
    def _():
        xn_ref[...] = _rms(x_ref[...], g_ref[...]).astype(BF16)

    @pl.when(j < na_tiles)
    def _():
        o_ref[...] = _dot(xn_ref[...], wa_ref[...]).astype(o_ref.dtype)

    @pl.when(j >= na_tiles)
    def _():
        o_ref[...] = _dot(xn_ref[...], wb_ref[...]).astype(o_ref.dtype)


def norm_matmul2(x, g, w_a, w_b, bm, bn, out_dtype):
    t, d = x.shape
    na_tiles, nb_tiles = w_a.shape[1] // bn, w_b.shape[1] // bn
    return pl.pallas_call(
        functools.partial(_norm_mm2_kernel, na_tiles=na_tiles),
        grid=(t // bm, na_tiles + nb_tiles),
        in_specs=[pl.BlockSpec((bm, d), lambda i, j: (i, 0)),
                  pl.BlockSpec((1, d), lambda i, j: (0, 0)),
                  pl.BlockSpec((d, bn), lambda i, j: (0, jnp.minimum(j, na_tiles - 1))),
                  pl.BlockSpec((d, bn), lambda i, j: (0, jnp.maximum(j - na_tiles, 0)))],
        out_specs=pl.BlockSpec((bm, bn), lambda i, j: (i, j)),
        out_shape=jax.ShapeDtypeStruct((t, (na_tiles + nb_tiles) * bn), out_dtype),
        scratch_shapes=[pltpu.VMEM((bm, d), BF16)],
        compiler_params=_cparams(("parallel", "arbitrary")),
    )(x, g, w_a, w_b)


def _q_kernel(cq_ref, g_ref, w_ref, cos_ref, sa_ref, sb_ref, gain_ref, o_ref):
    cqn = _rms(cq_ref[...], g_ref[...]).astype(BF16)
    q = _dot(cqn, w_ref[...])
    cos, sa, sb = cos_ref[...], sa_ref[...], sb_ref[...]
    for h in range(B_HEADS):
        lo = h * Q_HEAD_PAD
        nope = q[:, lo:lo + LANES]
        rp = _rope128(q[:, lo + LANES:lo + Q_HEAD_PAD], cos, sa, sb)
        ssq = jnp.sum(nope * nope, axis=-1, keepdims=True) + jnp.sum(rp * rp, axis=-1, keepdims=True)
        r = lax.rsqrt(ssq * (1.0 / B_QK) + EPS)
        o_ref[:, lo:lo + LANES] = (nope * r * gain_ref[:, lo:lo + LANES]).astype(o_ref.dtype)
        o_ref[:, lo + LANES:lo + Q_HEAD_PAD] = (rp * r * gain_ref[:, lo + LANES:lo + Q_HEAD_PAD]).astype(o_ref.dtype)


def q_proj(z_lat, g_q_a, w_uq_pad, cos, sa, sb, gain, bm):
    t = z_lat.shape[0]
    c = g_q_a.shape[1]
    n = w_uq_pad.shape[1]
    return pl.pallas_call(
        _q_kernel,
        grid=(t // bm,),
        in_specs=[pl.BlockSpec((bm, c), lambda i: (i, 0)),
                  pl.BlockSpec((1, c), lambda i: (0, 0)),
                  pl.BlockSpec((c, n), lambda i: (0, 0)),
                  pl.BlockSpec((bm, LANES), lambda i: (i, 0)),
                  pl.BlockSpec((bm, LANES), lambda i: (i, 0)),
                  pl.BlockSpec((bm, LANES), lambda i: (i, 0)),
                  pl.BlockSpec((1, n), lambda i: (0, 0))],
        out_specs=pl.BlockSpec((bm, n), lambda i: (i, 0)),
        out_shape=jax.ShapeDtypeStruct((t, n), BF16),
        compiler_params=_cparams(("parallel",)),
    )(z_lat, g_q_a, w_uq_pad, cos, sa, sb, gain)


def _kv_kernel(ckv_ref, sm_ref, g_ref, wuk_ref, wuv_ref, cos_ref, sa_ref, sb_ref,
               c_ref, kr_ref, k_ref, v_ref):
    c = _rms(ckv_ref[...], g_ref[...])
    c_ref[...] = c
    cb = c.astype(BF16)
    kn = _dot(cb, wuk_ref[...])
    v_ref[...] = _dot(cb, wuv_ref[...]).astype(v_ref.dtype)
    kr = _rope128(sm_ref[...], cos_ref[...], sa_ref[...], sb_ref[...])
    kr_ref[...] = kr
    ssr = jnp.sum(kr * kr, axis=-1, keepdims=True)
    for h in range(B_HEADS):
        nope = kn[:, h * B_NOPE:(h + 1) * B_NOPE]
        r = lax.rsqrt((jnp.sum(nope * nope, axis=-1, keepdims=True) + ssr) * (1.0 / B_QK) + EPS)
        lo = h * Q_HEAD_PAD
        k_ref[:, lo:lo + LANES] = (nope * r).astype(k_ref.dtype)
        k_ref[:, lo + LANES:lo + Q_HEAD_PAD] = (kr * r).astype(k_ref.dtype)


def kv_proj(z_lat, g_kv_a, w_uk, w_uv, cos, sa, sb, bm):
    t = z_lat.shape[0]
    c = g_kv_a.shape[1]
    small_blk = (2 * c) // LANES
    row = lambda i: (i, 0)
    const = lambda i: (0, 0)
    return pl.pallas_call(
        _kv_kernel,
        grid=(t // bm,),
        in_specs=[pl.BlockSpec((bm, c), lambda i: (i, 1)),
                  pl.BlockSpec((bm, LANES), lambda i: (i, small_blk)),
                  pl.BlockSpec((1, c), const),
                  pl.BlockSpec(w_uk.shape, const),
                  pl.BlockSpec(w_uv.shape, const),
                  pl.BlockSpec((bm, LANES), row),
                  pl.BlockSpec((bm, LANES), row),
                  pl.BlockSpec((bm, LANES), row)],
        out_specs=[pl.BlockSpec((bm, c), row),
                   pl.BlockSpec((bm, LANES), row),
                   pl.BlockSpec((bm, B_HEADS * Q_HEAD_PAD), row),
                   pl.BlockSpec((bm, B_HEADS * B_V), row)],
        out_shape=[jax.ShapeDtypeStruct((t, c), F32),
                   jax.ShapeDtypeStruct((t, LANES), F32),
                   jax.ShapeDtypeStruct((t, B_HEADS * Q_HEAD_PAD), BF16),
                   jax.ShapeDtypeStruct((t, B_HEADS * B_V), BF16)],
        compiler_params=_cparams(("parallel",)),
    )(z_lat, z_lat, g_kv_a, w_uk, w_uv, cos, sa, sb)


def _log_gates(gi, gf, bi, bf):
    cap = lambda x: GATE_CAP * jnp.tanh(x * (1.0 / GATE_CAP))
    li = cap(gi + bi)
    y = cap(gf + bf)
    lf = jnp.minimum(y, 0.0) - jnp.log(1.0 + jnp.exp(-jnp.abs(y)))
    return li, lf


def _mlstm_p_kernel(b_ref, q_ref, k_ref, v_ref, gir_ref, gfr_ref, gic_ref, gfc_ref,
                    hid_ref, c_out_ref, n_out_ref, m_out_ref, c_scr, n_scr, m_scr):
    ci = pl.program_id(1)
    L = q_ref.shape[0]

    @pl.when(ci == 0)
    def _():
        c_scr[...] = jnp.zeros_like(c_scr)
        n_scr[...] = jnp.zeros_like(n_scr)
        m_scr[...] = jnp.zeros_like(m_scr)

    t_idx = lax.broadcasted_iota(jnp.int32, (L, L), 0)
    s_idx = lax.broadcasted_iota(jnp.int32, (L, L), 1)
    causal = s_idx <= t_idx
    anti = t_idx <= s_idx
    for h in range(A_HEADS):
        bi = b_ref[h]
        bf = b_ref[A_HEADS + h]
        qb = q_ref[:, h * A_DQK:(h + 1) * A_DQK]
        kb = k_ref[:, h * A_DQK:(h + 1) * A_DQK]
        vb = v_ref[:, h * A_DV:(h + 1) * A_DV]
        q = qb.astype(F32)
        li_r, lf_r = _log_gates(gir_ref[h], gfr_ref[h], bi, bf)
        li_c, lf_c = _log_gates(gic_ref[h], gfc_ref[h], bi, bf)

        bcum_c = jnp.sum(jnp.where(causal, lf_r, 0.0), axis=1, keepdims=True)
        bcum_r = jnp.sum(jnp.where(anti, lf_c, 0.0), axis=0, keepdims=True)
        log_w = jnp.where(causal, bcum_c - bcum_r + li_r, NEG_INF)
        m_prev = m_scr[h][:, 0:1]
        c_prev = c_scr[h]
        n_prev = n_scr[h]
        log_state = bcum_c + m_prev
        m_row = jnp.maximum(log_state, jnp.max(log_w, axis=1, keepdims=True))
        w = jnp.exp(log_w - m_row) * K_SCALE
        w_state = jnp.exp(log_state - m_row)
        sc = _dot_nt(qb, kb) * w
        num = _dot(sc.astype(BF16), vb) + w_state * _dot(qb, c_prev.astype(BF16))
        den = jnp.sum(sc, axis=1, keepdims=True) + w_state * jnp.sum(q * n_prev, axis=1, keepdims=True)
        hid_ref[:, h * A_DV:(h + 1) * A_DV] = (
            num / jnp.maximum(jnp.abs(den), jnp.exp(-m_row))).astype(hid_ref.dtype)

        b_last = bcum_c[L - 1:L, :]
        log_k = b_last - bcum_c + li_c
        m_new = jnp.maximum(b_last + m_prev, jnp.max(log_k, axis=0, keepdims=True))
        wk = jnp.exp(log_k - m_new) * K_SCALE
        decay = jnp.exp(b_last + m_prev - m_new)
        kw = kb.astype(F32) * wk
        kwt = jnp.transpose(kw).astype(BF16)
        c_scr[h] = decay * c_prev + _dot(kwt, vb)
        n_scr[h] = decay * n_prev + jnp.sum(kw, axis=0, keepdims=True)
        m_scr[h] = jnp.broadcast_to(m_new, (1, LANES))

    @pl.when(ci == pl.num_programs(1) - 1)
    def _():
        c_out_ref[0] = c_scr[...]
        n_out_ref[0] = n_scr[...]
        m_out_ref[0] = m_scr[...]


def mlstm_prompt(z_main, b_gates, gi_r, gf_r, gi_c, gf_c, bp, s, chunk):
    nc = s // chunk
    qk_w, v_w = A_HEADS * A_DQK, A_HEADS * A_DV
    assert (2 * qk_w) % v_w == 0
    tok = lambda b, c, *_: b * nc + c
    return pl.pallas_call(
        _mlstm_p_kernel,
        grid_spec=pltpu.PrefetchScalarGridSpec(
            num_scalar_prefetch=1,
            grid=(bp, nc),
            in_specs=[pl.BlockSpec((chunk, qk_w), lambda b, c, *_: (tok(b, c), 0)),
                      pl.BlockSpec((chunk, qk_w), lambda b, c, *_: (tok(b, c), 1)),
                      pl.BlockSpec((chunk, v_w), lambda b, c, *_: (tok(b, c), (2 * qk_w) // v_w)),
                      pl.BlockSpec((A_HEADS, 1, chunk), lambda b, c, *_: (0, 0, tok(b, c))),
                      pl.BlockSpec((A_HEADS, 1, chunk), lambda b, c, *_: (0, 0, tok(b, c))),
                      pl.BlockSpec((A_HEADS, chunk, 1), lambda b, c, *_: (0, tok(b, c), 0)),
                      pl.BlockSpec((A_HEADS, chunk, 1), lambda b, c, *_: (0, tok(b, c), 0))],
            out_specs=[pl.BlockSpec((chunk, v_w), lambda b, c, *_: (tok(b, c), 0)),
                       pl.BlockSpec((1, A_HEADS, A_DQK, A_DV), lambda b, c, *_: (b, 0, 0, 0)),
                       pl.BlockSpec((1, A_HEADS, 1, A_DQK), lambda b, c, *_: (b, 0, 0, 0)),
                       pl.BlockSpec((1, A_HEADS, 1, LANES), lambda b, c, *_: (b, 0, 0, 0))],
            scratch_shapes=[pltpu.VMEM((A_HEADS, A_DQK, A_DV), F32),
                            pltpu.VMEM((A_HEADS, 1, A_DQK), F32),
                            pltpu.VMEM((A_HEADS, 1, LANES), F32)]),
        out_shape=[jax.ShapeDtypeStruct((bp * s, v_w), BF16),
                   jax.ShapeDtypeStruct((bp, A_HEADS, A_DQK, A_DV), F32),
                   jax.ShapeDtypeStruct((bp, A_HEADS, 1, A_DQK), F32),
                   jax.ShapeDtypeStruct((bp, A_HEADS, 1, LANES), F32)],
        compiler_params=_cparams(("parallel", "arbitrary")),
    )(b_gates, z_main, z_main, z_main, gi_r, gf_r, gi_c, gf_c)


def _mlstm_s_kernel(b_ref, q_ref, k_ref, v_ref, gir_ref, gfr_ref, gic_ref, gfc_ref,
                    mc_ref, nrow_ref, c_in_ref,
                    hid_ref, c_out_ref, n_out_ref, m_out_ref,
                    hc_scr, dec_scr, kwt_scr, q_scr, *, seq):
    h = pl.program_id(1)
    R = q_ref.shape[0]
    nseq = R // seq
    per8 = SUBLANES // seq
    bi = b_ref[h]
    bf = b_ref[A_HEADS + h]
    q = q_ref[...].astype(F32)
    k = k_ref[...].astype(F32)
    v = v_ref[...].astype(F32)
    q_scr[...] = q
    li_r, lf_r = _log_gates(gir_ref[0], gfr_ref[0], bi, bf)
    li_c, lf_c = _log_gates(gic_ref[0], gfc_ref[0], bi, bf)
    m_prev = mc_ref[0]

    t_idx = lax.broadcasted_iota(jnp.int32, (R, R), 0)
    s_idx = lax.broadcasted_iota(jnp.int32, (R, R), 1)
    same = (t_idx // seq) == (s_idx // seq)
    causal = same & (s_idx <= t_idx)
    bcum_c = jnp.sum(jnp.where(causal, lf_r, 0.0), axis=1, keepdims=True)
    bcum_r = jnp.sum(jnp.where(same & (t_idx <= s_idx), lf_c, 0.0), axis=0, keepdims=True)
    blast_c = jnp.sum(jnp.where(same, lf_r, 0.0), axis=1, keepdims=True)
    blast_r = jnp.sum(jnp.where(same, lf_c, 0.0), axis=0, keepdims=True)
    log_w = jnp.where(causal, bcum_c - bcum_r + li_r, NEG_INF)
    log_state = bcum_c + m_prev
    m_row = jnp.maximum(log_state, jnp.max(log_w, axis=1, keepdims=True))
    w = jnp.exp(log_w - m_row) * K_SCALE
    w_state = jnp.exp(log_state - m_row)

    log_k_c = blast_c - bcum_c + li_c
    log_k_r = blast_r - bcum_r + li_r
    seg_max = jnp.max(jnp.where(same, log_k_r, NEG_INF), axis=1, keepdims=True)
    m_new = jnp.maximum(blast_c + m_prev, seg_max)
    wk = jnp.exp(log_k_c - m_new) * K_SCALE
    decay = jnp.exp(blast_c + m_prev - m_new)
    kw = k * wk
    kwt_scr[...] = jnp.transpose(kw)
    dec_scr[...] = jnp.broadcast_to(decay, dec_scr.shape)

    row8 = lax.broadcasted_iota(jnp.int32, (SUBLANES, A_DV), 0)
    col = lax.broadcasted_iota(jnp.int32, (A_DQK, R), 1)

    def group(gidx, carry):
        r0 = pl.multiple_of(gidx * SUBLANES, SUBLANES)
        q8 = q_scr[pl.ds(r0, SUBLANES), :]
        hc8 = jnp.zeros((SUBLANES, A_DV), F32)
        for u in range(per8):
            sq = gidx * per8 + u
            c0 = c_in_ref[0, sq, 0]
            res = _dot(q8, c0)
            hc8 = jnp.where((row8 // seq) == u, res, hc8)
            kwt_m = jnp.where((col // seq) == sq, kwt_scr[...], 0.0)
            dsc = dec_scr[pl.ds(sq * seq, 1), 0:1]
            c_out_ref[0, sq, 0] = dsc * c0 + _dot(kwt_m.astype(BF16), v_ref[...])
        hc_scr[pl.ds(r0, SUBLANES), :] = hc8
        return carry

    lax.fori_loop(0, R // SUBLANES, group, 0)

    sc = _dot_nt(q, k) * w
    num = _dot(sc, v) + w_state * hc_scr[...]
    den = jnp.sum(sc, axis=1, keepdims=True) + w_state * jnp.sum(q * nrow_ref[0], axis=1, keepdims=True)
    hid_ref[...] = (num / jnp.maximum(jnp.abs(den), jnp.exp(-m_row))).astype(hid_ref.dtype)

    acc = kw
    shift = 1
    while shift < seq:
        acc = acc + jnp.where((lax.broadcasted_iota(jnp.int32, acc.shape, 0) % seq) >= shift,
                              pltpu.roll(acc, shift, 0), 0.0)
        shift *= 2
    n_out_ref[0] = decay * nrow_ref[0] + acc
    m_out_ref[0] = jnp.broadcast_to(m_new, (R, LANES))


def mlstm_sample(z_main_s, b_gates, gi_r, gf_r, gi_c, gf_c, m_col, n_rows, c_state, rb, seq):
    ts = z_main_s.shape[0]
    nseq_blk = rb // seq
    kcol = (A_HEADS * A_DQK) // A_DQK
    vcol = (2 * A_HEADS * A_DQK) // A_DV
    return pl.pallas_call(
        functools.partial(_mlstm_s_kernel, seq=seq),
        grid_spec=pltpu.PrefetchScalarGridSpec(
            num_scalar_prefetch=1,
            grid=(ts // rb, A_HEADS),
            in_specs=[pl.BlockSpec((rb, A_DQK), lambda i, h, *_: (i, h)),
                      pl.BlockSpec((rb, A_DQK), lambda i, h, *_: (i, kcol + h)),
                      pl.BlockSpec((rb, A_DV), lambda i, h, *_: (i, vcol + h)),
                      pl.BlockSpec((1, 1, rb), lambda i, h, *_: (h, 0, i)),
                      pl.BlockSpec((1, 1, rb), lambda i, h, *_: (h, 0, i)),
                      pl.BlockSpec((1, rb, 1), lambda i, h, *_: (h, i, 0)),
                      pl.BlockSpec((1, rb, 1), lambda i, h, *_: (h, i, 0)),
                      pl.BlockSpec((1, rb, 1), lambda i, h, *_: (h, i, 0)),
                      pl.BlockSpec((1, rb, A_DQK), lambda i, h, *_: (h, i, 0)),
                      pl.BlockSpec((1, nseq_blk, 1, A_DQK, A_DV), lambda i, h, *_: (0, i, h, 0, 0))],
            out_specs=[pl.BlockSpec((rb, A_DV), lambda i, h, *_: (i, h)),
                       pl.BlockSpec((1, nseq_blk, 1, A_DQK, A_DV), lambda i, h, *_: (0, i, h, 0, 0)),
                       pl.BlockSpec((1, rb, A_DQK), lambda i, h, *_: (h, i, 0)),
                       pl.BlockSpec((1, rb, LANES), lambda i, h, *_: (h, i, 0))],
            scratch_shapes=[pltpu.VMEM((rb, A_DV), F32),
                            pltpu.VMEM((rb, LANES), F32),
                            pltpu.VMEM((A_DQK, rb), F32),
                            pltpu.VMEM((rb, A_DQK), F32)]),
        out_shape=[jax.ShapeDtypeStruct((ts, A_HEADS * A_DV), BF16),
                   jax.ShapeDtypeStruct(c_state.shape, F32),
                   jax.ShapeDtypeStruct((A_HEADS, ts, A_DQK), F32),
                   jax.ShapeDtypeStruct((A_HEADS, ts, LANES), F32)],
        compiler_params=_cparams(("parallel", "parallel")),
    )(b_gates, z_main_s, z_main_s, z_main_s, gi_r, gf_r, gi_c, gf_c, m_col, n_rows, c_state)


def _attn_p_kernel(q_ref, k_ref, v_ref, o_ref, *, tq, heads):
    s = q_ref.shape[0]
    lower = lax.broadcasted_iota(jnp.int32, (tq, tq), 1) <= lax.broadcasted_iota(jnp.int32, (tq, tq), 0)
    for hh in range(heads):
        qk = slice(hh * Q_HEAD_PAD, (hh + 1) * Q_HEAD_PAD)
        vv = slice(hh * B_V, (hh + 1) * B_V)
        for qt in range(s // tq):
            lo, hi = qt * tq, (qt + 1) * tq
            q = q_ref[lo:hi, qk]
            sd = jnp.where(lower, _dot_nt(q, k_ref[lo:hi, qk]), NEG_INF)
            m = jnp.max(sd, axis=-1, keepdims=True)
            if qt:
                sp = _dot_nt(q, k_ref[0:lo, qk])
                m = jnp.maximum(m, jnp.max(sp, axis=-1, keepdims=True))
            pd = jnp.exp(sd - m)
            l = jnp.sum(pd, axis=-1, keepdims=True)
            o = _dot(pd.astype(BF16), v_ref[lo:hi, vv])
            if qt:
                pp = jnp.exp(sp - m)
                l = l + jnp.sum(pp, axis=-1, keepdims=True)
                o = o + _dot(pp.astype(BF16), v_ref[0:lo, vv])
            o_ref[lo:hi, vv] = (o / l).astype(o_ref.dtype)


def attn_prompt(q, k, v, bp, s, tq, heads=2):
    return pl.pallas_call(
        functools.partial(_attn_p_kernel, tq=tq, heads=heads),
        grid=(bp, B_HEADS // heads),
        in_specs=[pl.BlockSpec((s, heads * Q_HEAD_PAD), lambda b, h: (b, h)),
                  pl.BlockSpec((s, heads * Q_HEAD_PAD), lambda b, h: (b, h)),
                  pl.BlockSpec((s, heads * B_V), lambda b, h: (b, h))],
        out_specs=pl.BlockSpec((s, heads * B_V), lambda b, h: (b, h)),
        out_shape=jax.ShapeDtypeStruct((bp * s, B_HEADS * B_V), BF16),
        compiler_params=_cparams(("parallel", "parallel")),
    )(q, k, v)


def _q_absorb_kernel(q_ref, w_ref, o_ref):
    o_ref[...] = _dot(q_ref[...], w_ref[...]).astype(o_ref.dtype)


def q_absorb(q_s, w_uk_t):
    ts = q_s.shape[0]
    c = w_uk_t.shape[1]
    return pl.pallas_call(
        _q_absorb_kernel,
        grid=(B_HEADS,),
        in_specs=[pl.BlockSpec((ts, B_NOPE), lambda h: (0, 2 * h)),
                  pl.BlockSpec((B_NOPE, c), lambda h: (h, 0))],
        out_specs=pl.BlockSpec((ts, c), lambda h: (0, h)),
        out_shape=jax.ShapeDtypeStruct((ts, B_HEADS * c), BF16),
        compiler_params=_cparams(("parallel",)),
    )(q_s, w_uk_t)


def _attn_s_kernel(pt_ref, *refs, g_pages, page, kblk, seq):
    (lat_hbm, kr_hbm, wukt_ref, qabs_ref, q_ref, cnew_ref, knew_ref, o_ref,
     wcat_scr, lat_scr, m_scr, l_scr, acc_scr, lat_buf, kr_buf, lat_sem, kr_sem) = refs
    b = pl.program_id(0)
    j = pl.program_id(1)
    nj = pl.num_programs(1)
    nq = qabs_ref.shape[1]
    nkey = wukt_ref.shape[0]
    c = wukt_ref.shape[1]
    per_blk = kblk // page

    step = b * nj + j
    slot = step % 2

    def page_copies(bb, jj, sl, g):
        pg = pt_ref[bb, jj * g_pages + g]
        return (pltpu.make_async_copy(lat_hbm.at[0, pg], lat_buf.at[sl, g], lat_sem.at[sl]),
                pltpu.make_async_copy(kr_hbm.at[0, pg], kr_buf.at[sl, g], kr_sem.at[sl]))

    def start_pages(bb, jj, sl):
        for g in range(g_pages):
            for cp in page_copies(bb, jj, sl, g):
                cp.start()

    @pl.when(step == 0)
    def _():
        start_pages(0, 0, 0)
        wcat_scr[0:nkey, :] = wukt_ref[...]

    @pl.when(step + 1 < pl.num_programs(0) * nj)
    def _():
        nxt = step + 1
        start_pages(nxt // nj, nxt % nj, 1 - slot)

    for g in range(g_pages):
        for cp in page_copies(b, j, slot, g):
            cp.wait()
    lat_refs = [lat_buf.at[slot, g] for g in range(g_pages)]
    kr_refs = [kr_buf.at[slot, g] for g in range(g_pages)]

    @pl.when(j == 0)
    def _():
        wcat_scr[nkey:nkey + nq, :] = qabs_ref[0]
        m_scr[...] = jnp.full(m_scr.shape, NEG_INF, F32)
        l_scr[...] = jnp.zeros_like(l_scr)
        acc_scr[...] = jnp.zeros_like(acc_scr)

    qrope = q_ref[0][:, LANES:LANES + B_ROPE]

    def scores(lb, krt):
        a = _dot_nt(wcat_scr[...], lb)
        kt = a[0:nkey]
        ssq = jnp.sum((kt * kt).reshape(B_HEADS, B_NOPE, kblk), axis=1)
        ssr = jnp.sum(krt * krt, axis=0, keepdims=True)
        r = lax.rsqrt((ssq + ssr) * (1.0 / B_QK) + EPS)
        return (a[nkey:nkey + nq] + _dot(qrope, krt.astype(BF16))) * jnp.concatenate([r] * seq, axis=0)

    def fold(s, lat, carry):
        m, l, acc = carry
        m_new = jnp.maximum(m, jnp.max(s, axis=-1, keepdims=True))
        alpha = jnp.exp(m - m_new)
        p = jnp.exp(s - m_new)
        return m_new, alpha * l + jnp.sum(p, axis=-1, keepdims=True), alpha * acc + _dot(p.astype(BF16), lat)

    s_blocks = []
    for sb in range(g_pages // per_blk):
        pages = range(sb * per_blk, (sb + 1) * per_blk)
        lb = jnp.concatenate([lat_refs[g][...].astype(BF16) for g in pages], axis=0)
        krt = jnp.concatenate([kr_refs[g][...] for g in pages], axis=1)
        lat_scr[sb * kblk:(sb + 1) * kblk, :] = lb
        s_blocks.append(scores(lb, krt))
    carry = fold(jnp.concatenate(s_blocks, axis=1), lat_scr[...], (m_scr[:, 0:1], l_scr[:, 0:1], acc_scr[...]))
    m, l, acc = carry
    m_scr[...] = jnp.broadcast_to(m, m_scr.shape)
    l_scr[...] = jnp.broadcast_to(l, l_scr.shape)
    acc_scr[...] = acc

    @pl.when(j == pl.num_programs(1) - 1)
    def _():
        qf = q_ref[0].astype(F32)
        t_of_row = lax.broadcasted_iota(jnp.int32, (nq, 1), 0) // B_HEADS
        cols = []
        for jn in range(seq):
            kj = knew_ref[0, jn * B_HEADS:(jn + 1) * B_HEADS, :].astype(F32)
            sj = jnp.sum(qf * jnp.concatenate([kj] * seq, axis=0), axis=-1, keepdims=True)
            cols.append(jnp.where(t_of_row >= jn, sj, NEG_INF))
        m2 = m
        for sj in cols:
            m2 = jnp.maximum(m2, sj)
        alpha = jnp.exp(m - m2)
        l2 = alpha * l
        acc2 = alpha * acc
        for jn, sj in enumerate(cols):
            pj = jnp.exp(sj - m2)
            l2 = l2 + pj
            acc2 = acc2 + pj * cnew_ref[0, jn:jn + 1, :]
        o_ref[0] = (acc2 / l2).astype(o_ref.dtype)


def attn_sample(page_table, cache_lat, cache_kr_t, w_uk_t, q_abs, q_s, c_new, k_new, g_pages, seq):
    db, n_pages = page_table.shape
    page, c = cache_lat.shape[2], cache_lat.shape[3]
    kblk = MXU_COLS
    assert kblk % page == 0 and g_pages % (kblk // page) == 0 and n_pages % g_pages == 0
    nq = q_abs.shape[1]
    nkey = w_uk_t.shape[0]
    per_b = lambda b, j, pt: (b, 0, 0)
    return pl.pallas_call(
        functools.partial(_attn_s_kernel, g_pages=g_pages, page=page, kblk=kblk, seq=seq),
        grid_spec=pltpu.PrefetchScalarGridSpec(
            num_scalar_prefetch=1,
            grid=(db, n_pages // g_pages),
            in_specs=[
                pl.BlockSpec(memory_space=pl.ANY),
                pl.BlockSpec(memory_space=pl.ANY),
                pl.BlockSpec(w_uk_t.shape, lambda b, j, pt: (0, 0)),
                pl.BlockSpec((1, nq, c), per_b),
                pl.BlockSpec((1, nq, Q_HEAD_PAD), per_b),
                pl.BlockSpec((1, seq, c), per_b),
                pl.BlockSpec((1, nq, Q_HEAD_PAD), per_b)],
            out_specs=pl.BlockSpec((1, nq, c), per_b),
            scratch_shapes=[pltpu.VMEM((nkey + nq, c), BF16),
                            pltpu.VMEM((g_pages * page, c), BF16),
                            pltpu.VMEM((nq, LANES), F32),
                            pltpu.VMEM((nq, LANES), F32),
                            pltpu.VMEM((nq, c), F32),
                            pltpu.VMEM((2, g_pages, page, c), F32),
                            pltpu.VMEM((2, g_pages, B_ROPE, page), F32),
                            pltpu.SemaphoreType.DMA((2,)),
                            pltpu.SemaphoreType.DMA((2,))]),
        out_shape=jax.ShapeDtypeStruct((db, nq, c), BF16),
        compiler_params=_cparams(("arbitrary", "arbitrary")),
    )(page_table, cache_lat, cache_kr_t, w_uk_t, q_abs, q_s, c_new, k_new)


def _uv_kernel(o_ref, w_ref, h_ref):
    h_ref[...] = _dot(o_ref[...], w_ref[...]).astype(h_ref.dtype)


def uv_expand(o_lat2, w_uv):
    ts = o_lat2.shape[0]
    c = w_uv.shape[0]
    return pl.pallas_call(
        _uv_kernel,
        grid=(B_HEADS,),
        in_specs=[pl.BlockSpec((ts, c), lambda h: (0, h)),
                  pl.BlockSpec((c, B_V), lambda h: (0, h))],
        out_specs=pl.BlockSpec((ts, B_V), lambda h: (0, h)),
        out_shape=jax.ShapeDtypeStruct((ts, B_HEADS * B_V), BF16),
        compiler_params=_cparams(("parallel",)),
    )(o_lat2, w_uv)


def _merge_kernel(hid_ref, oa_ref, ga_ref, gb_ref, hb_ref, g_ref, wa_ref, wb_ref, o_ref, ha_scr):
    @pl.when(pl.program_id(1) == 0)
    def _():
        for h in range(A_HEADS):
            sl = slice(h * A_DV, (h + 1) * A_DV)
            gate = jax.nn.sigmoid(oa_ref[:, sl].astype(F32))
            ha_scr[:, sl] = (_rms(hid_ref[:, sl].astype(F32), g_ref[:, sl]) * gate).astype(BF16)

    a = _dot(ha_scr[...], wa_ref[...])
    b = _dot(hb_ref[...], wb_ref[...])
    o_ref[...] = (jax.nn.sigmoid(ga_ref[...].astype(F32)) * a
                  + jax.nn.sigmoid(gb_ref[...].astype(F32)) * b).astype(o_ref.dtype)


def merge(hid, z_main, h_b, g_out, w_a, w_b, bm, bn):
    t, d = hid.shape
    nj = d // bn
    return pl.pallas_call(
        _merge_kernel,
        grid=(t // bm, nj),
        in_specs=[pl.BlockSpec((bm, d), lambda i, j: (i, 0)),
                  pl.BlockSpec((bm, d), lambda i, j: (i, 2)),
                  pl.BlockSpec((bm, bn), lambda i, j: (i, 3 * nj + j)),
                  pl.BlockSpec((bm, bn), lambda i, j: (i, 4 * nj + j)),
                  pl.BlockSpec((bm, d), lambda i, j: (i, 0)),
                  pl.BlockSpec((1, d), lambda i, j: (0, 0)),
                  pl.BlockSpec((d, bn), lambda i, j: (0, j)),
                  pl.BlockSpec((d, bn), lambda i, j: (0, j))],
        out_specs=pl.BlockSpec((bm, bn), lambda i, j: (i, j)),
        out_shape=jax.ShapeDtypeStruct((t, d), BF16),
        scratch_shapes=[pltpu.VMEM((bm, d), BF16)],
        compiler_params=_cparams(("parallel", "arbitrary")),
    )(hid, z_main, z_main, z_main, h_b, g_out, w_a, w_b)


def _resid_mm_kernel(x_ref, a_ref, w_ref, o_ref):
    o_ref[...] = x_ref[...] + _dot(a_ref[...], w_ref[...])


def resid_matmul(x, a, w, bm, bn):
    t, d = x.shape
    k = a.shape[1]
    return pl.pallas_call(
        _resid_mm_kernel,
        grid=(t // bm, d // bn),
        in_specs=[pl.BlockSpec((bm, bn), lambda i, j: (i, j)),
                  pl.BlockSpec((bm, k), lambda i, j: (i, 0)),
                  pl.BlockSpec((k, bn), lambda i, j: (0, j))],
        out_specs=pl.BlockSpec((bm, bn), lambda i, j: (i, j)),
        out_shape=jax.ShapeDtypeStruct((t, d), F32),
        compiler_params=_cparams(("parallel", "parallel")),
    )(x, a, w)


def _ffn_p_kernel(x_ref, xh_ref, g_ref, wg_ref, wv_ref, cg_ref, cv_ref, bg_ref, bv_ref, wd_ref,
                  y_ref, ug_ref, uv_ref, xn_scr, acc_scr, *, seq_len):
    i = pl.program_id(0)
    j = pl.program_id(1)
    bm = x_ref.shape[0]
    hp = SUBLANES

    @pl.when(j == 0)
    def _():
        xn_scr[hp:, :] = _rms(x_ref[...], g_ref[...]).astype(BF16)
        keep = jnp.where((i * bm) % seq_len == 0, 0.0, 1.0)
        xn_scr[0:hp, :] = (_rms(xh_ref[...], g_ref[...]) * keep).astype(BF16)
        acc_scr[...] = jnp.zeros_like(acc_scr)

    xall = xn_scr[...]

    def conv(u, wc_ref, bc_ref):
        e2 = u[hp:, :]
        e1 = u[hp - 1:hp - 1 + bm, :]
        e0 = u[hp - 2:hp - 2 + bm, :]
        return bc_ref[...] + ((e0 * wc_ref[0:1, :] + e1 * wc_ref[1:2, :]) + e2 * wc_ref[2:3, :])

    ug = _dot(xall, wg_ref[...])
    uv = _dot(xall, wv_ref[...])
    ug_ref[0] = ug[bm:, :]
    uv_ref[0] = uv[bm:, :]
    cg = conv(ug, cg_ref, bg_ref)
    cv = conv(uv, cv_ref, bv_ref)
    act = (cg * jax.nn.sigmoid(cg) * cv).astype(BF16)
    acc_scr[...] += _dot(act, wd_ref[...])

    @pl.when(j == pl.num_programs(1) - 1)
    def _():
        y_ref[...] = x_ref[...] + acc_scr[...]


def _ffn_weight_specs(d, bf, nj):
    return [pl.BlockSpec((1, d), lambda i, j: (0, 0)),
            pl.BlockSpec((d, bf), lambda i, j: (0, j)),
            pl.BlockSpec((d, bf), lambda i, j: (0, nj + j)),
            pl.BlockSpec((CONV_W, bf), lambda i, j: (0, j)),
            pl.BlockSpec((CONV_W, bf), lambda i, j: (0, nj + j)),
            pl.BlockSpec((1, bf), lambda i, j: (0, j)),
            pl.BlockSpec((1, bf), lambda i, j: (0, nj + j)),
            pl.BlockSpec((bf, d), lambda i, j: (j, 0))]


def conv_ffn_prompt(x1, g, w_up, w_conv, b_conv, w_down, bm, bf, seq_len):
    rows, d = x1.shape
    f = w_down.shape[0]
    assert seq_len % bm == 0 and bm >= CONV_W - 1, "row blocks must not straddle sequences"
    nj = f // bf
    nb = rows // bm
    hb = bm // SUBLANES
    return pl.pallas_call(
        functools.partial(_ffn_p_kernel, seq_len=seq_len),
        grid=(nb, nj),
        in_specs=[pl.BlockSpec((bm, d), lambda i, j: (i, 0)),
                  pl.BlockSpec((SUBLANES, d), lambda i, j: (jnp.maximum(i * hb - 1, 0), 0))]
        + _ffn_weight_specs(d, bf, nj),
        out_specs=[pl.BlockSpec((bm, d), lambda i, j: (i, 0)),
                   pl.BlockSpec((1, SUBLANES, bf), lambda i, j: (i, 0, j)),
                   pl.BlockSpec((1, SUBLANES, bf), lambda i, j: (i, 0, j))],
        out_shape=[jax.ShapeDtypeStruct((rows, d), F32),
                   jax.ShapeDtypeStruct((nb, SUBLANES, f), F32),
                   jax.ShapeDtypeStruct((nb, SUBLANES, f), F32)],
        scratch_shapes=[pltpu.VMEM((bm + SUBLANES, d), BF16),
                        pltpu.VMEM((bm, d), F32)],
        compiler_params=_cparams(("parallel", "arbitrary")),
    )(x1, x1, g, w_up, w_up, w_conv, w_conv, b_conv, b_conv, w_down)


def _ffn_s_kernel(x_ref, g_ref, wg_ref, wv_ref, cg_ref, cv_ref, bg_ref, bv_ref, wd_ref, prev_ref,
                  y_ref, new_ref, xn_scr, acc_scr, *, seq):
    j = pl.program_id(1)
    nb = x_ref.shape[0] // seq

    @pl.when(j == 0)
    def _():
        xn_scr[...] = _rms(x_ref[...], g_ref[...]).astype(BF16)
        acc_scr[...] = jnp.zeros_like(acc_scr)

    xn = xn_scr[...]

    def conv(u, half, wc_ref, bc_ref):
        ext = [prev_ref[r, half] for r in range(CONV_W - 1)] + [u[t * nb:(t + 1) * nb, :] for t in range(seq)]
        for r in range(CONV_W - 1):
            new_ref[r, half] = ext[seq + r]
        return jnp.concatenate(
            [bc_ref[...] + ((ext[t] * wc_ref[0:1, :] + ext[t + 1] * wc_ref[1:2, :]) + ext[t + 2] * wc_ref[2:3, :])
             for t in range(seq)], axis=0)

    cg = conv(_dot(xn, wg_ref[...]), 0, cg_ref, bg_ref)
    cv = conv(_dot(xn, wv_ref[...]), 1, cv_ref, bv_ref)
    act = (cg * jax.nn.sigmoid(cg) * cv).astype(BF16)
    acc_scr[...] += _dot(act, wd_ref[...])

    @pl.when(j == pl.num_programs(1) - 1)
    def _():
        y_ref[...] = x_ref[...] + acc_scr[...]


def conv_ffn_sample(x1_t, prev_t, g, w_up, w_conv, b_conv, w_down, bf, seq):
    rows, d = x1_t.shape
    f = w_down.shape[0]
    nj = f // bf
    nb = rows // seq
    hist = pl.BlockSpec((CONV_W - 1, 2, nb, bf), lambda i, j: (0, 0, 0, j))
    return pl.pallas_call(
        functools.partial(_ffn_s_kernel, seq=seq),
        grid=(1, nj),
        in_specs=[pl.BlockSpec((rows, d), lambda i, j: (0, 0))] + _ffn_weight_specs(d, bf, nj) + [hist],
        out_specs=[pl.BlockSpec((rows, d), lambda i, j: (0, 0)), hist],
        out_shape=[jax.ShapeDtypeStruct((rows, d), F32),
                   jax.ShapeDtypeStruct(prev_t.shape, F32)],
        scratch_shapes=[pltpu.VMEM((rows, d), BF16),
                        pltpu.VMEM((rows, d), F32)],
        compiler_params=_cparams(("parallel", "arbitrary")),
    )(x1_t, g, w_up, w_up, w_conv, w_conv, b_conv, b_conv, w_down, prev_t)


def _pick(n, prefs):
    for p in prefs:
        if n % p == 0:
            return p
    return n


def _layer(x_prompt, x_sample, cache_lat, cache_kr, c_state, n_state, m_state, conv_state, page_table,
           g_attn_norm, w_in, b_gates, g_q_a, w_uq, g_qk_nope_q, g_qk_rope_q, g_kv_a, w_uk, w_uv,
           g_qk_nope_k, g_qk_rope_k, g_mlstm_out, w_branch_a, w_branch_b, w_out, g_ffn_norm,
           w_up, w_conv, b_conv, w_down):
    bp, s, d = x_prompt.shape
    db, seq, _ = x_sample.shape
    n_pages = page_table.shape[1]
    page = cache_lat.shape[2]
    past = n_pages * page
    tp, ts = bp * s, db * seq
    t = tp + ts
    c_lat = g_kv_a.shape[0]
    c_q = g_q_a.shape[0]
    f = w_down.shape[0]
    qk_w = A_HEADS * A_DQK
    v_w = A_HEADS * A_DV
    row = lambda v: v.reshape(1, -1)

    o_i = 2 * qk_w + 2 * v_w
    o_cq = o_i + 2 * A_HEADS
    o_kr = o_cq + c_q + c_lat
    o_g = o_kr + B_ROPE
    w_qkvo = w_in[:, :o_i].astype(BF16)
    w_gab = w_in[:, o_g:].astype(BF16)
    n_small = LANES - B_ROPE - 2 * A_HEADS
    w_lat = jnp.concatenate([w_in[:, o_cq:o_kr], w_in[:, o_kr:o_g], w_in[:, o_i:o_cq],
                             jnp.zeros((d, n_small), F32)], axis=1).astype(BF16)
    w_uq_pad = jnp.pad(w_uq, ((0, 0), (0, 0), (0, Q_HEAD_PAD - B_QK))).reshape(c_q, B_HEADS * Q_HEAD_PAD).astype(BF16)
    w_uk2 = w_uk.reshape(c_lat, B_HEADS * B_NOPE).astype(BF16)
    w_uk_t = jnp.transpose(w_uk.reshape(c_lat, B_HEADS * B_NOPE)).astype(BF16)
    w_uv2 = w_uv.reshape(c_lat, B_HEADS * B_V).astype(BF16)
    gq = jnp.concatenate([g_qk_nope_q * g_qk_nope_k, g_qk_rope_q * g_qk_rope_k, g_qk_rope_q * g_qk_rope_k,
                          jnp.zeros((Q_HEAD_PAD - B_QK,), F32)]) * ATTN_SCALE
    gq = jnp.tile(gq, B_HEADS).reshape(1, -1)

    w_a_b, w_b_b, w_out_b = w_branch_a.astype(BF16), w_branch_b.astype(BF16), w_out.astype(BF16)
    w_up_b, w_down_b = w_up.astype(BF16), w_down.astype(BF16)
    half = B_ROPE // 2
    freqs = ROPE_THETA ** (-jnp.arange(half, dtype=F32) / half)

    def token_stage(x, pos, bm):
        rows = x.shape[0]
        ang = pos.astype(F32)[:, None] * freqs
        cos, sin = jnp.cos(ang), jnp.sin(ang)
        zh = jnp.zeros((rows, half), F32)
        zr = jnp.zeros((rows, LANES - B_ROPE), F32)
        cos128 = jnp.concatenate([cos, cos, zr], axis=1)
        sin_a = jnp.concatenate([zh, sin, zr], axis=1)
        sin_b = jnp.concatenate([-sin, zh, zr], axis=1)
        bn = _pick(d, (1024, 512, 256, 128))
        z_main = norm_matmul2(x, row(g_attn_norm), w_qkvo, w_gab, bm, bn, BF16)
        z_lat = norm_matmul(x, row(g_attn_norm), w_lat, bm, w_lat.shape[1])
        bq = _pick(rows, (256, 128, 64, 32, 16, 8))
        q = q_proj(z_lat, row(g_q_a), w_uq_pad, cos128, sin_a, sin_b, gq, bq)
        c_kv, kr128, k, v = kv_proj(z_lat, row(g_kv_a), w_uk2, w_uv2, cos128, sin_a, sin_b, bq)
        g_t = jnp.transpose(z_lat[:, c_q + c_lat + B_ROPE:c_q + c_lat + B_ROPE + 2 * A_HEADS])
        gates = (g_t[:A_HEADS, None, :], g_t[A_HEADS:, None, :], g_t[:A_HEADS, :, None], g_t[A_HEADS:, :, None])
        return z_main, q, c_kv, kr128, k, v, gates

    def mix_stage(x, hid, z_main, h_b):
        bm = _pick(x.shape[0], (1024, 512, 256, 128, 64, 32, 16, 8))
        merged = merge(hid, z_main, h_b, row(g_mlstm_out), w_a_b, w_b_b, bm, _pick(d, (512, 256, 128)))
        return resid_matmul(x, merged, w_out_b, bm, _pick(d, (1024, 512, 256, 128)))

    bff = _pick(f, (512, 256, 128))

    xp = x_prompt.reshape(tp, d)
    bm_p = _pick(tp, (1024, 512, 256, 128, 64, 32, 16, 8))
    zm_p, q_p, ckv_p, kr_p, k_p, v_p, gates_p = token_stage(xp, jnp.tile(jnp.arange(s, dtype=jnp.int32), bp), bm_p)
    chunk = _pick(s, (256, 128, 64, 32, 16, 8))
    hid_p, c_p, n_p, m_p = mlstm_prompt(zm_p, b_gates, *gates_p, bp, s, chunk)
    hb_p = attn_prompt(q_p, k_p, v_p, bp, s, _pick(s, (512, 256, 128)))
    x1_p = mix_stage(xp, hid_p, zm_p, hb_p)
    bmf_p = _pick(s, (512, 256, 128, 64, 32, 16, 8))
    y_p, tg_p, tv_p = conv_ffn_prompt(x1_p, row(g_ffn_norm), w_up_b, w_conv, row(b_conv), w_down_b, bmf_p, bff, s)
    nb_seq = s // bmf_p
    tail_p = jnp.concatenate([tg_p, tv_p], axis=-1).reshape(bp, nb_seq, SUBLANES, 2 * f)
    conv_p = tail_p[:, nb_seq - 1, SUBLANES - (CONV_W - 1):, :]

    xs = x_sample.reshape(ts, d)
    bm_s = _pick(ts, (512, 256, 128, 64, 32, 16, 8))
    zm_s, q_s, ckv_s, kr_s, k_s, _, gates_s = token_stage(
        xs, jnp.tile(past + jnp.arange(seq, dtype=jnp.int32), db), bm_s)
    rb = _pick(ts, (128, 64, 32, 16, 8))
    m_col = jnp.repeat(jnp.transpose(m_state), seq, axis=1)[:, :, None]
    n_rows = jnp.repeat(jnp.transpose(n_state, (1, 0, 2)), seq, axis=1)
    hid_s, c_s, n_s_rows, m_s_rows = mlstm_sample(zm_s, b_gates, *gates_s, m_col, n_rows, c_state[None], rb, seq)
    n_s = jnp.transpose(n_s_rows[:, seq - 1::seq, :], (1, 0, 2))
    m_s = jnp.transpose(m_s_rows[:, seq - 1::seq, 0])

    q_abs = q_absorb(q_s, w_uk_t).reshape(db, seq * B_HEADS, c_lat)
    g_pages = _pick(n_pages, (32, 16, 8, 4, 2))
    o_lat = attn_sample(page_table, cache_lat, jnp.swapaxes(cache_kr, 2, 3), w_uk_t, q_abs,
                        q_s.reshape(db, seq * B_HEADS, Q_HEAD_PAD), ckv_s.reshape(db, seq, c_lat),
                        k_s.reshape(db, seq * B_HEADS, Q_HEAD_PAD), g_pages, seq)
    hb_s = uv_expand(o_lat.reshape(ts, B_HEADS * c_lat), w_uv2)
    x1_s = mix_stage(xs, hid_s, zm_s, hb_s)
    x1_t = jnp.transpose(x1_s.reshape(db, seq, d), (1, 0, 2)).reshape(ts, d)
    prev_t = jnp.transpose(conv_state.reshape(db, CONV_W - 1, 2, f), (1, 2, 0, 3))
    y_t, new_t = conv_ffn_sample(x1_t, prev_t, row(g_ffn_norm), w_up_b, w_conv, row(b_conv), w_down_b, bff, seq)
    y_s = jnp.transpose(y_t.reshape(seq, db, d), (1, 0, 2))
    conv_s = jnp.transpose(new_t, (2, 0, 1, 3)).reshape(db, CONV_W - 1, 2 * f)

    new_p = (ckv_p.reshape(bp, s, c_lat), kr_p[:, :B_ROPE].reshape(bp, s, B_ROPE),
             c_p, n_p[:, :, 0, :], m_p[:, :, 0, 0], conv_p)
    new_s = (ckv_s.reshape(db, seq, c_lat), kr_s[:, :B_ROPE].reshape(db, seq, B_ROPE),
             c_s[0], n_s, m_s, conv_s)
    return y_p.reshape(bp, s, d), y_s.reshape(db, seq, d), new_p, new_s


def kernel(x_prompt, x_sample, cache_kv_latent, cache_k_rope, state_mlstm_C, state_mlstm_n, state_mlstm_m, state_conv, page_table, g_attn_norm, w_in, b_gates, g_q_a, w_uq, g_qk_nope_q, g_qk_rope_q, g_kv_a, w_uk, w_uv, g_qk_nope_k, g_qk_rope_k, g_mlstm_out, w_branch_a, w_branch_b, w_out, g_ffn_norm, w_up, w_conv, b_conv, w_down):
    depth = w_in.shape[0]
    assert depth == 1, "single-layer trunk"
    l = 0
    y_p, y_s, new_p, new_s = _layer(
        x_prompt, x_sample, cache_kv_latent, cache_k_rope, state_mlstm_C[l], state_mlstm_n[l],
        state_mlstm_m[l], state_conv[l], page_table, g_attn_norm[l], w_in[l], b_gates[l], g_q_a[l], w_uq[l],
        g_qk_nope_q[l], g_qk_rope_q[l], g_kv_a[l], w_uk[l], w_uv[l], g_qk_nope_k[l], g_qk_rope_k[l],
        g_mlstm_out[l], w_branch_a[l], w_branch_b[l], w_out[l], g_ffn_norm[l], w_up[l], w_conv[l], b_conv[l],
        w_down[l])
    dts = (cache_kv_latent.dtype, cache_k_rope.dtype, state_mlstm_C.dtype, state_mlstm_n.dtype,
           state_mlstm_m.dtype, state_conv.dtype)
    st_p = tuple(a[None].astype(dt) for a, dt in zip(new_p, dts))
    st_s = tuple(a[None].astype(dt) for a, dt in zip(new_s, dts))
    return (y_p, y_s) + st_p + st_s
```

```python
import functools
import math

import jax
import jax.numpy as jnp
from jax import lax
from jax.experimental import pallas as pl
from jax.experimental.pallas import tpu as pltpu

F32 = jnp.float32
BF16 = jnp.bfloat16

A_HEADS = 8
A_DQK = 128
A_DV = 256
K_SCALE = A_DQK ** -0.5
GATE_CAP = 15.0
B_HEADS = 16
B_NOPE = 128
B_ROPE = 64
B_QK = B_NOPE + B_ROPE
B_V = 128
Q_HEAD_PAD = 256
ROPE_THETA = 10000.0
ATTN_SCALE = B_QK ** -0.5
CONV_W = 3
EPS = 1e-6
NEG_INF = float("-inf")

LANES = 128
SUBLANES = 8
MXU_COLS = 256
VMEM_LIMIT = 56 * 1024 * 1024


def _cparams(sem):
    return pltpu.CompilerParams(dimension_semantics=sem, vmem_limit_bytes=VMEM_LIMIT)


def _rms(x, g):
    r = lax.rsqrt(jnp.mean(x * x, axis=-1, keepdims=True) + EPS)
    return x * r * g


def _dot(a, b):
    return jnp.dot(a, b, preferred_element_type=F32)


def _dot_nt(a, b):
    return lax.dot_general(a, b, (((1,), (1,)), ((), ())), preferred_element_type=F32)


def _rope128(x, cos, sin_a, sin_b):
    return x * cos + pltpu.roll(x, 32, 1) * sin_a + pltpu.roll(x, 96, 1) * sin_b


def _norm_mm_kernel(x_ref, g_ref, w_ref, o_ref, xn_ref):
    @pl.when(pl.program_id(1) == 0)
    def _():
        xn_ref[...] = _rms(x_ref[...], g_ref[...]).astype(BF16)

    o_ref[...] = _dot(xn_ref[...], w_ref[...]).astype(o_ref.dtype)


def norm_matmul(x, g, w, bm, bn, out_dtype=F32):
    t, d = x.shape
    n = w.shape[1]
    return pl.pallas_call(
        _norm_mm_kernel,
        grid=(t // bm, n // bn),
        in_specs=[pl.BlockSpec((bm, d), lambda i, j: (i, 0)),
                  pl.BlockSpec((1, d), lambda i, j: (0, 0)),
                  pl.BlockSpec((d, bn), lambda i, j: (0, j))],
        out_specs=pl.BlockSpec((bm, bn), lambda i, j: (i, j)),
        out_shape=jax.ShapeDtypeStruct((t, n), out_dtype),
        scratch_shapes=[pltpu.VMEM((bm, d), BF16)],
        compiler_params=_cparams(("parallel", "arbitrary")),
    )(x, g, w)


def _norm_mm2_kernel(x_ref, g_ref, wa_ref, wb_ref, o_ref, xn_ref, *, na_tiles):
    j = pl.program_id(1)

    @pl.when(j == 0)
    def _():
        xn_ref[...] = _rms(x_ref[...], g_ref[...]).astype(BF16)

    @pl.when(j < na_tiles)
    def _():
        o_ref[...] = _dot(xn_ref[...], wa_ref[...]).astype(o_ref.dtype)

    @pl.when(j >= na_tiles)
    def _():
        o_ref[...] = _dot(xn_ref[...], wb_ref[...]).astype(o_ref.dtype)


def norm_matmul2(x, g, w_a, na_cols, w_b, bm, bn, out_dtype):
    t, d = x.shape
    assert na_cols % bn == 0 and w_b.shape[1] % bn == 0
    na_tiles, nb_tiles = na_cols // bn, w_b.shape[1] // bn
    return pl.pallas_call(
        functools.partial(_norm_mm2_kernel, na_tiles=na_tiles),
        grid=(t // bm, na_tiles + nb_tiles),
        in_specs=[pl.BlockSpec((bm, d), lambda i, j: (i, 0)),
                  pl.BlockSpec((1, d), lambda i, j: (0, 0)),
                  pl.BlockSpec((d, bn), lambda i, j: (0, jnp.minimum(j, na_tiles - 1))),
                  pl.BlockSpec((d, bn), lambda i, j: (0, jnp.maximum(j - na_tiles, 0)))],
        out_specs=pl.BlockSpec((bm, bn), lambda i, j: (i, j)),
        out_shape=jax.ShapeDtypeStruct((t, (na_tiles + nb_tiles) * bn), out_dtype),
        scratch_shapes=[pltpu.VMEM((bm, d), BF16)],
        compiler_params=_cparams(("parallel", "arbitrary")),
    )(x, g, w_a, w_b)


def _q_kernel(cq_ref, g_ref, w_ref, cos_ref, sr_ref, gain_ref, o_ref):
    cqn = _rms(cq_ref[...], g_ref[...]).astype(BF16)
    q = _dot(cqn, w_ref[...])
    cos, sr = cos_ref[...], sr_ref[...]
    for h in range(B_HEADS):
        lo = h * Q_HEAD_PAD
        nope = q[:, lo:lo + LANES]
        rp = q[:, lo + LANES:lo + Q_HEAD_PAD]
        rp = rp * cos + pltpu.roll(rp, B_ROPE // 2, 1) * sr
        ssq = jnp.sum(nope * nope + rp * rp, axis=-1, keepdims=True)
        r = lax.rsqrt(ssq * (1.0 / B_QK) + EPS)
        o_ref[:, lo:lo + LANES] = (nope * r * gain_ref[:, lo:lo + LANES]).astype(o_ref.dtype)
        o_ref[:, lo + LANES:lo + Q_HEAD_PAD] = (rp * r * gain_ref[:, lo + LANES:lo + Q_HEAD_PAD]).astype(o_ref.dtype)


def q_proj(z_lat, g_q_a, w_uq_pad, cos, sin_roll, gain, bm):
    t = z_lat.shape[0]
    c = g_q_a.shape[1]
    n = w_uq_pad.shape[1]
    return pl.pallas_call(
        _q_kernel,
        grid=(t // bm,),
        in_specs=[pl.BlockSpec((bm, c), lambda i: (i, 0)),
                  pl.BlockSpec((1, c), lambda i: (0, 0)),
                  pl.BlockSpec((c, n), lambda i: (0, 0)),
                  pl.BlockSpec((bm, LANES), lambda i: (i, 0)),
                  pl.BlockSpec((bm, LANES), lambda i: (i, 0)),
                  pl.BlockSpec((1, n), lambda i: (0, 0))],
        out_specs=pl.BlockSpec((bm, n), lambda i: (i, 0)),
        out_shape=jax.ShapeDtypeStruct((t, n), BF16),
        compiler_params=_cparams(("parallel",)),
    )(z_lat, g_q_a, w_uq_pad, cos, sin_roll, gain)


def _kv_kernel(ckv_ref, sm_ref, g_ref, wuk_ref, wuv_ref, cos_ref, sa_ref, sb_ref,
               c_ref, kr_ref, k_ref, v_ref):
    c = _rms(ckv_ref[...], g_ref[...])
    c_ref[...] = c
    cb = c.astype(BF16)
    kn = _dot(cb, wuk_ref[...])
    v_ref[...] = _dot(cb, wuv_ref[...]).astype(v_ref.dtype)
    kr = _rope128(sm_ref[...], cos_ref[...], sa_ref[...], sb_ref[...])
    kr_ref[...] = kr
    ssr = jnp.sum(kr * kr, axis=-1, keepdims=True)
    for h in range(B_HEADS):
        nope = kn[:, h * B_NOPE:(h + 1) * B_NOPE]
        r = lax.rsqrt((jnp.sum(nope * nope, axis=-1, keepdims=True) + ssr) * (1.0 / B_QK) + EPS)
        lo = h * Q_HEAD_PAD
        k_ref[:, lo:lo + LANES] = (nope * r).astype(k_ref.dtype)
        k_ref[:, lo + LANES:lo + Q_HEAD_PAD] = (kr * r).astype(k_ref.dtype)


def kv_proj(z_lat, g_kv_a, w_uk, w_uv, cos, sa, sb, bm):
    t = z_lat.shape[0]
    c = g_kv_a.shape[1]
    small_blk = (2 * c) // LANES
    row = lambda i: (i, 0)
    const = lambda i: (0, 0)
    return pl.pallas_call(
        _kv_kernel,
        grid=(t // bm,),
        in_specs=[pl.BlockSpec((bm, c), lambda i: (i, 1)),
                  pl.BlockSpec((bm, LANES), lambda i: (i, small_blk)),
                  pl.BlockSpec((1, c), const),
                  pl.BlockSpec(w_uk.shape, const),
                  pl.BlockSpec(w_uv.shape, const),
                  pl.BlockSpec((bm, LANES), row),
                  pl.BlockSpec((bm, LANES), row),
                  pl.BlockSpec((bm, LANES), row)],
        out_specs=[pl.BlockSpec((bm, c), row),
                   pl.BlockSpec((bm, LANES), row),
                   pl.BlockSpec((bm, B_HEADS * Q_HEAD_PAD), row),
                   pl.BlockSpec((bm, B_HEADS * B_V), row)],
        out_shape=[jax.ShapeDtypeStruct((t, c), F32),
                   jax.ShapeDtypeStruct((t, LANES), F32),
                   jax.ShapeDtypeStruct((t, B_HEADS * Q_HEAD_PAD), BF16),
                   jax.ShapeDtypeStruct((t, B_HEADS * B_V), BF16)],
        compiler_params=_cparams(("parallel",)),
    )(z_lat, z_lat, g_kv_a, w_uk, w_uv, cos, sa, sb)


def _log_gates(gi, gf, bi, bf):
    cap = lambda x: GATE_CAP * jnp.tanh(x * (1.0 / GATE_CAP))
    li = cap(gi + bi)
    y = cap(gf + bf)
    lf = jnp.minimum(y, 0.0) - jnp.log(1.0 + jnp.exp(-jnp.abs(y)))
    return li, lf


def _mlstm_p_kernel(b_ref, q_ref, k_ref, v_ref, gir_ref, gfr_ref, gic_ref, gfc_ref,
                    hid_ref, c_out_ref, n_out_ref, m_out_ref, c_scr, n_scr, m_scr):
    ci = pl.program_id(1)
    L = q_ref.shape[0]

    @pl.when(ci == 0)
    def _():
        c_scr[...] = jnp.zeros_like(c_scr)
        n_scr[...] = jnp.zeros_like(n_scr)
        m_scr[...] = jnp.zeros_like(m_scr)

    t_idx = lax.broadcasted_iota(jnp.int32, (L, L), 0)
    s_idx = lax.broadcasted_iota(jnp.int32, (L, L), 1)
    causal = s_idx <= t_idx
    anti = t_idx <= s_idx
    for h in range(A_HEADS):
        bi = b_ref[h]
        bf = b_ref[A_HEADS + h]
        qb = q_ref[:, h * A_DQK:(h + 1) * A_DQK]
        kb = k_ref[:, h * A_DQK:(h + 1) * A_DQK]
        vb = v_ref[:, h * A_DV:(h + 1) * A_DV]
        q = qb.astype(F32)
        li_r, lf_r = _log_gates(gir_ref[h], gfr_ref[h], bi, bf)
        li_c, lf_c = _log_gates(gic_ref[h], gfc_ref[h], bi, bf)

        bcum_c = jnp.sum(jnp.where(causal, lf_r, 0.0), axis=1, keepdims=True)
        bcum_r = jnp.sum(jnp.where(anti, lf_c, 0.0), axis=0, keepdims=True)
        log_w = jnp.where(causal, bcum_c - bcum_r + li_r, NEG_INF)
        m_prev = m_scr[h][:, 0:1]
        c_prev = c_scr[h]
        n_prev = n_scr[h]
        log_state = bcum_c + m_prev
        m_row = jnp.maximum(log_state, jnp.max(log_w, axis=1, keepdims=True))
        w = jnp.exp(log_w - m_row) * K_SCALE
        w_state = jnp.exp(log_state - m_row)
        sc = _dot_nt(qb, kb) * w
        num = _dot(sc.astype(BF16), vb) + w_state * _dot(qb, c_prev.astype(BF16))
        den = jnp.sum(sc, axis=1, keepdims=True) + w_state * jnp.sum(q * n_prev, axis=1, keepdims=True)
        hid_ref[:, h * A_DV:(h + 1) * A_DV] = (
            num / jnp.maximum(jnp.abs(den), jnp.exp(-m_row))).astype(hid_ref.dtype)

        b_last = bcum_c[L - 1:L, :]
        log_k = b_last - bcum_c + li_c
        m_new = jnp.maximum(b_last + m_prev, jnp.max(log_k, axis=0, keepdims=True))
        wk = jnp.exp(log_k - m_new) * K_SCALE
        decay = jnp.exp(b_last + m_prev - m_new)
        kw = kb.astype(F32) * wk
        kwt = jnp.transpose(kw).astype(BF16)
        c_scr[h] = decay * c_prev + _dot(kwt, vb)
        n_scr[h] = decay * n_prev + jnp.sum(kw, axis=0, keepdims=True)
        m_scr[h] = jnp.broadcast_to(m_new, (1, LANES))

    @pl.when(ci == pl.num_programs(1) - 1)
    def _():
        c_out_ref[0] = c_scr[...]
        n_out_ref[0] = n_scr[...]
        m_out_ref[0] = m_scr[...]


def mlstm_prompt(z_main, b_gates, gi_r, gf_r, gi_c, gf_c, bp, s, chunk):
    nc = s // chunk
    qk_w, v_w = A_HEADS * A_DQK, A_HEADS * A_DV
    assert (2 * qk_w) % v_w == 0
    tok = lambda b, c, *_: b * nc + c
    return pl.pallas_call(
        _mlstm_p_kernel,
        grid_spec=pltpu.PrefetchScalarGridSpec(
            num_scalar_prefetch=1,
            grid=(bp, nc),
            in_specs=[pl.BlockSpec((chunk, qk_w), lambda b, c, *_: (tok(b, c), 0)),
                      pl.BlockSpec((chunk, qk_w), lambda b, c, *_: (tok(b, c), 1)),
                      pl.BlockSpec((chunk, v_w), lambda b, c, *_: (tok(b, c), (2 * qk_w) // v_w)),
                      pl.BlockSpec((A_HEADS, 1, chunk), lambda b, c, *_: (0, 0, tok(b, c))),
                      pl.BlockSpec((A_HEADS, 1, chunk), lambda b, c, *_: (0, 0, tok(b, c))),
                      pl.BlockSpec((A_HEADS, chunk, 1), lambda b, c, *_: (0, tok(b, c), 0)),
                      pl.BlockSpec((A_HEADS, chunk, 1), lambda b, c, *_: (0, tok(b, c), 0))],
            out_specs=[pl.BlockSpec((chunk, v_w), lambda b, c, *_: (tok(b, c), 0)),
                       pl.BlockSpec((1, A_HEADS, A_DQK, A_DV), lambda b, c, *_: (b, 0, 0, 0)),
                       pl.BlockSpec((1, A_HEADS, 1, A_DQK), lambda b, c, *_: (b, 0, 0, 0)),
                       pl.BlockSpec((1, A_HEADS, 1, LANES), lambda b, c, *_: (b, 0, 0, 0))],
            scratch_shapes=[pltpu.VMEM((A_HEADS, A_DQK, A_DV), F32),
                            pltpu.VMEM((A_HEADS, 1, A_DQK), F32),
                            pltpu.VMEM((A_HEADS, 1, LANES), F32)]),
        out_shape=[jax.ShapeDtypeStruct((bp * s, v_w), BF16),
                   jax.ShapeDtypeStruct((bp, A_HEADS, A_DQK, A_DV), F32),
                   jax.ShapeDtypeStruct((bp, A_HEADS, 1, A_DQK), F32),
                   jax.ShapeDtypeStruct((bp, A_HEADS, 1, LANES), F32)],
        compiler_params=_cparams(("parallel", "arbitrary")),
    )(b_gates, z_main, z_main, z_main, gi_r, gf_r, gi_c, gf_c)


def _mlstm_s_kernel(b_ref, q_ref, k_ref, v_ref, gir_ref, gfr_ref, gic_ref, gfc_ref,
                    mc_ref, nrow_ref, c_in_ref,
                    hid_ref, c_out_ref, n_out_ref, m_out_ref,
                    hc_scr, dec_scr, kwt_scr, q_scr, *, seq):
    h = pl.program_id(1)
    R = q_ref.shape[0]
    nseq = R // seq
    per8 = SUBLANES // seq
    bi = b_ref[h]
    bf = b_ref[A_HEADS + h]
    q = q_ref[...].astype(F32)
    k = k_ref[...].astype(F32)
    v = v_ref[...].astype(F32)
    q_scr[...] = q
    li_r, lf_r = _log_gates(gir_ref[0], gfr_ref[0], bi, bf)
    li_c, lf_c = _log_gates(gic_ref[0], gfc_ref[0], bi, bf)
    m_prev = mc_ref[0]

    t_idx = lax.broadcasted_iota(jnp.int32, (R, R), 0)
    s_idx = lax.broadcasted_iota(jnp.int32, (R, R), 1)
    same = (t_idx // seq) == (s_idx // seq)
    causal = same & (s_idx <= t_idx)
    bcum_c = jnp.sum(jnp.where(causal, lf_r, 0.0), axis=1, keepdims=True)
    bcum_r = jnp.sum(jnp.where(same & (t_idx <= s_idx), lf_c, 0.0), axis=0, keepdims=True)
    blast_c = jnp.sum(jnp.where(same, lf_r, 0.0), axis=1, keepdims=True)
    blast_r = jnp.sum(jnp.where(same, lf_c, 0.0), axis=0, keepdims=True)
    log_w = jnp.where(causal, bcum_c - bcum_r + li_r, NEG_INF)
    log_state = bcum_c + m_prev
    m_row = jnp.maximum(log_state, jnp.max(log_w, axis=1, keepdims=True))
    w = jnp.exp(log_w - m_row) * K_SCALE
    w_state = jnp.exp(log_state - m_row)

    log_k_c = blast_c - bcum_c + li_c
    log_k_r = blast_r - bcum_r + li_r
    seg_max = jnp.max(jnp.where(same, log_k_r, NEG_INF), axis=1, keepdims=True)
    m_new = jnp.maximum(blast_c + m_prev, seg_max)
    wk = jnp.exp(log_k_c - m_new) * K_SCALE
    decay = jnp.exp(blast_c + m_prev - m_new)
    kw = k * wk
    kwt_scr[...] = jnp.transpose(kw)
    dec_scr[...] = jnp.broadcast_to(decay, dec_scr.shape)

    row8 = lax.broadcasted_iota(jnp.int32, (SUBLANES, A_DV), 0)
    col = lax.broadcasted_iota(jnp.int32, (A_DQK, R), 1)

    def group(gidx, carry):
        r0 = pl.multiple_of(gidx * SUBLANES, SUBLANES)
        q8 = q_scr[pl.ds(r0, SUBLANES), :]
        hc8 = jnp.zeros((SUBLANES, A_DV), F32)
        for u in range(per8):
            sq = gidx * per8 + u
            c0 = c_in_ref[0, sq, 0]
            res = _dot(q8, c0)
            hc8 = jnp.where((row8 // seq) == u, res, hc8)
            kwt_m = jnp.where((col // seq) == sq, kwt_scr[...], 0.0)
            dsc = dec_scr[pl.ds(sq * seq, 1), 0:1]
            c_out_ref[0, sq, 0] = dsc * c0 + _dot(kwt_m.astype(BF16), v_ref[...])
        hc_scr[pl.ds(r0, SUBLANES), :] = hc8
        return carry

    lax.fori_loop(0, R // SUBLANES, group, 0, unroll=4)

    sc = _dot_nt(q, k) * w
    num = _dot(sc, v) + w_state * hc_scr[...]
    den = jnp.sum(sc, axis=1, keepdims=True) + w_state * jnp.sum(q * nrow_ref[0], axis=1, keepdims=True)
    hid_ref[...] = (num / jnp.maximum(jnp.abs(den), jnp.exp(-m_row))).astype(hid_ref.dtype)

    acc = kw
    shift = 1
    while shift < seq:
        acc = acc + jnp.where((lax.broadcasted_iota(jnp.int32, acc.shape, 0) % seq) >= shift,
                              pltpu.roll(acc, shift, 0), 0.0)
        shift *= 2
    n_out_ref[0] = decay * nrow_ref[0] + acc
    m_out_ref[0] = jnp.broadcast_to(m_new, (R, LANES))


def mlstm_sample(z_main_s, b_gates, gi_r, gf_r, gi_c, gf_c, m_col, n_rows, c_state, rb, seq):
    ts = z_main_s.shape[0]
    nseq_blk = rb // seq
    kcol = (A_HEADS * A_DQK) // A_DQK
    vcol = (2 * A_HEADS * A_DQK) // A_DV
    return pl.pallas_call(
        functools.partial(_mlstm_s_kernel, seq=seq),
        grid_spec=pltpu.PrefetchScalarGridSpec(
            num_scalar_prefetch=1,
            grid=(ts // rb, A_HEADS),
            in_specs=[pl.BlockSpec((rb, A_DQK), lambda i, h, *_: (i, h)),
                      pl.BlockSpec((rb, A_DQK), lambda i, h, *_: (i, kcol + h)),
                      pl.BlockSpec((rb, A_DV), lambda i, h, *_: (i, vcol + h)),
                      pl.BlockSpec((1, 1, rb), lambda i, h, *_: (h, 0, i)),
                      pl.BlockSpec((1, 1, rb), lambda i, h, *_: (h, 0, i)),
                      pl.BlockSpec((1, rb, 1), lambda i, h, *_: (h, i, 0)),
                      pl.BlockSpec((1, rb, 1), lambda i, h, *_: (h, i, 0)),
                      pl.BlockSpec((1, rb, 1), lambda i, h, *_: (h, i, 0)),
                      pl.BlockSpec((1, rb, A_DQK), lambda i, h, *_: (h, i, 0)),
                      pl.BlockSpec((1, nseq_blk, 1, A_DQK, A_DV), lambda i, h, *_: (0, i, h, 0, 0))],
            out_specs=[pl.BlockSpec((rb, A_DV), lambda i, h, *_: (i, h)),
                       pl.BlockSpec((1, nseq_blk, 1, A_DQK, A_DV), lambda i, h, *_: (0, i, h, 0, 0)),
                       pl.BlockSpec((1, rb, A_DQK), lambda i, h, *_: (h, i, 0)),
                       pl.BlockSpec((1, rb, LANES), lambda i, h, *_: (h, i, 0))],
            scratch_shapes=[pltpu.VMEM((rb, A_DV), F32),
                            pltpu.VMEM((rb, LANES), F32),
                            pltpu.VMEM((A_DQK, rb), F32),
                            pltpu.VMEM((rb, A_DQK), F32)]),
        out_shape=[jax.ShapeDtypeStruct((ts, A_HEADS * A_DV), BF16),
                   jax.ShapeDtypeStruct(c_state.shape, F32),
                   jax.ShapeDtypeStruct((A_HEADS, ts, A_DQK), F32),
                   jax.ShapeDtypeStruct((A_HEADS, ts, LANES), F32)],
        compiler_params=_cparams(("parallel", "parallel")),
    )(b_gates, z_main_s, z_main_s, z_main_s, gi_r, gf_r, gi_c, gf_c, m_col, n_rows, c_state)


def _attn_p_kernel(q_ref, k_ref, v_ref, o_ref, *, tq, heads):
    s = q_ref.shape[0]
    lower = lax.broadcasted_iota(jnp.int32, (tq, tq), 1) <= lax.broadcasted_iota(jnp.int32, (tq, tq), 0)
    for hh in range(heads):
        qk = slice(hh * Q_HEAD_PAD, (hh + 1) * Q_HEAD_PAD)
        vv = slice(hh * B_V, (hh + 1) * B_V)
        for qt in range(s // tq):
            lo, hi = qt * tq, (qt + 1) * tq
            q = q_ref[lo:hi, qk]
            sd = jnp.where(lower, _dot_nt(q, k_ref[lo:hi, qk]), NEG_INF)
            m = jnp.max(sd, axis=-1, keepdims=True)
            if qt:
                sp = _dot_nt(q, k_ref[0:lo, qk])
                m = jnp.maximum(m, jnp.max(sp, axis=-1, keepdims=True))
            pd = jnp.exp(sd - m)
            l = jnp.sum(pd, axis=-1, keepdims=True)
            o = _dot(pd.astype(BF16), v_ref[lo:hi, vv])
            if qt:
                pp = jnp.exp(sp - m)
                l = l + jnp.sum(pp, axis=-1, keepdims=True)
                o = o + _dot(pp.astype(BF16), v_ref[0:lo, vv])
            o_ref[lo:hi, vv] = (o / l).astype(o_ref.dtype)


def attn_prompt(q, k, v, bp, s, tq, heads=2):
    return pl.pallas_call(
        functools.partial(_attn_p_kernel, tq=tq, heads=heads),
        grid=(bp, B_HEADS // heads),
        in_specs=[pl.BlockSpec((s, heads * Q_HEAD_PAD), lambda b, h: (b, h)),
                  pl.BlockSpec((s, heads * Q_HEAD_PAD), lambda b, h: (b, h)),
                  pl.BlockSpec((s, heads * B_V), lambda b, h: (b, h))],
        out_specs=pl.BlockSpec((s, heads * B_V), lambda b, h: (b, h)),
        out_shape=jax.ShapeDtypeStruct((bp * s, B_HEADS * B_V), BF16),
        compiler_params=_cparams(("parallel", "parallel")),
    )(q, k, v)


def _q_absorb_kernel(q_ref, w_ref, o_ref):
    o_ref[...] = _dot(q_ref[...], w_ref[...]).astype(o_ref.dtype)


def q_absorb(q_s, w_uk_t):
    ts = q_s.shape[0]
    c = w_uk_t.shape[1]
    return pl.pallas_call(
        _q_absorb_kernel,
        grid=(B_HEADS,),
        in_specs=[pl.BlockSpec((ts, B_NOPE), lambda h: (0, 2 * h)),
                  pl.BlockSpec((B_NOPE, c), lambda h: (h, 0))],
        out_specs=pl.BlockSpec((ts, c), lambda h: (0, h)),
        out_shape=jax.ShapeDtypeStruct((ts, B_HEADS * c), BF16),
        compiler_params=_cparams(("parallel",)),
    )(q_s, w_uk_t)


def _attn_s_kernel(pt_ref, *refs, g_pages, page, kblk, seq):
    (lat_hbm, kr_hbm, wukt_ref, qabs_ref, q_ref, cnew_ref, knew_ref, o_ref,
     wcat_scr, lat_scr, m_scr, l_scr, acc_scr, lat_buf, kr_buf, lat_sem, kr_sem) = refs
    b = pl.program_id(0)
    j = pl.program_id(1)
    nj = pl.num_programs(1)
    nq = qabs_ref.shape[1]
    nkey = wukt_ref.shape[0]
    c = wukt_ref.shape[1]
    per_blk = kblk // page

    step = b * nj + j
    slot = step % 2

    def page_copies(bb, jj, sl, g):
        pg = pt_ref[bb, jj * g_pages + g]
        return (pltpu.make_async_copy(lat_hbm.at[0, pg], lat_buf.at[sl, g], lat_sem.at[sl]),
                pltpu.make_async_copy(kr_hbm.at[0, pg], kr_buf.at[sl, g], kr_sem.at[sl]))

    def start_pages(bb, jj, sl):
        for g in range(g_pages):
            for cp in page_copies(bb, jj, sl, g):
                cp.start()

    @pl.when(step == 0)
    def _():
        start_pages(0, 0, 0)
        wcat_scr[0:nkey, :] = wukt_ref[...]

    @pl.when(step + 1 < pl.num_programs(0) * nj)
    def _():
        nxt = step + 1
        start_pages(nxt // nj, nxt % nj, 1 - slot)

    for g in range(g_pages):
        for cp in page_copies(b, j, slot, g):
            cp.wait()
    lat_refs = [lat_buf.at[slot, g] for g in range(g_pages)]
    kr_refs = [kr_buf.at[slot, g] for g in range(g_pages)]

    @pl.when(j == 0)
    def _():
        wcat_scr[nkey:nkey + nq, :] = qabs_ref[0]
        m_scr[...] = jnp.full(m_scr.shape, NEG_INF, F32)
        l_scr[...] = jnp.zeros_like(l_scr)
        acc_scr[...] = jnp.zeros_like(acc_scr)

    qrope = q_ref[0][:, LANES:LANES + B_ROPE]

    def scores(lb, krt):
        a = _dot_nt(wcat_scr[...], lb)
        kt = a[0:nkey]
        ssq = jnp.sum((kt * kt).reshape(B_HEADS, B_NOPE, kblk), axis=1)
        ssr = jnp.sum(krt * krt, axis=0, keepdims=True)
        r = lax.rsqrt((ssq + ssr) * (1.0 / B_QK) + EPS)
        return (a[nkey:nkey + nq] + _dot(qrope, krt.astype(BF16))) * jnp.concatenate([r] * seq, axis=0)

    def fold(s, lat, carry):
        m, l, acc = carry
        m_new = jnp.maximum(m, jnp.max(s, axis=-1, keepdims=True))
        alpha = jnp.exp(m - m_new)
        p = jnp.exp(s - m_new)
        return m_new, alpha * l + jnp.sum(p, axis=-1, keepdims=True), alpha * acc + _dot(p.astype(BF16), lat)

    s_blocks = []
    for sb in range(g_pages // per_blk):
        pages = range(sb * per_blk, (sb + 1) * per_blk)
        lb = jnp.concatenate([lat_refs[g][...].astype(BF16) for g in pages], axis=0)
        krt = jnp.concatenate([kr_refs[g][...] for g in pages], axis=1)
        lat_scr[sb * kblk:(sb + 1) * kblk, :] = lb
        s_blocks.append(scores(lb, krt))
    carry = fold(jnp.concatenate(s_blocks, axis=1), lat_scr[...], (m_scr[:, 0:1], l_scr[:, 0:1], acc_scr[...]))
    m, l, acc = carry
    m_scr[...] = jnp.broadcast_to(m, m_scr.shape)
    l_scr[...] = jnp.broadcast_to(l, l_scr.shape)
    acc_scr[...] = acc

    @pl.when(j == pl.num_programs(1) - 1)
    def _():
        qf = q_ref[0].astype(F32)
        t_of_row = lax.broadcasted_iota(jnp.int32, (nq, 1), 0) // B_HEADS
        cols = []
        for jn in range(seq):
            kj = knew_ref[0, jn * B_HEADS:(jn + 1) * B_HEADS, :].astype(F32)
            sj = jnp.sum(qf * jnp.concatenate([kj] * seq, axis=0), axis=-1, keepdims=True)
            cols.append(jnp.where(t_of_row >= jn, sj, NEG_INF))
        m2 = m
        for sj in cols:
            m2 = jnp.maximum(m2, sj)
        alpha = jnp.exp(m - m2)
        l2 = alpha * l
        acc2 = alpha * acc
        for jn, sj in enumerate(cols):
            pj = jnp.exp(sj - m2)
            l2 = l2 + pj
            acc2 = acc2 + pj * cnew_ref[0, jn:jn + 1, :]
        o_ref[0] = (acc2 / l2).astype(o_ref.dtype)


def attn_sample(page_table, cache_lat, cache_kr_t, w_uk_t, q_abs, q_s, c_new, k_new, g_pages, seq):
    db, n_pages = page_table.shape
    page, c = cache_lat.shape[2], cache_lat.shape[3]
    kblk = MXU_COLS
    assert kblk % page == 0 and g_pages % (kblk // page) == 0 and n_pages % g_pages == 0
    nq = q_abs.shape[1]
    nkey = w_uk_t.shape[0]
    per_b = lambda b, j, pt: (b, 0, 0)
    return pl.pallas_call(
        functools.partial(_attn_s_kernel, g_pages=g_pages, page=page, kblk=kblk, seq=seq),
        grid_spec=pltpu.PrefetchScalarGridSpec(
            num_scalar_prefetch=1,
            grid=(db, n_pages // g_pages),
            in_specs=[
                pl.BlockSpec(memory_space=pl.ANY),
                pl.BlockSpec(memory_space=pl.ANY),
                pl.BlockSpec(w_uk_t.shape, lambda b, j, pt: (0, 0)),
                pl.BlockSpec((1, nq, c), per_b),
                pl.BlockSpec((1, nq, Q_HEAD_PAD), per_b),
                pl.BlockSpec((1, seq, c), per_b),
                pl.BlockSpec((1, nq, Q_HEAD_PAD), per_b)],
            out_specs=pl.BlockSpec((1, nq, c), per_b),
            scratch_shapes=[pltpu.VMEM((nkey + nq, c), BF16),
                            pltpu.VMEM((g_pages * page, c), BF16),
                            pltpu.VMEM((nq, LANES), F32),
                            pltpu.VMEM((nq, LANES), F32),
                            pltpu.VMEM((nq, c), F32),
                            pltpu.VMEM((2, g_pages, page, c), F32),
                            pltpu.VMEM((2, g_pages, B_ROPE, page), F32),
                            pltpu.SemaphoreType.DMA((2,)),
                            pltpu.SemaphoreType.DMA((2,))]),
        out_shape=jax.ShapeDtypeStruct((db, nq, c), BF16),
        compiler_params=_cparams(("arbitrary", "arbitrary")),
    )(page_table, cache_lat, cache_kr_t, w_uk_t, q_abs, q_s, c_new, k_new)


def _uv_kernel(o_ref, w_ref, h_ref):
    h_ref[...] = _dot(o_ref[...], w_ref[...]).astype(h_ref.dtype)


def uv_expand(o_lat2, w_uv):
    ts = o_lat2.shape[0]
    c = w_uv.shape[0]
    return pl.pallas_call(
        _uv_kernel,
        grid=(B_HEADS,),
        in_specs=[pl.BlockSpec((ts, c), lambda h: (0, h)),
                  pl.BlockSpec((c, B_V), lambda h: (0, h))],
        out_specs=pl.BlockSpec((ts, B_V), lambda h: (0, h)),
        out_shape=jax.ShapeDtypeStruct((ts, B_HEADS * B_V), BF16),
        compiler_params=_cparams(("parallel",)),
    )(o_lat2, w_uv)


def _merge_kernel(hid_ref, oa_ref, ga_ref, gb_ref, hb_ref, g_ref, wa_ref, wb_ref, o_ref, ha_scr):
    @pl.when(pl.program_id(1) == 0)
    def _():
        for h in range(A_HEADS):
            sl = slice(h * A_DV, (h + 1) * A_DV)
            gate = jax.nn.sigmoid(oa_ref[:, sl].astype(F32))
            ha_scr[:, sl] = (_rms(hid_ref[:, sl].astype(F32), g_ref[:, sl]) * gate).astype(BF16)

    a = _dot(ha_scr[...], wa_ref[...])
    b = _dot(hb_ref[...], wb_ref[...])
    o_ref[...] = (jax.nn.sigmoid(ga_ref[...].astype(F32)) * a
                  + jax.nn.sigmoid(gb_ref[...].astype(F32)) * b).astype(o_ref.dtype)


def merge(hid, z_main, h_b, g_out, w_a, w_b, bm, bn):
    t, d = hid.shape
    nj = d // bn
    return pl.pallas_call(
        _merge_kernel,
        grid=(t // bm, nj),
        in_specs=[pl.BlockSpec((bm, d), lambda i, j: (i, 0)),
                  pl.BlockSpec((bm, d), lambda i, j: (i, 2)),
                  pl.BlockSpec((bm, bn), lambda i, j: (i, 3 * nj + j)),
                  pl.BlockSpec((bm, bn), lambda i, j: (i, 4 * nj + j)),
                  pl.BlockSpec((bm, d), lambda i, j: (i, 0)),
                  pl.BlockSpec((1, d), lambda i, j: (0, 0)),
                  pl.BlockSpec((d, bn), lambda i, j: (0, j)),
                  pl.BlockSpec((d, bn), lambda i, j: (0, j))],
        out_specs=pl.BlockSpec((bm, bn), lambda i, j: (i, j)),
        out_shape=jax.ShapeDtypeStruct((t, d), BF16),
        scratch_shapes=[pltpu.VMEM((bm, d), BF16)],
        compiler_params=_cparams(("parallel", "arbitrary")),
    )(hid, z_main, z_main, z_main, h_b, g_out, w_a, w_b)


def _resid_mm_kernel(x_ref, a_ref, w_ref, o_ref):
    o_ref[...] = x_ref[...] + _dot(a_ref[...], w_ref[...])


def resid_matmul(x, a, w, bm, bn):
    t, d = x.shape
    k = a.shape[1]
    return pl.pallas_call(
        _resid_mm_kernel,
        grid=(t // bm, d // bn),
        in_specs=[pl.BlockSpec((bm, bn), lambda i, j: (i, j)),
                  pl.BlockSpec((bm, k), lambda i, j: (i, 0)),
                  pl.BlockSpec((k, bn), lambda i, j: (0, j))],
        out_specs=pl.BlockSpec((bm, bn), lambda i, j: (i, j)),
        out_shape=jax.ShapeDtypeStruct((t, d), F32),
        compiler_params=_cparams(("parallel", "parallel")),
    )(x, a, w)


def _ffn_p_kernel(x_ref, xh_ref, g_ref, wg_ref, wv_ref, cg_ref, cv_ref, bg_ref, bv_ref, wd_ref,
                  y_ref, ug_ref, uv_ref, xn_scr, acc_scr, *, seq_len):
    i = pl.program_id(0)
    j = pl.program_id(1)
    bm = x_ref.shape[0]
    hp = SUBLANES

    @pl.when(j == 0)
    def _():
        xn_scr[hp:, :] = _rms(x_ref[...], g_ref[...]).astype(BF16)
        keep = jnp.where((i * bm) % seq_len == 0, 0.0, 1.0)
        xn_scr[0:hp, :] = (_rms(xh_ref[...], g_ref[...]) * keep).astype(BF16)
        acc_scr[...] = jnp.zeros_like(acc_scr)

    xall = xn_scr[...]

    def conv(u, wc_ref, bc_ref):
        e2 = u[hp:, :]
        e1 = u[hp - 1:hp - 1 + bm, :]
        e0 = u[hp - 2:hp - 2 + bm, :]
        return bc_ref[...] + ((e0 * wc_ref[0:1, :] + e1 * wc_ref[1:2, :]) + e2 * wc_ref[2:3, :])

    ug = _dot(xall, wg_ref[...])
    uv = _dot(xall, wv_ref[...])
    ug_ref[0] = ug[bm:, :]
    uv_ref[0] = uv[bm:, :]
    cg = conv(ug, cg_ref, bg_ref)
    cv = conv(uv, cv_ref, bv_ref)
    act = (cg * jax.nn.sigmoid(cg) * cv).astype(BF16)
    acc_scr[...] += _dot(act, wd_ref[...])

    @pl.when(j == pl.num_programs(1) - 1)
    def _():
        y_ref[...] = x_ref[...] + acc_scr[...]


def _ffn_weight_specs(d, bf, nj):
    return [pl.BlockSpec((1, d), lambda i, j: (0, 0)),
            pl.BlockSpec((d, bf), lambda i, j: (0, j)),
            pl.BlockSpec((d, bf), lambda i, j: (0, nj + j)),
            pl.BlockSpec((CONV_W, bf), lambda i, j: (0, j)),
            pl.BlockSpec((CONV_W, bf), lambda i, j: (0, nj + j)),
            pl.BlockSpec((1, bf), lambda i, j: (0, j)),
            pl.BlockSpec((1, bf), lambda i, j: (0, nj + j)),
            pl.BlockSpec((bf, d), lambda i, j: (j, 0))]


def conv_ffn_prompt(x1, g, w_up, w_conv, b_conv, w_down, bm, bf, seq_len):
    rows, d = x1.shape
    f = w_down.shape[0]
    assert seq_len % bm == 0 and bm >= CONV_W - 1, "row blocks must not straddle sequences"
    nj = f // bf
    nb = rows // bm
    hb = bm // SUBLANES
    return pl.pallas_call(
        functools.partial(_ffn_p_kernel, seq_len=seq_len),
        grid=(nb, nj),
        in_specs=[pl.BlockSpec((bm, d), lambda i, j: (i, 0)),
                  pl.BlockSpec((SUBLANES, d), lambda i, j: (jnp.maximum(i * hb - 1, 0), 0))]
        + _ffn_weight_specs(d, bf, nj),
        out_specs=[pl.BlockSpec((bm, d), lambda i, j: (i, 0)),
                   pl.BlockSpec((1, SUBLANES, bf), lambda i, j: (i, 0, j)),
                   pl.BlockSpec((1, SUBLANES, bf), lambda i, j: (i, 0, j))],
        out_shape=[jax.ShapeDtypeStruct((rows, d), F32),
                   jax.ShapeDtypeStruct((nb, SUBLANES, f), F32),
                   jax.ShapeDtypeStruct((nb, SUBLANES, f), F32)],
        scratch_shapes=[pltpu.VMEM((bm + SUBLANES, d), BF16),
                        pltpu.VMEM((bm, d), F32)],
        compiler_params=_cparams(("parallel", "arbitrary")),
    )(x1, x1, g, w_up, w_up, w_conv, w_conv, b_conv, b_conv, w_down)


def _ffn_s_kernel(x_ref, g_ref, wg_ref, wv_ref, cg_ref, cv_ref, bg_ref, bv_ref, wd_ref, prev_ref,
                  y_ref, new_ref, xn_scr, acc_scr, *, seq):
    j = pl.program_id(1)
    nb = x_ref.shape[0] // seq

    @pl.when(j == 0)
    def _():
        xn_scr[...] = _rms(x_ref[...], g_ref[...]).astype(BF16)
        acc_scr[...] = jnp.zeros_like(acc_scr)

    xn = xn_scr[...]

    def conv(u, half, wc_ref, bc_ref):
        ext = [prev_ref[r, half] for r in range(CONV_W - 1)] + [u[t * nb:(t + 1) * nb, :] for t in range(seq)]
        for r in range(CONV_W - 1):
            new_ref[r, half] = ext[seq + r]
        return jnp.concatenate(
            [bc_ref[...] + ((ext[t] * wc_ref[0:1, :] + ext[t + 1] * wc_ref[1:2, :]) + ext[t + 2] * wc_ref[2:3, :])
             for t in range(seq)], axis=0)

    cg = conv(_dot(xn, wg_ref[...]), 0, cg_ref, bg_ref)
    cv = conv(_dot(xn, wv_ref[...]), 1, cv_ref, bv_ref)
    act = (cg * jax.nn.sigmoid(cg) * cv).astype(BF16)
    acc_scr[...] += _dot(act, wd_ref[...])

    @pl.when(j == pl.num_programs(1) - 1)
    def _():
        y_ref[...] = x_ref[...] + acc_scr[...]


def conv_ffn_sample(x1_t, prev_t, g, w_up, w_conv, b_conv, w_down, bf, seq):
    rows, d = x1_t.shape
    f = w_down.shape[0]
    nj = f // bf
    nb = rows // seq
    hist = pl.BlockSpec((CONV_W - 1, 2, nb, bf), lambda i, j: (0, 0, 0, j))
    return pl.pallas_call(
        functools.partial(_ffn_s_kernel, seq=seq),
        grid=(1, nj),
        in_specs=[pl.BlockSpec((rows, d), lambda i, j: (0, 0))] + _ffn_weight_specs(d, bf, nj) + [hist],
        out_specs=[pl.BlockSpec((rows, d), lambda i, j: (0, 0)), hist],
        out_shape=[jax.ShapeDtypeStruct((rows, d), F32),
                   jax.ShapeDtypeStruct(prev_t.shape, F32)],
        scratch_shapes=[pltpu.VMEM((rows, d), BF16),
                        pltpu.VMEM((rows, d), F32)],
        compiler_params=_cparams(("parallel", "arbitrary")),
    )(x1_t, g, w_up, w_up, w_conv, w_conv, b_conv, b_conv, w_down, prev_t)


def _pick(n, prefs):
    for p in prefs:
        if n % p == 0:
            return p
    return n


def _layer(x_prompt, x_sample, cache_lat, cache_kr, c_state, n_state, m_state, conv_state, page_table,
           g_attn_norm, w_in, b_gates, g_q_a, w_uq, g_qk_nope_q, g_qk_rope_q, g_kv_a, w_uk, w_uv,
           g_qk_nope_k, g_qk_rope_k, g_mlstm_out, w_branch_a, w_branch_b, w_out, g_ffn_norm,
           w_up, w_conv, b_conv, w_down):
    bp, s, d = x_prompt.shape
    db, seq, _ = x_sample.shape
    n_pages = page_table.shape[1]
    page = cache_lat.shape[2]
    past = n_pages * page
    tp, ts = bp * s, db * seq
    t = tp + ts
    c_lat = g_kv_a.shape[0]
    c_q = g_q_a.shape[0]
    f = w_down.shape[0]
    qk_w = A_HEADS * A_DQK
    v_w = A_HEADS * A_DV
    row = lambda v: v.reshape(1, -1)

    o_i = 2 * qk_w + 2 * v_w
    o_cq = o_i + 2 * A_HEADS
    o_kr = o_cq + c_q + c_lat
    o_g = o_kr + B_ROPE
    w_in_b = w_in.astype(BF16)
    w_gab = w_in_b[:, o_g:]
    n_small = LANES - B_ROPE - 2 * A_HEADS
    w_lat = jnp.concatenate([w_in_b[:, o_cq:o_kr], w_in_b[:, o_kr:o_g], w_in_b[:, o_i:o_cq],
                             jnp.zeros((d, n_small), BF16)], axis=1)
    w_uq_pad = jnp.concatenate([w_uq, w_uq[:, :, B_NOPE:]], axis=2).reshape(c_q, B_HEADS * Q_HEAD_PAD).astype(BF16)
    w_uk2 = w_uk.reshape(c_lat, B_HEADS * B_NOPE).astype(BF16)
    w_uk_t = jnp.transpose(w_uk.reshape(c_lat, B_HEADS * B_NOPE)).astype(BF16)
    w_uv2 = w_uv.reshape(c_lat, B_HEADS * B_V).astype(BF16)
    gq = jnp.concatenate([g_qk_nope_q * g_qk_nope_k, g_qk_rope_q * g_qk_rope_k, g_qk_rope_q * g_qk_rope_k,
                          jnp.zeros((Q_HEAD_PAD - B_QK,), F32)]) * ATTN_SCALE
    gq = jnp.tile(gq, B_HEADS).reshape(1, -1)

    w_a_b, w_b_b, w_out_b = w_branch_a.astype(BF16), w_branch_b.astype(BF16), w_out.astype(BF16)
    w_up_b, w_down_b = w_up.astype(BF16), w_down.astype(BF16)
    half = B_ROPE // 2
    freqs = ROPE_THETA ** (-jnp.arange(half, dtype=F32) / half)

    def token_stage(x, pos, bm):
        rows = x.shape[0]
        ang = pos.astype(F32)[:, None] * freqs
        cos, sin = jnp.cos(ang), jnp.sin(ang)
        zh = jnp.zeros((rows, half), F32)
        zr = jnp.zeros((rows, LANES - B_ROPE), F32)
        cos128 = jnp.concatenate([cos, cos, zr], axis=1)
        sin_a = jnp.concatenate([zh, sin, zr], axis=1)
        sin_b = jnp.concatenate([-sin, zh, zr], axis=1)
        bn = _pick(d, (1024, 512, 256, 128))
        z_main = norm_matmul2(x, row(g_attn_norm), w_in_b, o_i, w_gab, bm, bn, BF16)
        z_lat = norm_matmul(x, row(g_attn_norm), w_lat, bm, w_lat.shape[1])
        bq = _pick(rows, (256, 128, 64, 32, 16, 8))
        q = q_proj(z_lat, row(g_q_a), w_uq_pad, cos128, sin_a + sin_b, gq, bq)
        c_kv, kr128, k, v = kv_proj(z_lat, row(g_kv_a), w_uk2, w_uv2, cos128, sin_a, sin_b, bq)
        g_t = jnp.transpose(z_lat[:, c_q + c_lat + B_ROPE:c_q + c_lat + B_ROPE + 2 * A_HEADS])
        gates = (g_t[:A_HEADS, None, :], g_t[A_HEADS:, None, :], g_t[:A_HEADS, :, None], g_t[A_HEADS:, :, None])
        return z_main, q, c_kv, kr128, k, v, gates

    def mix_stage(x, hid, z_main, h_b):
        bm = _pick(x.shape[0], (1024, 512, 256, 128, 64, 32, 16, 8))
        merged = merge(hid, z_main, h_b, row(g_mlstm_out), w_a_b, w_b_b, bm, _pick(d, (512, 256, 128)))
        return resid_matmul(x, merged, w_out_b, bm, _pick(d, (1024, 512, 256, 128)))

    bff = _pick(f, (512, 256, 128))

    xp = x_prompt.reshape(tp, d)
    bm_p = _pick(tp, (1024, 512, 256, 128, 64, 32, 16, 8))
    zm_p, q_p, ckv_p, kr_p, k_p, v_p, gates_p = token_stage(xp, jnp.tile(jnp.arange(s, dtype=jnp.int32), bp), bm_p)
    chunk = _pick(s, (256, 128, 64, 32, 16, 8))
    hid_p, c_p, n_p, m_p = mlstm_prompt(zm_p, b_gates, *gates_p, bp, s, chunk)
    hb_p = attn_prompt(q_p, k_p, v_p, bp, s, _pick(s, (512, 256, 128)))
    x1_p = mix_stage(xp, hid_p, zm_p, hb_p)
    bmf_p = _pick(s, (512, 256, 128, 64, 32, 16, 8))
    y_p, tg_p, tv_p = conv_ffn_prompt(x1_p, row(g_ffn_norm), w_up_b, w_conv, row(b_conv), w_down_b, bmf_p, bff, s)
    nb_seq = s // bmf_p
    tail_p = jnp.concatenate([tg_p, tv_p], axis=-1).reshape(bp, nb_seq, SUBLANES, 2 * f)
    conv_p = tail_p[:, nb_seq - 1, SUBLANES - (CONV_W - 1):, :]

    xs = x_sample.reshape(ts, d)
    bm_s = _pick(ts, (512, 256, 128, 64, 32, 16, 8))
    zm_s, q_s, ckv_s, kr_s, k_s, _, gates_s = token_stage(
        xs, jnp.tile(past + jnp.arange(seq, dtype=jnp.int32), db), bm_s)
    rb = _pick(ts, (128, 64, 32, 16, 8))
    m_col = jnp.repeat(jnp.transpose(m_state), seq, axis=1)[:, :, None]
    n_rows = jnp.repeat(jnp.transpose(n_state, (1, 0, 2)), seq, axis=1)
    hid_s, c_s, n_s_rows, m_s_rows = mlstm_sample(zm_s, b_gates, *gates_s, m_col, n_rows, c_state[None], rb, seq)
    n_s = jnp.transpose(n_s_rows[:, seq - 1::seq, :], (1, 0, 2))
    m_s = jnp.transpose(m_s_rows[:, seq - 1::seq, 0])

    q_abs = q_absorb(q_s, w_uk_t).reshape(db, seq * B_HEADS, c_lat)
    g_pages = _pick(n_pages, (32, 16, 8, 4, 2))
    o_lat = attn_sample(page_table, cache_lat, jnp.swapaxes(cache_kr, 2, 3), w_uk_t, q_abs,
                        q_s.reshape(db, seq * B_HEADS, Q_HEAD_PAD), ckv_s.reshape(db, seq, c_lat),
                        k_s.reshape(db, seq * B_HEADS, Q_HEAD_PAD), g_pages, seq)
    hb_s = uv_expand(o_lat.reshape(ts, B_HEADS * c_lat), w_uv2)
    x1_s = mix_stage(xs, hid_s, zm_s, hb_s)
    x1_t = jnp.transpose(x1_s.reshape(db, seq, d), (1, 0, 2)).reshape(ts, d)
    prev_t = jnp.transpose(conv_state.reshape(db, CONV_W - 1, 2, f), (1, 2, 0, 3))
    y_t, new_t = conv_ffn_sample(x1_t, prev_t, row(g_ffn_norm), w_up_b, w_conv, row(b_conv), w_down_b, bff, seq)
    y_s = jnp.transpose(y_t.reshape(seq, db, d), (1, 0, 2))
    conv_s = jnp.transpose(new_t, (2, 0, 1, 3)).reshape(db, CONV_W - 1, 2 * f)

    new_p = (ckv_p.reshape(bp, s, c_lat), kr_p[:, :B_ROPE].reshape(bp, s, B_ROPE),
             c_p, n_p[:, :, 0, :], m_p[:, :, 0, 0], conv_p)
    new_s = (ckv_s.reshape(db, seq, c_lat), kr_s[:, :B_ROPE].reshape(db, seq, B_ROPE),
             c_s[0], n_s, m_s, conv_s)
    return y_p.reshape(bp, s, d), y_s.reshape(db, seq, d), new_p, new_s


def kernel(x_prompt, x_sample, cache_kv_latent, cache_k_rope, state_mlstm_C, state_mlstm_n, state_mlstm_m, state_conv, page_table, g_attn_norm, w_in, b_gates, g_q_a, w_uq, g_qk_nope_q, g_qk_rope_q, g_kv_a, w_uk, w_uv, g_qk_nope_k, g_qk_rope_k, g_mlstm_out, w_branch_a, w_branch_b, w_out, g_ffn_norm, w_up, w_conv, b_conv, w_down):
    depth = w_in.shape[0]
    assert depth == 1, "single-layer trunk"
    l = 0
    y_p, y_s, new_p, new_s = _layer(
        x_prompt, x_sample, cache_kv_latent, cache_k_rope, state_mlstm_C[l], state_mlstm_n[l],
        state_mlstm_m[l], state_conv[l], page_table, g_attn_norm[l], w_in[l], b_gates[l], g_q_a[l], w_uq[l],
        g_qk_nope_q[l], g_qk_rope_q[l], g_kv_a[l], w_uk[l], w_uv[l], g_qk_nope_k[l], g_qk_rope_k[l],
        g_mlstm_out[l], w_branch_a[l], w_branch_b[l], w_out[l], g_ffn_norm[l], w_up[l], w_conv[l], b_conv[l],
        w_down[l])
    dts = (cache_kv_latent.dtype, cache_k_rope.dtype, state_mlstm_C.dtype, state_mlstm_n.dtype,
           state_mlstm_m.dtype, state_conv.dtype)
    st_p = tuple(a[None].astype(dt) for a, dt in zip(new_p, dts))
    st_s = tuple(a[None].astype(dt) for a, dt in zip(new_s, dts))
    return (y_p, y_s) + st_p + st_s
```

```python
import functools
import math

import jax
import jax.numpy as jnp
from jax import lax
from jax.experimental import pallas as pl
from jax.experimental.pallas import tpu as pltpu

F32 = jnp.float32
BF16 = jnp.bfloat16

A_HEADS = 8
A_DQK = 128
A_DV = 256
K_SCALE = A_DQK ** -0.5
GATE_CAP = 15.0
B_HEADS = 16
B_NOPE = 128
B_ROPE = 64
B_QK = B_NOPE + B_ROPE
B_V = 128
Q_HEAD_PAD = 256
ROPE_THETA = 10000.0
ATTN_SCALE = B_QK ** -0.5
CONV_W = 3
EPS = 1e-6
NEG_INF = float("-inf")

LANES = 128
SUBLANES = 8
MXU_COLS = 256
VMEM_LIMIT = 56 * 1024 * 1024


def _cparams(sem):
    return pltpu.CompilerParams(dimension_semantics=sem, vmem_limit_bytes=VMEM_LIMIT)


def _rms(x, g):
    r = lax.rsqrt(jnp.mean(x * x, axis=-1, keepdims=True) + EPS)
    return x * r * g


def _dot(a, b):
    return jnp.dot(a, b, preferred_element_type=F32)


def _dot_nt(a, b):
    return lax.dot_general(a, b, (((1,), (1,)), ((), ())), preferred_element_type=F32)


def _rope128(x, cos, sin_a, sin_b):
    return x * cos + pltpu.roll(x, 32, 1) * sin_a + pltpu.roll(x, 96, 1) * sin_b


def _norm_mm_kernel(x_ref, g_ref, w_ref, o_ref, xn_ref):
    @pl.when(pl.program_id(1) == 0)
    def _():
        xn_ref[...] = _rms(x_ref[...], g_ref[...]).astype(BF16)

    o_ref[...] = _dot(xn_ref[...], w_ref[...]).astype(o_ref.dtype)


def norm_matmul(x, g, w, bm, bn, out_dtype=F32):
    t, d = x.shape
    n = w.shape[1]
    return pl.pallas_call(
        _norm_mm_kernel,
        grid=(t // bm, n // bn),
        in_specs=[pl.BlockSpec((bm, d), lambda i, j: (i, 0)),
                  pl.BlockSpec((1, d), lambda i, j: (0, 0)),
                  pl.BlockSpec((d, bn), lambda i, j: (0, j))],
        out_specs=pl.BlockSpec((bm, bn), lambda i, j: (i, j)),
        out_shape=jax.ShapeDtypeStruct((t, n), out_dtype),
        scratch_shapes=[pltpu.VMEM((bm, d), BF16)],
        compiler_params=_cparams(("parallel", "arbitrary")),
    )(x, g, w)


def _norm_mm2_kernel(x_ref, g_ref, wa_ref, wb_ref, o_ref, xn_ref, *, na_tiles):
    j = pl.program_id(1)

    @pl.when(j == 0)
    def _():
        xn_ref[...] = _rms(x_ref[...], g_ref[...]).astype(BF16)

    @pl.when(j < na_tiles)
    def _():
        o_ref[...] = _dot(xn_ref[...], wa_ref[...]).astype(o_ref.dtype)

    @pl.when(j >= na_tiles)
    def _():
        o_ref[...] = _dot(xn_ref[...], wb_ref[...]).astype(o_ref.dtype)


def norm_matmul2(x, g, w_a, na_cols, w_b, bm, bn, out_dtype):
    t, d = x.shape
    assert na_cols % bn == 0 and w_b.shape[1] % bn == 0
    na_tiles, nb_tiles = na_cols // bn, w_b.shape[1] // bn
    return pl.pallas_call(
        functools.partial(_norm_mm2_kernel, na_tiles=na_tiles),
        grid=(t // bm, na_tiles + nb_tiles),
        in_specs=[pl.BlockSpec((bm, d), lambda i, j: (i, 0)),
                  pl.BlockSpec((1, d), lambda i, j: (0, 0)),
                  pl.BlockSpec((d, bn), lambda i, j: (0, jnp.minimum(j, na_tiles - 1))),
                  pl.BlockSpec((d, bn), lambda i, j: (0, jnp.maximum(j - na_tiles, 0)))],
        out_specs=pl.BlockSpec((bm, bn), lambda i, j: (i, j)),
        out_shape=jax.ShapeDtypeStruct((t, (na_tiles + nb_tiles) * bn), out_dtype),
        scratch_shapes=[pltpu.VMEM((bm, d), BF16)],
        compiler_params=_cparams(("parallel", "arbitrary")),
    )(x, g, w_a, w_b)


def _q_kernel(cq_ref, g_ref, w_ref, cos_ref, sr_ref, gain_ref, o_ref):
    cqn = _rms(cq_ref[...], g_ref[...]).astype(BF16)
    q = _dot(cqn, w_ref[...])
    cos, sr = cos_ref[...], sr_ref[...]
    for h in range(B_HEADS):
        lo = h * Q_HEAD_PAD
        nope = q[:, lo:lo + LANES]
        rp = q[:, lo + LANES:lo + Q_HEAD_PAD]
        rp = rp * cos + pltpu.roll(rp, B_ROPE // 2, 1) * sr
        ssq = jnp.sum(nope * nope + rp * rp, axis=-1, keepdims=True)
        r = lax.rsqrt(ssq * (1.0 / B_QK) + EPS)
        o_ref[:, lo:lo + LANES] = (nope * r * gain_ref[:, lo:lo + LANES]).astype(o_ref.dtype)
        o_ref[:, lo + LANES:lo + Q_HEAD_PAD] = (rp * r * gain_ref[:, lo + LANES:lo + Q_HEAD_PAD]).astype(o_ref.dtype)


def q_proj(z_lat, g_q_a, w_uq_pad, cos, sin_roll, gain, bm):
    t = z_lat.shape[0]
    c = g_q_a.shape[1]
    n = w_uq_pad.shape[1]
    return pl.pallas_call(
        _q_kernel,
        grid=(t // bm,),
        in_specs=[pl.BlockSpec((bm, c), lambda i: (i, 0)),
                  pl.BlockSpec((1, c), lambda i: (0, 0)),
                  pl.BlockSpec((c, n), lambda i: (0, 0)),
                  pl.BlockSpec((bm, LANES), lambda i: (i, 0)),
                  pl.BlockSpec((bm, LANES), lambda i: (i, 0)),
                  pl.BlockSpec((1, n), lambda i: (0, 0))],
        out_specs=pl.BlockSpec((bm, n), lambda i: (i, 0)),
        out_shape=jax.ShapeDtypeStruct((t, n), BF16),
        compiler_params=_cparams(("parallel",)),
    )(z_lat, g_q_a, w_uq_pad, cos, sin_roll, gain)


def _kv_kernel(ckv_ref, sm_ref, g_ref, wuk_ref, wuv_ref, cos_ref, sa_ref, sb_ref,
               c_ref, kr_ref, k_ref, v_ref):
    c = _rms(ckv_ref[...], g_ref[...])
    c_ref[...] = c
    cb = c.astype(BF16)
    kn = _dot(cb, wuk_ref[...])
    v_ref[...] = _dot(cb, wuv_ref[...]).astype(v_ref.dtype)
    kr = _rope128(sm_ref[...], cos_ref[...], sa_ref[...], sb_ref[...])
    kr_ref[...] = kr
    ssr = jnp.sum(kr * kr, axis=-1, keepdims=True)
    for h in range(B_HEADS):
        nope = kn[:, h * B_NOPE:(h + 1) * B_NOPE]
        r = lax.rsqrt((jnp.sum(nope * nope, axis=-1, keepdims=True) + ssr) * (1.0 / B_QK) + EPS)
        lo = h * Q_HEAD_PAD
        k_ref[:, lo:lo + LANES] = (nope * r).astype(k_ref.dtype)
        k_ref[:, lo + LANES:lo + Q_HEAD_PAD] = (kr * r).astype(k_ref.dtype)


def kv_proj(z_lat, g_kv_a, w_uk, w_uv, cos, sa, sb, bm):
    t = z_lat.shape[0]
    c = g_kv_a.shape[1]
    small_blk = (2 * c) // LANES
    row = lambda i: (i, 0)
    const = lambda i: (0, 0)
    return pl.pallas_call(
        _kv_kernel,
        grid=(t // bm,),
        in_specs=[pl.BlockSpec((bm, c), lambda i: (i, 1)),
                  pl.BlockSpec((bm, LANES), lambda i: (i, small_blk)),
                  pl.BlockSpec((1, c), const),
                  pl.BlockSpec(w_uk.shape, const),
                  pl.BlockSpec(w_uv.shape, const),
                  pl.BlockSpec((bm, LANES), row),
                  pl.BlockSpec((bm, LANES), row),
                  pl.BlockSpec((bm, LANES), row)],
        out_specs=[pl.BlockSpec((bm, c), row),
                   pl.BlockSpec((bm, LANES), row),
                   pl.BlockSpec((bm, B_HEADS * Q_HEAD_PAD), row),
                   pl.BlockSpec((bm, B_HEADS * B_V), row)],
        out_shape=[jax.ShapeDtypeStruct((t, c), F32),
                   jax.ShapeDtypeStruct((t, LANES), F32),
                   jax.ShapeDtypeStruct((t, B_HEADS * Q_HEAD_PAD), BF16),
                   jax.ShapeDtypeStruct((t, B_HEADS * B_V), BF16)],
        compiler_params=_cparams(("parallel",)),
    )(z_lat, z_lat, g_kv_a, w_uk, w_uv, cos, sa, sb)


def _log_gates(gi, gf, bi, bf):
    cap = lambda x: GATE_CAP * jnp.tanh(x * (1.0 / GATE_CAP))
    li = cap(gi + bi)
    y = cap(gf + bf)
    lf = jnp.minimum(y, 0.0) - jnp.log(1.0 + jnp.exp(-jnp.abs(y)))
    return li, lf


def _mlstm_p_kernel(b_ref, q_ref, k_ref, v_ref, gir_ref, gfr_ref, gic_ref, gfc_ref,
                    hid_ref, c_out_ref, n_out_ref, m_out_ref, c_scr, n_scr, m_scr):
    ci = pl.program_id(1)
    L = q_ref.shape[0]

    @pl.when(ci == 0)
    def _():
        c_scr[...] = jnp.zeros_like(c_scr)
        n_scr[...] = jnp.zeros_like(n_scr)
        m_scr[...] = jnp.zeros_like(m_scr)

    t_idx = lax.broadcasted_iota(jnp.int32, (L, L), 0)
    s_idx = lax.broadcasted_iota(jnp.int32, (L, L), 1)
    causal = s_idx <= t_idx
    anti = t_idx <= s_idx
    for h in range(A_HEADS):
        bi = b_ref[h]
        bf = b_ref[A_HEADS + h]
        qb = q_ref[:, h * A_DQK:(h + 1) * A_DQK]
        kb = k_ref[:, h * A_DQK:(h + 1) * A_DQK]
        vb = v_ref[:, h * A_DV:(h + 1) * A_DV]
        q = qb.astype(F32)
        li_r, lf_r = _log_gates(gir_ref[h], gfr_ref[h], bi, bf)
        li_c, lf_c = _log_gates(gic_ref[h], gfc_ref[h], bi, bf)

        bcum_c = jnp.sum(jnp.where(causal, lf_r, 0.0), axis=1, keepdims=True)
        bcum_r = jnp.sum(jnp.where(anti, lf_c, 0.0), axis=0, keepdims=True)
        log_w = jnp.where(causal, bcum_c - bcum_r + li_r, NEG_INF)
        m_prev = m_scr[h][:, 0:1]
        c_prev = c_scr[h]
        n_prev = n_scr[h]
        log_state = bcum_c + m_prev
        m_row = jnp.maximum(log_state, jnp.max(log_w, axis=1, keepdims=True))
        w = jnp.exp(log_w - m_row) * K_SCALE
        w_state = jnp.exp(log_state - m_row)
        sc = _dot_nt(qb, kb) * w
        num = _dot(sc.astype(BF16), vb) + w_state * _dot(qb, c_prev.astype(BF16))
        den = jnp.sum(sc, axis=1, keepdims=True) + w_state * jnp.sum(q * n_prev, axis=1, keepdims=True)
        hid_ref[:, h * A_DV:(h + 1) * A_DV] = (
            num / jnp.maximum(jnp.abs(den), jnp.exp(-m_row))).astype(hid_ref.dtype)

        b_last = bcum_c[L - 1:L, :]
        log_k = b_last - bcum_c + li_c
        m_new = jnp.maximum(b_last + m_prev, jnp.max(log_k, axis=0, keepdims=True))
        wk = jnp.exp(log_k - m_new) * K_SCALE
        decay = jnp.exp(b_last + m_prev - m_new)
        kw = kb.astype(F32) * wk
        kwt = jnp.transpose(kw).astype(BF16)
        c_scr[h] = decay * c_prev + _dot(kwt, vb)
        n_scr[h] = decay * n_prev + jnp.sum(kw, axis=0, keepdims=True)
        m_scr[h] = jnp.broadcast_to(m_new, (1, LANES))

    @pl.when(ci == pl.num_programs(1) - 1)
    def _():
        c_out_ref[0] = c_scr[...]
        n_out_ref[0] = n_scr[...]
        m_out_ref[0] = m_scr[...]


def mlstm_prompt(z_main, b_gates, gi_r, gf_r, gi_c, gf_c, bp, s, chunk):
    nc = s // chunk
    qk_w, v_w = A_HEADS * A_DQK, A_HEADS * A_DV
    assert (2 * qk_w) % v_w == 0
    tok = lambda b, c, *_: b * nc + c
    return pl.pallas_call(
        _mlstm_p_kernel,
        grid_spec=pltpu.PrefetchScalarGridSpec(
            num_scalar_prefetch=1,
            grid=(bp, nc),
            in_specs=[pl.BlockSpec((chunk, qk_w), lambda b, c, *_: (tok(b, c), 0)),
                      pl.BlockSpec((chunk, qk_w), lambda b, c, *_: (tok(b, c), 1)),
                      pl.BlockSpec((chunk, v_w), lambda b, c, *_: (tok(b, c), (2 * qk_w) // v_w)),
                      pl.BlockSpec((A_HEADS, 1, chunk), lambda b, c, *_: (0, 0, tok(b, c))),
                      pl.BlockSpec((A_HEADS, 1, chunk), lambda b, c, *_: (0, 0, tok(b, c))),
                      pl.BlockSpec((A_HEADS, chunk, 1), lambda b, c, *_: (0, tok(b, c), 0)),
                      pl.BlockSpec((A_HEADS, chunk, 1), lambda b, c, *_: (0, tok(b, c), 0))],
            out_specs=[pl.BlockSpec((chunk, v_w), lambda b, c, *_: (tok(b, c), 0)),
                       pl.BlockSpec((1, A_HEADS, A_DQK, A_DV), lambda b, c, *_: (b, 0, 0, 0)),
                       pl.BlockSpec((1, A_HEADS, 1, A_DQK), lambda b, c, *_: (b, 0, 0, 0)),
                       pl.BlockSpec((1, A_HEADS, 1, LANES), lambda b, c, *_: (b, 0, 0, 0))],
            scratch_shapes=[pltpu.VMEM((A_HEADS, A_DQK, A_DV), F32),
                            pltpu.VMEM((A_HEADS, 1, A_DQK), F32),
                            pltpu.VMEM((A_HEADS, 1, LANES), F32)]),
        out_shape=[jax.ShapeDtypeStruct((bp * s, v_w), BF16),
                   jax.ShapeDtypeStruct((bp, A_HEADS, A_DQK, A_DV), F32),
                   jax.ShapeDtypeStruct((bp, A_HEADS, 1, A_DQK), F32),
                   jax.ShapeDtypeStruct((bp, A_HEADS, 1, LANES), F32)],
        compiler_params=_cparams(("parallel", "arbitrary")),
    )(b_gates, z_main, z_main, z_main, gi_r, gf_r, gi_c, gf_c)


def _mlstm_s_kernel(b_ref, q_ref, k_ref, v_ref, gir_ref, gfr_ref, gic_ref, gfc_ref,
                    mc_ref, nrow_ref, c_in_ref,
                    hid_ref, c_out_ref, n_out_ref, m_out_ref,
                    hc_scr, dec_scr, kwt_scr, q_scr, *, seq):
    h = pl.program_id(1)
    R = q_ref.shape[0]
    nseq = R // seq
    per8 = SUBLANES // seq
    bi = b_ref[h]
    bf = b_ref[A_HEADS + h]
    q = q_ref[...].astype(F32)
    k = k_ref[...].astype(F32)
    v = v_ref[...].astype(F32)
    q_scr[...] = q
    li_r, lf_r = _log_gates(gir_ref[0], gfr_ref[0], bi, bf)
    li_c, lf_c = _log_gates(gic_ref[0], gfc_ref[0], bi, bf)
    m_prev = mc_ref[0]

    t_idx = lax.broadcasted_iota(jnp.int32, (R, R), 0)
    s_idx = lax.broadcasted_iota(jnp.int32, (R, R), 1)
    same = (t_idx // seq) == (s_idx // seq)
    causal = same & (s_idx <= t_idx)
    bcum_c = jnp.sum(jnp.where(causal, lf_r, 0.0), axis=1, keepdims=True)
    bcum_r = jnp.sum(jnp.where(same & (t_idx <= s_idx), lf_c, 0.0), axis=0, keepdims=True)
    blast_c = jnp.sum(jnp.where(same, lf_r, 0.0), axis=1, keepdims=True)
    blast_r = jnp.sum(jnp.where(same, lf_c, 0.0), axis=0, keepdims=True)
    log_w = jnp.where(causal, bcum_c - bcum_r + li_r, NEG_INF)
    log_state = bcum_c + m_prev
    m_row = jnp.maximum(log_state, jnp.max(log_w, axis=1, keepdims=True))
    w = jnp.exp(log_w - m_row) * K_SCALE
    w_state = jnp.exp(log_state - m_row)

    log_k_c = blast_c - bcum_c + li_c
    log_k_r = blast_r - bcum_r + li_r
    seg_max = jnp.max(jnp.where(same, log_k_r, NEG_INF), axis=1, keepdims=True)
    m_new = jnp.maximum(blast_c + m_prev, seg_max)
    wk = jnp.exp(log_k_c - m_new) * K_SCALE
    decay = jnp.exp(blast_c + m_prev - m_new)
    kw = k * wk
    kwt_scr[...] = jnp.transpose(kw)
    dec_scr[...] = jnp.broadcast_to(decay, dec_scr.shape)

    row8 = lax.broadcasted_iota(jnp.int32, (SUBLANES, A_DV), 0)
    col = lax.broadcasted_iota(jnp.int32, (A_DQK, R), 1)

    def group(gidx, carry):
        r0 = pl.multiple_of(gidx * SUBLANES, SUBLANES)
        q8 = q_scr[pl.ds(r0, SUBLANES), :]
        hc8 = jnp.zeros((SUBLANES, A_DV), F32)
        for u in range(per8):
            sq = gidx * per8 + u
            c0 = c_in_ref[0, sq, 0]
            res = _dot(q8, c0)
            hc8 = jnp.where((row8 // seq) == u, res, hc8)
            kwt_m = jnp.where((col // seq) == sq, kwt_scr[...], 0.0)
            dsc = dec_scr[pl.ds(sq * seq, 1), 0:1]
            c_out_ref[0, sq, 0] = dsc * c0 + _dot(kwt_m.astype(BF16), v_ref[...])
        hc_scr[pl.ds(r0, SUBLANES), :] = hc8
        return carry

    lax.fori_loop(0, R // SUBLANES, group, 0, unroll=4)

    sc = _dot_nt(q, k) * w
    num = _dot(sc, v) + w_state * hc_scr[...]
    den = jnp.sum(sc, axis=1, keepdims=True) + w_state * jnp.sum(q * nrow_ref[0], axis=1, keepdims=True)
    hid_ref[...] = (num / jnp.maximum(jnp.abs(den), jnp.exp(-m_row))).astype(hid_ref.dtype)

    acc = kw
    shift = 1
    while shift < seq:
        acc = acc + jnp.where((lax.broadcasted_iota(jnp.int32, acc.shape, 0) % seq) >= shift,
                              pltpu.roll(acc, shift, 0), 0.0)
        shift *= 2
    n_out_ref[0] = decay * nrow_ref[0] + acc
    m_out_ref[0] = jnp.broadcast_to(m_new, (R, LANES))


def mlstm_sample(z_main_s, b_gates, gi_r, gf_r, gi_c, gf_c, m_col, n_rows, c_state, rb, seq):
    ts = z_main_s.shape[0]
    nseq_blk = rb // seq
    kcol = (A_HEADS * A_DQK) // A_DQK
    vcol = (2 * A_HEADS * A_DQK) // A_DV
    return pl.pallas_call(
        functools.partial(_mlstm_s_kernel, seq=seq),
        grid_spec=pltpu.PrefetchScalarGridSpec(
            num_scalar_prefetch=1,
            grid=(ts // rb, A_HEADS),
            in_specs=[pl.BlockSpec((rb, A_DQK), lambda i, h, *_: (i, h)),
                      pl.BlockSpec((rb, A_DQK), lambda i, h, *_: (i, kcol + h)),
                      pl.BlockSpec((rb, A_DV), lambda i, h, *_: (i, vcol + h)),
                      pl.BlockSpec((1, 1, rb), lambda i, h, *_: (h, 0, i)),
                      pl.BlockSpec((1, 1, rb), lambda i, h, *_: (h, 0, i)),
                      pl.BlockSpec((1, rb, 1), lambda i, h, *_: (h, i, 0)),
                      pl.BlockSpec((1, rb, 1), lambda i, h, *_: (h, i, 0)),
                      pl.BlockSpec((1, rb, 1), lambda i, h, *_: (h, i, 0)),
                      pl.BlockSpec((1, rb, A_DQK), lambda i, h, *_: (h, i, 0)),
                      pl.BlockSpec((1, nseq_blk, 1, A_DQK, A_DV), lambda i, h, *_: (0, i, h, 0, 0))],
            out_specs=[pl.BlockSpec((rb, A_DV), lambda i, h, *_: (i, h)),
                       pl.BlockSpec((1, nseq_blk, 1, A_DQK, A_DV), lambda i, h, *_: (0, i, h, 0, 0)),
                       pl.BlockSpec((1, rb, A_DQK), lambda i, h, *_: (h, i, 0)),
                       pl.BlockSpec((1, rb, LANES), lambda i, h, *_: (h, i, 0))],
            scratch_shapes=[pltpu.VMEM((rb, A_DV), F32),
                            pltpu.VMEM((rb, LANES), F32),
                            pltpu.VMEM((A_DQK, rb), F32),
                            pltpu.VMEM((rb, A_DQK), F32)]),
        out_shape=[jax.ShapeDtypeStruct((ts, A_HEADS * A_DV), BF16),
                   jax.ShapeDtypeStruct(c_state.shape, F32),
                   jax.ShapeDtypeStruct((A_HEADS, ts, A_DQK), F32),
                   jax.ShapeDtypeStruct((A_HEADS, ts, LANES), F32)],
        compiler_params=_cparams(("parallel", "parallel")),
    )(b_gates, z_main_s, z_main_s, z_main_s, gi_r, gf_r, gi_c, gf_c, m_col, n_rows, c_state)


def _attn_p_kernel(q_ref, k_ref, v_ref, o_ref, *, tq, heads):
    s = q_ref.shape[0]
    lower = lax.broadcasted_iota(jnp.int32, (tq, tq), 1) <= lax.broadcasted_iota(jnp.int32, (tq, tq), 0)
    for hh in range(heads):
        qk = slice(hh * Q_HEAD_PAD, (hh + 1) * Q_HEAD_PAD)
        vv = slice(hh * B_V, (hh + 1) * B_V)
        for qt in range(s // tq):
            lo, hi = qt * tq, (qt + 1) * tq
            q = q_ref[lo:hi, qk]
            sd = jnp.where(lower, _dot_nt(q, k_ref[lo:hi, qk]), NEG_INF)
            m = jnp.max(sd, axis=-1, keepdims=True)
            if qt:
                sp = _dot_nt(q, k_ref[0:lo, qk])
                m = jnp.maximum(m, jnp.max(sp, axis=-1, keepdims=True))
            pd = jnp.exp(sd - m)
            l = jnp.sum(pd, axis=-1, keepdims=True)
            o = _dot(pd.astype(BF16), v_ref[lo:hi, vv])
            if qt:
                pp = jnp.exp(sp - m)
                l = l + jnp.sum(pp, axis=-1, keepdims=True)
                o = o + _dot(pp.astype(BF16), v_ref[0:lo, vv])
            o_ref[lo:hi, vv] = (o / l).astype(o_ref.dtype)


def attn_prompt(q, k, v, bp, s, tq, heads=2):
    return pl.pallas_call(
        functools.partial(_attn_p_kernel, tq=tq, heads=heads),
        grid=(bp, B_HEADS // heads),
        in_specs=[pl.BlockSpec((s, heads * Q_HEAD_PAD), lambda b, h: (b, h)),
                  pl.BlockSpec((s, heads * Q_HEAD_PAD), lambda b, h: (b, h)),
                  pl.BlockSpec((s, heads * B_V), lambda b, h: (b, h))],
        out_specs=pl.BlockSpec((s, heads * B_V), lambda b, h: (b, h)),
        out_shape=jax.ShapeDtypeStruct((bp * s, B_HEADS * B_V), BF16),
        compiler_params=_cparams(("parallel", "parallel")),
    )(q, k, v)


def _q_absorb_kernel(q_ref, w_ref, o_ref):
    o_ref[...] = _dot(q_ref[...], w_ref[...]).astype(o_ref.dtype)


def q_absorb(q_s, w_uk_t):
    ts = q_s.shape[0]
    c = w_uk_t.shape[1]
    return pl.pallas_call(
        _q_absorb_kernel,
        grid=(B_HEADS,),
        in_specs=[pl.BlockSpec((ts, B_NOPE), lambda h: (0, 2 * h)),
                  pl.BlockSpec((B_NOPE, c), lambda h: (h, 0))],
        out_specs=pl.BlockSpec((ts, c), lambda h: (0, h)),
        out_shape=jax.ShapeDtypeStruct((ts, B_HEADS * c), BF16),
        compiler_params=_cparams(("parallel",)),
    )(q_s, w_uk_t)


def _attn_s_kernel(pt_ref, *refs, g_pages, page, kblk, seq):
    (lat_hbm, kr_hbm, wukt_ref, qabs_ref, q_ref, cnew_ref, knew_ref, o_ref,
     wcat_scr, lat_scr, m_scr, l_scr, acc_scr, lat_buf, kr_buf, lat_sem, kr_sem) = refs
    b = pl.program_id(0)
    j = pl.program_id(1)
    nj = pl.num_programs(1)
    nq = qabs_ref.shape[1]
    nkey = wukt_ref.shape[0]
    c = wukt_ref.shape[1]
    per_blk = kblk // page

    step = b * nj + j
    slot = step % 2

    def page_copies(bb, jj, sl, g):
        pg = pt_ref[bb, jj * g_pages + g]
        return (pltpu.make_async_copy(lat_hbm.at[0, pg], lat_buf.at[sl, g], lat_sem.at[sl]),
                pltpu.make_async_copy(kr_hbm.at[0, pg], kr_buf.at[sl, g], kr_sem.at[sl]))

    def start_pages(bb, jj, sl):
        for g in range(g_pages):
            for cp in page_copies(bb, jj, sl, g):
                cp.start()

    @pl.when(step == 0)
    def _():
        start_pages(0, 0, 0)
        wcat_scr[0:nkey, :] = wukt_ref[...]

    @pl.when(step + 1 < pl.num_programs(0) * nj)
    def _():
        nxt = step + 1
        start_pages(nxt // nj, nxt % nj, 1 - slot)

    for g in range(g_pages):
        for cp in page_copies(b, j, slot, g):
            cp.wait()
    lat_refs = [lat_buf.at[slot, g] for g in range(g_pages)]
    kr_refs = [kr_buf.at[slot, g] for g in range(g_pages)]

    @pl.when(j == 0)
    def _():
        wcat_scr[nkey:nkey + nq, :] = qabs_ref[0]
        m_scr[...] = jnp.full(m_scr.shape, NEG_INF, F32)
        l_scr[...] = jnp.zeros_like(l_scr)
        acc_scr[...] = jnp.zeros_like(acc_scr)

    qrope = q_ref[0][:, LANES:LANES + B_ROPE]

    def scores(lb, krt):
        a = _dot_nt(wcat_scr[...], lb)
        kt = a[0:nkey]
        ssq = jnp.sum((kt * kt).reshape(B_HEADS, B_NOPE, kblk), axis=1)
        ssr = jnp.sum(krt * krt, axis=0, keepdims=True)
        r = lax.rsqrt((ssq + ssr) * (1.0 / B_QK) + EPS)
        return (a[nkey:nkey + nq] + _dot(qrope, krt.astype(BF16))) * jnp.concatenate([r] * seq, axis=0)

    def fold(s, lat, carry):
        m, l, acc = carry
        m_new = jnp.maximum(m, jnp.max(s, axis=-1, keepdims=True))
        alpha = jnp.exp(m - m_new)
        p = jnp.exp(s - m_new)
        return m_new, alpha * l + jnp.sum(p, axis=-1, keepdims=True), alpha * acc + _dot(p.astype(BF16), lat)

    s_blocks = []
    for sb in range(g_pages // per_blk):
        pages = range(sb * per_blk, (sb + 1) * per_blk)
        lb = jnp.concatenate([lat_refs[g][...].astype(BF16) for g in pages], axis=0)
        krt = jnp.concatenate([kr_refs[g][...] for g in pages], axis=1)
        lat_scr[sb * kblk:(sb + 1) * kblk, :] = lb
        s_blocks.append(scores(lb, krt))
    carry = fold(jnp.concatenate(s_blocks, axis=1), lat_scr[...], (m_scr[:, 0:1], l_scr[:, 0:1], acc_scr[...]))
    m, l, acc = carry
    m_scr[...] = jnp.broadcast_to(m, m_scr.shape)
    l_scr[...] = jnp.broadcast_to(l, l_scr.shape)
    acc_scr[...] = acc

    @pl.when(j == pl.num_programs(1) - 1)
    def _():
        qf = q_ref[0].astype(F32)
        t_of_row = lax.broadcasted_iota(jnp.int32, (nq, 1), 0) // B_HEADS
        cols = []
        for jn in range(seq):
            kj = knew_ref[0, jn * B_HEADS:(jn + 1) * B_HEADS, :].astype(F32)
            sj = jnp.sum(qf * jnp.concatenate([kj] * seq, axis=0), axis=-1, keepdims=True)
            cols.append(jnp.where(t_of_row >= jn, sj, NEG_INF))
        m2 = m
        for sj in cols:
            m2 = jnp.maximum(m2, sj)
        alpha = jnp.exp(m - m2)
        l2 = alpha * l
        acc2 = alpha * acc
        for jn, sj in enumerate(cols):
            pj = jnp.exp(sj - m2)
            l2 = l2 + pj
            acc2 = acc2 + pj * cnew_ref[0, jn:jn + 1, :]
        o_ref[0] = (acc2 / l2).astype(o_ref.dtype)


def attn_sample(page_table, cache_lat, cache_kr_t, w_uk_t, q_abs, q_s, c_new, k_new, g_pages, seq):
    db, n_pages = page_table.shape
    page, c = cache_lat.shape[2], cache_lat.shape[3]
    kblk = MXU_COLS
    assert kblk % page == 0 and g_pages % (kblk // page) == 0 and n_pages % g_pages == 0
    nq = q_abs.shape[1]
    nkey = w_uk_t.shape[0]
    per_b = lambda b, j, pt: (b, 0, 0)
    return pl.pallas_call(
        functools.partial(_attn_s_kernel, g_pages=g_pages, page=page, kblk=kblk, seq=seq),
        grid_spec=pltpu.PrefetchScalarGridSpec(
            num_scalar_prefetch=1,
            grid=(db, n_pages // g_pages),
            in_specs=[
                pl.BlockSpec(memory_space=pl.ANY),
                pl.BlockSpec(memory_space=pl.ANY),
                pl.BlockSpec(w_uk_t.shape, lambda b, j, pt: (0, 0)),
                pl.BlockSpec((1, nq, c), per_b),
                pl.BlockSpec((1, nq, Q_HEAD_PAD), per_b),
                pl.BlockSpec((1, seq, c), per_b),
                pl.BlockSpec((1, nq, Q_HEAD_PAD), per_b)],
            out_specs=pl.BlockSpec((1, nq, c), per_b),
            scratch_shapes=[pltpu.VMEM((nkey + nq, c), BF16),
                            pltpu.VMEM((g_pages * page, c), BF16),
                            pltpu.VMEM((nq, LANES), F32),
                            pltpu.VMEM((nq, LANES), F32),
                            pltpu.VMEM((nq, c), F32),
                            pltpu.VMEM((2, g_pages, page, c), F32),
                            pltpu.VMEM((2, g_pages, B_ROPE, page), F32),
                            pltpu.SemaphoreType.DMA((2,)),
                            pltpu.SemaphoreType.DMA((2,))]),
        out_shape=jax.ShapeDtypeStruct((db, nq, c), BF16),
        compiler_params=_cparams(("arbitrary", "arbitrary")),
    )(page_table, cache_lat, cache_kr_t, w_uk_t, q_abs, q_s, c_new, k_new)


def _uv_kernel(o_ref, w_ref, h_ref):
    h_ref[...] = _dot(o_ref[...], w_ref[...]).astype(h_ref.dtype)


def uv_expand(o_lat2, w_uv):
    ts = o_lat2.shape[0]
    c = w_uv.shape[0]
    return pl.pallas_call(
        _uv_kernel,
        grid=(B_HEADS,),
        in_specs=[pl.BlockSpec((ts, c), lambda h: (0, h)),
                  pl.BlockSpec((c, B_V), lambda h: (0, h))],
        out_specs=pl.BlockSpec((ts, B_V), lambda h: (0, h)),
        out_shape=jax.ShapeDtypeStruct((ts, B_HEADS * B_V), BF16),
        compiler_params=_cparams(("parallel",)),
    )(o_lat2, w_uv)


def _merge_kernel(hid_ref, oa_ref, ga_ref, gb_ref, hb_ref, g_ref, wa_ref, wb_ref, o_ref, ha_scr):
    @pl.when(pl.program_id(1) == 0)
    def _():
        for h in range(A_HEADS):
            sl = slice(h * A_DV, (h + 1) * A_DV)
            gate = jax.nn.sigmoid(oa_ref[:, sl].astype(F32))
            ha_scr[:, sl] = (_rms(hid_ref[:, sl].astype(F32), g_ref[:, sl]) * gate).astype(BF16)

    a = _dot(ha_scr[...], wa_ref[...])
    b = _dot(hb_ref[...], wb_ref[...])
    o_ref[...] = (jax.nn.sigmoid(ga_ref[...].astype(F32)) * a
                  + jax.nn.sigmoid(gb_ref[...].astype(F32)) * b).astype(o_ref.dtype)


def merge(hid, z_main, h_b, g_out, w_a, w_b, bm, bn):
    t, d = hid.shape
    nj = d // bn
    return pl.pallas_call(
        _merge_kernel,
        grid=(t // bm, nj),
        in_specs=[pl.BlockSpec((bm, d), lambda i, j: (i, 0)),
                  pl.BlockSpec((bm, d), lambda i, j: (i, 2)),
                  pl.BlockSpec((bm, bn), lambda i, j: (i, 3 * nj + j)),
                  pl.BlockSpec((bm, bn), lambda i, j: (i, 4 * nj + j)),
                  pl.BlockSpec((bm, d), lambda i, j: (i, 0)),
                  pl.BlockSpec((1, d), lambda i, j: (0, 0)),
                  pl.BlockSpec((d, bn), lambda i, j: (0, j)),
                  pl.BlockSpec((d, bn), lambda i, j: (0, j))],
        out_specs=pl.BlockSpec((bm, bn), lambda i, j: (i, j)),
        out_shape=jax.ShapeDtypeStruct((t, d), BF16),
        scratch_shapes=[pltpu.VMEM((bm, d), BF16)],
        compiler_params=_cparams(("parallel", "arbitrary")),
    )(hid, z_main, z_main, z_main, h_b, g_out, w_a, w_b)


def _resid_mm_kernel(x_ref, a_ref, w_ref, o_ref):
    o_ref[...] = x_ref[...] + _dot(a_ref[...], w_ref[...])


def resid_matmul(x, a, w, bm, bn):
    t, d = x.shape
    k = a.shape[1]
    return pl.pallas_call(
        _resid_mm_kernel,
        grid=(t // bm, d // bn),
        in_specs=[pl.BlockSpec((bm, bn), lambda i, j: (i, j)),
                  pl.BlockSpec((bm, k), lambda i, j: (i, 0)),
                  pl.BlockSpec((k, bn), lambda i, j: (0, j))],
        out_specs=pl.BlockSpec((bm, bn), lambda i, j: (i, j)),
        out_shape=jax.ShapeDtypeStruct((t, d), F32),
        compiler_params=_cparams(("parallel", "parallel")),
    )(x, a, w)


def _ffn_p_kernel(x_ref, xh_ref, g_ref, wg_ref, wv_ref, cg_ref, cv_ref, bg_ref, bv_ref, wd_ref,
                  y_ref, ug_ref, uv_ref, xn_scr, acc_scr, *, seq_len):
    i = pl.program_id(0)
    j = pl.program_id(1)
    bm = x_ref.shape[0]
    hp = SUBLANES

    @pl.when(j == 0)
    def _():
        xn_scr[hp:, :] = _rms(x_ref[...], g_ref[...]).astype(BF16)
        keep = jnp.where((i * bm) % seq_len == 0, 0.0, 1.0)
        xn_scr[0:hp, :] = (_rms(xh_ref[...], g_ref[...]) * keep).astype(BF16)
        acc_scr[...] = jnp.zeros_like(acc_scr)

    xall = xn_scr[...]

    def conv(u, wc_ref, bc_ref):
        e2 = u[hp:, :]
        e1 = u[hp - 1:hp - 1 + bm, :]
        e0 = u[hp - 2:hp - 2 + bm, :]
        return bc_ref[...] + ((e0 * wc_ref[0:1, :] + e1 * wc_ref[1:2, :]) + e2 * wc_ref[2:3, :])

    ug = _dot(xall, wg_ref[...])
    uv = _dot(xall, wv_ref[...])
    ug_ref[0] = ug[bm:, :]
    uv_ref[0] = uv[bm:, :]
    cg = conv(ug, cg_ref, bg_ref)
    cv = conv(uv, cv_ref, bv_ref)
    act = (cg * jax.nn.sigmoid(cg) * cv).astype(BF16)
    acc_scr[...] += _dot(act, wd_ref[...])

    @pl.when(j == pl.num_programs(1) - 1)
    def _():
        y_ref[...] = x_ref[...] + acc_scr[...]


def _ffn_weight_specs(d, bf, nj):
    return [pl.BlockSpec((1, d), lambda i, j: (0, 0)),
            pl.BlockSpec((d, bf), lambda i, j: (0, j)),
            pl.BlockSpec((d, bf), lambda i, j: (0, nj + j)),
            pl.BlockSpec((CONV_W, bf), lambda i, j: (0, j)),
            pl.BlockSpec((CONV_W, bf), lambda i, j: (0, nj + j)),
            pl.BlockSpec((1, bf), lambda i, j: (0, j)),
            pl.BlockSpec((1, bf), lambda i, j: (0, nj + j)),
            pl.BlockSpec((bf, d), lambda i, j: (j, 0))]


def conv_ffn_prompt(x1, g, w_up, w_conv, b_conv, w_down, bm, bf, seq_len):
    rows, d = x1.shape
    f = w_down.shape[0]
    assert seq_len % bm == 0 and bm >= CONV_W - 1, "row blocks must not straddle sequences"
    nj = f // bf
    nb = rows // bm
    hb = bm // SUBLANES
    return pl.pallas_call(
        functools.partial(_ffn_p_kernel, seq_len=seq_len),
        grid=(nb, nj),
        in_specs=[pl.BlockSpec((bm, d), lambda i, j: (i, 0)),
                  pl.BlockSpec((SUBLANES, d), lambda i, j: (jnp.maximum(i * hb - 1, 0), 0))]
        + _ffn_weight_specs(d, bf, nj),
        out_specs=[pl.BlockSpec((bm, d), lambda i, j: (i, 0)),
                   pl.BlockSpec((1, SUBLANES, bf), lambda i, j: (i, 0, j)),
                   pl.BlockSpec((1, SUBLANES, bf), lambda i, j: (i, 0, j))],
        out_shape=[jax.ShapeDtypeStruct((rows, d), F32),
                   jax.ShapeDtypeStruct((nb, SUBLANES, f), F32),
                   jax.ShapeDtypeStruct((nb, SUBLANES, f), F32)],
        scratch_shapes=[pltpu.VMEM((bm + SUBLANES, d), BF16),
                        pltpu.VMEM((bm, d), F32)],
        compiler_params=_cparams(("parallel", "arbitrary")),
    )(x1, x1, g, w_up, w_up, w_conv, w_conv, b_conv, b_conv, w_down)


def _ffn_s_kernel(x_ref, g_ref, wg_ref, wv_ref, cg_ref, cv_ref, bg_ref, bv_ref, wd_ref, prev_ref,
                  y_ref, new_ref, xn_scr, acc_scr, *, seq):
    j = pl.program_id(1)
    nb = x_ref.shape[0] // seq

    @pl.when(j == 0)
    def _():
        xn_scr[...] = _rms(x_ref[...], g_ref[...]).astype(BF16)
        acc_scr[...] = jnp.zeros_like(acc_scr)

    xn = xn_scr[...]

    def conv(u, half, wc_ref, bc_ref):
        ext = [prev_ref[r, half] for r in range(CONV_W - 1)] + [u[t * nb:(t + 1) * nb, :] for t in range(seq)]
        for r in range(CONV_W - 1):
            new_ref[r, half] = ext[seq + r]
        return jnp.concatenate(
            [bc_ref[...] + ((ext[t] * wc_ref[0:1, :] + ext[t + 1] * wc_ref[1:2, :]) + ext[t + 2] * wc_ref[2:3, :])
             for t in range(seq)], axis=0)

    cg = conv(_dot(xn, wg_ref[...]), 0, cg_ref, bg_ref)
    cv = conv(_dot(xn, wv_ref[...]), 1, cv_ref, bv_ref)
    act = (cg * jax.nn.sigmoid(cg) * cv).astype(BF16)
    acc_scr[...] += _dot(act, wd_ref[...])

    @pl.when(j == pl.num_programs(1) - 1)
    def _():
        y_ref[...] = x_ref[...] + acc_scr[...]


def conv_ffn_sample(x1_t, prev_t, g, w_up, w_conv, b_conv, w_down, bf, seq):
    rows, d = x1_t.shape
    f = w_down.shape[0]
    nj = f // bf
    nb = rows // seq
    hist = pl.BlockSpec((CONV_W - 1, 2, nb, bf), lambda i, j: (0, 0, 0, j))
    return pl.pallas_call(
        functools.partial(_ffn_s_kernel, seq=seq),
        grid=(1, nj),
        in_specs=[pl.BlockSpec((rows, d), lambda i, j: (0, 0))] + _ffn_weight_specs(d, bf, nj) + [hist],
        out_specs=[pl.BlockSpec((rows, d), lambda i, j: (0, 0)), hist],
        out_shape=[jax.ShapeDtypeStruct((rows, d), F32),
                   jax.ShapeDtypeStruct(prev_t.shape, F32)],
        scratch_shapes=[pltpu.VMEM((rows, d), BF16),
                        pltpu.VMEM((rows, d), F32)],
        compiler_params=_cparams(("parallel", "arbitrary")),
    )(x1_t, g, w_up, w_up, w_conv, w_conv, b_conv, b_conv, w_down, prev_t)


def _pick(n, prefs):
    for p in prefs:
        if n % p == 0:
            return p
    return n


def _layer(x_prompt, x_sample, cache_lat, cache_kr, c_state, n_state, m_state, conv_state, page_table,
           g_attn_norm, w_in, b_gates, g_q_a, w_uq, g_qk_nope_q, g_qk_rope_q, g_kv_a, w_uk, w_uv,
           g_qk_nope_k, g_qk_rope_k, g_mlstm_out, w_branch_a, w_branch_b, w_out, g_ffn_norm,
           w_up, w_conv, b_conv, w_down):
    bp, s, d = x_prompt.shape
    db, seq, _ = x_sample.shape
    n_pages = page_table.shape[1]
    page = cache_lat.shape[2]
    past = n_pages * page
    tp, ts = bp * s, db * seq
    t = tp + ts
    c_lat = g_kv_a.shape[0]
    c_q = g_q_a.shape[0]
    f = w_down.shape[0]
    qk_w = A_HEADS * A_DQK
    v_w = A_HEADS * A_DV
    row = lambda v: v.reshape(1, -1)

    o_i = 2 * qk_w + 2 * v_w
    o_cq = o_i + 2 * A_HEADS
    o_kr = o_cq + c_q + c_lat
    o_g = o_kr + B_ROPE
    w_in_b = w_in.astype(BF16)
    w_gab = w_in_b[:, o_g:]
    n_small = LANES - B_ROPE - 2 * A_HEADS
    w_lat = jnp.concatenate([w_in_b[:, o_cq:o_kr], w_in_b[:, o_kr:o_g], w_in_b[:, o_i:o_cq],
                             jnp.zeros((d, n_small), BF16)], axis=1)
    w_uq_pad = jnp.concatenate([w_uq, w_uq[:, :, B_NOPE:]], axis=2).reshape(c_q, B_HEADS * Q_HEAD_PAD).astype(BF16)
    w_uk2 = w_uk.reshape(c_lat, B_HEADS * B_NOPE).astype(BF16)
    w_uk_t = jnp.transpose(w_uk.reshape(c_lat, B_HEADS * B_NOPE)).astype(BF16)
    w_uv2 = w_uv.reshape(c_lat, B_HEADS * B_V).astype(BF16)
    gq = jnp.concatenate([g_qk_nope_q * g_qk_nope_k, g_qk_rope_q * g_qk_rope_k, g_qk_rope_q * g_qk_rope_k,
                          jnp.zeros((Q_HEAD_PAD - B_QK,), F32)]) * ATTN_SCALE
    gq = jnp.tile(gq, B_HEADS).reshape(1, -1)

    w_a_b, w_b_b, w_out_b = w_branch_a.astype(BF16), w_branch_b.astype(BF16), w_out.astype(BF16)
    w_up_b, w_down_b = w_up.astype(BF16), w_down.astype(BF16)
    half = B_ROPE // 2
    freqs = ROPE_THETA ** (-jnp.arange(half, dtype=F32) / half)

    def token_stage(x, pos, bm):
        rows = x.shape[0]
        ang = pos.astype(F32)[:, None] * freqs
        cos, sin = jnp.cos(ang), jnp.sin(ang)
        zh = jnp.zeros((rows, half), F32)
        zr = jnp.zeros((rows, LANES - B_ROPE), F32)
        cos128 = jnp.concatenate([cos, cos, zr], axis=1)
        sin_a = jnp.concatenate([zh, sin, zr], axis=1)
        sin_b = jnp.concatenate([-sin, zh, zr], axis=1)
        bn = _pick(d, (1024, 512, 256, 128))
        z_main = norm_matmul2(x, row(g_attn_norm), w_in_b, o_i, w_gab, bm, bn, BF16)
        z_lat = norm_matmul(x, row(g_attn_norm), w_lat, bm, w_lat.shape[1])
        bq = _pick(rows, (256, 128, 64, 32, 16, 8))
        q = q_proj(z_lat, row(g_q_a), w_uq_pad, cos128, sin_a + sin_b, gq, bq)
        c_kv, kr128, k, v = kv_proj(z_lat, row(g_kv_a), w_uk2, w_uv2, cos128, sin_a, sin_b, bq)
        g_t = jnp.transpose(z_lat[:, c_q + c_lat + B_ROPE:c_q + c_lat + B_ROPE + 2 * A_HEADS])
        gates = (g_t[:A_HEADS, None, :], g_t[A_HEADS:, None, :], g_t[:A_HEADS, :, None], g_t[A_HEADS:, :, None])
        return z_main, q, c_kv, kr128, k, v, gates

    def mix_stage(x, hid, z_main, h_b):
        bm = _pick(x.shape[0], (1024, 512, 256, 128, 64, 32, 16, 8))
        merged = merge(hid, z_main, h_b, row(g_mlstm_out), w_a_b, w_b_b, bm, _pick(d, (512, 256, 128)))
        return resid_matmul(x, merged, w_out_b, bm, _pick(d, (1024, 512, 256, 128)))

    bff = _pick(f, (512, 256, 128))

    xp = x_prompt.reshape(tp, d)
    bm_p = _pick(tp, (1024, 512, 256, 128, 64, 32, 16, 8))
    zm_p, q_p, ckv_p, kr_p, k_p, v_p, gates_p = token_stage(xp, jnp.tile(jnp.arange(s, dtype=jnp.int32), bp), bm_p)
    chunk = _pick(s, (256, 128, 64, 32, 16, 8))
    hid_p, c_p, n_p, m_p = mlstm_prompt(zm_p, b_gates, *gates_p, bp, s, chunk)
    hb_p = attn_prompt(q_p, k_p, v_p, bp, s, _pick(s, (512, 256, 128)))
    x1_p = mix_stage(xp, hid_p, zm_p, hb_p)
    bmf_p = _pick(s, (512, 256, 128, 64, 32, 16, 8))
    y_p, tg_p, tv_p = conv_ffn_prompt(x1_p, row(g_ffn_norm), w_up_b, w_conv, row(b_conv), w_down_b, bmf_p, bff, s)
    nb_seq = s // bmf_p
    tail_p = jnp.concatenate([tg_p, tv_p], axis=-1).reshape(bp, nb_seq, SUBLANES, 2 * f)
    conv_p = tail_p[:, nb_seq - 1, SUBLANES - (CONV_W - 1):, :]

    xs = x_sample.reshape(ts, d)
    bm_s = _pick(ts, (512, 256, 128, 64, 32, 16, 8))
    zm_s, q_s, ckv_s, kr_s, k_s, _, gates_s = token_stage(
        xs, jnp.tile(past + jnp.arange(seq, dtype=jnp.int32), db), bm_s)
    rb = _pick(ts, (128, 64, 32, 16, 8))
    m_col = jnp.repeat(jnp.transpose(m_state), seq, axis=1)[:, :, None]
    n_rows = jnp.repeat(jnp.transpose(n_state, (1, 0, 2)), seq, axis=1)
    hid_s, c_s, n_s_rows, m_s_rows = mlstm_sample(zm_s, b_gates, *gates_s, m_col, n_rows, c_state[None], rb, seq)
    n_s = jnp.transpose(n_s_rows[:, seq - 1::seq, :], (1, 0, 2))
    m_s = jnp.transpose(m_s_rows[:, seq - 1::seq, 0])

    q_abs = q_absorb(q_s, w_uk_t).reshape(db, seq * B_HEADS, c_lat)
    g_pages = _pick(n_pages, (16, 8, 4, 2))
    o_lat = attn_sample(page_table, cache_lat, jnp.swapaxes(cache_kr, 2, 3), w_uk_t, q_abs,
                        q_s.reshape(db, seq * B_HEADS, Q_HEAD_PAD), ckv_s.reshape(db, seq, c_lat),
                        k_s.reshape(db, seq * B_HEADS, Q_HEAD_PAD), g_pages, seq)
    hb_s = uv_expand(o_lat.reshape(ts, B_HEADS * c_lat), w_uv2)
    x1_s = mix_stage(xs, hid_s, zm_s, hb_s)
    x1_t = jnp.transpose(x1_s.reshape(db, seq, d), (1, 0, 2)).reshape(ts, d)
    prev_t = jnp.transpose(conv_state.reshape(db, CONV_W - 1, 2, f), (1, 2, 0, 3))
    y_t, new_t = conv_ffn_sample(x1_t, prev_t, row(g_ffn_norm), w_up_b, w_conv, row(b_conv), w_down_b, bff, seq)
    y_s = jnp.transpose(y_t.reshape(seq, db, d), (1, 0, 2))
    conv_s = jnp.transpose(new_t, (2, 0, 1, 3)).reshape(db, CONV_W - 1, 2 * f)

    new_p = (ckv_p.reshape(bp, s, c_lat), kr_p[:, :B_ROPE].reshape(bp, s, B_ROPE),
             c_p, n_p[:, :, 0, :], m_p[:, :, 0, 0], conv_p)
    new_s = (ckv_s.reshape(db, seq, c_lat), kr_s[:, :B_ROPE].reshape(db, seq, B_ROPE),
             c_s[0], n_s, m_s, conv_s)
    return y_p.reshape(bp, s, d), y_s.reshape(db, seq, d), new_p, new_s


def kernel(x_prompt, x_sample, cache_kv_latent, cache_k_rope, state_mlstm_C, state_mlstm_n, state_mlstm_m, state_conv, page_table, g_attn_norm, w_in, b_gates, g_q_a, w_uq, g_qk_nope_q, g_qk_rope_q, g_kv_a, w_uk, w_uv, g_qk_nope_k, g_qk_rope_k, g_mlstm_out, w_branch_a, w_branch_b, w_out, g_ffn_norm, w_up, w_conv, b_conv, w_down):
    depth = w_in.shape[0]
    assert depth == 1, "single-layer trunk"
    l = 0
    y_p, y_s, new_p, new_s = _layer(
        x_prompt, x_sample, cache_kv_latent, cache_k_rope, state_mlstm_C[l], state_mlstm_n[l],
        state_mlstm_m[l], state_conv[l], page_table, g_attn_norm[l], w_in[l], b_gates[l], g_q_a[l], w_uq[l],
        g_qk_nope_q[l], g_qk_rope_q[l], g_kv_a[l], w_uk[l], w_uv[l], g_qk_nope_k[l], g_qk_rope_k[l],
        g_mlstm_out[l], w_branch_a[l], w_branch_b[l], w_out[l], g_ffn_norm[l], w_up[l], w_conv[l], b_conv[l],
        w_down[l])
    dts = (cache_kv_latent.dtype, cache_k_rope.dtype, state_mlstm_C.dtype, state_mlstm_n.dtype,
           state_mlstm_m.dtype, state_conv.dtype)
    st_p = tuple(a[None].astype(dt) for a, dt in zip(new_p, dts))
    st_s = tuple(a[None].astype(dt) for a, dt in zip(new_s, dts))
    return (y_p, y_s) + st_p + st_s
```

```python
import functools
import math

import jax
import jax.numpy as jnp
from jax import lax
from jax.experimental import pallas as pl
from jax.experimental.pallas import tpu as pltpu

F32 = jnp.float32
BF16 = jnp.bfloat16

A_HEADS = 8
A_DQK = 128
A_DV = 256
K_SCALE = A_DQK ** -0.5
GATE_CAP = 15.0
B_HEADS = 16
B_NOPE = 128
B_ROPE = 64
B_QK = B_NOPE + B_ROPE
B_V = 128
Q_HEAD_PAD = 256
ROPE_THETA = 10000.0
ATTN_SCALE = B_QK ** -0.5
CONV_W = 3
EPS = 1e-6
NEG_INF = float("-inf")

LANES = 128
SUBLANES = 8
MXU_COLS = 256
VMEM_LIMIT = 56 * 1024 * 1024


def _cparams(sem):
    return pltpu.CompilerParams(dimension_semantics=sem, vmem_limit_bytes=VMEM_LIMIT)


def _rms(x, g):
    r = lax.rsqrt(jnp.mean(x * x, axis=-1, keepdims=True) + EPS)
    return x * r * g


def _dot(a, b):
    return jnp.dot(a, b, preferred_element_type=F32)


def _dot_nt(a, b):
    return lax.dot_general(a, b, (((1,), (1,)), ((), ())), preferred_element_type=F32)


def _rope128(x, cos, sin_a, sin_b):
    return x * cos + pltpu.roll(x, 32, 1) * sin_a + pltpu.roll(x, 96, 1) * sin_b


def _norm_mm_kernel(x_ref, g_ref, w_ref, o_ref, xn_ref):
    @pl.when(pl.program_id(1) == 0)
    def _():
        xn_ref[...] = _rms(x_ref[...], g_ref[...]).astype(BF16)

    o_ref[...] = _dot(xn_ref[...], w_ref[...]).astype(o_ref.dtype)


def norm_matmul(x, g, w, bm, bn, out_dtype=F32):
    t, d = x.shape
    n = w.shape[1]
    return pl.pallas_call(
        _norm_mm_kernel,
        grid=(t // bm, n // bn),
        in_specs=[pl.BlockSpec((bm, d), lambda i, j: (i, 0)),
                  pl.BlockSpec((1, d), lambda i, j: (0, 0)),
                  pl.BlockSpec((d, bn), lambda i, j: (0, j))],
        out_specs=pl.BlockSpec((bm, bn), lambda i, j: (i, j)),
        out_shape=jax.ShapeDtypeStruct((t, n), out_dtype),
        scratch_shapes=[pltpu.VMEM((bm, d), BF16)],
        compiler_params=_cparams(("parallel", "arbitrary")),
    )(x, g, w)


def _norm_mm2_kernel(x_ref, g_ref, wa_ref, wb_ref, o_ref, xn_ref, *, na_tiles):
    j = pl.program_id(1)

    @pl.when(j == 0)
    def _():
        xn_ref[...] = _rms(x_ref[...], g_ref[...]).astype(BF16)

    @pl.when(j < na_tiles)
    def _():
        o_ref[...] = _dot(xn_ref[...], wa_ref[...]).astype(o_ref.dtype)

    @pl.when(j >= na_tiles)
    def _():
        o_ref[...] = _dot(xn_ref[...], wb_ref[...]).astype(o_ref.dtype)


def norm_matmul2(x, g, w_a, na_cols, w_b, bm, bn, out_dtype):
    t, d = x.shape
    assert na_cols % bn == 0 and w_b.shape[1] % bn == 0
    na_tiles, nb_tiles = na_cols // bn, w_b.shape[1] // bn
    return pl.pallas_call(
        functools.partial(_norm_mm2_kernel, na_tiles=na_tiles),
        grid=(t // bm, na_tiles + nb_tiles),
        in_specs=[pl.BlockSpec((bm, d), lambda i, j: (i, 0)),
                  pl.BlockSpec((1, d), lambda i, j: (0, 0)),
                  pl.BlockSpec((d, bn), lambda i, j: (0, jnp.minimum(j, na_tiles - 1))),
                  pl.BlockSpec((d, bn), lambda i, j: (0, jnp.maximum(j - na_tiles, 0)))],
        out_specs=pl.BlockSpec((bm, bn), lambda i, j: (i, j)),
        out_shape=jax.ShapeDtypeStruct((t, (na_tiles + nb_tiles) * bn), out_dtype),
        scratch_shapes=[pltpu.VMEM((bm, d), BF16)],
        compiler_params=_cparams(("parallel", "arbitrary")),
    )(x, g, w_a, w_b)


def _q_kernel(cq_ref, g_ref, w_ref, cos_ref, sr_ref, gain_ref, o_ref):
    cqn = _rms(cq_ref[...], g_ref[...]).astype(BF16)
    q = _dot(cqn, w_ref[...])
    cos, sr = cos_ref[...], sr_ref[...]
    for h in range(B_HEADS):
        lo = h * Q_HEAD_PAD
        nope = q[:, lo:lo + LANES]
        rp = q[:, lo + LANES:lo + Q_HEAD_PAD]
        rp = rp * cos + pltpu.roll(rp, B_ROPE // 2, 1) * sr
        ssq = jnp.sum(nope * nope + rp * rp, axis=-1, keepdims=True)
        r = lax.rsqrt(ssq * (1.0 / B_QK) + EPS)
        o_ref[:, lo:lo + LANES] = (nope * r * gain_ref[:, lo:lo + LANES]).astype(o_ref.dtype)
        o_ref[:, lo + LANES:lo + Q_HEAD_PAD] = (rp * r * gain_ref[:, lo + LANES:lo + Q_HEAD_PAD]).astype(o_ref.dtype)


def q_proj(z_lat, g_q_a, w_uq_pad, cos, sin_roll, gain, bm):
    t = z_lat.shape[0]
    c = g_q_a.shape[1]
    n = w_uq_pad.shape[1]
    return pl.pallas_call(
        _q_kernel,
        grid=(t // bm,),
        in_specs=[pl.BlockSpec((bm, c), lambda i: (i, 0)),
                  pl.BlockSpec((1, c), lambda i: (0, 0)),
                  pl.BlockSpec((c, n), lambda i: (0, 0)),
                  pl.BlockSpec((bm, LANES), lambda i: (i, 0)),
                  pl.BlockSpec((bm, LANES), lambda i: (i, 0)),
                  pl.BlockSpec((1, n), lambda i: (0, 0))],
        out_specs=pl.BlockSpec((bm, n), lambda i: (i, 0)),
        out_shape=jax.ShapeDtypeStruct((t, n), BF16),
        compiler_params=_cparams(("parallel",)),
    )(z_lat, g_q_a, w_uq_pad, cos, sin_roll, gain)


def _kv_kernel(ckv_ref, sm_ref, g_ref, wuk_ref, wuv_ref, cos_ref, sa_ref, sb_ref,
               c_ref, kr_ref, k_ref, v_ref):
    c = _rms(ckv_ref[...], g_ref[...])
    c_ref[...] = c
    cb = c.astype(BF16)
    kn = _dot(cb, wuk_ref[...])
    v_ref[...] = _dot(cb, wuv_ref[...]).astype(v_ref.dtype)
    kr = _rope128(sm_ref[...], cos_ref[...], sa_ref[...], sb_ref[...])
    kr_ref[...] = kr
    ssr = jnp.sum(kr * kr, axis=-1, keepdims=True)
    for h in range(B_HEADS):
        nope = kn[:, h * B_NOPE:(h + 1) * B_NOPE]
        r = lax.rsqrt((jnp.sum(nope * nope, axis=-1, keepdims=True) + ssr) * (1.0 / B_QK) + EPS)
        lo = h * Q_HEAD_PAD
        k_ref[:, lo:lo + LANES] = (nope * r).astype(k_ref.dtype)
        k_ref[:, lo + LANES:lo + Q_HEAD_PAD] = (kr * r).astype(k_ref.dtype)


def kv_proj(z_lat, g_kv_a, w_uk, w_uv, cos, sa, sb, bm):
    t = z_lat.shape[0]
    c = g_kv_a.shape[1]
    small_blk = (2 * c) // LANES
    row = lambda i: (i, 0)
    const = lambda i: (0, 0)
    return pl.pallas_call(
        _kv_kernel,
        grid=(t // bm,),
        in_specs=[pl.BlockSpec((bm, c), lambda i: (i, 1)),
                  pl.BlockSpec((bm, LANES), lambda i: (i, small_blk)),
                  pl.BlockSpec((1, c), const),
                  pl.BlockSpec(w_uk.shape, const),
                  pl.BlockSpec(w_uv.shape, const),
                  pl.BlockSpec((bm, LANES), row),
                  pl.BlockSpec((bm, LANES), row),
                  pl.BlockSpec((bm, LANES), row)],
        out_specs=[pl.BlockSpec((bm, c), row),
                   pl.BlockSpec((bm, LANES), row),
                   pl.BlockSpec((bm, B_HEADS * Q_HEAD_PAD), row),
                   pl.BlockSpec((bm, B_HEADS * B_V), row)],
        out_shape=[jax.ShapeDtypeStruct((t, c), F32),
                   jax.ShapeDtypeStruct((t, LANES), F32),
                   jax.ShapeDtypeStruct((t, B_HEADS * Q_HEAD_PAD), BF16),
                   jax.ShapeDtypeStruct((t, B_HEADS * B_V), BF16)],
        compiler_params=_cparams(("parallel",)),
    )(z_lat, z_lat, g_kv_a, w_uk, w_uv, cos, sa, sb)


def _log_gates(gi, gf, bi, bf):
    cap = lambda x: GATE_CAP * jnp.tanh(x * (1.0 / GATE_CAP))
    li = cap(gi + bi)
    y = cap(gf + bf)
    lf = jnp.minimum(y, 0.0) - jnp.log(1.0 + jnp.exp(-jnp.abs(y)))
    return li, lf


def _mlstm_p_kernel(b_ref, q_ref, k_ref, v_ref, gir_ref, gfr_ref, gic_ref, gfc_ref,
                    hid_ref, c_out_ref, n_out_ref, m_out_ref, c_scr, n_scr, m_scr):
    ci = pl.program_id(1)
    L = q_ref.shape[0]

    @pl.when(ci == 0)
    def _():
        c_scr[...] = jnp.zeros_like(c_scr)
        n_scr[...] = jnp.zeros_like(n_scr)
        m_scr[...] = jnp.zeros_like(m_scr)

    t_idx = lax.broadcasted_iota(jnp.int32, (L, L), 0)
    s_idx = lax.broadcasted_iota(jnp.int32, (L, L), 1)
    causal = s_idx <= t_idx
    anti = t_idx <= s_idx
    for h in range(A_HEADS):
        bi = b_ref[h]
        bf = b_ref[A_HEADS + h]
        qb = q_ref[:, h * A_DQK:(h + 1) * A_DQK]
        kb = k_ref[:, h * A_DQK:(h + 1) * A_DQK]
        vb = v_ref[:, h * A_DV:(h + 1) * A_DV]
        q = qb.astype(F32)
        li_r, lf_r = _log_gates(gir_ref[h], gfr_ref[h], bi, bf)
        li_c, lf_c = _log_gates(gic_ref[h], gfc_ref[h], bi, bf)

        bcum_c = jnp.sum(jnp.where(causal, lf_r, 0.0), axis=1, keepdims=True)
        bcum_r = jnp.sum(jnp.where(anti, lf_c, 0.0), axis=0, keepdims=True)
        log_w = jnp.where(causal, bcum_c - bcum_r + li_r, NEG_INF)
        m_prev = m_scr[h][:, 0:1]
        c_prev = c_scr[h]
        n_prev = n_scr[h]
        log_state = bcum_c + m_prev
        m_row = jnp.maximum(log_state, jnp.max(log_w, axis=1, keepdims=True))
        w = jnp.exp(log_w - m_row) * K_SCALE
        w_state = jnp.exp(log_state - m_row)
        sc = _dot_nt(qb, kb) * w
        num = _dot(sc.astype(BF16), vb) + w_state * _dot(qb, c_prev.astype(BF16))
        den = jnp.sum(sc, axis=1, keepdims=True) + w_state * jnp.sum(q * n_prev, axis=1, keepdims=True)
        hid_ref[:, h * A_DV:(h + 1) * A_DV] = (
            num / jnp.maximum(jnp.abs(den), jnp.exp(-m_row))).astype(hid_ref.dtype)

        b_last = bcum_c[L - 1:L, :]
        log_k = b_last - bcum_c + li_c
        m_new = jnp.maximum(b_last + m_prev, jnp.max(log_k, axis=0, keepdims=True))
        wk = jnp.exp(log_k - m_new) * K_SCALE
        decay = jnp.exp(b_last + m_prev - m_new)
        kw = kb.astype(F32) * wk
        kwt = jnp.transpose(kw).astype(BF16)
        c_scr[h] = decay * c_prev + _dot(kwt, vb)
        n_scr[h] = decay * n_prev + jnp.sum(kw, axis=0, keepdims=True)
        m_scr[h] = jnp.broadcast_to(m_new, (1, LANES))

    @pl.when(ci == pl.num_programs(1) - 1)
    def _():
        c_out_ref[0] = c_scr[...]
        n_out_ref[0] = n_scr[...]
        m_out_ref[0] = m_scr[...]


def mlstm_prompt(z_main, b_gates, gi_r, gf_r, gi_c, gf_c, bp, s, chunk):
    nc = s // chunk
    qk_w, v_w = A_HEADS * A_DQK, A_HEADS * A_DV
    assert (2 * qk_w) % v_w == 0
    tok = lambda b, c, *_: b * nc + c
    return pl.pallas_call(
        _mlstm_p_kernel,
        grid_spec=pltpu.PrefetchScalarGridSpec(
            num_scalar_prefetch=1,
            grid=(bp, nc),
            in_specs=[pl.BlockSpec((chunk, qk_w), lambda b, c, *_: (tok(b, c), 0)),
                      pl.BlockSpec((chunk, qk_w), lambda b, c, *_: (tok(b, c), 1)),
                      pl.BlockSpec((chunk, v_w), lambda b, c, *_: (tok(b, c), (2 * qk_w) // v_w)),
                      pl.BlockSpec((A_HEADS, 1, chunk), lambda b, c, *_: (0, 0, tok(b, c))),
                      pl.BlockSpec((A_HEADS, 1, chunk), lambda b, c, *_: (0, 0, tok(b, c))),
                      pl.BlockSpec((A_HEADS, chunk, 1), lambda b, c, *_: (0, tok(b, c), 0)),
                      pl.BlockSpec((A_HEADS, chunk, 1), lambda b, c, *_: (0, tok(b, c), 0))],
            out_specs=[pl.BlockSpec((chunk, v_w), lambda b, c, *_: (tok(b, c), 0)),
                       pl.BlockSpec((1, A_HEADS, A_DQK, A_DV), lambda b, c, *_: (b, 0, 0, 0)),
                       pl.BlockSpec((1, A_HEADS, 1, A_DQK), lambda b, c, *_: (b, 0, 0, 0)),
                       pl.BlockSpec((1, A_HEADS, 1, LANES), lambda b, c, *_: (b, 0, 0, 0))],
            scratch_shapes=[pltpu.VMEM((A_HEADS, A_DQK, A_DV), F32),
                            pltpu.VMEM((A_HEADS, 1, A_DQK), F32),
                            pltpu.VMEM((A_HEADS, 1, LANES), F32)]),
        out_shape=[jax.ShapeDtypeStruct((bp * s, v_w), BF16),
                   jax.ShapeDtypeStruct((bp, A_HEADS, A_DQK, A_DV), F32),
                   jax.ShapeDtypeStruct((bp, A_HEADS, 1, A_DQK), F32),
                   jax.ShapeDtypeStruct((bp, A_HEADS, 1, LANES), F32)],
        compiler_params=_cparams(("parallel", "arbitrary")),
    )(b_gates, z_main, z_main, z_main, gi_r, gf_r, gi_c, gf_c)


def _mlstm_s_kernel(b_ref, q_ref, k_ref, v_ref, gir_ref, gfr_ref, gic_ref, gfc_ref,
                    mc_ref, nrow_ref, c_in_ref,
                    hid_ref, c_out_ref, n_out_ref, m_out_ref,
                    hc_scr, dec_scr, kwt_scr, q_scr, *, seq):
    h = pl.program_id(1)
    R = q_ref.shape[0]
    nseq = R // seq
    per8 = SUBLANES // seq
    bi = b_ref[h]
    bf = b_ref[A_HEADS + h]
    q = q_ref[...].astype(F32)
    k = k_ref[...].astype(F32)
    v = v_ref[...].astype(F32)
    q_scr[...] = q
    li_r, lf_r = _log_gates(gir_ref[0], gfr_ref[0], bi, bf)
    li_c, lf_c = _log_gates(gic_ref[0], gfc_ref[0], bi, bf)
    m_prev = mc_ref[0]

    t_idx = lax.broadcasted_iota(jnp.int32, (R, R), 0)
    s_idx = lax.broadcasted_iota(jnp.int32, (R, R), 1)
    same = (t_idx // seq) == (s_idx // seq)
    causal = same & (s_idx <= t_idx)
    bcum_c = jnp.sum(jnp.where(causal, lf_r, 0.0), axis=1, keepdims=True)
    bcum_r = jnp.sum(jnp.where(same & (t_idx <= s_idx), lf_c, 0.0), axis=0, keepdims=True)
    blast_c = jnp.sum(jnp.where(same, lf_r, 0.0), axis=1, keepdims=True)
    blast_r = jnp.sum(jnp.where(same, lf_c, 0.0), axis=0, keepdims=True)
    log_w = jnp.where(causal, bcum_c - bcum_r + li_r, NEG_INF)
    log_state = bcum_c + m_prev
    m_row = jnp.maximum(log_state, jnp.max(log_w, axis=1, keepdims=True))
    w = jnp.exp(log_w - m_row) * K_SCALE
    w_state = jnp.exp(log_state - m_row)

    log_k_c = blast_c - bcum_c + li_c
    log_k_r = blast_r - bcum_r + li_r
    seg_max = jnp.max(jnp.where(same, log_k_r, NEG_INF), axis=1, keepdims=True)
    m_new = jnp.maximum(blast_c + m_prev, seg_max)
    wk = jnp.exp(log_k_c - m_new) * K_SCALE
    decay = jnp.exp(blast_c + m_prev - m_new)
    kw = k * wk
    kwt_scr[...] = jnp.transpose(kw)
    dec_scr[...] = jnp.broadcast_to(decay, dec_scr.shape)

    row8 = lax.broadcasted_iota(jnp.int32, (SUBLANES, A_DV), 0)
    col = lax.broadcasted_iota(jnp.int32, (A_DQK, R), 1)

    def group(gidx, carry):
        r0 = pl.multiple_of(gidx * SUBLANES, SUBLANES)
        q8 = q_scr[pl.ds(r0, SUBLANES), :]
        hc8 = jnp.zeros((SUBLANES, A_DV), F32)
        for u in range(per8):
            sq = gidx * per8 + u
            c0 = c_in_ref[0, sq, 0]
            res = _dot(q8, c0)
            hc8 = jnp.where((row8 // seq) == u, res, hc8)
            kwt_m = jnp.where((col // seq) == sq, kwt_scr[...], 0.0)
            dsc = dec_scr[pl.ds(sq * seq, 1), 0:1]
            c_out_ref[0, sq, 0] = dsc * c0 + _dot(kwt_m.astype(BF16), v_ref[...])
        hc_scr[pl.ds(r0, SUBLANES), :] = hc8
        return carry

    lax.fori_loop(0, R // SUBLANES, group, 0, unroll=4)

    sc = _dot_nt(q, k) * w
    num = _dot(sc, v) + w_state * hc_scr[...]
    den = jnp.sum(sc, axis=1, keepdims=True) + w_state * jnp.sum(q * nrow_ref[0], axis=1, keepdims=True)
    hid_ref[...] = (num / jnp.maximum(jnp.abs(den), jnp.exp(-m_row))).astype(hid_ref.dtype)

    acc = kw
    shift = 1
    while shift < seq:
        acc = acc + jnp.where((lax.broadcasted_iota(jnp.int32, acc.shape, 0) % seq) >= shift,
                              pltpu.roll(acc, shift, 0), 0.0)
        shift *= 2
    n_out_ref[0] = decay * nrow_ref[0] + acc
    m_out_ref[0] = jnp.broadcast_to(m_new, (R, LANES))


def mlstm_sample(z_main_s, b_gates, gi_r, gf_r, gi_c, gf_c, m_col, n_rows, c_state, rb, seq):
    ts = z_main_s.shape[0]
    nseq_blk = rb // seq
    kcol = (A_HEADS * A_DQK) // A_DQK
    vcol = (2 * A_HEADS * A_DQK) // A_DV
    return pl.pallas_call(
        functools.partial(_mlstm_s_kernel, seq=seq),
        grid_spec=pltpu.PrefetchScalarGridSpec(
            num_scalar_prefetch=1,
            grid=(ts // rb, A_HEADS),
            in_specs=[pl.BlockSpec((rb, A_DQK), lambda i, h, *_: (i, h)),
                      pl.BlockSpec((rb, A_DQK), lambda i, h, *_: (i, kcol + h)),
                      pl.BlockSpec((rb, A_DV), lambda i, h, *_: (i, vcol + h)),
                      pl.BlockSpec((1, 1, rb), lambda i, h, *_: (h, 0, i)),
                      pl.BlockSpec((1, 1, rb), lambda i, h, *_: (h, 0, i)),
                      pl.BlockSpec((1, rb, 1), lambda i, h, *_: (h, i, 0)),
                      pl.BlockSpec((1, rb, 1), lambda i, h, *_: (h, i, 0)),
                      pl.BlockSpec((1, rb, 1), lambda i, h, *_: (h, i, 0)),
                      pl.BlockSpec((1, rb, A_DQK), lambda i, h, *_: (h, i, 0)),
                      pl.BlockSpec((1, nseq_blk, 1, A_DQK, A_DV), lambda i, h, *_: (0, i, h, 0, 0))],
            out_specs=[pl.BlockSpec((rb, A_DV), lambda i, h, *_: (i, h)),
                       pl.BlockSpec((1, nseq_blk, 1, A_DQK, A_DV), lambda i, h, *_: (0, i, h, 0, 0)),
                       pl.BlockSpec((1, rb, A_DQK), lambda i, h, *_: (h, i, 0)),
                       pl.BlockSpec((1, rb, LANES), lambda i, h, *_: (h, i, 0))],
            scratch_shapes=[pltpu.VMEM((rb, A_DV), F32),
                            pltpu.VMEM((rb, LANES), F32),
                            pltpu.VMEM((A_DQK, rb), F32),
                            pltpu.VMEM((rb, A_DQK), F32)]),
        out_shape=[jax.ShapeDtypeStruct((ts, A_HEADS * A_DV), BF16),
                   jax.ShapeDtypeStruct(c_state.shape, F32),
                   jax.ShapeDtypeStruct((A_HEADS, ts, A_DQK), F32),
                   jax.ShapeDtypeStruct((A_HEADS, ts, LANES), F32)],
        compiler_params=_cparams(("parallel", "parallel")),
    )(b_gates, z_main_s, z_main_s, z_main_s, gi_r, gf_r, gi_c, gf_c, m_col, n_rows, c_state)


def _attn_p_kernel(q_ref, k_ref, v_ref, o_ref, *, tq, heads):
    s = q_ref.shape[0]
    lower = lax.broadcasted_iota(jnp.int32, (tq, tq), 1) <= lax.broadcasted_iota(jnp.int32, (tq, tq), 0)
    for hh in range(heads):
        qk = slice(hh * Q_HEAD_PAD, (hh + 1) * Q_HEAD_PAD)
        vv = slice(hh * B_V, (hh + 1) * B_V)
        for qt in range(s // tq):
            lo, hi = qt * tq, (qt + 1) * tq
            q = q_ref[lo:hi, qk]
            sd = jnp.where(lower, _dot_nt(q, k_ref[lo:hi, qk]), NEG_INF)
            m = jnp.max(sd, axis=-1, keepdims=True)
            if qt:
                sp = _dot_nt(q, k_ref[0:lo, qk])
                m = jnp.maximum(m, jnp.max(sp, axis=-1, keepdims=True))
            pd = jnp.exp(sd - m)
            l = jnp.sum(pd, axis=-1, keepdims=True)
            o = _dot(pd.astype(BF16), v_ref[lo:hi, vv])
            if qt:
                pp = jnp.exp(sp - m)
                l = l + jnp.sum(pp, axis=-1, keepdims=True)
                o = o + _dot(pp.astype(BF16), v_ref[0:lo, vv])
            o_ref[lo:hi, vv] = (o / l).astype(o_ref.dtype)


def attn_prompt(q, k, v, bp, s, tq, heads=2):
    return pl.pallas_call(
        functools.partial(_attn_p_kernel, tq=tq, heads=heads),
        grid=(bp, B_HEADS // heads),
        in_specs=[pl.BlockSpec((s, heads * Q_HEAD_PAD), lambda b, h: (b, h)),
                  pl.BlockSpec((s, heads * Q_HEAD_PAD), lambda b, h: (b, h)),
                  pl.BlockSpec((s, heads * B_V), lambda b, h: (b, h))],
        out_specs=pl.BlockSpec((s, heads * B_V), lambda b, h: (b, h)),
        out_shape=jax.ShapeDtypeStruct((bp * s, B_HEADS * B_V), BF16),
        compiler_params=_cparams(("parallel", "parallel")),
    )(q, k, v)


def _q_absorb_kernel(q_ref, w_ref, o_ref):
    o_ref[...] = _dot(q_ref[...], w_ref[...]).astype(o_ref.dtype)


def q_absorb(q_s, w_uk_t):
    ts = q_s.shape[0]
    c = w_uk_t.shape[1]
    return pl.pallas_call(
        _q_absorb_kernel,
        grid=(B_HEADS,),
        in_specs=[pl.BlockSpec((ts, B_NOPE), lambda h: (0, 2 * h)),
                  pl.BlockSpec((B_NOPE, c), lambda h: (h, 0))],
        out_specs=pl.BlockSpec((ts, c), lambda h: (0, h)),
        out_shape=jax.ShapeDtypeStruct((ts, B_HEADS * c), BF16),
        compiler_params=_cparams(("parallel",)),
    )(q_s, w_uk_t)


def _attn_s_kernel(pt_ref, *refs, g_pages, page, kblk, seq):
    (lat_hbm, kr_hbm, wukt_ref, qabs_ref, q_ref, cnew_ref, knew_ref, o_ref,
     wcat_scr, lat_scr, lat_buf, kr_buf, lat_sem, kr_sem) = refs
    b = pl.program_id(0)
    n_grp = lat_buf.shape[0]
    nq = qabs_ref.shape[1]
    nkey = wukt_ref.shape[0]
    per_blk = kblk // page
    blk_per_grp = g_pages // per_blk

    def page_copies(bb, grp, g):
        pg = pt_ref[bb, grp * g_pages + g]
        return (pltpu.make_async_copy(lat_hbm.at[0, pg], lat_buf.at[grp, g], lat_sem.at[grp]),
                pltpu.make_async_copy(kr_hbm.at[0, pg], kr_buf.at[grp, g], kr_sem.at[grp]))

    def start_pages(bb, grp):
        for g in range(g_pages):
            for cp in page_copies(bb, grp, g):
                cp.start()

    @pl.when(b == 0)
    def _():
        for grp in range(n_grp):
            start_pages(0, grp)
        wcat_scr[0:nkey, :] = wukt_ref[...]

    wcat_scr[nkey:nkey + nq, :] = qabs_ref[0]
    qrope = q_ref[0][:, LANES:LANES + B_ROPE]

    def scores(lb, krt):
        a = _dot_nt(wcat_scr[...], lb)
        kt = a[0:nkey]
        ssq = jnp.sum((kt * kt).reshape(B_HEADS, B_NOPE, kblk), axis=1)
        ssr = jnp.sum(krt * krt, axis=0, keepdims=True)
        r = lax.rsqrt((ssq + ssr) * (1.0 / B_QK) + EPS)
        return (a[nkey:nkey + nq] + _dot(qrope, krt.astype(BF16))) * jnp.concatenate([r] * seq, axis=0)

    s_blocks = []
    for grp in range(n_grp):
        for g in range(g_pages):
            for cp in page_copies(b, grp, g):
                cp.wait()
        for sb in range(blk_per_grp):
            pages = range(sb * per_blk, (sb + 1) * per_blk)
            lb = jnp.concatenate([lat_buf[grp, g].astype(BF16) for g in pages], axis=0)
            krt = jnp.concatenate([kr_buf[grp, g] for g in pages], axis=1)
            blk = grp * blk_per_grp + sb
            lat_scr[blk * kblk:(blk + 1) * kblk, :] = lb
            s_blocks.append(scores(lb, krt))

        @pl.when(b + 1 < pl.num_programs(0))
        def _():
            start_pages(b + 1, grp)

    qf = q_ref[0].astype(F32)
    t_of_row = lax.broadcasted_iota(jnp.int32, (nq, 1), 0) // B_HEADS
    cols = []
    for jn in range(seq):
        kj = knew_ref[0, jn * B_HEADS:(jn + 1) * B_HEADS, :].astype(F32)
        sj = jnp.sum(qf * jnp.concatenate([kj] * seq, axis=0), axis=-1, keepdims=True)
        cols.append(jnp.where(t_of_row >= jn, sj, NEG_INF))

    s = jnp.concatenate(s_blocks, axis=1)
    m = jnp.max(s, axis=-1, keepdims=True)
    for sj in cols:
        m = jnp.maximum(m, sj)
    p = jnp.exp(s - m)
    l = jnp.sum(p, axis=-1, keepdims=True)
    acc = _dot(p.astype(BF16), lat_scr[...])
    for jn, sj in enumerate(cols):
        pj = jnp.exp(sj - m)
        l = l + pj
        acc = acc + pj * cnew_ref[0, jn:jn + 1, :]
    o_ref[0] = (acc / l).astype(o_ref.dtype)


def attn_sample(page_table, cache_lat, cache_kr_t, w_uk_t, q_abs, q_s, c_new, k_new, g_pages, seq):
    db, n_pages = page_table.shape
    page, c = cache_lat.shape[2], cache_lat.shape[3]
    kblk = MXU_COLS
    assert kblk % page == 0 and g_pages % (kblk // page) == 0 and n_pages % g_pages == 0
    nq = q_abs.shape[1]
    nkey = w_uk_t.shape[0]
    n_grp = n_pages // g_pages
    per_b = lambda b, pt: (b, 0, 0)
    return pl.pallas_call(
        functools.partial(_attn_s_kernel, g_pages=g_pages, page=page, kblk=kblk, seq=seq),
        grid_spec=pltpu.PrefetchScalarGridSpec(
            num_scalar_prefetch=1,
            grid=(db,),
            in_specs=[
                pl.BlockSpec(memory_space=pl.ANY),
                pl.BlockSpec(memory_space=pl.ANY),
                pl.BlockSpec(w_uk_t.shape, lambda b, pt: (0, 0)),
                pl.BlockSpec((1, nq, c), per_b),
                pl.BlockSpec((1, nq, Q_HEAD_PAD), per_b),
                pl.BlockSpec((1, seq, c), per_b),
                pl.BlockSpec((1, nq, Q_HEAD_PAD), per_b)],
            out_specs=pl.BlockSpec((1, nq, c), per_b),
            scratch_shapes=[pltpu.VMEM((nkey + nq, c), BF16),
                            pltpu.VMEM((n_pages * page, c), BF16),
                            pltpu.VMEM((n_grp, g_pages, page, c), F32),
                            pltpu.VMEM((n_grp, g_pages, B_ROPE, page), F32),
                            pltpu.SemaphoreType.DMA((n_grp,)),
                            pltpu.SemaphoreType.DMA((n_grp,))]),
        out_shape=jax.ShapeDtypeStruct((db, nq, c), BF16),
        compiler_params=_cparams(("arbitrary",)),
    )(page_table, cache_lat, cache_kr_t, w_uk_t, q_abs, q_s, c_new, k_new)


def _uv_kernel(o_ref, w_ref, h_ref):
    h_ref[...] = _dot(o_ref[...], w_ref[...]).astype(h_ref.dtype)


def uv_expand(o_lat2, w_uv):
    ts = o_lat2.shape[0]
    c = w_uv.shape[0]
    return pl.pallas_call(
        _uv_kernel,
        grid=(B_HEADS,),
        in_specs=[pl.BlockSpec((ts, c), lambda h: (0, h)),
                  pl.BlockSpec((c, B_V), lambda h: (0, h))],
        out_specs=pl.BlockSpec((ts, B_V), lambda h: (0, h)),
        out_shape=jax.ShapeDtypeStruct((ts, B_HEADS * B_V), BF16),
        compiler_params=_cparams(("parallel",)),
    )(o_lat2, w_uv)


def _merge_kernel(hid_ref, oa_ref, ga_ref, gb_ref, hb_ref, g_ref, wa_ref, wb_ref, o_ref, ha_scr):
    @pl.when(pl.program_id(1) == 0)
    def _():
        for h in range(A_HEADS):
            sl = slice(h * A_DV, (h + 1) * A_DV)
            gate = jax.nn.sigmoid(oa_ref[:, sl].astype(F32))
            ha_scr[:, sl] = (_rms(hid_ref[:, sl].astype(F32), g_ref[:, sl]) * gate).astype(BF16)

    a = _dot(ha_scr[...], wa_ref[...])
    b = _dot(hb_ref[...], wb_ref[...])
    o_ref[...] = (jax.nn.sigmoid(ga_ref[...].astype(F32)) * a
                  + jax.nn.sigmoid(gb_ref[...].astype(F32)) * b).astype(o_ref.dtype)


def merge(hid, z_main, h_b, g_out, w_a, w_b, bm, bn):
    t, d = hid.shape
    nj = d // bn
    return pl.pallas_call(
        _merge_kernel,
        grid=(t // bm, nj),
        in_specs=[pl.BlockSpec((bm, d), lambda i, j: (i, 0)),
                  pl.BlockSpec((bm, d), lambda i, j: (i, 2)),
                  pl.BlockSpec((bm, bn), lambda i, j: (i, 3 * nj + j)),
                  pl.BlockSpec((bm, bn), lambda i, j: (i, 4 * nj + j)),
                  pl.BlockSpec((bm, d), lambda i, j: (i, 0)),
                  pl.BlockSpec((1, d), lambda i, j: (0, 0)),
                  pl.BlockSpec((d, bn), lambda i, j: (0, j)),
                  pl.BlockSpec((d, bn), lambda i, j: (0, j))],
        out_specs=pl.BlockSpec((bm, bn), lambda i, j: (i, j)),
        out_shape=jax.ShapeDtypeStruct((t, d), BF16),
        scratch_shapes=[pltpu.VMEM((bm, d), BF16)],
        compiler_params=_cparams(("parallel", "arbitrary")),
    )(hid, z_main, z_main, z_main, h_b, g_out, w_a, w_b)


def _resid_mm_kernel(x_ref, a_ref, w_ref, o_ref):
    o_ref[...] = x_ref[...] + _dot(a_ref[...], w_ref[...])


def resid_matmul(x, a, w, bm, bn):
    t, d = x.shape
    k = a.shape[1]
    return pl.pallas_call(
        _resid_mm_kernel,
        grid=(t // bm, d // bn),
        in_specs=[pl.BlockSpec((bm, bn), lambda i, j: (i, j)),
                  pl.BlockSpec((bm, k), lambda i, j: (i, 0)),
                  pl.BlockSpec((k, bn), lambda i, j: (0, j))],
        out_specs=pl.BlockSpec((bm, bn), lambda i, j: (i, j)),
        out_shape=jax.ShapeDtypeStruct((t, d), F32),
        compiler_params=_cparams(("parallel", "parallel")),
    )(x, a, w)


def _ffn_p_kernel(x_ref, xh_ref, g_ref, wg_ref, wv_ref, cg_ref, cv_ref, bg_ref, bv_ref, wd_ref,
                  y_ref, ug_ref, uv_ref, xn_scr, acc_scr, *, seq_len):
    i = pl.program_id(0)
    j = pl.program_id(1)
    bm = x_ref.shape[0]
    hp = SUBLANES

    @pl.when(j == 0)
    def _():
        xn_scr[hp:, :] = _rms(x_ref[...], g_ref[...]).astype(BF16)
        keep = jnp.where((i * bm) % seq_len == 0, 0.0, 1.0)
        xn_scr[0:hp, :] = (_rms(xh_ref[...], g_ref[...]) * keep).astype(BF16)
        acc_scr[...] = jnp.zeros_like(acc_scr)

    xall = xn_scr[...]

    def conv(u, wc_ref, bc_ref):
        e2 = u[hp:, :]
        e1 = u[hp - 1:hp - 1 + bm, :]
        e0 = u[hp - 2:hp - 2 + bm, :]
        return bc_ref[...] + ((e0 * wc_ref[0:1, :] + e1 * wc_ref[1:2, :]) + e2 * wc_ref[2:3, :])

    ug = _dot(xall, wg_ref[...])
    uv = _dot(xall, wv_ref[...])
    ug_ref[0] = ug[bm:, :]
    uv_ref[0] = uv[bm:, :]
    cg = conv(ug, cg_ref, bg_ref)
    cv = conv(uv, cv_ref, bv_ref)
    act = (cg * jax.nn.sigmoid(cg) * cv).astype(BF16)
    acc_scr[...] += _dot(act, wd_ref[...])

    @pl.when(j == pl.num_programs(1) - 1)
    def _():
        y_ref[...] = x_ref[...] + acc_scr[...]


def _ffn_weight_specs(d, bf, nj):
    return [pl.BlockSpec((1, d), lambda i, j: (0, 0)),
            pl.BlockSpec((d, bf), lambda i, j: (0, j)),
            pl.BlockSpec((d, bf), lambda i, j: (0, nj + j)),
            pl.BlockSpec((CONV_W, bf), lambda i, j: (0, j)),
            pl.BlockSpec((CONV_W, bf), lambda i, j: (0, nj + j)),
            pl.BlockSpec((1, bf), lambda i, j: (0, j)),
            pl.BlockSpec((1, bf), lambda i, j: (0, nj + j)),
            pl.BlockSpec((bf, d), lambda i, j: (j, 0))]


def conv_ffn_prompt(x1, g, w_up, w_conv, b_conv, w_down, bm, bf, seq_len):
    rows, d = x1.shape
    f = w_down.shape[0]
    assert seq_len % bm == 0 and bm >= CONV_W - 1, "row blocks must not straddle sequences"
    nj = f // bf
    nb = rows // bm
    hb = bm // SUBLANES
    return pl.pallas_call(
        functools.partial(_ffn_p_kernel, seq_len=seq_len),
        grid=(nb, nj),
        in_specs=[pl.BlockSpec((bm, d), lambda i, j: (i, 0)),
                  pl.BlockSpec((SUBLANES, d), lambda i, j: (jnp.maximum(i * hb - 1, 0), 0))]
        + _ffn_weight_specs(d, bf, nj),
        out_specs=[pl.BlockSpec((bm, d), lambda i, j: (i, 0)),
                   pl.BlockSpec((1, SUBLANES, bf), lambda i, j: (i, 0, j)),
                   pl.BlockSpec((1, SUBLANES, bf), lambda i, j: (i, 0, j))],
        out_shape=[jax.ShapeDtypeStruct((rows, d), F32),
                   jax.ShapeDtypeStruct((nb, SUBLANES, f), F32),
                   jax.ShapeDtypeStruct((nb, SUBLANES, f), F32)],
        scratch_shapes=[pltpu.VMEM((bm + SUBLANES, d), BF16),
                        pltpu.VMEM((bm, d), F32)],
        compiler_params=_cparams(("parallel", "arbitrary")),
    )(x1, x1, g, w_up, w_up, w_conv, w_conv, b_conv, b_conv, w_down)


def _ffn_s_kernel(x_ref, g_ref, wg_ref, wv_ref, cg_ref, cv_ref, bg_ref, bv_ref, wd_ref, prev_ref,
                  y_ref, new_ref, xn_scr, acc_scr, *, seq):
    j = pl.program_id(1)
    nb = x_ref.shape[0] // seq

    @pl.when(j == 0)
    def _():
        xn_scr[...] = _rms(x_ref[...], g_ref[...]).astype(BF16)
        acc_scr[...] = jnp.zeros_like(acc_scr)

    xn = xn_scr[...]

    def conv(u, half, wc_ref, bc_ref):
        ext = [prev_ref[r, half] for r in range(CONV_W - 1)] + [u[t * nb:(t + 1) * nb, :] for t in range(seq)]
        for r in range(CONV_W - 1):
            new_ref[r, half] = ext[seq + r]
        return jnp.concatenate(
            [bc_ref[...] + ((ext[t] * wc_ref[0:1, :] + ext[t + 1] * wc_ref[1:2, :]) + ext[t + 2] * wc_ref[2:3, :])
             for t in range(seq)], axis=0)

    cg = conv(_dot(xn, wg_ref[...]), 0, cg_ref, bg_ref)
    cv = conv(_dot(xn, wv_ref[...]), 1, cv_ref, bv_ref)
    act = (cg * jax.nn.sigmoid(cg) * cv).astype(BF16)
    acc_scr[...] += _dot(act, wd_ref[...])

    @pl.when(j == pl.num_programs(1) - 1)
    def _():
        y_ref[...] = x_ref[...] + acc_scr[...]


def conv_ffn_sample(x1_t, prev_t, g, w_up, w_conv, b_conv, w_down, bf, seq):
    rows, d = x1_t.shape
    f = w_down.shape[0]
    nj = f // bf
    nb = rows // seq
    hist = pl.BlockSpec((CONV_W - 1, 2, nb, bf), lambda i, j: (0, 0, 0, j))
    return pl.pallas_call(
        functools.partial(_ffn_s_kernel, seq=seq),
        grid=(1, nj),
        in_specs=[pl.BlockSpec((rows, d), lambda i, j: (0, 0))] + _ffn_weight_specs(d, bf, nj) + [hist],
        out_specs=[pl.BlockSpec((rows, d), lambda i, j: (0, 0)), hist],
        out_shape=[jax.ShapeDtypeStruct((rows, d), F32),
                   jax.ShapeDtypeStruct(prev_t.shape, F32)],
        scratch_shapes=[pltpu.VMEM((rows, d), BF16),
                        pltpu.VMEM((rows, d), F32)],
        compiler_params=_cparams(("parallel", "arbitrary")),
    )(x1_t, g, w_up, w_up, w_conv, w_conv, b_conv, b_conv, w_down, prev_t)


def _pick(n, prefs):
    for p in prefs:
        if n % p == 0:
            return p
    return n


def _layer(x_prompt, x_sample, cache_lat, cache_kr, c_state, n_state, m_state, conv_state, page_table,
           g_attn_norm, w_in, b_gates, g_q_a, w_uq, g_qk_nope_q, g_qk_rope_q, g_kv_a, w_uk, w_uv,
           g_qk_nope_k, g_qk_rope_k, g_mlstm_out, w_branch_a, w_branch_b, w_out, g_ffn_norm,
           w_up, w_conv, b_conv, w_down):
    bp, s, d = x_prompt.shape
    db, seq, _ = x_sample.shape
    n_pages = page_table.shape[1]
    page = cache_lat.shape[2]
    past = n_pages * page
    tp, ts = bp * s, db * seq
    t = tp + ts
    c_lat = g_kv_a.shape[0]
    c_q = g_q_a.shape[0]
    f = w_down.shape[0]
    qk_w = A_HEADS * A_DQK
    v_w = A_HEADS * A_DV
    row = lambda v: v.reshape(1, -1)

    o_i = 2 * qk_w + 2 * v_w
    o_cq = o_i + 2 * A_HEADS
    o_kr = o_cq + c_q + c_lat
    o_g = o_kr + B_ROPE
    w_in_b = w_in.astype(BF16)
    w_gab = w_in_b[:, o_g:]
    n_small = LANES - B_ROPE - 2 * A_HEADS
    w_lat = jnp.concatenate([w_in_b[:, o_cq:o_kr], w_in_b[:, o_kr:o_g], w_in_b[:, o_i:o_cq],
                             jnp.zeros((d, n_small), BF16)], axis=1)
    w_uq_pad = jnp.concatenate([w_uq, w_uq[:, :, B_NOPE:]], axis=2).reshape(c_q, B_HEADS * Q_HEAD_PAD).astype(BF16)
    w_uk2 = w_uk.reshape(c_lat, B_HEADS * B_NOPE).astype(BF16)
    w_uk_t = jnp.transpose(w_uk.reshape(c_lat, B_HEADS * B_NOPE)).astype(BF16)
    w_uv2 = w_uv.reshape(c_lat, B_HEADS * B_V).astype(BF16)
    gq = jnp.concatenate([g_qk_nope_q * g_qk_nope_k, g_qk_rope_q * g_qk_rope_k, g_qk_rope_q * g_qk_rope_k,
                          jnp.zeros((Q_HEAD_PAD - B_QK,), F32)]) * ATTN_SCALE
    gq = jnp.tile(gq, B_HEADS).reshape(1, -1)

    w_a_b, w_b_b, w_out_b = w_branch_a.astype(BF16), w_branch_b.astype(BF16), w_out.astype(BF16)
    w_up_b, w_down_b = w_up.astype(BF16), w_down.astype(BF16)
    half = B_ROPE // 2
    freqs = ROPE_THETA ** (-jnp.arange(half, dtype=F32) / half)

    def token_stage(x, pos, bm):
        rows = x.shape[0]
        ang = pos.astype(F32)[:, None] * freqs
        cos, sin = jnp.cos(ang), jnp.sin(ang)
        zh = jnp.zeros((rows, half), F32)
        zr = jnp.zeros((rows, LANES - B_ROPE), F32)
        cos128 = jnp.concatenate([cos, cos, zr], axis=1)
        sin_a = jnp.concatenate([zh, sin, zr], axis=1)
        sin_b = jnp.concatenate([-sin, zh, zr], axis=1)
        bn = _pick(d, (1024, 512, 256, 128))
        z_main = norm_matmul2(x, row(g_attn_norm), w_in_b, o_i, w_gab, bm, bn, BF16)
        z_lat = norm_matmul(x, row(g_attn_norm), w_lat, bm, w_lat.shape[1])
        bq = _pick(rows, (256, 128, 64, 32, 16, 8))
        q = q_proj(z_lat, row(g_q_a), w_uq_pad, cos128, sin_a + sin_b, gq, bq)
        c_kv, kr128, k, v = kv_proj(z_lat, row(g_kv_a), w_uk2, w_uv2, cos128, sin_a, sin_b, bq)
        g_t = jnp.transpose(z_lat[:, c_q + c_lat + B_ROPE:c_q + c_lat + B_ROPE + 2 * A_HEADS])
        gates = (g_t[:A_HEADS, None, :], g_t[A_HEADS:, None, :], g_t[:A_HEADS, :, None], g_t[A_HEADS:, :, None])
        return z_main, q, c_kv, kr128, k, v, gates

    def mix_stage(x, hid, z_main, h_b):
        bm = _pick(x.shape[0], (1024, 512, 256, 128, 64, 32, 16, 8))
        merged = merge(hid, z_main, h_b, row(g_mlstm_out), w_a_b, w_b_b, bm, _pick(d, (512, 256, 128)))
        return resid_matmul(x, merged, w_out_b, bm, _pick(d, (1024, 512, 256, 128)))

    bff = _pick(f, (512, 256, 128))

    xp = x_prompt.reshape(tp, d)
    bm_p = _pick(tp, (1024, 512, 256, 128, 64, 32, 16, 8))
    zm_p, q_p, ckv_p, kr_p, k_p, v_p, gates_p = token_stage(xp, jnp.tile(jnp.arange(s, dtype=jnp.int32), bp), bm_p)
    chunk = _pick(s, (256, 128, 64, 32, 16, 8))
    hid_p, c_p, n_p, m_p = mlstm_prompt(zm_p, b_gates, *gates_p, bp, s, chunk)
    hb_p = attn_prompt(q_p, k_p, v_p, bp, s, _pick(s, (512, 256, 128)))
    x1_p = mix_stage(xp, hid_p, zm_p, hb_p)
    bmf_p = _pick(s, (512, 256, 128, 64, 32, 16, 8))
    y_p, tg_p, tv_p = conv_ffn_prompt(x1_p, row(g_ffn_norm), w_up_b, w_conv, row(b_conv), w_down_b, bmf_p, bff, s)
    nb_seq = s // bmf_p
    tail_p = jnp.concatenate([tg_p, tv_p], axis=-1).reshape(bp, nb_seq, SUBLANES, 2 * f)
    conv_p = tail_p[:, nb_seq - 1, SUBLANES - (CONV_W - 1):, :]

    xs = x_sample.reshape(ts, d)
    bm_s = _pick(ts, (512, 256, 128, 64, 32, 16, 8))
    zm_s, q_s, ckv_s, kr_s, k_s, _, gates_s = token_stage(
        xs, jnp.tile(past + jnp.arange(seq, dtype=jnp.int32), db), bm_s)
    rb = _pick(ts, (128, 64, 32, 16, 8))
    m_col = jnp.repeat(jnp.transpose(m_state), seq, axis=1)[:, :, None]
    n_rows = jnp.repeat(jnp.transpose(n_state, (1, 0, 2)), seq, axis=1)
    hid_s, c_s, n_s_rows, m_s_rows = mlstm_sample(zm_s, b_gates, *gates_s, m_col, n_rows, c_state[None], rb, seq)
    n_s = jnp.transpose(n_s_rows[:, seq - 1::seq, :], (1, 0, 2))
    m_s = jnp.transpose(m_s_rows[:, seq - 1::seq, 0])

    q_abs = q_absorb(q_s, w_uk_t).reshape(db, seq * B_HEADS, c_lat)
    g_pages = _pick(n_pages, (32, 16, 8, 4, 2))
    o_lat = attn_sample(page_table, cache_lat, jnp.swapaxes(cache_kr, 2, 3), w_uk_t, q_abs,
                        q_s.reshape(db, seq * B_HEADS, Q_HEAD_PAD), ckv_s.reshape(db, seq, c_lat),
                        k_s.reshape(db, seq * B_HEADS, Q_HEAD_PAD), g_pages, seq)
    hb_s = uv_expand(o_lat.reshape(ts, B_HEADS * c_lat), w_uv2)
    x1_s = mix_stage(xs, hid_s, zm_s, hb_s)
    x1_t = jnp.transpose(x1_s.reshape(db, seq, d), (1, 0, 2)).reshape(ts, d)
    prev_t = jnp.transpose(conv_state.reshape(db, CONV_W - 1, 2, f), (1, 2, 0, 3))
    y_t, new_t = conv_ffn_sample(x1_t, prev_t, row(g_ffn_norm), w_up_b, w_conv, row(b_conv), w_down_b, bff, seq)
    y_s = jnp.transpose(y_t.reshape(seq, db, d), (1, 0, 2))
    conv_s = jnp.transpose(new_t, (2, 0, 1, 3)).reshape(db, CONV_W - 1, 2 * f)

    new_p = (ckv_p.reshape(bp, s, c_lat), kr_p[:, :B_ROPE].reshape(bp, s, B_ROPE),
             c_p, n_p[:, :, 0, :], m_p[:, :, 0, 0], conv_p)
    new_s = (ckv_s.reshape(db, seq, c_lat), kr_s[:, :B_ROPE].reshape(db, seq, B_ROPE),
             c_s[0], n_s, m_s, conv_s)
    return y_p.reshape(bp, s, d), y_s.reshape(db, seq, d), new_p, new_s


def kernel(x_prompt, x_sample, cache_kv_latent, cache_k_rope, state_mlstm_C, state_mlstm_n, state_mlstm_m, state_conv, page_table, g_attn_norm, w_in, b_gates, g_q_a, w_uq, g_qk_nope_q, g_qk_rope_q, g_kv_a, w_uk, w_uv, g_qk_nope_k, g_qk_rope_k, g_mlstm_out, w_branch_a, w_branch_b, w_out, g_ffn_norm, w_up, w_conv, b_conv, w_down):
    depth = w_in.shape[0]
    assert depth == 1, "single-layer trunk"
    l = 0
    y_p, y_s, new_p, new_s = _layer(
        x_prompt, x_sample, cache_kv_latent, cache_k_rope, state_mlstm_C[l], state_mlstm_n[l],
        state_mlstm_m[l], state_conv[l], page_table, g_attn_norm[l], w_in[l], b_gates[l], g_q_a[l], w_uq[l],
        g_qk_nope_q[l], g_qk_rope_q[l], g_kv_a[l], w_uk[l], w_uv[l], g_qk_nope_k[l], g_qk_rope_k[l],
        g_mlstm_out[l], w_branch_a[l], w_branch_b[l], w_out[l], g_ffn_norm[l], w_up[l], w_conv[l], b_conv[l],
        w_down[l])
    dts = (cache_kv_latent.dtype, cache_k_rope.dtype, state_mlstm_C.dtype, state_mlstm_n.dtype,
           state_mlstm_m.dtype, state_conv.dtype)
    st_p = tuple(a[None].astype(dt) for a, dt in zip(new_p, dts))
    st_s = tuple(a[None].astype(dt) for a, dt in zip(new_s, dts))
    return (y_p, y_s) + st_p + st_s
```

```python
import functools
import math

import jax
import jax.numpy as jnp
from jax import lax
from jax.experimental import pallas as pl
from jax.experimental.pallas import tpu as pltpu

F32 = jnp.float32
BF16 = jnp.bfloat16

A_HEADS = 8
A_DQK = 128
A_DV = 256
K_SCALE = A_DQK ** -0.5
GATE_CAP = 15.0
B_HEADS = 16
B_NOPE = 128
B_ROPE = 64
B_QK = B_NOPE + B_ROPE
B_V = 128
Q_HEAD_PAD = 256
ROPE_THETA = 10000.0
ATTN_SCALE = B_QK ** -0.5
CONV_W = 3
EPS = 1e-6
NEG_INF = float("-inf")

LANES = 128
SUBLANES = 8
MXU_COLS = 256
VMEM_LIMIT = 56 * 1024 * 1024


def _cparams(sem):
    return pltpu.CompilerParams(dimension_semantics=sem, vmem_limit_bytes=VMEM_LIMIT)


def _rms(x, g):
    r = lax.rsqrt(jnp.mean(x * x, axis=-1, keepdims=True) + EPS)
    return x * r * g


def _dot(a, b):
    return jnp.dot(a, b, preferred_element_type=F32)


def _dot_nt(a, b):
    return lax.dot_general(a, b, (((1,), (1,)), ((), ())), preferred_element_type=F32)


def _rope128(x, cos, sin_a, sin_b):
    return x * cos + pltpu.roll(x, 32, 1) * sin_a + pltpu.roll(x, 96, 1) * sin_b


def _norm_mm_kernel(x_ref, g_ref, w_ref, o_ref, xn_ref):
    @pl.when(pl.program_id(1) == 0)
    def _():
        xn_ref[...] = _rms(x_ref[...], g_ref[...]).astype(BF16)

    o_ref[...] = _dot(xn_ref[...], w_ref[...]).astype(o_ref.dtype)


def norm_matmul(x, g, w, bm, bn, out_dtype=F32):
    t, d = x.shape
    n = w.shape[1]
    return pl.pallas_call(
        _norm_mm_kernel,
        grid=(t // bm, n // bn),
        in_specs=[pl.BlockSpec((bm, d), lambda i, j: (i, 0)),
                  pl.BlockSpec((1, d), lambda i, j: (0, 0)),
                  pl.BlockSpec((d, bn), lambda i, j: (0, j))],
        out_specs=pl.BlockSpec((bm, bn), lambda i, j: (i, j)),
        out_shape=jax.ShapeDtypeStruct((t, n), out_dtype),
        scratch_shapes=[pltpu.VMEM((bm, d), BF16)],
        compiler_params=_cparams(("parallel", "arbitrary")),
    )(x, g, w)


def _norm_mm2_kernel(x_ref, g_ref, wa_ref, wb_ref, o_ref, xn_ref, *, na_tiles):
    j = pl.program_id(1)

    @pl.when(j == 0)
    def _():
        xn_ref[...] = _rms(x_ref[...], g_ref[...]).astype(BF16)

    @pl.when(j < na_tiles)
    def _():
        o_ref[...] = _dot(xn_ref[...], wa_ref[...]).astype(o_ref.dtype)

    @pl.when(j >= na_tiles)
    def _():
        o_ref[...] = _dot(xn_ref[...], wb_ref[...]).astype(o_ref.dtype)


def norm_matmul2(x, g, w_a, na_cols, w_b, bm, bn, out_dtype):
    t, d = x.shape
    assert na_cols % bn == 0 and w_b.shape[1] % bn == 0
    na_tiles, nb_tiles = na_cols // bn, w_b.shape[1] // bn
    return pl.pallas_call(
        functools.partial(_norm_mm2_kernel, na_tiles=na_tiles),
        grid=(t // bm, na_tiles + nb_tiles),
        in_specs=[pl.BlockSpec((bm, d), lambda i, j: (i, 0)),
                  pl.BlockSpec((1, d), lambda i, j: (0, 0)),
                  pl.BlockSpec((d, bn), lambda i, j: (0, jnp.minimum(j, na_tiles - 1))),
                  pl.BlockSpec((d, bn), lambda i, j: (0, jnp.maximum(j - na_tiles, 0)))],
        out_specs=pl.BlockSpec((bm, bn), lambda i, j: (i, j)),
        out_shape=jax.ShapeDtypeStruct((t, (na_tiles + nb_tiles) * bn), out_dtype),
        scratch_shapes=[pltpu.VMEM((bm, d), BF16)],
        compiler_params=_cparams(("parallel", "arbitrary")),
    )(x, g, w_a, w_b)


def _q_kernel(cq_ref, g_ref, w_ref, cos_ref, sr_ref, gain_ref, o_ref):
    cqn = _rms(cq_ref[...], g_ref[...]).astype(BF16)
    q = _dot(cqn, w_ref[...])
    cos, sr = cos_ref[...], sr_ref[...]
    for h in range(B_HEADS):
        lo = h * Q_HEAD_PAD
        nope = q[:, lo:lo + LANES]
        rp = q[:, lo + LANES:lo + Q_HEAD_PAD]
        rp = rp * cos + pltpu.roll(rp, B_ROPE // 2, 1) * sr
        ssq = jnp.sum(nope * nope + rp * rp, axis=-1, keepdims=True)
        r = lax.rsqrt(ssq * (1.0 / B_QK) + EPS)
        o_ref[:, lo:lo + LANES] = (nope * r * gain_ref[:, lo:lo + LANES]).astype(o_ref.dtype)
        o_ref[:, lo + LANES:lo + Q_HEAD_PAD] = (rp * r * gain_ref[:, lo + LANES:lo + Q_HEAD_PAD]).astype(o_ref.dtype)


def q_proj(z_lat, g_q_a, w_uq_pad, cos, sin_roll, gain, bm):
    t = z_lat.shape[0]
    c = g_q_a.shape[1]
    n = w_uq_pad.shape[1]
    return pl.pallas_call(
        _q_kernel,
        grid=(t // bm,),
        in_specs=[pl.BlockSpec((bm, c), lambda i: (i, 0)),
                  pl.BlockSpec((1, c), lambda i: (0, 0)),
                  pl.BlockSpec((c, n), lambda i: (0, 0)),
                  pl.BlockSpec((bm, LANES), lambda i: (i, 0)),
                  pl.BlockSpec((bm, LANES), lambda i: (i, 0)),
                  pl.BlockSpec((1, n), lambda i: (0, 0))],
        out_specs=pl.BlockSpec((bm, n), lambda i: (i, 0)),
        out_shape=jax.ShapeDtypeStruct((t, n), BF16),
        compiler_params=_cparams(("parallel",)),
    )(z_lat, g_q_a, w_uq_pad, cos, sin_roll, gain)


def _kv_kernel(ckv_ref, sm_ref, g_ref, wuk_ref, wuv_ref, cos_ref, sa_ref, sb_ref,
               c_ref, kr_ref, k_ref, v_ref):
    c = _rms(ckv_ref[...], g_ref[...])
    c_ref[...] = c
    cb = c.astype(BF16)
    kn = _dot(cb, wuk_ref[...])
    v_ref[...] = _dot(cb, wuv_ref[...]).astype(v_ref.dtype)
    kr = _rope128(sm_ref[...], cos_ref[...], sa_ref[...], sb_ref[...])
    kr_ref[...] = kr
    ssr = jnp.sum(kr * kr, axis=-1, keepdims=True)
    for h in range(B_HEADS):
        nope = kn[:, h * B_NOPE:(h + 1) * B_NOPE]
        r = lax.rsqrt((jnp.sum(nope * nope, axis=-1, keepdims=True) + ssr) * (1.0 / B_QK) + EPS)
        lo = h * Q_HEAD_PAD
        k_ref[:, lo:lo + LANES] = (nope * r).astype(k_ref.dtype)
        k_ref[:, lo + LANES:lo + Q_HEAD_PAD] = (kr * r).astype(k_ref.dtype)


def kv_proj(z_lat, g_kv_a, w_uk, w_uv, cos, sa, sb, bm):
    t = z_lat.shape[0]
    c = g_kv_a.shape[1]
    small_blk = (2 * c) // LANES
    row = lambda i: (i, 0)
    const = lambda i: (0, 0)
    return pl.pallas_call(
        _kv_kernel,
        grid=(t // bm,),
        in_specs=[pl.BlockSpec((bm, c), lambda i: (i, 1)),
                  pl.BlockSpec((bm, LANES), lambda i: (i, small_blk)),
                  pl.BlockSpec((1, c), const),
                  pl.BlockSpec(w_uk.shape, const),
                  pl.BlockSpec(w_uv.shape, const),
                  pl.BlockSpec((bm, LANES), row),
                  pl.BlockSpec((bm, LANES), row),
                  pl.BlockSpec((bm, LANES), row)],
        out_specs=[pl.BlockSpec((bm, c), row),
                   pl.BlockSpec((bm, LANES), row),
                   pl.BlockSpec((bm, B_HEADS * Q_HEAD_PAD), row),
                   pl.BlockSpec((bm, B_HEADS * B_V), row)],
        out_shape=[jax.ShapeDtypeStruct((t, c), F32),
                   jax.ShapeDtypeStruct((t, LANES), F32),
                   jax.ShapeDtypeStruct((t, B_HEADS * Q_HEAD_PAD), BF16),
                   jax.ShapeDtypeStruct((t, B_HEADS * B_V), BF16)],
        compiler_params=_cparams(("parallel",)),
    )(z_lat, z_lat, g_kv_a, w_uk, w_uv, cos, sa, sb)


def _log_gates(gi, gf, bi, bf):
    cap = lambda x: GATE_CAP * jnp.tanh(x * (1.0 / GATE_CAP))
    li = cap(gi + bi)
    y = cap(gf + bf)
    lf = jnp.minimum(y, 0.0) - jnp.log(1.0 + jnp.exp(-jnp.abs(y)))
    return li, lf


def _mlstm_p_kernel(b_ref, q_ref, k_ref, v_ref, gir_ref, gfr_ref, gic_ref, gfc_ref,
                    hid_ref, c_out_ref, n_out_ref, m_out_ref, c_scr, n_scr, m_scr):
    ci = pl.program_id(1)
    L = q_ref.shape[0]

    @pl.when(ci == 0)
    def _():
        c_scr[...] = jnp.zeros_like(c_scr)
        n_scr[...] = jnp.zeros_like(n_scr)
        m_scr[...] = jnp.zeros_like(m_scr)

    t_idx = lax.broadcasted_iota(jnp.int32, (L, L), 0)
    s_idx = lax.broadcasted_iota(jnp.int32, (L, L), 1)
    causal = s_idx <= t_idx
    anti = t_idx <= s_idx
    for h in range(A_HEADS):
        bi = b_ref[h]
        bf = b_ref[A_HEADS + h]
        qb = q_ref[:, h * A_DQK:(h + 1) * A_DQK]
        kb = k_ref[:, h * A_DQK:(h + 1) * A_DQK]
        vb = v_ref[:, h * A_DV:(h + 1) * A_DV]
        q = qb.astype(F32)
        li_r, lf_r = _log_gates(gir_ref[h], gfr_ref[h], bi, bf)
        li_c, lf_c = _log_gates(gic_ref[h], gfc_ref[h], bi, bf)

        bcum_c = jnp.sum(jnp.where(causal, lf_r, 0.0), axis=1, keepdims=True)
        bcum_r = jnp.sum(jnp.where(anti, lf_c, 0.0), axis=0, keepdims=True)
        log_w = jnp.where(causal, bcum_c - bcum_r + li_r, NEG_INF)
        m_prev = m_scr[h][:, 0:1]
        c_prev = c_scr[h]
        n_prev = n_scr[h]
        log_state = bcum_c + m_prev
        m_row = jnp.maximum(log_state, jnp.max(log_w, axis=1, keepdims=True))
        w = jnp.exp(log_w - m_row) * K_SCALE
        w_state = jnp.exp(log_state - m_row)
        sc = _dot_nt(qb, kb) * w
        num = _dot(sc.astype(BF16), vb) + w_state * _dot(qb, c_prev.astype(BF16))
        den = jnp.sum(sc, axis=1, keepdims=True) + w_state * jnp.sum(q * n_prev, axis=1, keepdims=True)
        hid_ref[:, h * A_DV:(h + 1) * A_DV] = (
            num / jnp.maximum(jnp.abs(den), jnp.exp(-m_row))).astype(hid_ref.dtype)

        b_last = bcum_c[L - 1:L, :]
        log_k = b_last - bcum_c + li_c
        m_new = jnp.maximum(b_last + m_prev, jnp.max(log_k, axis=0, keepdims=True))
        wk = jnp.exp(log_k - m_new) * K_SCALE
        decay = jnp.exp(b_last + m_prev - m_new)
        kw = kb.astype(F32) * wk
        kwt = jnp.transpose(kw).astype(BF16)
        c_scr[h] = decay * c_prev + _dot(kwt, vb)
        n_scr[h] = decay * n_prev + jnp.sum(kw, axis=0, keepdims=True)
        m_scr[h] = jnp.broadcast_to(m_new, (1, LANES))

    @pl.when(ci == pl.num_programs(1) - 1)
    def _():
        c_out_ref[0] = c_scr[...]
        n_out_ref[0] = n_scr[...]
        m_out_ref[0] = m_scr[...]


def mlstm_prompt(z_main, b_gates, gi_r, gf_r, gi_c, gf_c, bp, s, chunk):
    nc = s // chunk
    qk_w, v_w = A_HEADS * A_DQK, A_HEADS * A_DV
    assert (2 * qk_w) % v_w == 0
    tok = lambda b, c, *_: b * nc + c
    return pl.pallas_call(
        _mlstm_p_kernel,
        grid_spec=pltpu.PrefetchScalarGridSpec(
            num_scalar_prefetch=1,
            grid=(bp, nc),
            in_specs=[pl.BlockSpec((chunk, qk_w), lambda b, c, *_: (tok(b, c), 0)),
                      pl.BlockSpec((chunk, qk_w), lambda b, c, *_: (tok(b, c), 1)),
                      pl.BlockSpec((chunk, v_w), lambda b, c, *_: (tok(b, c), (2 * qk_w) // v_w)),
                      pl.BlockSpec((A_HEADS, 1, chunk), lambda b, c, *_: (0, 0, tok(b, c))),
                      pl.BlockSpec((A_HEADS, 1, chunk), lambda b, c, *_: (0, 0, tok(b, c))),
                      pl.BlockSpec((A_HEADS, chunk, 1), lambda b, c, *_: (0, tok(b, c), 0)),
                      pl.BlockSpec((A_HEADS, chunk, 1), lambda b, c, *_: (0, tok(b, c), 0))],
            out_specs=[pl.BlockSpec((chunk, v_w), lambda b, c, *_: (tok(b, c), 0)),
                       pl.BlockSpec((1, A_HEADS, A_DQK, A_DV), lambda b, c, *_: (b, 0, 0, 0)),
                       pl.BlockSpec((1, A_HEADS, 1, A_DQK), lambda b, c, *_: (b, 0, 0, 0)),
                       pl.BlockSpec((1, A_HEADS, 1, LANES), lambda b, c, *_: (b, 0, 0, 0))],
            scratch_shapes=[pltpu.VMEM((A_HEADS, A_DQK, A_DV), F32),
                            pltpu.VMEM((A_HEADS, 1, A_DQK), F32),
                            pltpu.VMEM((A_HEADS, 1, LANES), F32)]),
        out_shape=[jax.ShapeDtypeStruct((bp * s, v_w), BF16),
                   jax.ShapeDtypeStruct((bp, A_HEADS, A_DQK, A_DV), F32),
                   jax.ShapeDtypeStruct((bp, A_HEADS, 1, A_DQK), F32),
                   jax.ShapeDtypeStruct((bp, A_HEADS, 1, LANES), F32)],
        compiler_params=_cparams(("parallel", "arbitrary")),
    )(b_gates, z_main, z_main, z_main, gi_r, gf_r, gi_c, gf_c)


def _mlstm_s_kernel(b_ref, q_ref, k_ref, v_ref, gir_ref, gfr_ref, gic_ref, gfc_ref,
                    mc_ref, nrow_ref, c_in_ref,
                    hid_ref, c_out_ref, n_out_ref, m_out_ref,
                    hc_scr, dec_scr, kwt_scr, q_scr, *, seq):
    h = pl.program_id(1)
    R = q_ref.shape[0]
    nseq = R // seq
    per8 = SUBLANES // seq
    bi = b_ref[h]
    bf = b_ref[A_HEADS + h]
    q = q_ref[...].astype(F32)
    k = k_ref[...].astype(F32)
    v = v_ref[...].astype(F32)
    q_scr[...] = q
    li_r, lf_r = _log_gates(gir_ref[0], gfr_ref[0], bi, bf)
    li_c, lf_c = _log_gates(gic_ref[0], gfc_ref[0], bi, bf)
    m_prev = mc_ref[0]

    t_idx = lax.broadcasted_iota(jnp.int32, (R, R), 0)
    s_idx = lax.broadcasted_iota(jnp.int32, (R, R), 1)
    same = (t_idx // seq) == (s_idx // seq)
    causal = same & (s_idx <= t_idx)
    bcum_c = jnp.sum(jnp.where(causal, lf_r, 0.0), axis=1, keepdims=True)
    bcum_r = jnp.sum(jnp.where(same & (t_idx <= s_idx), lf_c, 0.0), axis=0, keepdims=True)
    blast_c = jnp.sum(jnp.where(same, lf_r, 0.0), axis=1, keepdims=True)
    blast_r = jnp.sum(jnp.where(same, lf_c, 0.0), axis=0, keepdims=True)
    log_w = jnp.where(causal, bcum_c - bcum_r + li_r, NEG_INF)
    log_state = bcum_c + m_prev
    m_row = jnp.maximum(log_state, jnp.max(log_w, axis=1, keepdims=True))
    w = jnp.exp(log_w - m_row) * K_SCALE
    w_state = jnp.exp(log_state - m_row)

    log_k_c = blast_c - bcum_c + li_c
    log_k_r = blast_r - bcum_r + li_r
    seg_max = jnp.max(jnp.where(same, log_k_r, NEG_INF), axis=1, keepdims=True)
    m_new = jnp.maximum(blast_c + m_prev, seg_max)
    wk = jnp.exp(log_k_c - m_new) * K_SCALE
    decay = jnp.exp(blast_c + m_prev - m_new)
    kw = k * wk
    kwt_scr[...] = jnp.transpose(kw)
    dec_scr[...] = jnp.broadcast_to(decay, dec_scr.shape)

    row8 = lax.broadcasted_iota(jnp.int32, (SUBLANES, A_DV), 0)
    col = lax.broadcasted_iota(jnp.int32, (A_DQK, R), 1)

    def group(gidx, carry):
        r0 = pl.multiple_of(gidx * SUBLANES, SUBLANES)
        q8 = q_scr[pl.ds(r0, SUBLANES), :]
        hc8 = jnp.zeros((SUBLANES, A_DV), F32)
        for u in range(per8):
            sq = gidx * per8 + u
            c0 = c_in_ref[0, sq, 0]
            res = _dot(q8, c0)
            hc8 = jnp.where((row8 // seq) == u, res, hc8)
            kwt_m = jnp.where((col // seq) == sq, kwt_scr[...], 0.0)
            dsc = dec_scr[pl.ds(sq * seq, 1), 0:1]
            c_out_ref[0, sq, 0] = dsc * c0 + _dot(kwt_m.astype(BF16), v_ref[...])
        hc_scr[pl.ds(r0, SUBLANES), :] = hc8
        return carry

    lax.fori_loop(0, R // SUBLANES, group, 0, unroll=4)

    sc = _dot_nt(q, k) * w
    num = _dot(sc, v) + w_state * hc_scr[...]
    den = jnp.sum(sc, axis=1, keepdims=True) + w_state * jnp.sum(q * nrow_ref[0], axis=1, keepdims=True)
    hid_ref[...] = (num / jnp.maximum(jnp.abs(den), jnp.exp(-m_row))).astype(hid_ref.dtype)

    acc = kw
    shift = 1
    while shift < seq:
        acc = acc + jnp.where((lax.broadcasted_iota(jnp.int32, acc.shape, 0) % seq) >= shift,
                              pltpu.roll(acc, shift, 0), 0.0)
        shift *= 2
    n_out_ref[0] = decay * nrow_ref[0] + acc
    m_out_ref[0] = jnp.broadcast_to(m_new, (R, LANES))


def mlstm_sample(z_main_s, b_gates, gi_r, gf_r, gi_c, gf_c, m_col, n_rows, c_state, rb, seq):
    ts = z_main_s.shape[0]
    nseq_blk = rb // seq
    kcol = (A_HEADS * A_DQK) // A_DQK
    vcol = (2 * A_HEADS * A_DQK) // A_DV
    return pl.pallas_call(
        functools.partial(_mlstm_s_kernel, seq=seq),
        grid_spec=pltpu.PrefetchScalarGridSpec(
            num_scalar_prefetch=1,
            grid=(ts // rb, A_HEADS),
            in_specs=[pl.BlockSpec((rb, A_DQK), lambda i, h, *_: (i, h)),
                      pl.BlockSpec((rb, A_DQK), lambda i, h, *_: (i, kcol + h)),
                      pl.BlockSpec((rb, A_DV), lambda i, h, *_: (i, vcol + h)),
                      pl.BlockSpec((1, 1, rb), lambda i, h, *_: (h, 0, i)),
                      pl.BlockSpec((1, 1, rb), lambda i, h, *_: (h, 0, i)),
                      pl.BlockSpec((1, rb, 1), lambda i, h, *_: (h, i, 0)),
                      pl.BlockSpec((1, rb, 1), lambda i, h, *_: (h, i, 0)),
                      pl.BlockSpec((1, rb, 1), lambda i, h, *_: (h, i, 0)),
                      pl.BlockSpec((1, rb, A_DQK), lambda i, h, *_: (h, i, 0)),
                      pl.BlockSpec((1, nseq_blk, 1, A_DQK, A_DV), lambda i, h, *_: (0, i, h, 0, 0))],
            out_specs=[pl.BlockSpec((rb, A_DV), lambda i, h, *_: (i, h)),
                       pl.BlockSpec((1, nseq_blk, 1, A_DQK, A_DV), lambda i, h, *_: (0, i, h, 0, 0)),
                       pl.BlockSpec((1, rb, A_DQK), lambda i, h, *_: (h, i, 0)),
                       pl.BlockSpec((1, rb, LANES), lambda i, h, *_: (h, i, 0))],
            scratch_shapes=[pltpu.VMEM((rb, A_DV), F32),
                            pltpu.VMEM((rb, LANES), F32),
                            pltpu.VMEM((A_DQK, rb), F32),
                            pltpu.VMEM((rb, A_DQK), F32)]),
        out_shape=[jax.ShapeDtypeStruct((ts, A_HEADS * A_DV), BF16),
                   jax.ShapeDtypeStruct(c_state.shape, F32),
                   jax.ShapeDtypeStruct((A_HEADS, ts, A_DQK), F32),
                   jax.ShapeDtypeStruct((A_HEADS, ts, LANES), F32)],
        compiler_params=_cparams(("parallel", "parallel")),
    )(b_gates, z_main_s, z_main_s, z_main_s, gi_r, gf_r, gi_c, gf_c, m_col, n_rows, c_state)


def _attn_p_kernel(q_ref, k_ref, v_ref, o_ref, *, tq, heads):
    s = q_ref.shape[0]
    lower = lax.broadcasted_iota(jnp.int32, (tq, tq), 1) <= lax.broadcasted_iota(jnp.int32, (tq, tq), 0)
    for hh in range(heads):
        qk = slice(hh * Q_HEAD_PAD, (hh + 1) * Q_HEAD_PAD)
        vv = slice(hh * B_V, (hh + 1) * B_V)
        for qt in reversed(range(s // tq)):
            lo, hi = qt * tq, (qt + 1) * tq
            q = q_ref[lo:hi, qk]
            sd = jnp.where(lower, _dot_nt(q, k_ref[lo:hi, qk]), NEG_INF)
            m = jnp.max(sd, axis=-1, keepdims=True)
            if qt:
                sp = _dot_nt(q, k_ref[0:lo, qk])
                m = jnp.maximum(m, jnp.max(sp, axis=-1, keepdims=True))
            pd = jnp.exp(sd - m)
            l = jnp.sum(pd, axis=-1, keepdims=True)
            o = _dot(pd.astype(BF16), v_ref[lo:hi, vv])
            if qt:
                pp = jnp.exp(sp - m)
                l = l + jnp.sum(pp, axis=-1, keepdims=True)
                o = o + _dot(pp.astype(BF16), v_ref[0:lo, vv])
            o_ref[lo:hi, vv] = (o / l).astype(o_ref.dtype)


def attn_prompt(q, k, v, bp, s, tq, heads=2):
    return pl.pallas_call(
        functools.partial(_attn_p_kernel, tq=tq, heads=heads),
        grid=(bp, B_HEADS // heads),
        in_specs=[pl.BlockSpec((s, heads * Q_HEAD_PAD), lambda b, h: (b, h)),
                  pl.BlockSpec((s, heads * Q_HEAD_PAD), lambda b, h: (b, h)),
                  pl.BlockSpec((s, heads * B_V), lambda b, h: (b, h))],
        out_specs=pl.BlockSpec((s, heads * B_V), lambda b, h: (b, h)),
        out_shape=jax.ShapeDtypeStruct((bp * s, B_HEADS * B_V), BF16),
        compiler_params=_cparams(("parallel", "parallel")),
    )(q, k, v)


def _q_absorb_kernel(q_ref, w_ref, o_ref):
    o_ref[...] = _dot(q_ref[...], w_ref[...]).astype(o_ref.dtype)


def q_absorb(q_s, w_uk_t):
    ts = q_s.shape[0]
    c = w_uk_t.shape[1]
    return pl.pallas_call(
        _q_absorb_kernel,
        grid=(B_HEADS,),
        in_specs=[pl.BlockSpec((ts, B_NOPE), lambda h: (0, 2 * h)),
                  pl.BlockSpec((B_NOPE, c), lambda h: (h, 0))],
        out_specs=pl.BlockSpec((ts, c), lambda h: (0, h)),
        out_shape=jax.ShapeDtypeStruct((ts, B_HEADS * c), BF16),
        compiler_params=_cparams(("parallel",)),
    )(q_s, w_uk_t)


def _attn_s_kernel(pt_ref, *refs, g_pages, page, kblk, seq):
    (lat_hbm, kr_hbm, wukt_ref, qabs_ref, q_ref, cnew_ref, knew_ref, o_ref,
     wcat_scr, lat_scr, lat_buf, kr_buf, lat_sem, kr_sem) = refs
    b = pl.program_id(0)
    n_grp = lat_buf.shape[0]
    nq = qabs_ref.shape[1]
    nkey = wukt_ref.shape[0]
    per_blk = kblk // page
    blk_per_grp = g_pages // per_blk

    def page_copies(bb, grp, g):
        pg = pt_ref[bb, grp * g_pages + g]
        return (pltpu.make_async_copy(lat_hbm.at[0, pg], lat_buf.at[grp, g], lat_sem.at[grp]),
                pltpu.make_async_copy(kr_hbm.at[0, pg], kr_buf.at[grp, g], kr_sem.at[grp]))

    def start_pages(bb, grp):
        for g in range(g_pages):
            for cp in page_copies(bb, grp, g):
                cp.start()

    @pl.when(b == 0)
    def _():
        for grp in range(n_grp):
            start_pages(0, grp)
        wcat_scr[0:nkey, :] = wukt_ref[...]

    wcat_scr[nkey:nkey + nq, :] = qabs_ref[0]
    qrope = q_ref[0][:, LANES:LANES + B_ROPE]

    def scores(lb, krt):
        a = _dot_nt(wcat_scr[...], lb)
        kt = a[0:nkey]
        ssq = jnp.sum((kt * kt).reshape(B_HEADS, B_NOPE, kblk), axis=1)
        ssr = jnp.sum(krt * krt, axis=0, keepdims=True)
        r = lax.rsqrt((ssq + ssr) * (1.0 / B_QK) + EPS)
        return (a[nkey:nkey + nq] + _dot(qrope, krt.astype(BF16))) * jnp.concatenate([r] * seq, axis=0)

    s_blocks = []
    for grp in range(n_grp):
        for g in range(g_pages):
            for cp in page_copies(b, grp, g):
                cp.wait()
        for sb in range(blk_per_grp):
            pages = range(sb * per_blk, (sb + 1) * per_blk)
            lb = jnp.concatenate([lat_buf[grp, g].astype(BF16) for g in pages], axis=0)
            krt = jnp.concatenate([kr_buf[grp, g] for g in pages], axis=1)
            blk = grp * blk_per_grp + sb
            lat_scr[blk * kblk:(blk + 1) * kblk, :] = lb
            s_blocks.append(scores(lb, krt))

        @pl.when(b + 1 < pl.num_programs(0))
        def _():
            start_pages(b + 1, grp)

    qf = q_ref[0].astype(F32)
    t_of_row = lax.broadcasted_iota(jnp.int32, (nq, 1), 0) // B_HEADS
    cols = []
    for jn in range(seq):
        kj = knew_ref[0, jn * B_HEADS:(jn + 1) * B_HEADS, :].astype(F32)
        sj = jnp.sum(qf * jnp.concatenate([kj] * seq, axis=0), axis=-1, keepdims=True)
        cols.append(jnp.where(t_of_row >= jn, sj, NEG_INF))

    s = jnp.concatenate(s_blocks, axis=1)
    m = jnp.max(s, axis=-1, keepdims=True)
    for sj in cols:
        m = jnp.maximum(m, sj)
    p = jnp.exp(s - m)
    l = jnp.sum(p, axis=-1, keepdims=True)
    acc = _dot(p.astype(BF16), lat_scr[...])
    for jn, sj in enumerate(cols):
        pj = jnp.exp(sj - m)
        l = l + pj
        acc = acc + pj * cnew_ref[0, jn:jn + 1, :]
    o_ref[0] = (acc / l).astype(o_ref.dtype)


def attn_sample(page_table, cache_lat, cache_kr_t, w_uk_t, q_abs, q_s, c_new, k_new, g_pages, seq):
    db, n_pages = page_table.shape
    page, c = cache_lat.shape[2], cache_lat.shape[3]
    kblk = MXU_COLS
    assert kblk % page == 0 and g_pages % (kblk // page) == 0 and n_pages % g_pages == 0
    nq = q_abs.shape[1]
    nkey = w_uk_t.shape[0]
    n_grp = n_pages // g_pages
    per_b = lambda b, pt: (b, 0, 0)
    return pl.pallas_call(
        functools.partial(_attn_s_kernel, g_pages=g_pages, page=page, kblk=kblk, seq=seq),
        grid_spec=pltpu.PrefetchScalarGridSpec(
            num_scalar_prefetch=1,
            grid=(db,),
            in_specs=[
                pl.BlockSpec(memory_space=pl.ANY),
                pl.BlockSpec(memory_space=pl.ANY),
                pl.BlockSpec(w_uk_t.shape, lambda b, pt: (0, 0)),
                pl.BlockSpec((1, nq, c), per_b),
                pl.BlockSpec((1, nq, Q_HEAD_PAD), per_b),
                pl.BlockSpec((1, seq, c), per_b),
                pl.BlockSpec((1, nq, Q_HEAD_PAD), per_b)],
            out_specs=pl.BlockSpec((1, nq, c), per_b),
            scratch_shapes=[pltpu.VMEM((nkey + nq, c), BF16),
                            pltpu.VMEM((n_pages * page, c), BF16),
                            pltpu.VMEM((n_grp, g_pages, page, c), F32),
                            pltpu.VMEM((n_grp, g_pages, B_ROPE, page), F32),
                            pltpu.SemaphoreType.DMA((n_grp,)),
                            pltpu.SemaphoreType.DMA((n_grp,))]),
        out_shape=jax.ShapeDtypeStruct((db, nq, c), BF16),
        compiler_params=_cparams(("arbitrary",)),
    )(page_table, cache_lat, cache_kr_t, w_uk_t, q_abs, q_s, c_new, k_new)


def _uv_kernel(o_ref, w_ref, h_ref):
    h_ref[...] = _dot(o_ref[...], w_ref[...]).astype(h_ref.dtype)


def uv_expand(o_lat2, w_uv):
    ts = o_lat2.shape[0]
    c = w_uv.shape[0]
    return pl.pallas_call(
        _uv_kernel,
        grid=(B_HEADS,),
        in_specs=[pl.BlockSpec((ts, c), lambda h: (0, h)),
                  pl.BlockSpec((c, B_V), lambda h: (0, h))],
        out_specs=pl.BlockSpec((ts, B_V), lambda h: (0, h)),
        out_shape=jax.ShapeDtypeStruct((ts, B_HEADS * B_V), BF16),
        compiler_params=_cparams(("parallel",)),
    )(o_lat2, w_uv)


def _merge_kernel(hid_ref, oa_ref, ga_ref, gb_ref, hb_ref, g_ref, wa_ref, wb_ref, o_ref, ha_scr):
    @pl.when(pl.program_id(1) == 0)
    def _():
        for h in range(A_HEADS):
            sl = slice(h * A_DV, (h + 1) * A_DV)
            gate = jax.nn.sigmoid(oa_ref[:, sl].astype(F32))
            ha_scr[:, sl] = (_rms(hid_ref[:, sl].astype(F32), g_ref[:, sl]) * gate).astype(BF16)

    a = _dot(ha_scr[...], wa_ref[...])
    b = _dot(hb_ref[...], wb_ref[...])
    o_ref[...] = (jax.nn.sigmoid(ga_ref[...].astype(F32)) * a
                  + jax.nn.sigmoid(gb_ref[...].astype(F32)) * b).astype(o_ref.dtype)


def merge(hid, z_main, h_b, g_out, w_a, w_b, bm, bn):
    t, d = hid.shape
    nj = d // bn
    return pl.pallas_call(
        _merge_kernel,
        grid=(t // bm, nj),
        in_specs=[pl.BlockSpec((bm, d), lambda i, j: (i, 0)),
                  pl.BlockSpec((bm, d), lambda i, j: (i, 2)),
                  pl.BlockSpec((bm, bn), lambda i, j: (i, 3 * nj + j)),
                  pl.BlockSpec((bm, bn), lambda i, j: (i, 4 * nj + j)),
                  pl.BlockSpec((bm, d), lambda i, j: (i, 0)),
                  pl.BlockSpec((1, d), lambda i, j: (0, 0)),
                  pl.BlockSpec((d, bn), lambda i, j: (0, j)),
                  pl.BlockSpec((d, bn), lambda i, j: (0, j))],
        out_specs=pl.BlockSpec((bm, bn), lambda i, j: (i, j)),
        out_shape=jax.ShapeDtypeStruct((t, d), BF16),
        scratch_shapes=[pltpu.VMEM((bm, d), BF16)],
        compiler_params=_cparams(("parallel", "arbitrary")),
    )(hid, z_main, z_main, z_main, h_b, g_out, w_a, w_b)


def _resid_mm_kernel(x_ref, a_ref, w_ref, o_ref):
    o_ref[...] = x_ref[...] + _dot(a_ref[...], w_ref[...])


def resid_matmul(x, a, w, bm, bn):
    t, d = x.shape
    k = a.shape[1]
    return pl.pallas_call(
        _resid_mm_kernel,
        grid=(t // bm, d // bn),
        in_specs=[pl.BlockSpec((bm, bn), lambda i, j: (i, j)),
                  pl.BlockSpec((bm, k), lambda i, j: (i, 0)),
                  pl.BlockSpec((k, bn), lambda i, j: (0, j))],
        out_specs=pl.BlockSpec((bm, bn), lambda i, j: (i, j)),
        out_shape=jax.ShapeDtypeStruct((t, d), F32),
        compiler_params=_cparams(("parallel", "parallel")),
    )(x, a, w)


def _ffn_p_kernel(x_ref, xh_ref, g_ref, wg_ref, wv_ref, cg_ref, cv_ref, bg_ref, bv_ref, wd_ref,
                  y_ref, ug_ref, uv_ref, xn_scr, acc_scr, u_scr, *, seq_len):
    i = pl.program_id(0)
    j = pl.program_id(1)
    bm = x_ref.shape[0]
    hp = SUBLANES

    @pl.when(j == 0)
    def _():
        xn_scr[hp:, :] = _rms(x_ref[...], g_ref[...]).astype(BF16)
        keep = jnp.where((i * bm) % seq_len == 0, 0.0, 1.0)
        xn_scr[0:hp, :] = (_rms(xh_ref[...], g_ref[...]) * keep).astype(BF16)
        acc_scr[...] = jnp.zeros_like(acc_scr)

    xall = xn_scr[...]

    def conv(half, wc_ref, bc_ref):
        e2 = u_scr[half, hp:hp + bm, :]
        e1 = u_scr[half, hp - 1:hp - 1 + bm, :]
        e0 = u_scr[half, hp - 2:hp - 2 + bm, :]
        return bc_ref[...] + ((e0 * wc_ref[0:1, :] + e1 * wc_ref[1:2, :]) + e2 * wc_ref[2:3, :])

    u_scr[0] = _dot(xall, wg_ref[...])
    u_scr[1] = _dot(xall, wv_ref[...])
    ug_ref[0] = u_scr[0, bm:, :]
    uv_ref[0] = u_scr[1, bm:, :]
    cg = conv(0, cg_ref, bg_ref)
    cv = conv(1, cv_ref, bv_ref)
    act = (cg * jax.nn.sigmoid(cg) * cv).astype(BF16)
    acc_scr[...] += _dot(act, wd_ref[...])

    @pl.when(j == pl.num_programs(1) - 1)
    def _():
        y_ref[...] = x_ref[...] + acc_scr[...]


def _ffn_weight_specs(d, bf, nj):
    return [pl.BlockSpec((1, d), lambda i, j: (0, 0)),
            pl.BlockSpec((d, bf), lambda i, j: (0, j)),
            pl.BlockSpec((d, bf), lambda i, j: (0, nj + j)),
            pl.BlockSpec((CONV_W, bf), lambda i, j: (0, j)),
            pl.BlockSpec((CONV_W, bf), lambda i, j: (0, nj + j)),
            pl.BlockSpec((1, bf), lambda i, j: (0, j)),
            pl.BlockSpec((1, bf), lambda i, j: (0, nj + j)),
            pl.BlockSpec((bf, d), lambda i, j: (j, 0))]


def conv_ffn_prompt(x1, g, w_up, w_conv, b_conv, w_down, bm, bf, seq_len):
    rows, d = x1.shape
    f = w_down.shape[0]
    assert seq_len % bm == 0 and bm >= CONV_W - 1, "row blocks must not straddle sequences"
    nj = f // bf
    nb = rows // bm
    hb = bm // SUBLANES
    return pl.pallas_call(
        functools.partial(_ffn_p_kernel, seq_len=seq_len),
        grid=(nb, nj),
        in_specs=[pl.BlockSpec((bm, d), lambda i, j: (i, 0)),
                  pl.BlockSpec((SUBLANES, d), lambda i, j: (jnp.maximum(i * hb - 1, 0), 0))]
        + _ffn_weight_specs(d, bf, nj),
        out_specs=[pl.BlockSpec((bm, d), lambda i, j: (i, 0)),
                   pl.BlockSpec((1, SUBLANES, bf), lambda i, j: (i, 0, j)),
                   pl.BlockSpec((1, SUBLANES, bf), lambda i, j: (i, 0, j))],
        out_shape=[jax.ShapeDtypeStruct((rows, d), F32),
                   jax.ShapeDtypeStruct((nb, SUBLANES, f), F32),
                   jax.ShapeDtypeStruct((nb, SUBLANES, f), F32)],
        scratch_shapes=[pltpu.VMEM((bm + SUBLANES, d), BF16),
                        pltpu.VMEM((bm, d), F32),
                        pltpu.VMEM((2, bm + SUBLANES, bf), F32)],
        compiler_params=_cparams(("parallel", "arbitrary")),
    )(x1, x1, g, w_up, w_up, w_conv, w_conv, b_conv, b_conv, w_down)


def _ffn_s_kernel(x_ref, g_ref, wg_ref, wv_ref, cg_ref, cv_ref, bg_ref, bv_ref, wd_ref, prev_ref,
                  y_ref, new_ref, xn_scr, acc_scr, *, seq):
    j = pl.program_id(1)
    nb = x_ref.shape[0] // seq

    @pl.when(j == 0)
    def _():
        xn_scr[...] = _rms(x_ref[...], g_ref[...]).astype(BF16)
        acc_scr[...] = jnp.zeros_like(acc_scr)

    xn = xn_scr[...]

    def conv(u, half, wc_ref, bc_ref):
        ext = [prev_ref[r, half] for r in range(CONV_W - 1)] + [u[t * nb:(t + 1) * nb, :] for t in range(seq)]
        for r in range(CONV_W - 1):
            new_ref[r, half] = ext[seq + r]
        return jnp.concatenate(
            [bc_ref[...] + ((ext[t] * wc_ref[0:1, :] + ext[t + 1] * wc_ref[1:2, :]) + ext[t + 2] * wc_ref[2:3, :])
             for t in range(seq)], axis=0)

    cg = conv(_dot(xn, wg_ref[...]), 0, cg_ref, bg_ref)
    cv = conv(_dot(xn, wv_ref[...]), 1, cv_ref, bv_ref)
    act = (cg * jax.nn.sigmoid(cg) * cv).astype(BF16)
    acc_scr[...] += _dot(act, wd_ref[...])

    @pl.when(j == pl.num_programs(1) - 1)
    def _():
        y_ref[...] = x_ref[...] + acc_scr[...]


def conv_ffn_sample(x1_t, prev_t, g, w_up, w_conv, b_conv, w_down, bf, seq):
    rows, d = x1_t.shape
    f = w_down.shape[0]
    nj = f // bf
    nb = rows // seq
    hist = pl.BlockSpec((CONV_W - 1, 2, nb, bf), lambda i, j: (0, 0, 0, j))
    return pl.pallas_call(
        functools.partial(_ffn_s_kernel, seq=seq),
        grid=(1, nj),
        in_specs=[pl.BlockSpec((rows, d), lambda i, j: (0, 0))] + _ffn_weight_specs(d, bf, nj) + [hist],
        out_specs=[pl.BlockSpec((rows, d), lambda i, j: (0, 0)), hist],
        out_shape=[jax.ShapeDtypeStruct((rows, d), F32),
                   jax.ShapeDtypeStruct(prev_t.shape, F32)],
        scratch_shapes=[pltpu.VMEM((rows, d), BF16),
                        pltpu.VMEM((rows, d), F32)],
        compiler_params=_cparams(("parallel", "arbitrary")),
    )(x1_t, g, w_up, w_up, w_conv, w_conv, b_conv, b_conv, w_down, prev_t)


def _pick(n, prefs):
    for p in prefs:
        if n % p == 0:
            return p
    return n


def _layer(x_prompt, x_sample, cache_lat, cache_kr, c_state, n_state, m_state, conv_state, page_table,
           g_attn_norm, w_in, b_gates, g_q_a, w_uq, g_qk_nope_q, g_qk_rope_q, g_kv_a, w_uk, w_uv,
           g_qk_nope_k, g_qk_rope_k, g_mlstm_out, w_branch_a, w_branch_b, w_out, g_ffn_norm,
           w_up, w_conv, b_conv, w_down):
    bp, s, d = x_prompt.shape
    db, seq, _ = x_sample.shape
    n_pages = page_table.shape[1]
    page = cache_lat.shape[2]
    past = n_pages * page
    tp, ts = bp * s, db * seq
    t = tp + ts
    c_lat = g_kv_a.shape[0]
    c_q = g_q_a.shape[0]
    f = w_down.shape[0]
    qk_w = A_HEADS * A_DQK
    v_w = A_HEADS * A_DV
    row = lambda v: v.reshape(1, -1)

    o_i = 2 * qk_w + 2 * v_w
    o_cq = o_i + 2 * A_HEADS
    o_kr = o_cq + c_q + c_lat
    o_g = o_kr + B_ROPE
    w_in_b = w_in.astype(BF16)
    w_gab = w_in_b[:, o_g:]
    n_small = LANES - B_ROPE - 2 * A_HEADS
    w_lat = jnp.concatenate([w_in_b[:, o_cq:o_kr], w_in_b[:, o_kr:o_g], w_in_b[:, o_i:o_cq],
                             jnp.zeros((d, n_small), BF16)], axis=1)
    w_uq_pad = jnp.concatenate([w_uq, w_uq[:, :, B_NOPE:]], axis=2).reshape(c_q, B_HEADS * Q_HEAD_PAD).astype(BF16)
    w_uk2 = w_uk.reshape(c_lat, B_HEADS * B_NOPE).astype(BF16)
    w_uk_t = jnp.transpose(w_uk.reshape(c_lat, B_HEADS * B_NOPE)).astype(BF16)
    w_uv2 = w_uv.reshape(c_lat, B_HEADS * B_V).astype(BF16)
    gq = jnp.concatenate([g_qk_nope_q * g_qk_nope_k, g_qk_rope_q * g_qk_rope_k, g_qk_rope_q * g_qk_rope_k,
                          jnp.zeros((Q_HEAD_PAD - B_QK,), F32)]) * ATTN_SCALE
    gq = jnp.tile(gq, B_HEADS).reshape(1, -1)

    w_a_b, w_b_b, w_out_b = w_branch_a.astype(BF16), w_branch_b.astype(BF16), w_out.astype(BF16)
    w_up_b, w_down_b = w_up.astype(BF16), w_down.astype(BF16)
    half = B_ROPE // 2
    freqs = ROPE_THETA ** (-jnp.arange(half, dtype=F32) / half)

    def token_stage(x, pos, bm):
        rows = x.shape[0]
        ang = pos.astype(F32)[:, None] * freqs
        cos, sin = jnp.cos(ang), jnp.sin(ang)
        zh = jnp.zeros((rows, half), F32)
        zr = jnp.zeros((rows, LANES - B_ROPE), F32)
        cos128 = jnp.concatenate([cos, cos, zr], axis=1)
        sin_a = jnp.concatenate([zh, sin, zr], axis=1)
        sin_b = jnp.concatenate([-sin, zh, zr], axis=1)
        bn = _pick(d, (1024, 512, 256, 128))
        z_main = norm_matmul2(x, row(g_attn_norm), w_in_b, o_i, w_gab, bm, bn, BF16)
        z_lat = norm_matmul(x, row(g_attn_norm), w_lat, bm, w_lat.shape[1])
        bq = _pick(rows, (256, 128, 64, 32, 16, 8))
        q = q_proj(z_lat, row(g_q_a), w_uq_pad, cos128, sin_a + sin_b, gq, bq)
        c_kv, kr128, k, v = kv_proj(z_lat, row(g_kv_a), w_uk2, w_uv2, cos128, sin_a, sin_b, bq)
        g_t = jnp.transpose(z_lat[:, c_q + c_lat + B_ROPE:c_q + c_lat + B_ROPE + 2 * A_HEADS])
        gates = (g_t[:A_HEADS, None, :], g_t[A_HEADS:, None, :], g_t[:A_HEADS, :, None], g_t[A_HEADS:, :, None])
        return z_main, q, c_kv, kr128, k, v, gates

    def mix_stage(x, hid, z_main, h_b):
        bm = _pick(x.shape[0], (1024, 512, 256, 128, 64, 32, 16, 8))
        merged = merge(hid, z_main, h_b, row(g_mlstm_out), w_a_b, w_b_b, bm, _pick(d, (512, 256, 128)))
        return resid_matmul(x, merged, w_out_b, bm, _pick(d, (1024, 512, 256, 128)))

    bff = _pick(f, (512, 256, 128))

    xp = x_prompt.reshape(tp, d)
    bm_p = _pick(tp, (1024, 512, 256, 128, 64, 32, 16, 8))
    zm_p, q_p, ckv_p, kr_p, k_p, v_p, gates_p = token_stage(xp, jnp.tile(jnp.arange(s, dtype=jnp.int32), bp), bm_p)
    chunk = _pick(s, (256, 128, 64, 32, 16, 8))
    hid_p, c_p, n_p, m_p = mlstm_prompt(zm_p, b_gates, *gates_p, bp, s, chunk)
    hb_p = attn_prompt(q_p, k_p, v_p, bp, s, _pick(s, (512, 256, 128)))
    x1_p = mix_stage(xp, hid_p, zm_p, hb_p)
    bmf_p = _pick(s, (512, 256, 128, 64, 32, 16, 8))
    y_p, tg_p, tv_p = conv_ffn_prompt(x1_p, row(g_ffn_norm), w_up_b, w_conv, row(b_conv), w_down_b, bmf_p, bff, s)
    nb_seq = s // bmf_p
    tail_p = jnp.concatenate([tg_p, tv_p], axis=-1).reshape(bp, nb_seq, SUBLANES, 2 * f)
    conv_p = tail_p[:, nb_seq - 1, SUBLANES - (CONV_W - 1):, :]

    xs = x_sample.reshape(ts, d)
    bm_s = _pick(ts, (512, 256, 128, 64, 32, 16, 8))
    zm_s, q_s, ckv_s, kr_s, k_s, _, gates_s = token_stage(
        xs, jnp.tile(past + jnp.arange(seq, dtype=jnp.int32), db), bm_s)
    rb = _pick(ts, (128, 64, 32, 16, 8))
    m_col = jnp.repeat(jnp.transpose(m_state), seq, axis=1)[:, :, None]
    n_rows = jnp.repeat(jnp.transpose(n_state, (1, 0, 2)), seq, axis=1)
    hid_s, c_s, n_s_rows, m_s_rows = mlstm_sample(zm_s, b_gates, *gates_s, m_col, n_rows, c_state[None], rb, seq)
    n_s = jnp.transpose(n_s_rows[:, seq - 1::seq, :], (1, 0, 2))
    m_s = jnp.transpose(m_s_rows[:, seq - 1::seq, 0])

    q_abs = q_absorb(q_s, w_uk_t).reshape(db, seq * B_HEADS, c_lat)
    g_pages = _pick(n_pages, (32, 16, 8, 4, 2))
    o_lat = attn_sample(page_table, cache_lat, jnp.swapaxes(cache_kr, 2, 3), w_uk_t, q_abs,
                        q_s.reshape(db, seq * B_HEADS, Q_HEAD_PAD), ckv_s.reshape(db, seq, c_lat),
                        k_s.reshape(db, seq * B_HEADS, Q_HEAD_PAD), g_pages, seq)
    hb_s = uv_expand(o_lat.reshape(ts, B_HEADS * c_lat), w_uv2)
    x1_s = mix_stage(xs, hid_s, zm_s, hb_s)
    x1_t = jnp.transpose(x1_s.reshape(db, seq, d), (1, 0, 2)).reshape(ts, d)
    prev_t = jnp.transpose(conv_state.reshape(db, CONV_W - 1, 2, f), (1, 2, 0, 3))
    y_t, new_t = conv_ffn_sample(x1_t, prev_t, row(g_ffn_norm), w_up_b, w_conv, row(b_conv), w_down_b, bff, seq)
    y_s = jnp.transpose(y_t.reshape(seq, db, d), (1, 0, 2))
    conv_s = jnp.transpose(new_t, (2, 0, 1, 3)).reshape(db, CONV_W - 1, 2 * f)

    new_p = (ckv_p.reshape(bp, s, c_lat), kr_p[:, :B_ROPE].reshape(bp, s, B_ROPE),
             c_p, n_p[:, :, 0, :], m_p[:, :, 0, 0], conv_p)
    new_s = (ckv_s.reshape(db, seq, c_lat), kr_s[:, :B_ROPE].reshape(db, seq, B_ROPE),
             c_s[0], n_s, m_s, conv_s)
    return y_p.reshape(bp, s, d), y_s.reshape(db, seq, d), new_p, new_s


def kernel(x_prompt, x_sample, cache_kv_latent, cache_k_rope, state_mlstm_C, state_mlstm_n, state_mlstm_m, state_conv, page_table, g_attn_norm, w_in, b_gates, g_q_a, w_uq, g_qk_nope_q, g_qk_rope_q, g_kv_a, w_uk, w_uv, g_qk_nope_k, g_qk_rope_k, g_mlstm_out, w_branch_a, w_branch_b, w_out, g_ffn_norm, w_up, w_conv, b_conv, w_down):
    depth = w_in.shape[0]
    assert depth == 1, "single-layer trunk"
    l = 0
    y_p, y_s, new_p, new_s = _layer(
        x_prompt, x_sample, cache_kv_latent, cache_k_rope, state_mlstm_C[l], state_mlstm_n[l],
        state_mlstm_m[l], state_conv[l], page_table, g_attn_norm[l], w_in[l], b_gates[l], g_q_a[l], w_uq[l],
        g_qk_nope_q[l], g_qk_rope_q[l], g_kv_a[l], w_uk[l], w_uv[l], g_qk_nope_k[l], g_qk_rope_k[l],
        g_mlstm_out[l], w_branch_a[l], w_branch_b[l], w_out[l], g_ffn_norm[l], w_up[l], w_conv[l], b_conv[l],
        w_down[l])
    dts = (cache_kv_latent.dtype, cache_k_rope.dtype, state_mlstm_C.dtype, state_mlstm_n.dtype,
           state_mlstm_m.dtype, state_conv.dtype)
    st_p = tuple(a[None].astype(dt) for a, dt in zip(new_p, dts))
    st_s = tuple(a[None].astype(dt) for a, dt in zip(new_s, dts))
    return (y_p, y_s) + st_p + st_s
```

```python
import functools
import math

import jax
import jax.numpy as jnp
from jax import lax
from jax.experimental import pallas as pl
from jax.experimental.pallas import tpu as pltpu

F32 = jnp.float32
BF16 = jnp.bfloat16

A_HEADS = 8
A_DQK = 128
A_DV = 256
K_SCALE = A_DQK ** -0.5
GATE_CAP = 15.0
B_HEADS = 16
B_NOPE = 128
B_ROPE = 64
B_QK = B_NOPE + B_ROPE
B_V = 128
Q_HEAD_PAD = 256
ROPE_THETA = 10000.0
ATTN_SCALE = B_QK ** -0.5
CONV_W = 3
EPS = 1e-6
NEG_INF = float("-inf")

LANES = 128
SUBLANES = 8
MXU_COLS = 256
VMEM_LIMIT = 56 * 1024 * 1024


def _cparams(sem):
    return pltpu.CompilerParams(dimension_semantics=sem, vmem_limit_bytes=VMEM_LIMIT)


def _rms(x, g):
    r = lax.rsqrt(jnp.mean(x * x, axis=-1, keepdims=True) + EPS)
    return x * r * g


def _dot(a, b):
    return jnp.dot(a, b, preferred_element_type=F32)


def _dot_nt(a, b):
    return lax.dot_general(a, b, (((1,), (1,)), ((), ())), preferred_element_type=F32)


def _rope128(x, cos, sin_a, sin_b):
    return x * cos + pltpu.roll(x, 32, 1) * sin_a + pltpu.roll(x, 96, 1) * sin_b


def _norm_mm_kernel(x_ref, g_ref, w_ref, o_ref, xn_ref):
    @pl.when(pl.program_id(1) == 0)
    def _():
        xn_ref[...] = _rms(x_ref[...], g_ref[...]).astype(BF16)

    o_ref[...] = _dot(xn_ref[...], w_ref[...]).astype(o_ref.dtype)


def norm_matmul(x, g, w, bm, bn, out_dtype=F32):
    t, d = x.shape
    n = w.shape[1]
    return pl.pallas_call(
        _norm_mm_kernel,
        grid=(t // bm, n // bn),
        in_specs=[pl.BlockSpec((bm, d), lambda i, j: (i, 0)),
                  pl.BlockSpec((1, d), lambda i, j: (0, 0)),
                  pl.BlockSpec((d, bn), lambda i, j: (0, j))],
        out_specs=pl.BlockSpec((bm, bn), lambda i, j: (i, j)),
        out_shape=jax.ShapeDtypeStruct((t, n), out_dtype),
        scratch_shapes=[pltpu.VMEM((bm, d), BF16)],
        compiler_params=_cparams(("parallel", "arbitrary")),
    )(x, g, w)


def _norm_mm2_kernel(x_ref, g_ref, wa_ref, wb_ref, o_ref, xn_ref, *, na_tiles):
    j = pl.program_id(1)

    @pl.when(j == 0)
    def _():
        xn_ref[...] = _rms(x_ref[...], g_ref[...]).astype(BF16)

    w = jnp.where(j < na_tiles, wa_ref[...], wb_ref[...])
    o_ref[...] = _dot(xn_ref[...], w).astype(o_ref.dtype)


def norm_matmul2(x, g, w_a, na_cols, w_b, bm, bn, out_dtype):
    t, d = x.shape
    assert na_cols % bn == 0 and w_b.shape[1] % bn == 0
    na_tiles, nb_tiles = na_cols // bn, w_b.shape[1] // bn
    return pl.pallas_call(
        functools.partial(_norm_mm2_kernel, na_tiles=na_tiles),
        grid=(t // bm, na_tiles + nb_tiles),
        in_specs=[pl.BlockSpec((bm, d), lambda i, j: (i, 0)),
                  pl.BlockSpec((1, d), lambda i, j: (0, 0)),
                  pl.BlockSpec((d, bn), lambda i, j: (0, jnp.minimum(j, na_tiles - 1))),
                  pl.BlockSpec((d, bn), lambda i, j: (0, jnp.maximum(j - na_tiles, 0)))],
        out_specs=pl.BlockSpec((bm, bn), lambda i, j: (i, j)),
        out_shape=jax.ShapeDtypeStruct((t, (na_tiles + nb_tiles) * bn), out_dtype),
        scratch_shapes=[pltpu.VMEM((bm, d), BF16)],
        compiler_params=_cparams(("parallel", "arbitrary")),
    )(x, g, w_a, w_b)


def _q_kernel(cq_ref, g_ref, w_ref, cos_ref, sr_ref, gain_ref, o_ref):
    cqn = _rms(cq_ref[...], g_ref[...]).astype(BF16)
    q = _dot(cqn, w_ref[...])
    cos, sr = cos_ref[...], sr_ref[...]
    for h in range(B_HEADS):
        lo = h * Q_HEAD_PAD
        nope = q[:, lo:lo + LANES]
        rp = q[:, lo + LANES:lo + Q_HEAD_PAD]
        rp = rp * cos + pltpu.roll(rp, B_ROPE // 2, 1) * sr
        ssq = jnp.sum(nope * nope + rp * rp, axis=-1, keepdims=True)
        r = lax.rsqrt(ssq * (1.0 / B_QK) + EPS)
        o_ref[:, lo:lo + LANES] = (nope * r * gain_ref[:, lo:lo + LANES]).astype(o_ref.dtype)
        o_ref[:, lo + LANES:lo + Q_HEAD_PAD] = (rp * r * gain_ref[:, lo + LANES:lo + Q_HEAD_PAD]).astype(o_ref.dtype)


def q_proj(z_lat, g_q_a, w_uq_pad, cos, sin_roll, gain, bm):
    t = z_lat.shape[0]
    c = g_q_a.shape[1]
    n = w_uq_pad.shape[1]
    return pl.pallas_call(
        _q_kernel,
        grid=(t // bm,),
        in_specs=[pl.BlockSpec((bm, c), lambda i: (i, 0)),
                  pl.BlockSpec((1, c), lambda i: (0, 0)),
                  pl.BlockSpec((c, n), lambda i: (0, 0)),
                  pl.BlockSpec((bm, LANES), lambda i: (i, 0)),
                  pl.BlockSpec((bm, LANES), lambda i: (i, 0)),
                  pl.BlockSpec((1, n), lambda i: (0, 0))],
        out_specs=pl.BlockSpec((bm, n), lambda i: (i, 0)),
        out_shape=jax.ShapeDtypeStruct((t, n), BF16),
        compiler_params=_cparams(("parallel",)),
    )(z_lat, g_q_a, w_uq_pad, cos, sin_roll, gain)


def _kv_kernel(ckv_ref, sm_ref, g_ref, wuk_ref, wuv_ref, cos_ref, sa_ref, sb_ref,
               c_ref, kr_ref, k_ref, v_ref):
    c = _rms(ckv_ref[...], g_ref[...])
    c_ref[...] = c
    cb = c.astype(BF16)
    kn = _dot(cb, wuk_ref[...])
    v_ref[...] = _dot(cb, wuv_ref[...]).astype(v_ref.dtype)
    kr = _rope128(sm_ref[...], cos_ref[...], sa_ref[...], sb_ref[...])
    kr_ref[...] = kr
    ssr = jnp.sum(kr * kr, axis=-1, keepdims=True)
    for h in range(B_HEADS):
        nope = kn[:, h * B_NOPE:(h + 1) * B_NOPE]
        r = lax.rsqrt((jnp.sum(nope * nope, axis=-1, keepdims=True) + ssr) * (1.0 / B_QK) + EPS)
        lo = h * Q_HEAD_PAD
        k_ref[:, lo:lo + LANES] = (nope * r).astype(k_ref.dtype)
        k_ref[:, lo + LANES:lo + Q_HEAD_PAD] = (kr * r).astype(k_ref.dtype)


def kv_proj(z_lat, g_kv_a, w_uk, w_uv, cos, sa, sb, bm):
    t = z_lat.shape[0]
    c = g_kv_a.shape[1]
    small_blk = (2 * c) // LANES
    row = lambda i: (i, 0)
    const = lambda i: (0, 0)
    return pl.pallas_call(
        _kv_kernel,
        grid=(t // bm,),
        in_specs=[pl.BlockSpec((bm, c), lambda i: (i, 1)),
                  pl.BlockSpec((bm, LANES), lambda i: (i, small_blk)),
                  pl.BlockSpec((1, c), const),
                  pl.BlockSpec(w_uk.shape, const),
                  pl.BlockSpec(w_uv.shape, const),
                  pl.BlockSpec((bm, LANES), row),
                  pl.BlockSpec((bm, LANES), row),
                  pl.BlockSpec((bm, LANES), row)],
        out_specs=[pl.BlockSpec((bm, c), row),
                   pl.BlockSpec((bm, LANES), row),
                   pl.BlockSpec((bm, B_HEADS * Q_HEAD_PAD), row),
                   pl.BlockSpec((bm, B_HEADS * B_V), row)],
        out_shape=[jax.ShapeDtypeStruct((t, c), F32),
                   jax.ShapeDtypeStruct((t, LANES), F32),
                   jax.ShapeDtypeStruct((t, B_HEADS * Q_HEAD_PAD), BF16),
                   jax.ShapeDtypeStruct((t, B_HEADS * B_V), BF16)],
        compiler_params=_cparams(("parallel",)),
    )(z_lat, z_lat, g_kv_a, w_uk, w_uv, cos, sa, sb)


def _log_gates(gi, gf, bi, bf):
    cap = lambda x: GATE_CAP * jnp.tanh(x * (1.0 / GATE_CAP))
    li = cap(gi + bi)
    y = cap(gf + bf)
    lf = jnp.minimum(y, 0.0) - jnp.log(1.0 + jnp.exp(-jnp.abs(y)))
    return li, lf


def _mlstm_p_kernel(b_ref, q_ref, k_ref, v_ref, gir_ref, gfr_ref, gic_ref, gfc_ref,
                    hid_ref, c_out_ref, n_out_ref, m_out_ref, c_scr, n_scr, m_scr):
    ci = pl.program_id(1)
    L = q_ref.shape[0]

    @pl.when(ci == 0)
    def _():
        c_scr[...] = jnp.zeros_like(c_scr)
        n_scr[...] = jnp.zeros_like(n_scr)
        m_scr[...] = jnp.zeros_like(m_scr)

    t_idx = lax.broadcasted_iota(jnp.int32, (L, L), 0)
    s_idx = lax.broadcasted_iota(jnp.int32, (L, L), 1)
    causal = s_idx <= t_idx
    anti = t_idx <= s_idx
    for h in range(A_HEADS):
        bi = b_ref[h]
        bf = b_ref[A_HEADS + h]
        qb = q_ref[:, h * A_DQK:(h + 1) * A_DQK]
        kb = k_ref[:, h * A_DQK:(h + 1) * A_DQK]
        vb = v_ref[:, h * A_DV:(h + 1) * A_DV]
        q = qb.astype(F32)
        li_r, lf_r = _log_gates(gir_ref[h], gfr_ref[h], bi, bf)
        li_c, lf_c = _log_gates(gic_ref[h], gfc_ref[h], bi, bf)

        bcum_c = jnp.sum(jnp.where(causal, lf_r, 0.0), axis=1, keepdims=True)
        bcum_r = jnp.sum(jnp.where(anti, lf_c, 0.0), axis=0, keepdims=True)
        log_w = jnp.where(causal, bcum_c - bcum_r + li_r, NEG_INF)
        m_prev = m_scr[h][:, 0:1]
        c_prev = c_scr[h]
        n_prev = n_scr[h]
        log_state = bcum_c + m_prev
        m_row = jnp.maximum(log_state, jnp.max(log_w, axis=1, keepdims=True))
        w = jnp.exp(log_w - m_row) * K_SCALE
        w_state = jnp.exp(log_state - m_row)
        sc = _dot_nt(qb, kb) * w
        num = _dot(sc.astype(BF16), vb) + w_state * _dot(qb, c_prev.astype(BF16))
        den = jnp.sum(sc, axis=1, keepdims=True) + w_state * jnp.sum(q * n_prev, axis=1, keepdims=True)
        hid_ref[:, h * A_DV:(h + 1) * A_DV] = (
            num / jnp.maximum(jnp.abs(den), jnp.exp(-m_row))).astype(hid_ref.dtype)

        b_last = bcum_c[L - 1:L, :]
        log_k = b_last - bcum_c + li_c
        m_new = jnp.maximum(b_last + m_prev, jnp.max(log_k, axis=0, keepdims=True))
        wk = jnp.exp(log_k - m_new) * K_SCALE
        decay = jnp.exp(b_last + m_prev - m_new)
        kw = kb.astype(F32) * wk
        kwt = jnp.transpose(kw).astype(BF16)
        c_scr[h] = decay * c_prev + _dot(kwt, vb)
        n_scr[h] = decay * n_prev + jnp.sum(kw, axis=0, keepdims=True)
        m_scr[h] = jnp.broadcast_to(m_new, (1, LANES))

    @pl.when(ci == pl.num_programs(1) - 1)
    def _():
        c_out_ref[0] = c_scr[...]
        n_out_ref[0] = n_scr[...]
        m_out_ref[0] = m_scr[...]


def mlstm_prompt(z_main, b_gates, gi_r, gf_r, gi_c, gf_c, bp, s, chunk):
    nc = s // chunk
    qk_w, v_w = A_HEADS * A_DQK, A_HEADS * A_DV
    assert (2 * qk_w) % v_w == 0
    tok = lambda b, c, *_: b * nc + c
    return pl.pallas_call(
        _mlstm_p_kernel,
        grid_spec=pltpu.PrefetchScalarGridSpec(
            num_scalar_prefetch=1,
            grid=(bp, nc),
            in_specs=[pl.BlockSpec((chunk, qk_w), lambda b, c, *_: (tok(b, c), 0)),
                      pl.BlockSpec((chunk, qk_w), lambda b, c, *_: (tok(b, c), 1)),
                      pl.BlockSpec((chunk, v_w), lambda b, c, *_: (tok(b, c), (2 * qk_w) // v_w)),
                      pl.BlockSpec((A_HEADS, 1, chunk), lambda b, c, *_: (0, 0, tok(b, c))),
                      pl.BlockSpec((A_HEADS, 1, chunk), lambda b, c, *_: (0, 0, tok(b, c))),
                      pl.BlockSpec((A_HEADS, chunk, 1), lambda b, c, *_: (0, tok(b, c), 0)),
                      pl.BlockSpec((A_HEADS, chunk, 1), lambda b, c, *_: (0, tok(b, c), 0))],
            out_specs=[pl.BlockSpec((chunk, v_w), lambda b, c, *_: (tok(b, c), 0)),
                       pl.BlockSpec((1, A_HEADS, A_DQK, A_DV), lambda b, c, *_: (b, 0, 0, 0)),
                       pl.BlockSpec((1, A_HEADS, 1, A_DQK), lambda b, c, *_: (b, 0, 0, 0)),
                       pl.BlockSpec((1, A_HEADS, 1, LANES), lambda b, c, *_: (b, 0, 0, 0))],
            scratch_shapes=[pltpu.VMEM((A_HEADS, A_DQK, A_DV), F32),
                            pltpu.VMEM((A_HEADS, 1, A_DQK), F32),
                            pltpu.VMEM((A_HEADS, 1, LANES), F32)]),
        out_shape=[jax.ShapeDtypeStruct((bp * s, v_w), BF16),
                   jax.ShapeDtypeStruct((bp, A_HEADS, A_DQK, A_DV), F32),
                   jax.ShapeDtypeStruct((bp, A_HEADS, 1, A_DQK), F32),
                   jax.ShapeDtypeStruct((bp, A_HEADS, 1, LANES), F32)],
        compiler_params=_cparams(("parallel", "arbitrary")),
    )(b_gates, z_main, z_main, z_main, gi_r, gf_r, gi_c, gf_c)


def _mlstm_s_kernel(b_ref, q_ref, k_ref, v_ref, gir_ref, gfr_ref, gic_ref, gfc_ref,
                    mc_ref, nrow_ref, c_in_ref,
                    hid_ref, c_out_ref, n_out_ref, m_out_ref,
                    hc_scr, dec_scr, kwt_scr, q_scr, *, seq):
    h = pl.program_id(1)
    R = q_ref.shape[0]
    nseq = R // seq
    per8 = SUBLANES // seq
    bi = b_ref[h]
    bf = b_ref[A_HEADS + h]
    q = q_ref[...].astype(F32)
    k = k_ref[...].astype(F32)
    v = v_ref[...].astype(F32)
    q_scr[...] = q
    li_r, lf_r = _log_gates(gir_ref[0], gfr_ref[0], bi, bf)
    li_c, lf_c = _log_gates(gic_ref[0], gfc_ref[0], bi, bf)
    m_prev = mc_ref[0]

    t_idx = lax.broadcasted_iota(jnp.int32, (R, R), 0)
    s_idx = lax.broadcasted_iota(jnp.int32, (R, R), 1)
    same = (t_idx // seq) == (s_idx // seq)
    causal = same & (s_idx <= t_idx)
    bcum_c = jnp.sum(jnp.where(causal, lf_r, 0.0), axis=1, keepdims=True)
    bcum_r = jnp.sum(jnp.where(same & (t_idx <= s_idx), lf_c, 0.0), axis=0, keepdims=True)
    blast_c = jnp.sum(jnp.where(same, lf_r, 0.0), axis=1, keepdims=True)
    blast_r = jnp.sum(jnp.where(same, lf_c, 0.0), axis=0, keepdims=True)
    log_w = jnp.where(causal, bcum_c - bcum_r + li_r, NEG_INF)
    log_state = bcum_c + m_prev
    m_row = jnp.maximum(log_state, jnp.max(log_w, axis=1, keepdims=True))
    w = jnp.exp(log_w - m_row) * K_SCALE
    w_state = jnp.exp(log_state - m_row)

    log_k_c = blast_c - bcum_c + li_c
    log_k_r = blast_r - bcum_r + li_r
    seg_max = jnp.max(jnp.where(same, log_k_r, NEG_INF), axis=1, keepdims=True)
    m_new = jnp.maximum(blast_c + m_prev, seg_max)
    wk = jnp.exp(log_k_c - m_new) * K_SCALE
    decay = jnp.exp(blast_c + m_prev - m_new)
    kw = k * wk
    kwt_scr[...] = jnp.transpose(kw)
    dec_scr[...] = jnp.broadcast_to(decay, dec_scr.shape)

    row8 = lax.broadcasted_iota(jnp.int32, (SUBLANES, A_DV), 0)
    col = lax.broadcasted_iota(jnp.int32, (A_DQK, R), 1)

    def group(gidx, carry):
        r0 = pl.multiple_of(gidx * SUBLANES, SUBLANES)
        q8 = q_scr[pl.ds(r0, SUBLANES), :]
        hc8 = jnp.zeros((SUBLANES, A_DV), F32)
        for u in range(per8):
            sq = gidx * per8 + u
            c0 = c_in_ref[0, sq, 0]
            res = _dot(q8, c0)
            hc8 = jnp.where((row8 // seq) == u, res, hc8)
            kwt_m = jnp.where((col // seq) == sq, kwt_scr[...], 0.0)
            dsc = dec_scr[pl.ds(sq * seq, 1), 0:1]
            c_out_ref[0, sq, 0] = dsc * c0 + _dot(kwt_m.astype(BF16), v_ref[...])
        hc_scr[pl.ds(r0, SUBLANES), :] = hc8
        return carry

    lax.fori_loop(0, R // SUBLANES, group, 0, unroll=4)

    sc = _dot_nt(q, k) * w
    num = _dot(sc, v) + w_state * hc_scr[...]
    den = jnp.sum(sc, axis=1, keepdims=True) + w_state * jnp.sum(q * nrow_ref[0], axis=1, keepdims=True)
    hid_ref[...] = (num / jnp.maximum(jnp.abs(den), jnp.exp(-m_row))).astype(hid_ref.dtype)

    acc = kw
    shift = 1
    while shift < seq:
        acc = acc + jnp.where((lax.broadcasted_iota(jnp.int32, acc.shape, 0) % seq) >= shift,
                              pltpu.roll(acc, shift, 0), 0.0)
        shift *= 2
    n_out_ref[0] = decay * nrow_ref[0] + acc
    m_out_ref[0] = jnp.broadcast_to(m_new, (R, LANES))


def mlstm_sample(z_main_s, b_gates, gi_r, gf_r, gi_c, gf_c, m_col, n_rows, c_state, rb, seq):
    ts = z_main_s.shape[0]
    nseq_blk = rb // seq
    kcol = (A_HEADS * A_DQK) // A_DQK
    vcol = (2 * A_HEADS * A_DQK) // A_DV
    return pl.pallas_call(
        functools.partial(_mlstm_s_kernel, seq=seq),
        grid_spec=pltpu.PrefetchScalarGridSpec(
            num_scalar_prefetch=1,
            grid=(ts // rb, A_HEADS),
            in_specs=[pl.BlockSpec((rb, A_DQK), lambda i, h, *_: (i, h)),
                      pl.BlockSpec((rb, A_DQK), lambda i, h, *_: (i, kcol + h)),
                      pl.BlockSpec((rb, A_DV), lambda i, h, *_: (i, vcol + h)),
                      pl.BlockSpec((1, 1, rb), lambda i, h, *_: (h, 0, i)),
                      pl.BlockSpec((1, 1, rb), lambda i, h, *_: (h, 0, i)),
                      pl.BlockSpec((1, rb, 1), lambda i, h, *_: (h, i, 0)),
                      pl.BlockSpec((1, rb, 1), lambda i, h, *_: (h, i, 0)),
                      pl.BlockSpec((1, rb, 1), lambda i, h, *_: (h, i, 0)),
                      pl.BlockSpec((1, rb, A_DQK), lambda i, h, *_: (h, i, 0)),
                      pl.BlockSpec((1, nseq_blk, 1, A_DQK, A_DV), lambda i, h, *_: (0, i, h, 0, 0))],
            out_specs=[pl.BlockSpec((rb, A_DV), lambda i, h, *_: (i, h)),
                       pl.BlockSpec((1, nseq_blk, 1, A_DQK, A_DV), lambda i, h, *_: (0, i, h, 0, 0)),
                       pl.BlockSpec((1, rb, A_DQK), lambda i, h, *_: (h, i, 0)),
                       pl.BlockSpec((1, rb, LANES), lambda i, h, *_: (h, i, 0))],
            scratch_shapes=[pltpu.VMEM((rb, A_DV), F32),
                            pltpu.VMEM((rb, LANES), F32),
                            pltpu.VMEM((A_DQK, rb), F32),
                            pltpu.VMEM((rb, A_DQK), F32)]),
        out_shape=[jax.ShapeDtypeStruct((ts, A_HEADS * A_DV), BF16),
                   jax.ShapeDtypeStruct(c_state.shape, F32),
                   jax.ShapeDtypeStruct((A_HEADS, ts, A_DQK), F32),
                   jax.ShapeDtypeStruct((A_HEADS, ts, LANES), F32)],
        compiler_params=_cparams(("parallel", "parallel")),
    )(b_gates, z_main_s, z_main_s, z_main_s, gi_r, gf_r, gi_c, gf_c, m_col, n_rows, c_state)


def _attn_p_kernel(q_ref, k_ref, v_ref, o_ref, *, tq, heads):
    s = q_ref.shape[0]
    lower = lax.broadcasted_iota(jnp.int32, (tq, tq), 1) <= lax.broadcasted_iota(jnp.int32, (tq, tq), 0)
    for hh in range(heads):
        qk = slice(hh * Q_HEAD_PAD, (hh + 1) * Q_HEAD_PAD)
        vv = slice(hh * B_V, (hh + 1) * B_V)
        for qt in reversed(range(s // tq)):
            lo, hi = qt * tq, (qt + 1) * tq
            q = q_ref[lo:hi, qk]
            sd = jnp.where(lower, _dot_nt(q, k_ref[lo:hi, qk]), NEG_INF)
            m = jnp.max(sd, axis=-1, keepdims=True)
            if qt:
                sp = _dot_nt(q, k_ref[0:lo, qk])
                m = jnp.maximum(m, jnp.max(sp, axis=-1, keepdims=True))
            pd = jnp.exp(sd - m)
            l = jnp.sum(pd, axis=-1, keepdims=True)
            o = _dot(pd.astype(BF16), v_ref[lo:hi, vv])
            if qt:
                pp = jnp.exp(sp - m)
                l = l + jnp.sum(pp, axis=-1, keepdims=True)
                o = o + _dot(pp.astype(BF16), v_ref[0:lo, vv])
            o_ref[lo:hi, vv] = (o / l).astype(o_ref.dtype)


def attn_prompt(q, k, v, bp, s, tq, heads=2):
    return pl.pallas_call(
        functools.partial(_attn_p_kernel, tq=tq, heads=heads),
        grid=(bp, B_HEADS // heads),
        in_specs=[pl.BlockSpec((s, heads * Q_HEAD_PAD), lambda b, h: (b, h)),
                  pl.BlockSpec((s, heads * Q_HEAD_PAD), lambda b, h: (b, h)),
                  pl.BlockSpec((s, heads * B_V), lambda b, h: (b, h))],
        out_specs=pl.BlockSpec((s, heads * B_V), lambda b, h: (b, h)),
        out_shape=jax.ShapeDtypeStruct((bp * s, B_HEADS * B_V), BF16),
        compiler_params=_cparams(("parallel", "parallel")),
    )(q, k, v)


def _q_absorb_kernel(q_ref, w_ref, o_ref):
    o_ref[...] = _dot(q_ref[...], w_ref[...]).astype(o_ref.dtype)


def q_absorb(q_s, w_uk_t):
    ts = q_s.shape[0]
    c = w_uk_t.shape[1]
    return pl.pallas_call(
        _q_absorb_kernel,
        grid=(B_HEADS,),
        in_specs=[pl.BlockSpec((ts, B_NOPE), lambda h: (0, 2 * h)),
                  pl.BlockSpec((B_NOPE, c), lambda h: (h, 0))],
        out_specs=pl.BlockSpec((ts, c), lambda h: (0, h)),
        out_shape=jax.ShapeDtypeStruct((ts, B_HEADS * c), BF16),
        compiler_params=_cparams(("parallel",)),
    )(q_s, w_uk_t)


def _attn_s_kernel(pt_ref, *refs, g_pages, page, kblk, seq):
    (lat_hbm, kr_hbm, wukt_ref, qabs_ref, q_ref, cnew_ref, knew_ref, o_ref,
     wcat_scr, lat_scr, lat_buf, kr_buf, lat_sem, kr_sem) = refs
    b = pl.program_id(0)
    n_grp = lat_buf.shape[0]
    nq = qabs_ref.shape[1]
    nkey = wukt_ref.shape[0]
    per_blk = kblk // page
    blk_per_grp = g_pages // per_blk

    def page_copies(bb, grp, g):
        pg = pt_ref[bb, grp * g_pages + g]
        return (pltpu.make_async_copy(lat_hbm.at[0, pg], lat_buf.at[grp, g], lat_sem.at[grp]),
                pltpu.make_async_copy(kr_hbm.at[0, pg], kr_buf.at[grp, g], kr_sem.at[grp]))

    def start_pages(bb, grp):
        for g in range(g_pages):
            for cp in page_copies(bb, grp, g):
                cp.start()

    @pl.when(b == 0)
    def _():
        for grp in range(n_grp):
            start_pages(0, grp)
        wcat_scr[0:nkey, :] = wukt_ref[...]

    wcat_scr[nkey:nkey + nq, :] = qabs_ref[0]
    qrope = q_ref[0][:, LANES:LANES + B_ROPE]

    def scores(lb, krt):
        a = _dot_nt(wcat_scr[...], lb)
        kt = a[0:nkey]
        ssq = jnp.sum((kt * kt).reshape(B_HEADS, B_NOPE, kblk), axis=1)
        ssr = jnp.sum(krt * krt, axis=0, keepdims=True)
        r = lax.rsqrt((ssq + ssr) * (1.0 / B_QK) + EPS)
        return (a[nkey:nkey + nq] + _dot(qrope, krt.astype(BF16))) * jnp.concatenate([r] * seq, axis=0)

    s_blocks = []
    for grp in range(n_grp):
        for g in range(g_pages):
            for cp in page_copies(b, grp, g):
                cp.wait()
        for sb in range(blk_per_grp):
            pages = range(sb * per_blk, (sb + 1) * per_blk)
            lb = jnp.concatenate([lat_buf[grp, g].astype(BF16) for g in pages], axis=0)
            krt = jnp.concatenate([kr_buf[grp, g] for g in pages], axis=1)
            blk = grp * blk_per_grp + sb
            lat_scr[blk * kblk:(blk + 1) * kblk, :] = lb
            s_blocks.append(scores(lb, krt))

        @pl.when(b + 1 < pl.num_programs(0))
        def _():
            start_pages(b + 1, grp)

    qf = q_ref[0].astype(F32)
    t_of_row = lax.broadcasted_iota(jnp.int32, (nq, 1), 0) // B_HEADS
    cols = []
    for jn in range(seq):
        kj = knew_ref[0, jn * B_HEADS:(jn + 1) * B_HEADS, :].astype(F32)
        sj = jnp.sum(qf * jnp.concatenate([kj] * seq, axis=0), axis=-1, keepdims=True)
        cols.append(jnp.where(t_of_row >= jn, sj, NEG_INF))

    s = jnp.concatenate(s_blocks, axis=1)
    m = jnp.max(s, axis=-1, keepdims=True)
    for sj in cols:
        m = jnp.maximum(m, sj)
    p = jnp.exp(s - m)
    l = jnp.sum(p, axis=-1, keepdims=True)
    acc = _dot(p.astype(BF16), lat_scr[...])
    for jn, sj in enumerate(cols):
        pj = jnp.exp(sj - m)
        l = l + pj
        acc = acc + pj * cnew_ref[0, jn:jn + 1, :]
    o_ref[0] = (acc / l).astype(o_ref.dtype)


def attn_sample(page_table, cache_lat, cache_kr_t, w_uk_t, q_abs, q_s, c_new, k_new, g_pages, seq):
    db, n_pages = page_table.shape
    page, c = cache_lat.shape[2], cache_lat.shape[3]
    kblk = MXU_COLS
    assert kblk % page == 0 and g_pages % (kblk // page) == 0 and n_pages % g_pages == 0
    nq = q_abs.shape[1]
    nkey = w_uk_t.shape[0]
    n_grp = n_pages // g_pages
    per_b = lambda b, pt: (b, 0, 0)
    return pl.pallas_call(
        functools.partial(_attn_s_kernel, g_pages=g_pages, page=page, kblk=kblk, seq=seq),
        grid_spec=pltpu.PrefetchScalarGridSpec(
            num_scalar_prefetch=1,
            grid=(db,),
            in_specs=[
                pl.BlockSpec(memory_space=pl.ANY),
                pl.BlockSpec(memory_space=pl.ANY),
                pl.BlockSpec(w_uk_t.shape, lambda b, pt: (0, 0)),
                pl.BlockSpec((1, nq, c), per_b),
                pl.BlockSpec((1, nq, Q_HEAD_PAD), per_b),
                pl.BlockSpec((1, seq, c), per_b),
                pl.BlockSpec((1, nq, Q_HEAD_PAD), per_b)],
            out_specs=pl.BlockSpec((1, nq, c), per_b),
            scratch_shapes=[pltpu.VMEM((nkey + nq, c), BF16),
                            pltpu.VMEM((n_pages * page, c), BF16),
                            pltpu.VMEM((n_grp, g_pages, page, c), F32),
                            pltpu.VMEM((n_grp, g_pages, B_ROPE, page), F32),
                            pltpu.SemaphoreType.DMA((n_grp,)),
                            pltpu.SemaphoreType.DMA((n_grp,))]),
        out_shape=jax.ShapeDtypeStruct((db, nq, c), BF16),
        compiler_params=_cparams(("arbitrary",)),
    )(page_table, cache_lat, cache_kr_t, w_uk_t, q_abs, q_s, c_new, k_new)


def _uv_kernel(o_ref, w_ref, h_ref):
    h_ref[...] = _dot(o_ref[...], w_ref[...]).astype(h_ref.dtype)


def uv_expand(o_lat2, w_uv):
    ts = o_lat2.shape[0]
    c = w_uv.shape[0]
    return pl.pallas_call(
        _uv_kernel,
        grid=(B_HEADS,),
        in_specs=[pl.BlockSpec((ts, c), lambda h: (0, h)),
                  pl.BlockSpec((c, B_V), lambda h: (0, h))],
        out_specs=pl.BlockSpec((ts, B_V), lambda h: (0, h)),
        out_shape=jax.ShapeDtypeStruct((ts, B_HEADS * B_V), BF16),
        compiler_params=_cparams(("parallel",)),
    )(o_lat2, w_uv)


def _merge_kernel(hid_ref, oa_ref, ga_ref, gb_ref, hb_ref, g_ref, wa_ref, wb_ref, o_ref, ha_scr):
    @pl.when(pl.program_id(1) == 0)
    def _():
        for h in range(A_HEADS):
            sl = slice(h * A_DV, (h + 1) * A_DV)
            gate = jax.nn.sigmoid(oa_ref[:, sl].astype(F32))
            ha_scr[:, sl] = (_rms(hid_ref[:, sl].astype(F32), g_ref[:, sl]) * gate).astype(BF16)

    a = _dot(ha_scr[...], wa_ref[...])
    b = _dot(hb_ref[...], wb_ref[...])
    o_ref[...] = (jax.nn.sigmoid(ga_ref[...].astype(F32)) * a
                  + jax.nn.sigmoid(gb_ref[...].astype(F32)) * b).astype(o_ref.dtype)


def merge(hid, z_main, h_b, g_out, w_a, w_b, bm, bn):
    t, d = hid.shape
    nj = d // bn
    return pl.pallas_call(
        _merge_kernel,
        grid=(t // bm, nj),
        in_specs=[pl.BlockSpec((bm, d), lambda i, j: (i, 0)),
                  pl.BlockSpec((bm, d), lambda i, j: (i, 2)),
                  pl.BlockSpec((bm, bn), lambda i, j: (i, 3 * nj + j)),
                  pl.BlockSpec((bm, bn), lambda i, j: (i, 4 * nj + j)),
                  pl.BlockSpec((bm, d), lambda i, j: (i, 0)),
                  pl.BlockSpec((1, d), lambda i, j: (0, 0)),
                  pl.BlockSpec((d, bn), lambda i, j: (0, j)),
                  pl.BlockSpec((d, bn), lambda i, j: (0, j))],
        out_specs=pl.BlockSpec((bm, bn), lambda i, j: (i, j)),
        out_shape=jax.ShapeDtypeStruct((t, d), BF16),
        scratch_shapes=[pltpu.VMEM((bm, d), BF16)],
        compiler_params=_cparams(("parallel", "arbitrary")),
    )(hid, z_main, z_main, z_main, h_b, g_out, w_a, w_b)


def _resid_mm_kernel(x_ref, a_ref, w_ref, o_ref):
    o_ref[...] = x_ref[...] + _dot(a_ref[...], w_ref[...])


def resid_matmul(x, a, w, bm, bn):
    t, d = x.shape
    k = a.shape[1]
    return pl.pallas_call(
        _resid_mm_kernel,
        grid=(t // bm, d // bn),
        in_specs=[pl.BlockSpec((bm, bn), lambda i, j: (i, j)),
                  pl.BlockSpec((bm, k), lambda i, j: (i, 0)),
                  pl.BlockSpec((k, bn), lambda i, j: (0, j))],
        out_specs=pl.BlockSpec((bm, bn), lambda i, j: (i, j)),
        out_shape=jax.ShapeDtypeStruct((t, d), F32),
        compiler_params=_cparams(("parallel", "parallel")),
    )(x, a, w)


def _ffn_p_kernel(x_ref, xh_ref, g_ref, wg_ref, wv_ref, cg_ref, cv_ref, bg_ref, bv_ref, wd_ref,
                  y_ref, ug_ref, uv_ref, xn_scr, acc_scr, u_scr, *, seq_len):
    i = pl.program_id(0)
    j = pl.program_id(1)
    bm = x_ref.shape[0]
    hp = SUBLANES

    @pl.when(j == 0)
    def _():
        xn_scr[hp:, :] = _rms(x_ref[...], g_ref[...]).astype(BF16)
        keep = jnp.where((i * bm) % seq_len == 0, 0.0, 1.0)
        xn_scr[0:hp, :] = (_rms(xh_ref[...], g_ref[...]) * keep).astype(BF16)
        acc_scr[...] = jnp.zeros_like(acc_scr)

    xall = xn_scr[...]

    def conv(half, wc_ref, bc_ref):
        e2 = u_scr[half, hp:hp + bm, :]
        e1 = u_scr[half, hp - 1:hp - 1 + bm, :]
        e0 = u_scr[half, hp - 2:hp - 2 + bm, :]
        return bc_ref[...] + ((e0 * wc_ref[0:1, :] + e1 * wc_ref[1:2, :]) + e2 * wc_ref[2:3, :])

    u_scr[0] = _dot(xall, wg_ref[...])
    u_scr[1] = _dot(xall, wv_ref[...])
    ug_ref[0] = u_scr[0, bm:, :]
    uv_ref[0] = u_scr[1, bm:, :]
    cg = conv(0, cg_ref, bg_ref)
    cv = conv(1, cv_ref, bv_ref)
    act = (cg * jax.nn.sigmoid(cg) * cv).astype(BF16)
    acc_scr[...] += _dot(act, wd_ref[...])

    @pl.when(j == pl.num_programs(1) - 1)
    def _():
        y_ref[...] = x_ref[...] + acc_scr[...]


def _ffn_weight_specs(d, bf, nj):
    return [pl.BlockSpec((1, d), lambda i, j: (0, 0)),
            pl.BlockSpec((d, bf), lambda i, j: (0, j)),
            pl.BlockSpec((d, bf), lambda i, j: (0, nj + j)),
            pl.BlockSpec((CONV_W, bf), lambda i, j: (0, j)),
            pl.BlockSpec((CONV_W, bf), lambda i, j: (0, nj + j)),
            pl.BlockSpec((1, bf), lambda i, j: (0, j)),
            pl.BlockSpec((1, bf), lambda i, j: (0, nj + j)),
            pl.BlockSpec((bf, d), lambda i, j: (j, 0))]


def conv_ffn_prompt(x1, g, w_up, w_conv, b_conv, w_down, bm, bf, seq_len):
    rows, d = x1.shape
    f = w_down.shape[0]
    assert seq_len % bm == 0 and bm >= CONV_W - 1, "row blocks must not straddle sequences"
    nj = f // bf
    nb = rows // bm
    hb = bm // SUBLANES
    return pl.pallas_call(
        functools.partial(_ffn_p_kernel, seq_len=seq_len),
        grid=(nb, nj),
        in_specs=[pl.BlockSpec((bm, d), lambda i, j: (i, 0)),
                  pl.BlockSpec((SUBLANES, d), lambda i, j: (jnp.maximum(i * hb - 1, 0), 0))]
        + _ffn_weight_specs(d, bf, nj),
        out_specs=[pl.BlockSpec((bm, d), lambda i, j: (i, 0)),
                   pl.BlockSpec((1, SUBLANES, bf), lambda i, j: (i, 0, j)),
                   pl.BlockSpec((1, SUBLANES, bf), lambda i, j: (i, 0, j))],
        out_shape=[jax.ShapeDtypeStruct((rows, d), F32),
                   jax.ShapeDtypeStruct((nb, SUBLANES, f), F32),
                   jax.ShapeDtypeStruct((nb, SUBLANES, f), F32)],
        scratch_shapes=[pltpu.VMEM((bm + SUBLANES, d), BF16),
                        pltpu.VMEM((bm, d), F32),
                        pltpu.VMEM((2, bm + SUBLANES, bf), F32)],
        compiler_params=_cparams(("parallel", "arbitrary")),
    )(x1, x1, g, w_up, w_up, w_conv, w_conv, b_conv, b_conv, w_down)


def _ffn_s_kernel(x_ref, g_ref, wg_ref, wv_ref, cg_ref, cv_ref, bg_ref, bv_ref, wd_ref, prev_ref,
                  y_ref, new_ref, xn_scr, acc_scr, *, seq):
    j = pl.program_id(1)
    nb = x_ref.shape[0] // seq

    @pl.when(j == 0)
    def _():
        xn_scr[...] = _rms(x_ref[...], g_ref[...]).astype(BF16)
        acc_scr[...] = jnp.zeros_like(acc_scr)

    xn = xn_scr[...]

    def conv(u, half, wc_ref, bc_ref):
        ext = [prev_ref[r, half] for r in range(CONV_W - 1)] + [u[t * nb:(t + 1) * nb, :] for t in range(seq)]
        for r in range(CONV_W - 1):
            new_ref[r, half] = ext[seq + r]
        return jnp.concatenate(
            [bc_ref[...] + ((ext[t] * wc_ref[0:1, :] + ext[t + 1] * wc_ref[1:2, :]) + ext[t + 2] * wc_ref[2:3, :])
             for t in range(seq)], axis=0)

    cg = conv(_dot(xn, wg_ref[...]), 0, cg_ref, bg_ref)
    cv = conv(_dot(xn, wv_ref[...]), 1, cv_ref, bv_ref)
    act = (cg * jax.nn.sigmoid(cg) * cv).astype(BF16)
    acc_scr[...] += _dot(act, wd_ref[...])

    @pl.when(j == pl.num_programs(1) - 1)
    def _():
        y_ref[...] = x_ref[...] + acc_scr[...]


def conv_ffn_sample(x1_t, prev_t, g, w_up, w_conv, b_conv, w_down, bf, seq):
    rows, d = x1_t.shape
    f = w_down.shape[0]
    nj = f // bf
    nb = rows // seq
    hist = pl.BlockSpec((CONV_W - 1, 2, nb, bf), lambda i, j: (0, 0, 0, j))
    return pl.pallas_call(
        functools.partial(_ffn_s_kernel, seq=seq),
        grid=(1, nj),
        in_specs=[pl.BlockSpec((rows, d), lambda i, j: (0, 0))] + _ffn_weight_specs(d, bf, nj) + [hist],
        out_specs=[pl.BlockSpec((rows, d), lambda i, j: (0, 0)), hist],
        out_shape=[jax.ShapeDtypeStruct((rows, d), F32),
                   jax.ShapeDtypeStruct(prev_t.shape, F32)],
        scratch_shapes=[pltpu.VMEM((rows, d), BF16),
                        pltpu.VMEM((rows, d), F32)],
        compiler_params=_cparams(("parallel", "arbitrary")),
    )(x1_t, g, w_up, w_up, w_conv, w_conv, b_conv, b_conv, w_down, prev_t)


def _pick(n, prefs):
    for p in prefs:
        if n % p == 0:
            return p
    return n


def _layer(x_prompt, x_sample, cache_lat, cache_kr, c_state, n_state, m_state, conv_state, page_table,
           g_attn_norm, w_in, b_gates, g_q_a, w_uq, g_qk_nope_q, g_qk_rope_q, g_kv_a, w_uk, w_uv,
           g_qk_nope_k, g_qk_rope_k, g_mlstm_out, w_branch_a, w_branch_b, w_out, g_ffn_norm,
           w_up, w_conv, b_conv, w_down):
    bp, s, d = x_prompt.shape
    db, seq, _ = x_sample.shape
    n_pages = page_table.shape[1]
    page = cache_lat.shape[2]
    past = n_pages * page
    tp, ts = bp * s, db * seq
    t = tp + ts
    c_lat = g_kv_a.shape[0]
    c_q = g_q_a.shape[0]
    f = w_down.shape[0]
    qk_w = A_HEADS * A_DQK
    v_w = A_HEADS * A_DV
    row = lambda v: v.reshape(1, -1)

    o_i = 2 * qk_w + 2 * v_w
    o_cq = o_i + 2 * A_HEADS
    o_kr = o_cq + c_q + c_lat
    o_g = o_kr + B_ROPE
    w_in_b = w_in.astype(BF16)
    w_gab = w_in_b[:, o_g:]
    n_small = LANES - B_ROPE - 2 * A_HEADS
    w_lat = jnp.concatenate([w_in_b[:, o_cq:o_kr], w_in_b[:, o_kr:o_g], w_in_b[:, o_i:o_cq],
                             jnp.zeros((d, n_small), BF16)], axis=1)
    w_uq_pad = jnp.concatenate([w_uq, w_uq[:, :, B_NOPE:]], axis=2).reshape(c_q, B_HEADS * Q_HEAD_PAD).astype(BF16)
    w_uk2 = w_uk.reshape(c_lat, B_HEADS * B_NOPE).astype(BF16)
    w_uk_t = jnp.transpose(w_uk.reshape(c_lat, B_HEADS * B_NOPE)).astype(BF16)
    w_uv2 = w_uv.reshape(c_lat, B_HEADS * B_V).astype(BF16)
    gq = jnp.concatenate([g_qk_nope_q * g_qk_nope_k, g_qk_rope_q * g_qk_rope_k, g_qk_rope_q * g_qk_rope_k,
                          jnp.zeros((Q_HEAD_PAD - B_QK,), F32)]) * ATTN_SCALE
    gq = jnp.tile(gq, B_HEADS).reshape(1, -1)

    w_a_b, w_b_b, w_out_b = w_branch_a.astype(BF16), w_branch_b.astype(BF16), w_out.astype(BF16)
    w_up_b, w_down_b = w_up.astype(BF16), w_down.astype(BF16)
    half = B_ROPE // 2
    freqs = ROPE_THETA ** (-jnp.arange(half, dtype=F32) / half)

    def token_stage(x, pos, bm):
        rows = x.shape[0]
        ang = pos.astype(F32)[:, None] * freqs
        cos, sin = jnp.cos(ang), jnp.sin(ang)
        zh = jnp.zeros((rows, half), F32)
        zr = jnp.zeros((rows, LANES - B_ROPE), F32)
        cos128 = jnp.concatenate([cos, cos, zr], axis=1)
        sin_a = jnp.concatenate([zh, sin, zr], axis=1)
        sin_b = jnp.concatenate([-sin, zh, zr], axis=1)
        bn = _pick(d, (1024, 512, 256, 128))
        z_main = norm_matmul2(x, row(g_attn_norm), w_in_b, o_i, w_gab, bm, bn, BF16)
        z_lat = norm_matmul(x, row(g_attn_norm), w_lat, bm, w_lat.shape[1])
        bq = _pick(rows, (256, 128, 64, 32, 16, 8))
        q = q_proj(z_lat, row(g_q_a), w_uq_pad, cos128, sin_a + sin_b, gq, bq)
        c_kv, kr128, k, v = kv_proj(z_lat, row(g_kv_a), w_uk2, w_uv2, cos128, sin_a, sin_b, bq)
        g_t = jnp.transpose(z_lat[:, c_q + c_lat + B_ROPE:c_q + c_lat + B_ROPE + 2 * A_HEADS])
        gates = (g_t[:A_HEADS, None, :], g_t[A_HEADS:, None, :], g_t[:A_HEADS, :, None], g_t[A_HEADS:, :, None])
        return z_main, q, c_kv, kr128, k, v, gates

    def mix_stage(x, hid, z_main, h_b):
        bm = _pick(x.shape[0], (1024, 512, 256, 128, 64, 32, 16, 8))
        merged = merge(hid, z_main, h_b, row(g_mlstm_out), w_a_b, w_b_b, bm, _pick(d, (512, 256, 128)))
        return resid_matmul(x, merged, w_out_b, bm, _pick(d, (1024, 512, 256, 128)))

    bff = _pick(f, (512, 256, 128))

    xp = x_prompt.reshape(tp, d)
    bm_p = _pick(tp, (1024, 512, 256, 128, 64, 32, 16, 8))
    zm_p, q_p, ckv_p, kr_p, k_p, v_p, gates_p = token_stage(xp, jnp.tile(jnp.arange(s, dtype=jnp.int32), bp), bm_p)
    chunk = _pick(s, (256, 128, 64, 32, 16, 8))
    hid_p, c_p, n_p, m_p = mlstm_prompt(zm_p, b_gates, *gates_p, bp, s, chunk)
    hb_p = attn_prompt(q_p, k_p, v_p, bp, s, _pick(s, (512, 256, 128)))
    x1_p = mix_stage(xp, hid_p, zm_p, hb_p)
    bmf_p = _pick(s, (512, 256, 128, 64, 32, 16, 8))
    y_p, tg_p, tv_p = conv_ffn_prompt(x1_p, row(g_ffn_norm), w_up_b, w_conv, row(b_conv), w_down_b, bmf_p, bff, s)
    nb_seq = s // bmf_p
    tail_p = jnp.concatenate([tg_p, tv_p], axis=-1).reshape(bp, nb_seq, SUBLANES, 2 * f)
    conv_p = tail_p[:, nb_seq - 1, SUBLANES - (CONV_W - 1):, :]

    xs = x_sample.reshape(ts, d)
    bm_s = _pick(ts, (512, 256, 128, 64, 32, 16, 8))
    zm_s, q_s, ckv_s, kr_s, k_s, _, gates_s = token_stage(
        xs, jnp.tile(past + jnp.arange(seq, dtype=jnp.int32), db), bm_s)
    rb = _pick(ts, (128, 64, 32, 16, 8))
    m_col = jnp.repeat(jnp.transpose(m_state), seq, axis=1)[:, :, None]
    n_rows = jnp.repeat(jnp.transpose(n_state, (1, 0, 2)), seq, axis=1)
    hid_s, c_s, n_s_rows, m_s_rows = mlstm_sample(zm_s, b_gates, *gates_s, m_col, n_rows, c_state[None], rb, seq)
    n_s = jnp.transpose(n_s_rows[:, seq - 1::seq, :], (1, 0, 2))
    m_s = jnp.transpose(m_s_rows[:, seq - 1::seq, 0])

    q_abs = q_absorb(q_s, w_uk_t).reshape(db, seq * B_HEADS, c_lat)
    g_pages = _pick(n_pages, (32, 16, 8, 4, 2))
    o_lat = attn_sample(page_table, cache_lat, jnp.swapaxes(cache_kr, 2, 3), w_uk_t, q_abs,
                        q_s.reshape(db, seq * B_HEADS, Q_HEAD_PAD), ckv_s.reshape(db, seq, c_lat),
                        k_s.reshape(db, seq * B_HEADS, Q_HEAD_PAD), g_pages, seq)
    hb_s = uv_expand(o_lat.reshape(ts, B_HEADS * c_lat), w_uv2)
    x1_s = mix_stage(xs, hid_s, zm_s, hb_s)
    x1_t = jnp.transpose(x1_s.reshape(db, seq, d), (1, 0, 2)).reshape(ts, d)
    prev_t = jnp.transpose(conv_state.reshape(db, CONV_W - 1, 2, f), (1, 2, 0, 3))
    y_t, new_t = conv_ffn_sample(x1_t, prev_t, row(g_ffn_norm), w_up_b, w_conv, row(b_conv), w_down_b, bff, seq)
    y_s = jnp.transpose(y_t.reshape(seq, db, d), (1, 0, 2))
    conv_s = jnp.transpose(new_t, (2, 0, 1, 3)).reshape(db, CONV_W - 1, 2 * f)

    new_p = (ckv_p.reshape(bp, s, c_lat), kr_p[:, :B_ROPE].reshape(bp, s, B_ROPE),
             c_p, n_p[:, :, 0, :], m_p[:, :, 0, 0], conv_p)
    new_s = (ckv_s.reshape(db, seq, c_lat), kr_s[:, :B_ROPE].reshape(db, seq, B_ROPE),
             c_s[0], n_s, m_s, conv_s)
    return y_p.reshape(bp, s, d), y_s.reshape(db, seq, d), new_p, new_s


def kernel(x_prompt, x_sample, cache_kv_latent, cache_k_rope, state_mlstm_C, state_mlstm_n, state_mlstm_m, state_conv, page_table, g_attn_norm, w_in, b_gates, g_q_a, w_uq, g_qk_nope_q, g_qk_rope_q, g_kv_a, w_uk, w_uv, g_qk_nope_k, g_qk_rope_k, g_mlstm_out, w_branch_a, w_branch_b, w_out, g_ffn_norm, w_up, w_conv, b_conv, w_down):
    depth = w_in.shape[0]
    assert depth == 1, "single-layer trunk"
    l = 0
    y_p, y_s, new_p, new_s = _layer(
        x_prompt, x_sample, cache_kv_latent, cache_k_rope, state_mlstm_C[l], state_mlstm_n[l],
        state_mlstm_m[l], state_conv[l], page_table, g_attn_norm[l], w_in[l], b_gates[l], g_q_a[l], w_uq[l],
        g_qk_nope_q[l], g_qk_rope_q[l], g_kv_a[l], w_uk[l], w_uv[l], g_qk_nope_k[l], g_qk_rope_k[l],
        g_mlstm_out[l], w_branch_a[l], w_branch_b[l], w_out[l], g_ffn_norm[l], w_up[l], w_conv[l], b_conv[l],
        w_down[l])
    dts = (cache_kv_latent.dtype, cache_k_rope.dtype, state_mlstm_C.dtype, state_mlstm_n.dtype,
           state_mlstm_m.dtype, state_conv.dtype)
    st_p = tuple(a[None].astype(dt) for a, dt in zip(new_p, dts))
    st_s = tuple(a[None].astype(dt) for a, dt in zip(new_s, dts))
    return (y_p, y_s) + st_p + st_s
```
